```python
import math
import jax
import jax.numpy as jnp
from jax import lax
import numpy as np

D_MODEL = 1024
BATCH = 1
SEQ = 16384
DEPTH = 2
DEC_BATCH = 32
DEC_SEQ = 1
PAST_LEN = 16384
PAGE_SIZE = 128

N_EVEN = (DEPTH + 1) // 2
N_ODD = DEPTH // 2
S5_WIDTH = D_MODEL // 2
S5_GROUP = 16
S5_GROUPS = S5_WIDTH // S5_GROUP
S5_STATE = 64
SGU_WIDTH = D_MODEL // 2
SGU_HEADS = 4
SGU_HEAD_DIM = SGU_WIDTH // SGU_HEADS
CHUNK = 128
EVEN_IN = S5_WIDTH + 2 * SGU_WIDTH
HEAD_DIM = 64
N_HEADS = D_MODEL // HEAD_DIM
ROT_DIM = HEAD_DIM // 4
ROPE_THETA = 500000.0
DIL_BRANCHES = ((128, 1), (512, 4), (2048, 16))
BAND = 128
WIN_MAX = 2048
D_FF = ((8 * D_MODEL // 3 + 127) // 128) * 128
CONV_W = 3
EPS = 1e-6
NEG_INF = -1e30

kernel_name = 'hybrid_s5_sgu_dilated_attn_decoder_step'


def rms_norm(x, g):
    x32 = x.astype(jnp.float32)
    y = x32 * lax.rsqrt(jnp.mean(x32 * x32, axis=-1, keepdims=True) + EPS)
    return (y * g.astype(jnp.float32)).astype(x.dtype)


def ada_modulation(c, w, b):
    mod = jax.nn.silu(c) @ w + b
    return jnp.split(mod[:, None, :], 6, axis=-1)


def partial_rope(x, pos):
    half = ROT_DIM // 2
    inv = jnp.power(ROPE_THETA, -jnp.arange(half, dtype=jnp.float32) * 2.0 / ROT_DIM)
    ang = pos.astype(jnp.float32)[:, None] * inv[None, :]
    cos = jnp.cos(ang)[None, :, None, :]
    sin = jnp.sin(ang)[None, :, None, :]
    x32 = x.astype(jnp.float32)
    x1 = x32[..., :half]
    x2 = x32[..., half:ROT_DIM]
    out = jnp.concatenate([x1 * cos - x2 * sin, x1 * sin + x2 * cos, x32[..., ROT_DIM:]], axis=-1)
    return out.astype(x.dtype)


def s5_mixer(xa, h0_re, h0_im, lam_re, lam_im, log_dt, b_re, b_im, c_re, c_im, d_skip, w_glu, b_glu):
    f32 = jnp.float32
    bsz, L, _ = xa.shape
    u = xa.astype(f32).reshape(bsz, L, S5_GROUPS, S5_GROUP)
    dt = jnp.exp(log_dt.astype(f32))[:, None]
    lr = lam_re.astype(f32)
    li = lam_im.astype(f32)
    mag = jnp.exp(lr * dt)
    ar = mag * jnp.cos(li * dt)
    ai = mag * jnp.sin(li * dt)
    den = lr * lr + li * li
    fr = ((ar - 1.0) * lr + ai * li) / den
    fi = (ai * lr - (ar - 1.0) * li) / den
    br = b_re.astype(f32)
    bi = b_im.astype(f32)
    bbr = fr[..., None] * br - fi[..., None] * bi
    bbi = fr[..., None] * bi + fi[..., None] * br
    ur = jnp.einsum('blgp,gnp->blgn', u, bbr)
    ui = jnp.einsum('blgp,gnp->blgn', u, bbi)
    h0r = h0_re.astype(f32)
    h0i = h0_im.astype(f32)
    ur = ur.at[:, 0].add(ar * h0r - ai * h0i)
    ui = ui.at[:, 0].add(ar * h0i + ai * h0r)

    def combine(e1, e2):
        a1r, a1i, b1r, b1i = e1
        a2r, a2i, b2r, b2i = e2
        return (a2r * a1r - a2i * a1i, a2r * a1i + a2i * a1r,
                a2r * b1r - a2i * b1i + b2r, a2r * b1i + a2i * b1r + b2i)

    _, _, hr, hi = lax.associative_scan(
        combine, (jnp.broadcast_to(ar, ur.shape), jnp.broadcast_to(ai, ur.shape), ur, ui), axis=1)
    y = (jnp.einsum('blgn,gpn->blgp', hr, c_re.astype(f32))
         - jnp.einsum('blgn,gpn->blgp', hi, c_im.astype(f32))
         + d_skip.astype(f32).reshape(S5_GROUPS, S5_GROUP) * u)
    y = jax.nn.gelu(y.reshape(bsz, L, S5_WIDTH))
    out = y * jax.nn.sigmoid(y @ w_glu.astype(f32) + b_glu.astype(f32))
    return out.astype(xa.dtype), hr[:, -1].astype(h0_re.dtype), hi[:, -1].astype(h0_im.dtype)


def sgu_mixer(u, v, ln_g, ln_b, w_s, b_s):
    bsz, L, _ = v.shape
    v32 = v.astype(jnp.float32)
    mu = jnp.mean(v32, axis=-1, keepdims=True)
    var = jnp.mean(jnp.square(v32 - mu), axis=-1, keepdims=True)
    vn = ((v32 - mu) * lax.rsqrt(var + EPS) * ln_g.astype(jnp.float32) + ln_b.astype(jnp.float32)).astype(v.dtype)
    Lp = -(-L // CHUNK) * CHUNK
    vc = jnp.pad(vn, ((0, 0), (0, Lp - L), (0, 0))).reshape(bsz, Lp // CHUNK, CHUNK, SGU_HEADS, SGU_HEAD_DIM)
    causal = jnp.tril(jnp.ones((CHUNK, CHUNK), dtype=w_s.dtype))
    s = jnp.einsum('hij,bcjhe->bcihe', w_s * causal[None], vc) + b_s.T[None, None, :, :, None]
    s = s.reshape(bsz, Lp, SGU_WIDTH)[:, :L]
    return u * s, vn


def dilated_band(q, k, v, d):
    f32 = jnp.float32
    bsz, L, H, E = q.shape
    span = BAND * d
    Lp = -(-L // span) * span
    nb = Lp // span

    def to_sub(t):
        t = jnp.pad(t.astype(f32), ((0, 0), (0, Lp - L), (0, 0), (0, 0)))
        t = t.reshape(bsz, Lp // d, d, H, E).transpose(0, 2, 1, 3, 4)
        return t.reshape(bsz, d, nb, BAND, H, E)

    def with_prev(t):
        prev = jnp.concatenate([jnp.zeros_like(t[:, :, :1]), t[:, :, :-1]], axis=2)
        return jnp.concatenate([prev, t], axis=3)

    qb = to_sub(q)
    kk = with_prev(to_sub(k))
    vv = with_prev(to_sub(v))
    s = jnp.einsum('brnqhe,brnkhe->brnhqk', qb, kk) * (HEAD_DIM ** -0.5)
    i = jnp.arange(BAND)[:, None]
    j = jnp.arange(2 * BAND)[None, :]
    band = (j >= i) & (j <= i + BAND)
    mask = band[None] & ((jnp.arange(nb)[:, None, None] > 0) | (j >= BAND)[None])
    s = jnp.where(mask[None, None, :, None], s, NEG_INF)
    m = jnp.max(s, axis=-1)
    p = jnp.exp(s - m[..., None])
    den = jnp.sum(p, axis=-1)
    m = m.transpose(0, 1, 2, 4, 3)
    den = den.transpose(0, 1, 2, 4, 3)
    o = jnp.einsum('brnhqk,brnkhe->brnqhe', p, vv) / den[..., None]

    def from_sub(t):
        t = t.reshape((bsz, d, Lp // d) + t.shape[4:])
        t = jnp.moveaxis(t, 1, 2)
        t = t.reshape((bsz, Lp) + t.shape[3:])
        return t[:, :L]

    return from_sub(o), from_sub(m), from_sub(den)


def dilated_gather(q, k_all, v_all, window, d, n_past):
    f32 = jnp.float32
    Lq = q.shape[1]
    n_k = window // d + 1
    idx = n_past + jnp.arange(Lq)[:, None] - d * jnp.arange(n_k)[None, :]
    valid = idx >= 0
    idx = jnp.maximum(idx, 0)
    kg = k_all[:, idx].astype(f32)
    vg = v_all[:, idx].astype(f32)
    s = jnp.einsum('bqhe,bqkhe->bqhk', q.astype(f32), kg) * (HEAD_DIM ** -0.5)
    s = jnp.where(valid[None, :, None, :], s, NEG_INF)
    m = jnp.max(s, axis=-1)
    p = jnp.exp(s - m[..., None])
    den = jnp.sum(p, axis=-1)
    o = jnp.einsum('bqhk,bqkhe->bqhe', p, vg) / den[..., None]
    return o, m, den


def merge_branches(parts):
    o = jnp.stack([pt[0] for pt in parts])
    m = jnp.stack([pt[1] for pt in parts])
    den = jnp.stack([pt[2] for pt in parts])
    w = den * jnp.exp(m - jnp.max(m, axis=0, keepdims=True))
    return jnp.sum(w[..., None] * o, axis=0) / jnp.sum(w, axis=0)[..., None]


def conv_ffn(h, buf, w_up, conv_w, conv_b, w_down):
    L = h.shape[1]
    up = h @ w_up
    a = up[..., :D_FF]
    g = up[..., D_FF:]
    xa = jnp.concatenate([buf.astype(a.dtype), a], axis=1)
    y = conv_b
    for tap in range(CONV_W):
        y = y + conv_w[tap] * xa[:, tap:tap + L]
    out = (jax.nn.gelu(y) * g) @ w_down
    return out, xa[:, L:]


def trunk(x, c, is_prompt, s5_re0, s5_im0, ck, cv, conv0, p):
    bsz, L, _ = x.shape
    pos = jnp.arange(L, dtype=jnp.int32) + (0 if is_prompt else PAST_LEN)
    s5r, s5i, vrows, ks, vs, convs = [], [], [], [], [], []
    for layer in range(DEPTH):
        sh1, sc1, g1, sh2, sc2, g2 = ada_modulation(c, p['ada_w'][layer], p['ada_b'][layer])
        h = rms_norm(x, p['norm_g'][layer, 0]) * (1 + sc1) + sh1
        if layer % 2 == 0:
            e = layer // 2
            proj = h @ p['ev_w_in'][e]
            xa = proj[..., :S5_WIDTH]
            u = jax.nn.gelu(proj[..., S5_WIDTH:S5_WIDTH + SGU_WIDTH])
            v = jax.nn.gelu(proj[..., S5_WIDTH + SGU_WIDTH:])
            a_out, hr, hi = s5_mixer(xa, s5_re0[e], s5_im0[e], p['s5_lam_re'][e], p['s5_lam_im'][e],
                                     p['s5_log_dt'][e], p['s5_b_re'][e], p['s5_b_im'][e], p['s5_c_re'][e],
                                     p['s5_c_im'][e], p['s5_d'][e], p['s5_w_glu'][e], p['s5_b_glu'][e])
            b_out, vn = sgu_mixer(u, v, p['sg_ln_g'][e], p['sg_ln_b'][e], p['sg_w'][e], p['sg_b'][e])
            mix = jnp.concatenate([a_out, b_out], axis=-1) @ p['ev_w_out'][e]
            s5r.append(hr)
            s5i.append(hi)
            vrows.append(vn)
        else:
            o_i = layer // 2
            qkv = (h @ p['od_w_qkv'][o_i]).reshape(bsz, L, 3, N_HEADS, HEAD_DIM)
            q = partial_rope(qkv[:, :, 0], pos)
            k = partial_rope(qkv[:, :, 1], pos)
            v = qkv[:, :, 2]
            if is_prompt:
                parts = [dilated_band(q, k, v, d) for (_, d) in DIL_BRANCHES]
                keep = min(WIN_MAX, L)
                ks.append(k[:, L - keep:])
                vs.append(v[:, L - keep:])
            else:
                n_past = ck.shape[2]
                k_all = jnp.concatenate([ck[o_i].astype(k.dtype), k], axis=1)
                v_all = jnp.concatenate([cv[o_i].astype(v.dtype), v], axis=1)
                parts = [dilated_gather(q, k_all, v_all, w, d, n_past) for (w, d) in DIL_BRANCHES]
                ks.append(k)
                vs.append(v)
            att = merge_branches(parts).astype(x.dtype)
            mix = att.reshape(bsz, L, N_HEADS * HEAD_DIM) @ p['od_w_o'][o_i]
        x = x + g1 * mix
        h = rms_norm(x, p['norm_g'][layer, 1]) * (1 + sc2) + sh2
        ff, buf = conv_ffn(h, conv0[layer], p['ffn_w_up'][layer], p['ffn_conv_w'][layer],
                           p['ffn_conv_b'][layer], p['ffn_w_down'][layer])
        convs.append(buf)
        x = x + g2 * ff
    y = rms_norm(x, p['final_g'])
    return y, jnp.stack(s5r), jnp.stack(s5i), jnp.stack(vrows), jnp.stack(ks), jnp.stack(vs), jnp.stack(convs)


def setup_inputs(seed: int = 0) -> dict:
    key = jax.random.key(seed)
    ks = iter(jax.random.split(key, 48))
    f32 = jnp.float32

    def nrm(shape, scale=1.0):
        return jax.random.normal(next(ks), shape, f32) * scale

    buf_len = min(WIN_MAX, PAST_LEN)
    inp = {}
    inp['x_prompt'] = nrm((BATCH, SEQ, D_MODEL))
    inp['x_sample'] = nrm((DEC_BATCH, DEC_SEQ, D_MODEL))
    inp['c_prompt'] = nrm((BATCH, D_MODEL))
    inp['c_sample'] = nrm((DEC_BATCH, D_MODEL))
    inp['state_s5_re'] = nrm((N_EVEN, DEC_BATCH, S5_GROUPS, S5_STATE), 0.3)
    inp['state_s5_im'] = nrm((N_EVEN, DEC_BATCH, S5_GROUPS, S5_STATE), 0.3)
    inp['cache_c_k'] = nrm((N_ODD, DEC_BATCH, buf_len, N_HEADS, HEAD_DIM))
    inp['cache_c_v'] = nrm((N_ODD, DEC_BATCH, buf_len, N_HEADS, HEAD_DIM))
    inp['state_ffn_conv'] = nrm((DEPTH, DEC_BATCH, CONV_W - 1, D_FF))
    inp['ada_w'] = nrm((DEPTH, D_MODEL, 6 * D_MODEL), 0.5 * D_MODEL ** -0.5)
    inp['ada_b'] = nrm((DEPTH, 6 * D_MODEL), 0.01)
    inp['norm_g'] = 1.0 + nrm((DEPTH, 2, D_MODEL), 0.01)
    inp['final_g'] = 1.0 + nrm((D_MODEL,), 0.01)
    inp['ev_w_in'] = nrm((N_EVEN, D_MODEL, EVEN_IN), D_MODEL ** -0.5)
    inp['ev_w_out'] = nrm((N_EVEN, S5_WIDTH + SGU_WIDTH, D_MODEL), (S5_WIDTH + SGU_WIDTH) ** -0.5)
    inp['s5_lam_re'] = -0.5 + nrm((N_EVEN, S5_GROUPS, S5_STATE), 0.01)
    inp['s5_lam_im'] = math.pi * jnp.arange(S5_STATE, dtype=f32) + nrm((N_EVEN, S5_GROUPS, S5_STATE), 0.01)
    inp['s5_log_dt'] = jax.random.uniform(next(ks), (N_EVEN, S5_GROUPS), f32, math.log(1e-3), math.log(1e-1))
    inp['s5_b_re'] = nrm((N_EVEN, S5_GROUPS, S5_STATE, S5_GROUP), (2 * S5_GROUP) ** -0.5)
    inp['s5_b_im'] = nrm((N_EVEN, S5_GROUPS, S5_STATE, S5_GROUP), (2 * S5_GROUP) ** -0.5)
    inp['s5_c_re'] = nrm((N_EVEN, S5_GROUPS, S5_GROUP, S5_STATE), (2 * S5_STATE) ** -0.5)
    inp['s5_c_im'] = nrm((N_EVEN, S5_GROUPS, S5_GROUP, S5_STATE), (2 * S5_STATE) ** -0.5)
    inp['s5_d'] = nrm((N_EVEN, S5_WIDTH))
    inp['s5_w_glu'] = nrm((N_EVEN, S5_WIDTH, S5_WIDTH), S5_WIDTH ** -0.5)
    inp['s5_b_glu'] = nrm((N_EVEN, S5_WIDTH), 0.01)
    inp['sg_ln_g'] = 1.0 + nrm((N_EVEN, SGU_WIDTH), 0.01)
    inp['sg_ln_b'] = nrm((N_EVEN, SGU_WIDTH), 0.01)
    inp['sg_w'] = nrm((N_EVEN, SGU_HEADS, CHUNK, CHUNK), CHUNK ** -0.5)
    inp['sg_b'] = 1.0 + nrm((N_EVEN, SGU_HEADS, CHUNK), 0.01)
    inp['od_w_qkv'] = nrm((N_ODD, D_MODEL, 3 * N_HEADS * HEAD_DIM), D_MODEL ** -0.5)
    inp['od_w_o'] = nrm((N_ODD, N_HEADS * HEAD_DIM, D_MODEL), (N_HEADS * HEAD_DIM) ** -0.5)
    inp['ffn_w_up'] = nrm((DEPTH, D_MODEL, 2 * D_FF), D_MODEL ** -0.5)
    inp['ffn_conv_w'] = nrm((DEPTH, CONV_W, D_FF), CONV_W ** -0.5)
    inp['ffn_conv_b'] = nrm((DEPTH, D_FF), 0.01)
    inp['ffn_w_down'] = nrm((DEPTH, D_FF, D_MODEL), D_FF ** -0.5)
    return inp


def reference(x_prompt, x_sample, c_prompt, c_sample, state_s5_re, state_s5_im, cache_c_k, cache_c_v,
              state_ffn_conv, ada_w, ada_b, norm_g, final_g, ev_w_in, ev_w_out, s5_lam_re, s5_lam_im,
              s5_log_dt, s5_b_re, s5_b_im, s5_c_re, s5_c_im, s5_d, s5_w_glu, s5_b_glu, sg_ln_g, sg_ln_b,
              sg_w, sg_b, od_w_qkv, od_w_o, ffn_w_up, ffn_conv_w, ffn_conv_b, ffn_w_down):
    params = dict(ada_w=ada_w, ada_b=ada_b, norm_g=norm_g, final_g=final_g, ev_w_in=ev_w_in,
                  ev_w_out=ev_w_out, s5_lam_re=s5_lam_re, s5_lam_im=s5_lam_im, s5_log_dt=s5_log_dt,
                  s5_b_re=s5_b_re, s5_b_im=s5_b_im, s5_c_re=s5_c_re, s5_c_im=s5_c_im, s5_d=s5_d,
                  s5_w_glu=s5_w_glu, s5_b_glu=s5_b_glu, sg_ln_g=sg_ln_g, sg_ln_b=sg_ln_b, sg_w=sg_w,
                  sg_b=sg_b, od_w_qkv=od_w_qkv, od_w_o=od_w_o, ffn_w_up=ffn_w_up, ffn_conv_w=ffn_conv_w,
                  ffn_conv_b=ffn_conv_b, ffn_w_down=ffn_w_down)
    bp = x_prompt.shape[0]
    zero_s5 = jnp.zeros((N_EVEN, bp, S5_GROUPS, S5_STATE), x_prompt.dtype)
    zero_conv = jnp.zeros((DEPTH, bp, CONV_W - 1, D_FF), x_prompt.dtype)
    y_prompt, s5_re_p, s5_im_p, _, k_p, v_p, conv_p = trunk(
        x_prompt, c_prompt, True, zero_s5, zero_s5, None, None, zero_conv, params)
    y_sample, s5_re_s, s5_im_s, sgu_v_s, k_s, v_s, conv_s = trunk(
        x_sample, c_sample, False, state_s5_re, state_s5_im, cache_c_k, cache_c_v, state_ffn_conv, params)
    return (y_prompt, y_sample, s5_re_p, s5_im_p, s5_re_s, s5_im_s, sgu_v_s, k_p, v_p, k_s, v_s, conv_p, conv_s)
```

```python
import functools
import math

import jax
import jax.numpy as jnp
from jax import lax
from jax.experimental import pallas as pl
from jax.experimental.pallas import tpu as pltpu

F32 = jnp.float32
BF16 = jnp.bfloat16

D_MODEL = 1024
SEQ = 16384
DEC_BATCH = 32
PAST_LEN = 16384
S5_WIDTH = 512
S5_GROUP = 16
S5_GROUPS = 32
S5_STATE = 64
SGU_WIDTH = 512
SGU_HEADS = 4
CHUNK = 128
EVEN_IN = S5_WIDTH + 2 * SGU_WIDTH
HEAD_DIM = 64
N_HEADS = 16
ROT_DIM = 16
ROPE_THETA = 500000.0
DIL_BRANCHES = ((128, 1), (512, 4), (2048, 16))
BAND = 128
WIN_MAX = 2048
D_FF = 2816
EPS = 1e-6
NEG_INF = -1e30

ROW_TILE = 512
MOD_ROWS = 40
S5_T = 16
S5_SEG = 8
S5_CL = SEQ // (S5_T * S5_SEG)
FF_CHUNKS = ((0, 1024), (1024, 1024), (2048, 768))
VMEM_LIMIT = 56 * 1024 * 1024


def _cparams(*sem):
    return pltpu.CompilerParams(dimension_semantics=sem, vmem_limit_bytes=VMEM_LIMIT)


def _const_spec(shape):
    nd = len(shape)
    return pl.BlockSpec(shape, lambda *_: (0,) * nd)


def _weight_spec(shape):
    nd = len(shape)
    return pl.BlockSpec(shape, lambda *_: (0,) * nd, pipeline_mode=pl.Buffered(1))


def _gelu(x):
    return jax.nn.gelu(x)


def _mod_norm(x, ng, shift, scale):
    ms = jnp.mean(x * x, axis=-1, keepdims=True)
    return (x * lax.rsqrt(ms + EPS) * ng) * (1.0 + scale) + shift


def _ada_body(c_ref, w_ref, b_ref, o_ref):
    c = c_ref[...]
    s = c * jax.nn.sigmoid(c)
    o_ref[...] = jnp.dot(s.astype(BF16), w_ref[...].astype(BF16),
                         preferred_element_type=F32) + b_ref[...]


def _ada_call(c_all, ada_w, ada_b):
    depth = ada_w.shape[0]
    nt = 1536
    return pl.pallas_call(
        _ada_body,
        grid=(depth, 6 * D_MODEL // nt),
        in_specs=[
            pl.BlockSpec((MOD_ROWS, D_MODEL), lambda l, j: (0, 0)),
            pl.BlockSpec((None, D_MODEL, nt), lambda l, j: (l, 0, j)),
            pl.BlockSpec((None, 1, nt), lambda l, j: (l, 0, j)),
        ],
        out_specs=pl.BlockSpec((None, MOD_ROWS, nt), lambda l, j: (l, 0, j)),
        out_shape=jax.ShapeDtypeStruct((depth, MOD_ROWS, 6 * D_MODEL), F32),
        compiler_params=_cparams("arbitrary", "arbitrary"),
        name="ada_mod",
    )(c_all, ada_w, ada_b.reshape(depth, 1, 6 * D_MODEL))


def _even_pre_body(sample, tm, x_ref, mod_ref, ng_ref, win_ref, lng_ref, lnb_ref, wt_ref, bs_ref,
                   xa_ref, bout_ref, *vn_out):
    h = _mod_norm(x_ref[...], ng_ref[...], mod_ref[:, 0:D_MODEL], mod_ref[:, D_MODEL:2 * D_MODEL])
    proj = jnp.dot(h.astype(BF16), win_ref[...], preferred_element_type=F32)
    xa_ref[...] = proj[:, :S5_WIDTH]
    u = _gelu(proj[:, S5_WIDTH:S5_WIDTH + SGU_WIDTH])
    v = _gelu(proj[:, S5_WIDTH + SGU_WIDTH:])
    mu = jnp.mean(v, axis=-1, keepdims=True)
    var = jnp.mean(jnp.square(v - mu), axis=-1, keepdims=True)
    vn = (v - mu) * lax.rsqrt(var + EPS) * lng_ref[...] + lnb_ref[...]
    if sample:
        vn_out[0][...] = vn
        bout_ref[...] = (u * (vn * bs_ref[0:1, :] + bs_ref[1:2, :])).astype(BF16)
    else:
        vnb = vn.astype(BF16)
        hd = SGU_WIDTH // SGU_HEADS
        for ci in range(tm // CHUNK):
            rows = slice(ci * CHUNK, (ci + 1) * CHUNK)
            for hh in range(SGU_HEADS):
                cols = slice(hh * hd, (hh + 1) * hd)
                s = jnp.dot(wt_ref[hh], vnb[rows, cols], preferred_element_type=F32) + bs_ref[:, cols]
                bout_ref[rows, cols] = (u[rows, cols] * s).astype(BF16)


def _even_pre_call(sample, x, mod, ng, win, lng, lnb, wt, bs):
    rows = x.shape[0]
    tm = rows if sample else ROW_TILE
    row_spec = lambda w: pl.BlockSpec((tm, w), lambda i: (i, 0))
    out_shape = [jax.ShapeDtypeStruct((rows, S5_WIDTH), F32),
                 jax.ShapeDtypeStruct((rows, SGU_WIDTH), BF16)]
    out_specs = [row_spec(S5_WIDTH), row_spec(SGU_WIDTH)]
    if sample:
        out_shape.append(jax.ShapeDtypeStruct((rows, SGU_WIDTH), F32))
        out_specs.append(row_spec(SGU_WIDTH))
    return pl.pallas_call(
        functools.partial(_even_pre_body, sample, tm),
        grid=(rows // tm,),
        in_specs=[row_spec(D_MODEL), _const_spec(mod.shape), _const_spec(ng.shape), _weight_spec(win.shape),
                  _const_spec(lng.shape), _const_spec(lnb.shape), _const_spec(wt.shape), _const_spec(bs.shape)],
        out_specs=out_specs,
        out_shape=out_shape,
        compiler_params=_cparams("arbitrary"),
        name="even_pre_sample" if sample else "even_pre_prompt",
    )(x, mod, ng, win, lng, lnb, wt, bs)


def _s5_prompt_body(x_ref, mt_ref, pbr_ref, pbi_ref, pcr_ref, pci_ref, a16_ref, aseg_ref,
                    y_ref, hfin_ref, s_re, s_im, hp_re, hp_im, pw_re, pw_im):
    x = x_ref[...]
    tile = (S5_CL, S5_SEG, S5_STATE)
    s_re[...] = jnp.dot(x, pbr_ref[...], preferred_element_type=F32).reshape(tile)
    s_im[...] = jnp.dot(x, pbi_ref[...], preferred_element_type=F32).reshape(tile)
    a_r = jnp.broadcast_to(a16_ref[0:1, :], (S5_SEG, S5_STATE))
    a_i = jnp.broadcast_to(a16_ref[1:2, :], (S5_SEG, S5_STATE))

    def step(cl, carry):
        hr, hi, pr, pi = carry
        hp_re[cl] = hr
        hp_im[cl] = hi
        pw_re[cl] = pr
        pw_im[cl] = pi
        nhr = a_r * hr - a_i * hi + s_re[cl]
        nhi = a_r * hi + a_i * hr + s_im[cl]
        return nhr, nhi, a_r * pr - a_i * pi, a_r * pi + a_i * pr

    zero = jnp.zeros((S5_SEG, S5_STATE), F32)
    hr, hi, _, _ = lax.fori_loop(0, S5_CL, step, (zero, zero, zero + 1.0, zero))

    g_r = aseg_ref[0:1, :]
    g_i = aseg_ref[1:2, :]
    er = jnp.zeros((1, S5_STATE), F32)
    ei = jnp.zeros((1, S5_STATE), F32)
    ent_r, ent_i = [], []
    for s in range(S5_SEG):
        ent_r.append(er)
        ent_i.append(ei)
        er, ei = (g_r * er - g_i * ei + hr[s:s + 1, :], g_r * ei + g_i * er + hi[s:s + 1, :])
    hfin_ref[0:1, :] = er
    hfin_ref[1:2, :] = ei
    e_r = jnp.concatenate(ent_r, axis=0)[None]
    e_i = jnp.concatenate(ent_i, axis=0)[None]

    pr = pw_re[...]
    pi = pw_im[...]
    rows = S5_CL * S5_SEG
    h_r = (hp_re[...] + pr * e_r - pi * e_i).reshape(rows, S5_STATE).astype(BF16)
    h_i = (hp_im[...] + pr * e_i + pi * e_r).reshape(rows, S5_STATE).astype(BF16)
    y_ref[...] = (jnp.dot(x, mt_ref[...], preferred_element_type=F32)
                  + jnp.dot(h_r, pcr_ref[...], preferred_element_type=F32)
                  + jnp.dot(h_i, pci_ref[...], preferred_element_type=F32))


def _s5_prompt_call(xt, mt, pbr, pbi, pcr, pci, a16, aseg):
    g, rows, width = xt.shape
    grp = lambda a: pl.BlockSpec((None,) + a.shape[1:], lambda i: (i,) + (0,) * (a.ndim - 1))
    scratch = [pltpu.VMEM((S5_CL, S5_SEG, S5_STATE), F32) for _ in range(6)]
    return pl.pallas_call(
        _s5_prompt_body,
        grid=(g,),
        in_specs=[grp(xt), grp(mt), grp(pbr), grp(pbi), grp(pcr), grp(pci), grp(a16), grp(aseg)],
        out_specs=[pl.BlockSpec((None, rows, width), lambda i: (i, 0, 0)),
                   pl.BlockSpec((None, 2, S5_STATE), lambda i: (i, 0, 0))],
        out_shape=[jax.ShapeDtypeStruct((g, rows, width), F32),
                   jax.ShapeDtypeStruct((g, 2, S5_STATE), F32)],
        scratch_shapes=scratch,
        compiler_params=_cparams("arbitrary"),
        name="s5_prompt",
    )(xt, mt, pbr, pbi, pcr, pci, a16, aseg)


def _s5_sample_body(xa_ref, h0r_ref, h0i_ref, ar_ref, ai_ref, bdr_ref, bdi_ref, cdr_ref, cdi_ref,
                    yc_ref, hr_ref, hi_ref):
    u = xa_ref[...].astype(BF16)
    ar = ar_ref[...]
    ai = ai_ref[...]
    h0r = h0r_ref[...]
    h0i = h0i_ref[...]
    hr = ar * h0r - ai * h0i + jnp.dot(u, bdr_ref[...], preferred_element_type=F32)
    hi = ar * h0i + ai * h0r + jnp.dot(u, bdi_ref[...], preferred_element_type=F32)
    hr_ref[...] = hr
    hi_ref[...] = hi
    yc_ref[...] = (jnp.dot(hr.astype(BF16), cdr_ref[...], preferred_element_type=F32)
                   - jnp.dot(hi.astype(BF16), cdi_ref[...], preferred_element_type=F32))


def _s5_sample_call(xa, h0r, h0i, ar, ai, bdr, bdi, cdr, cdi):
    rows = xa.shape[0]
    n = S5_GROUPS * S5_STATE
    args = (xa, h0r, h0i, ar, ai, bdr, bdi, cdr, cdi)
    return pl.pallas_call(
        _s5_sample_body,
        grid=(1,),
        in_specs=[_const_spec(a.shape) for a in args],
        out_specs=[_const_spec((rows, S5_WIDTH)), _const_spec((rows, n)), _const_spec((rows, n))],
        out_shape=[jax.ShapeDtypeStruct((rows, S5_WIDTH), F32),
                   jax.ShapeDtypeStruct((rows, n), F32),
                   jax.ShapeDtypeStruct((rows, n), F32)],
        compiler_params=_cparams("arbitrary"),
        name="s5_sample",
    )(*args)


def _even_post_body(x_ref, yc_ref, xa_ref, bout_ref, mod_ref, d_ref, wglu_ref, bglu_ref, wout_ref, o_ref):
    y = _gelu(yc_ref[...] + d_ref[...] * xa_ref[...])
    gate = jax.nn.sigmoid(jnp.dot(y.astype(BF16), wglu_ref[...], preferred_element_type=F32) + bglu_ref[...])
    a_out = (y * gate).astype(BF16)
    mix = (jnp.dot(a_out, wout_ref[0:S5_WIDTH, :], preferred_element_type=F32)
           + jnp.dot(bout_ref[...], wout_ref[S5_WIDTH:, :], preferred_element_type=F32))
    o_ref[...] = x_ref[...] + mod_ref[:, 2 * D_MODEL:3 * D_MODEL] * mix


def _even_post_call(x, yc, xa, bout, mod, d, wglu, bglu, wout):
    rows = x.shape[0]
    tm = min(rows, ROW_TILE)
    row_spec = lambda w: pl.BlockSpec((tm, w), lambda i: (i, 0))
    return pl.pallas_call(
        _even_post_body,
        grid=(rows // tm,),
        in_specs=[row_spec(D_MODEL), row_spec(S5_WIDTH), row_spec(S5_WIDTH), row_spec(SGU_WIDTH),
                  _const_spec(mod.shape), _const_spec(d.shape), _const_spec(wglu.shape),
                  _const_spec(bglu.shape), _weight_spec(wout.shape)],
        out_specs=row_spec(D_MODEL),
        out_shape=jax.ShapeDtypeStruct((rows, D_MODEL), F32),
        compiler_params=_cparams("arbitrary"),
        name="even_post",
    )(x, yc, xa, bout, mod, d, wglu, bglu, wout)


def _ffn_body(sample, final, tm, *refs):
    refs = list(refs)
    x_ref, mod_ref, ng_ref, wup_ref, cw_ref, cb_ref, wdn_ref = refs[:7]
    pos = 7
    if sample:
        p2_ref, p1_ref = refs[pos:pos + 2]
        pos += 2
    if final:
        fg_ref = refs[pos]
        pos += 1
    o_ref, conv_ref = refs[pos:pos + 2]
    pos += 2
    if not sample:
        carry_ref = refs[pos]

        @pl.when(pl.program_id(0) == 0)
        def _():
            carry_ref[...] = jnp.zeros_like(carry_ref)

    x = x_ref[...]
    h = _mod_norm(x, ng_ref[...], mod_ref[:, 3 * D_MODEL:4 * D_MODEL], mod_ref[:, 4 * D_MODEL:5 * D_MODEL])
    hb = h.astype(BF16)
    acc = jnp.zeros((tm, D_MODEL), F32)
    if not sample:
        row = lax.broadcasted_iota(jnp.int32, (tm, 1), 0)
    for c0, cw in FF_CHUNKS:
        cols = slice(c0, c0 + cw)
        a = jnp.dot(hb, wup_ref[:, cols], preferred_element_type=F32)
        g = jnp.dot(hb, wup_ref[:, D_FF + c0:D_FF + c0 + cw], preferred_element_type=F32)
        if sample:
            am2 = p2_ref[:, cols]
            am1 = p1_ref[:, cols]
            conv_ref[:, cols] = a
        else:
            prev2 = carry_ref[0:1, cols]
            prev1 = carry_ref[1:2, cols]
            am1 = jnp.where(row == 0, prev1, pltpu.roll(a, 1, 0))
            am2 = jnp.where(row == 0, prev2, jnp.where(row == 1, prev1, pltpu.roll(a, 2, 0)))
            carry_ref[0:2, cols] = a[tm - 2:tm, :]
        y = cb_ref[:, cols] + cw_ref[0:1, cols] * am2 + cw_ref[1:2, cols] * am1 + cw_ref[2:3, cols] * a
        act = (_gelu(y) * g).astype(BF16)
        acc = acc + jnp.dot(act, wdn_ref[cols, :], preferred_element_type=F32)
    out = x + mod_ref[:, 5 * D_MODEL:6 * D_MODEL] * acc
    if final:
        ms = jnp.mean(out * out, axis=-1, keepdims=True)
        out = out * lax.rsqrt(ms + EPS) * fg_ref[...]
    o_ref[...] = out
    if not sample:
        conv_ref[...] = carry_ref[0:2, :]


def _ffn_call(sample, x, mod, ng, wup, cw, cb, wdn, prev=None, final_g=None):
    rows = x.shape[0]
    tm = rows if sample else ROW_TILE
    final = final_g is not None
    row_spec = lambda w: pl.BlockSpec((tm, w), lambda i: (i, 0))
    args = [x, mod, ng, wup, cw, cb, wdn]
    in_specs = [row_spec(D_MODEL), _const_spec(mod.shape), _const_spec(ng.shape), _weight_spec(wup.shape),
                _const_spec(cw.shape), _const_spec(cb.shape), _weight_spec(wdn.shape)]
    if sample:
        args += [prev[0], prev[1]]
        in_specs += [_const_spec(prev[0].shape), _const_spec(prev[1].shape)]
    if final:
        args.append(final_g)
        in_specs.append(_const_spec(final_g.shape))
    conv_rows = rows if sample else 2
    return pl.pallas_call(
        functools.partial(_ffn_body, sample, final, tm),
        grid=(rows // tm,),
        in_specs=in_specs,
        out_specs=[row_spec(D_MODEL), _const_spec((conv_rows, D_FF))],
        out_shape=[jax.ShapeDtypeStruct((rows, D_MODEL), F32),
                   jax.ShapeDtypeStruct((conv_rows, D_FF), F32)],
        scratch_shapes=[] if sample else [pltpu.VMEM((8, D_FF), F32)],
        compiler_params=_cparams("arbitrary"),
        name="ffn_sample" if sample else "ffn_prompt",
    )(*args)


def _odd_pre_body(x_ref, mod_ref, ng_ref, wqkv_ref, rc_ref, ra_ref, rb_ref,
                  q_ref, k_ref, v_ref, k32_ref, v32_ref):
    h = _mod_norm(x_ref[...], ng_ref[...], mod_ref[:, 0:D_MODEL], mod_ref[:, D_MODEL:2 * D_MODEL])
    qkv = jnp.dot(h.astype(BF16), wqkv_ref[...], preferred_element_type=F32)
    rc = rc_ref[...]
    ra = ra_ref[...]
    rb = rb_ref[...]
    lanes = rc.shape[1]
    half = ROT_DIM // 2

    def rope(t):
        return t * rc + pltpu.roll(t, half, 1) * ra + pltpu.roll(t, lanes - half, 1) * rb

    for j in range(D_MODEL // lanes):
        cols = slice(j * lanes, (j + 1) * lanes)
        q = rope(qkv[:, j * lanes:(j + 1) * lanes])
        k = rope(qkv[:, D_MODEL + j * lanes:D_MODEL + (j + 1) * lanes])
        q_ref[:, cols] = (q * (HEAD_DIM ** -0.5)).astype(BF16)
        k_ref[:, cols] = k.astype(BF16)
        k32_ref[:, cols] = k
    v = qkv[:, 2 * D_MODEL:]
    v_ref[...] = v.astype(BF16)
    v32_ref[...] = v


def _odd_pre_call(x, mod, ng, wqkv, rc, ra, rb, keep):
    rows = x.shape[0]
    tm = min(rows, ROW_TILE)
    nt = rows // tm
    first_kept = (rows - keep) // tm
    row_spec = lambda w: pl.BlockSpec((tm, w), lambda i: (i, 0))
    keep_spec = pl.BlockSpec((tm, D_MODEL), lambda i: (jnp.maximum(i - first_kept, 0), 0))
    return pl.pallas_call(
        _odd_pre_body,
        grid=(nt,),
        in_specs=[row_spec(D_MODEL), _const_spec(mod.shape), _const_spec(ng.shape), _weight_spec(wqkv.shape),
                  row_spec(rc.shape[1]), row_spec(rc.shape[1]), row_spec(rc.shape[1])],
        out_specs=[row_spec(D_MODEL), row_spec(D_MODEL), row_spec(D_MODEL), keep_spec, keep_spec],
        out_shape=[jax.ShapeDtypeStruct((rows, D_MODEL), BF16)] * 3
        + [jax.ShapeDtypeStruct((keep, D_MODEL), F32)] * 2,
        compiler_params=_cparams("arbitrary"),
        name="odd_pre",
    )(x, mod, ng, wqkv, rc, ra, rb)


def _attn_prompt_body(first, last, *refs):
    refs = list(refs)
    q_ref, kp_ref, kc_ref, vp_ref, vc_ref = refs[:5]
    pos = 5
    if not first:
        oin_ref, sin_ref = refs[pos:pos + 2]
        pos += 2
    o_ref = refs[pos]
    pos += 1
    if not last:
        sout_ref = refs[pos]

    blk = pl.program_id(1)
    qi = lax.broadcasted_iota(jnp.int32, (BAND, 2 * BAND), 0)
    kj = lax.broadcasted_iota(jnp.int32, (BAND, 2 * BAND), 1)
    lo = jnp.maximum(qi, jnp.where(blk == 0, BAND, 0))
    bias = jnp.where((kj >= lo) & (kj <= qi + BAND), 0.0, NEG_INF).astype(F32)
    lane = lax.broadcasted_iota(jnp.int32, (BAND, 2 * HEAD_DIM), 1)
    low_half = lane < HEAD_DIM
    lane_row = lax.broadcasted_iota(jnp.int32, (1, 2 * HEAD_DIM), 1)
    head_keep = [jnp.where(lane_row < HEAD_DIM, 1.0, 0.0).astype(BF16),
                 jnp.where(lane_row < HEAD_DIM, 0.0, 1.0).astype(BF16)]
    if not first:
        st_in = sin_ref[...]
    if not last:
        st_out = jnp.zeros((BAND, 128), F32)
        st_lane = lax.broadcasted_iota(jnp.int32, (BAND, 128), 1)

    for pair in range(N_HEADS // 2):
        cols = slice(pair * 2 * HEAD_DIM, (pair + 1) * 2 * HEAD_DIM)
        qp = q_ref[:, cols]
        kk = jnp.concatenate([kp_ref[:, cols], kc_ref[:, cols]], axis=0)
        vv = jnp.concatenate([vp_ref[:, cols], vc_ref[:, cols]], axis=0)
        halves = []
        for sub in range(2):
            head = 2 * pair + sub
            qm = qp * head_keep[sub]
            s = lax.dot_general(qm, kk, (((1,), (1,)), ((), ())), preferred_element_type=F32) + bias
            m_b = jnp.max(s, axis=-1, keepdims=True)
            p = jnp.exp(s - m_b)
            l_b = jnp.sum(p, axis=-1, keepdims=True)
            pv = jnp.dot(p.astype(BF16), vv, preferred_element_type=F32)
            if first:
                m_new, l_new = m_b, l_b
                o_h = pv / l_b
            else:
                m_run = st_in[:, head:head + 1]
                l_run = st_in[:, N_HEADS + head:N_HEADS + head + 1]
                m_new = jnp.maximum(m_run, m_b)
                w_run = l_run * jnp.exp(m_run - m_new)
                w_b = jnp.exp(m_b - m_new)
                l_new = w_run + w_b * l_b
                o_h = (w_run * oin_ref[:, cols].astype(F32) + w_b * pv) / l_new
            halves.append(o_h)
            if not last:
                st_out = jnp.where(st_lane == head, m_new, st_out)
                st_out = jnp.where(st_lane == N_HEADS + head, l_new, st_out)
        o_ref[:, cols] = jnp.where(low_half, halves[0], halves[1]).astype(o_ref.dtype)
    if not last:
        sout_ref[...] = st_out


def _attn_prompt_call(d, first, last, q, k, v, o_in=None, st_in=None):
    rows = q.shape[0]
    nb = rows // (BAND * d)
    view = lambda a, w: a.reshape(rows // d, d * w)
    cur = lambda w: pl.BlockSpec((BAND, w), lambda r, b: (b, r))
    prev = pl.BlockSpec((BAND, D_MODEL), lambda r, b: (jnp.maximum(b - 1, 0), r))
    args = [view(q, D_MODEL), view(k, D_MODEL), view(k, D_MODEL), view(v, D_MODEL), view(v, D_MODEL)]
    in_specs = [cur(D_MODEL), prev, cur(D_MODEL), prev, cur(D_MODEL)]
    if not first:
        args += [view(o_in, D_MODEL), view(st_in, 128)]
        in_specs += [cur(D_MODEL), cur(128)]
    out_shape = [jax.ShapeDtypeStruct((rows // d, d * D_MODEL), BF16)]
    out_specs = [cur(D_MODEL)]
    if not last:
        out_shape.append(jax.ShapeDtypeStruct((rows // d, d * 128), F32))
        out_specs.append(cur(128))
    outs = pl.pallas_call(
        functools.partial(_attn_prompt_body, first, last),
        grid=(d, nb),
        in_specs=in_specs,
        out_specs=out_specs,
        out_shape=out_shape,
        compiler_params=_cparams("arbitrary", "arbitrary"),
        name="attn_prompt_d%d" % d,
    )(*args)
    o = outs[0].reshape(rows, D_MODEL)
    st = None if last else outs[1].reshape(rows, 128)
    return o, st


def _attn_sample_body(q_ref, kn_ref, vn_ref, sel_ref, selt_ref, *refs):
    caches = refs[:6]
    o_ref = refs[6]
    b = pl.program_id(0)
    q = q_ref[pl.ds(b, 1), :]
    k_new = kn_ref[pl.ds(b, 1), :]
    v_new = vn_ref[pl.ds(b, 1), :]
    sel = sel_ref[...]
    selt = selt_ref[...]
    rows8 = lambda t: jnp.broadcast_to(t, (8, t.shape[1]))
    s_new = jnp.dot(rows8(k_new * q).astype(BF16), sel, preferred_element_type=F32)[0:1]
    ms, ls, pvs = [], [], []
    for g in range(3):
        kc = caches[2 * g][...]
        vc = caches[2 * g + 1][...]
        s = jnp.dot((kc * q).astype(BF16), sel, preferred_element_type=F32)
        m = jnp.maximum(jnp.max(s, axis=0, keepdims=True), s_new)
        p = jnp.exp(s - m)
        p_new = jnp.exp(s_new - m)
        ls.append(jnp.sum(p, axis=0, keepdims=True) + p_new)
        pe = jnp.dot(p.astype(BF16), selt, preferred_element_type=F32)
        pne = jnp.dot(rows8(p_new).astype(BF16), selt, preferred_element_type=F32)[0:1]
        pvs.append(jnp.sum(pe * vc, axis=0, keepdims=True) + pne * v_new)
        ms.append(m)
    m_all = jnp.maximum(jnp.maximum(ms[0], ms[1]), ms[2])
    ws = [jnp.exp(m - m_all) for m in ms]
    tot = ws[0] * ls[0] + ws[1] * ls[1] + ws[2] * ls[2]
    out = jnp.zeros_like(q)
    for g in range(3):
        coef = rows8(ws[g] / tot)
        hi = coef.astype(BF16)
        lo = (coef - hi.astype(F32)).astype(BF16)
        ce = jnp.dot(hi, selt, preferred_element_type=F32) + jnp.dot(lo, selt, preferred_element_type=F32)
        out = out + ce[0:1] * pvs[g]
    o_ref[pl.ds(b, 1), :] = out


def _attn_sample_call(q, k_new, v_new, ck, cv, sel, selt):
    bsz, past = ck.shape[0], ck.shape[1]
    args = [q, k_new, v_new, sel, selt]
    in_specs = [_const_spec(a.shape) for a in args]
    for window, d in DIL_BRANCHES:
        blk = (past - window) // (d * BAND)
        spec = pl.BlockSpec((None, BAND, D_MODEL), lambda b, blk=blk: (b, blk, 0))
        for cache in (ck, cv):
            args.append(cache.reshape(bsz, past // d, d * D_MODEL))
            in_specs.append(spec)
    return pl.pallas_call(
        _attn_sample_body,
        grid=(bsz,),
        in_specs=in_specs,
        out_specs=_const_spec((bsz, D_MODEL)),
        out_shape=jax.ShapeDtypeStruct((bsz, D_MODEL), F32),
        compiler_params=_cparams("arbitrary"),
        name="attn_sample",
    )(*args)


def _odd_post_body(x_ref, att_ref, mod_ref, wo_ref, o_ref):
    mix = jnp.dot(att_ref[...].astype(BF16), wo_ref[...], preferred_element_type=F32)
    o_ref[...] = x_ref[...] + mod_ref[:, 2 * D_MODEL:3 * D_MODEL] * mix


def _odd_post_call(x, att, mod, wo):
    rows = x.shape[0]
    tm = min(rows, ROW_TILE)
    row_spec = pl.BlockSpec((tm, D_MODEL), lambda i: (i, 0))
    return pl.pallas_call(
        _odd_post_body,
        grid=(rows // tm,),
        in_specs=[row_spec, row_spec, _const_spec(mod.shape), _weight_spec(wo.shape)],
        out_specs=row_spec,
        out_shape=jax.ShapeDtypeStruct((rows, D_MODEL), F32),
        compiler_params=_cparams("arbitrary"),
        name="odd_post",
    )(x, att, mod, wo)


def _s5_tables(lam_re, lam_im, log_dt, b_re, b_im, c_re, c_im):
    hp = lax.Precision.HIGHEST
    dt = jnp.exp(log_dt)[:, None]
    lr, li = lam_re, lam_im
    ks = jnp.arange(S5_T + 1, dtype=F32)[:, None, None]
    mag = jnp.exp(ks * (lr * dt))
    pw_r = mag * jnp.cos(ks * (li * dt))
    pw_i = mag * jnp.sin(ks * (li * dt))
    ar, ai = pw_r[1], pw_i[1]
    den = lr * lr + li * li
    fr = ((ar - 1.0) * lr + ai * li) / den
    fi = (ai * lr - (ar - 1.0) * li) / den
    bbr = fr[..., None] * b_re - fi[..., None] * b_im
    bbi = fr[..., None] * b_im + fi[..., None] * b_re
    ca_r = c_re[None] * pw_r[:, :, None, :] - c_im[None] * pw_i[:, :, None, :]
    ca_i = c_re[None] * pw_i[:, :, None, :] + c_im[None] * pw_r[:, :, None, :]
    kern = (jnp.einsum('kgpn,gnq->kgpq', ca_r[:S5_T], bbr, precision=hp)
            - jnp.einsum('kgpn,gnq->kgpq', ca_i[:S5_T], bbi, precision=hp))
    ti = jnp.arange(S5_T)[:, None]
    to = jnp.arange(S5_T)[None, :]
    lag = to - ti
    toe = jnp.where((lag >= 0)[:, :, None, None, None], kern[jnp.clip(lag, 0, S5_T - 1)], 0.0)
    width = S5_T * S5_GROUP
    mt = toe.transpose(2, 0, 4, 1, 3).reshape(S5_GROUPS, width, width)
    rev_r = pw_r[S5_T - 1::-1][:S5_T]
    rev_i = pw_i[S5_T - 1::-1][:S5_T]
    pb_r = rev_r[..., None] * bbr[None] - rev_i[..., None] * bbi[None]
    pb_i = rev_r[..., None] * bbi[None] + rev_i[..., None] * bbr[None]
    pb_r = pb_r.transpose(1, 0, 3, 2).reshape(S5_GROUPS, width, S5_STATE)
    pb_i = pb_i.transpose(1, 0, 3, 2).reshape(S5_GROUPS, width, S5_STATE)
    pc_r = ca_r[1:].transpose(1, 3, 0, 2).reshape(S5_GROUPS, S5_STATE, width)
    pc_i = (-ca_i[1:]).transpose(1, 3, 0, 2).reshape(S5_GROUPS, S5_STATE, width)
    a16 = jnp.stack([pw_r[S5_T], pw_i[S5_T]], axis=1)
    sr, si = pw_r[S5_T], pw_i[S5_T]
    for _ in range(int(math.log2(S5_CL))):
        sr, si = sr * sr - si * si, 2.0 * sr * si
    aseg = jnp.stack([sr, si], axis=1)
    eye = jnp.eye(S5_GROUPS, dtype=F32)
    n_all = S5_GROUPS * S5_STATE
    bd_r = jnp.einsum('gnp,gh->gphn', bbr, eye).reshape(S5_WIDTH, n_all)
    bd_i = jnp.einsum('gnp,gh->gphn', bbi, eye).reshape(S5_WIDTH, n_all)
    cd_r = jnp.einsum('gpn,gh->gnhp', c_re, eye).reshape(n_all, S5_WIDTH)
    cd_i = jnp.einsum('gpn,gh->gnhp', c_im, eye).reshape(n_all, S5_WIDTH)
    return dict(mt=mt.astype(BF16), pb_r=pb_r.astype(BF16), pb_i=pb_i.astype(BF16),
                pc_r=pc_r.astype(BF16), pc_i=pc_i.astype(BF16), a16=a16, aseg=aseg,
                a_r=ar.reshape(1, n_all), a_i=ai.reshape(1, n_all),
                bd_r=bd_r.astype(BF16), bd_i=bd_i.astype(BF16),
                cd_r=cd_r.astype(BF16), cd_i=cd_i.astype(BF16))


def _rope_tables(pos):
    half = ROT_DIM // 2
    inv = jnp.power(ROPE_THETA, -jnp.arange(half, dtype=F32) * 2.0 / ROT_DIM)
    ang = pos.astype(F32)[:, None] * inv[None, :]
    cos, sin = jnp.cos(ang), jnp.sin(ang)
    rows = pos.shape[0]
    ones = jnp.ones((rows, HEAD_DIM - ROT_DIM), F32)
    zeros = jnp.zeros((rows, HEAD_DIM - ROT_DIM), F32)
    zh = jnp.zeros((rows, half), F32)
    rc = jnp.concatenate([cos, cos, ones], axis=1)
    ra = jnp.concatenate([zh, sin, zeros], axis=1)
    rb = jnp.concatenate([-sin, zh, zeros], axis=1)
    tile2 = lambda t: jnp.concatenate([t, t], axis=1)
    return tile2(rc), tile2(ra), tile2(rb)


def _trunk(sample, x, mods, state, w):
    outs = {}
    mod = mods[0]
    if sample:
        xa, bout, vn = _even_pre_call(True, x, mod, w['ng'][0][0], w['ev_w_in'], w['sg_ln_g'], w['sg_ln_b'],
                                      w['sg_wt'], w['sg_row0'])
        outs['vn'] = vn
        t = w['s5']
        yc, hr, hi = _s5_sample_call(xa, state['s5_re'], state['s5_im'], t['a_r'], t['a_i'],
                                     t['bd_r'], t['bd_i'], t['cd_r'], t['cd_i'])
        outs['s5_re'], outs['s5_im'] = hr, hi
    else:
        xa, bout = _even_pre_call(False, x, mod, w['ng'][0][0], w['ev_w_in'], w['sg_ln_g'], w['sg_ln_b'],
                                  w['sg_wt'], w['sg_bias'])
        t = w['s5']
        xt = (xa.reshape(S5_SEG, S5_CL, S5_T, S5_GROUPS, S5_GROUP).transpose(3, 1, 0, 2, 4)
              .reshape(S5_GROUPS, S5_CL * S5_SEG, S5_T * S5_GROUP).astype(BF16))
        yt, hfin = _s5_prompt_call(xt, t['mt'], t['pb_r'], t['pb_i'], t['pc_r'], t['pc_i'], t['a16'], t['aseg'])
        yc = (yt.reshape(S5_GROUPS, S5_CL, S5_SEG, S5_T, S5_GROUP).transpose(2, 1, 3, 0, 4)
              .reshape(SEQ, S5_WIDTH))
        outs['s5_re'], outs['s5_im'] = hfin[:, 0], hfin[:, 1]
    x = _even_post_call(x, yc, xa, bout, mod, w['s5_d'], w['s5_w_glu'], w['s5_b_glu'], w['ev_w_out'])
    prev = (state['conv'][0][:, 0], state['conv'][0][:, 1]) if sample else None
    x, conv0 = _ffn_call(sample, x, mod, w['ng'][0][1], w['ffn_w_up'][0], w['ffn_conv_w'][0],
                         w['ffn_conv_b'][0], w['ffn_w_down'][0], prev=prev)
    mod = mods[1]
    rows = x.shape[0]
    keep = rows if sample else min(WIN_MAX, rows)
    q, k, v, k32, v32 = _odd_pre_call(x, mod, w['ng'][1][0], w['od_w_qkv'], *w['rope'], keep)
    outs['k'], outs['v'] = k32, v32
    if sample:
        att = _attn_sample_call(q.astype(F32), k32, v32, state['ck'], state['cv'], w['sel'], w['selt'])
    else:
        att, st = _attn_prompt_call(1, True, False, q, k, v)
        att, st = _attn_prompt_call(4, False, False, q, k, v, att, st)
        att, _ = _attn_prompt_call(16, False, True, q, k, v, att, st)
    x = _odd_post_call(x, att, mod, w['od_w_o'])
    prev = (state['conv'][1][:, 0], state['conv'][1][:, 1]) if sample else None
    y, conv1 = _ffn_call(sample, x, mod, w['ng'][1][1], w['ffn_w_up'][1], w['ffn_conv_w'][1],
                         w['ffn_conv_b'][1], w['ffn_w_down'][1], prev=prev, final_g=w['final_g'])
    outs['y'] = y
    outs['conv'] = (conv0, conv1)
    return outs


def kernel(x_prompt, x_sample, c_prompt, c_sample, state_s5_re, state_s5_im, cache_c_k, cache_c_v,
           state_ffn_conv, ada_w, ada_b, norm_g, final_g, ev_w_in, ev_w_out, s5_lam_re, s5_lam_im,
           s5_log_dt, s5_b_re, s5_b_im, s5_c_re, s5_c_im, s5_d, s5_w_glu, s5_b_glu, sg_ln_g, sg_ln_b,
           sg_w, sg_b, od_w_qkv, od_w_o, ffn_w_up, ffn_conv_w, ffn_conv_b, ffn_w_down):
    bp, seq, _ = x_prompt.shape
    bs = x_sample.shape[0]
    assert bp == 1 and seq == SEQ and bs == DEC_BATCH and x_sample.shape[1] == 1

    c_all = jnp.concatenate([c_sample, c_prompt, jnp.zeros((MOD_ROWS - bs - bp, D_MODEL), F32)], axis=0)
    mod_all = _ada_call(c_all, ada_w, ada_b)
    mods_s = [mod_all[l, :bs] for l in range(2)]
    mods_p = [mod_all[l, bs:bs + 1] for l in range(2)]

    hd = SGU_WIDTH // SGU_HEADS
    causal = jnp.tril(jnp.ones((CHUNK, CHUNK), F32))
    heads = jnp.arange(D_MODEL) // HEAD_DIM
    sel = (heads[:, None] == jnp.arange(128)[None, :]).astype(BF16)
    w = dict(
        ng=[[norm_g[l, j].reshape(1, D_MODEL) for j in range(2)] for l in range(2)],
        final_g=final_g.reshape(1, D_MODEL),
        ev_w_in=ev_w_in[0].astype(BF16), ev_w_out=ev_w_out[0].astype(BF16),
        sg_ln_g=sg_ln_g[0].reshape(1, SGU_WIDTH), sg_ln_b=sg_ln_b[0].reshape(1, SGU_WIDTH),
        sg_wt=(sg_w[0] * causal[None]).astype(BF16),
        sg_bias=jnp.repeat(sg_b[0].T, hd, axis=1),
        sg_row0=jnp.stack([jnp.repeat(sg_w[0, :, 0, 0], hd), jnp.repeat(sg_b[0, :, 0], hd)], axis=0),
        s5=_s5_tables(s5_lam_re[0], s5_lam_im[0], s5_log_dt[0], s5_b_re[0], s5_b_im[0], s5_c_re[0], s5_c_im[0]),
        s5_d=s5_d[0].reshape(1, S5_WIDTH), s5_w_glu=s5_w_glu[0].astype(BF16),
        s5_b_glu=s5_b_glu[0].reshape(1, S5_WIDTH),
        od_w_qkv=od_w_qkv[0].astype(BF16), od_w_o=od_w_o[0].astype(BF16),
        ffn_w_up=ffn_w_up.astype(BF16), ffn_w_down=ffn_w_down.astype(BF16),
        ffn_conv_w=ffn_conv_w, ffn_conv_b=ffn_conv_b.reshape(2, 1, D_FF),
        sel=sel, selt=sel.T,
    )

    wp = dict(w, rope=_rope_tables(jnp.arange(seq, dtype=jnp.int32)))
    ws = dict(w, rope=_rope_tables(jnp.full((bs,), PAST_LEN, jnp.int32)))

    p = _trunk(False, x_prompt[0], mods_p, None, wp)
    n_all = S5_GROUPS * S5_STATE
    state = dict(s5_re=state_s5_re[0].reshape(bs, n_all), s5_im=state_s5_im[0].reshape(bs, n_all),
                 ck=cache_c_k[0].reshape(bs, -1, D_MODEL), cv=cache_c_v[0].reshape(bs, -1, D_MODEL),
                 conv=state_ffn_conv)
    s = _trunk(True, x_sample[:, 0], mods_s, state, ws)

    keep = min(WIN_MAX, seq)
    kv_p = lambda a: a.reshape(1, 1, keep, N_HEADS, HEAD_DIM)
    kv_s = lambda a: a.reshape(1, bs, 1, N_HEADS, HEAD_DIM)
    s5_p = lambda a: a.reshape(1, 1, S5_GROUPS, S5_STATE)
    s5_s = lambda a: a.reshape(1, bs, S5_GROUPS, S5_STATE)
    conv_p = jnp.stack([c.reshape(1, 2, D_FF) for c in p['conv']])
    conv_s = jnp.stack([jnp.stack([state_ffn_conv[l][:, 1], s['conv'][l]], axis=1) for l in range(2)])
    return (p['y'][None], s['y'][:, None], s5_p(p['s5_re']), s5_p(p['s5_im']),
            s5_s(s['s5_re']), s5_s(s['s5_im']), s['vn'].reshape(1, bs, 1, SGU_WIDTH),
            kv_p(p['k']), kv_p(p['v']), kv_s(s['k']), kv_s(s['v']), conv_p, conv_s)
```

```python
import functools
import math

import jax
import jax.numpy as jnp
from jax import lax
from jax.experimental import pallas as pl
from jax.experimental.pallas import tpu as pltpu

F32 = jnp.float32
BF16 = jnp.bfloat16

D_MODEL = 1024
SEQ = 16384
DEC_BATCH = 32
PAST_LEN = 16384
S5_WIDTH = 512
S5_GROUP = 16
S5_GROUPS = 32
S5_STATE = 64
SGU_WIDTH = 512
SGU_HEADS = 4
CHUNK = 128
EVEN_IN = S5_WIDTH + 2 * SGU_WIDTH
HEAD_DIM = 64
N_HEADS = 16
ROT_DIM = 16
ROPE_THETA = 500000.0
DIL_BRANCHES = ((128, 1), (512, 4), (2048, 16))
BAND = 128
WIN_MAX = 2048
D_FF = 2816
EPS = 1e-6
NEG_INF = -1e30

ROW_TILE = 512
MOD_ROWS = 40
S5_T = 16
S5_SEG = 8
S5_CL = SEQ // (S5_T * S5_SEG)
FF_CHUNKS = ((0, 1024), (1024, 1024), (2048, 768))
VMEM_LIMIT = 56 * 1024 * 1024


def _cparams(*sem):
    return pltpu.CompilerParams(dimension_semantics=sem, vmem_limit_bytes=VMEM_LIMIT)


def _const_spec(shape):
    nd = len(shape)
    return pl.BlockSpec(shape, lambda *_: (0,) * nd)


def _weight_spec(shape):
    nd = len(shape)
    return pl.BlockSpec(shape, lambda *_: (0,) * nd, pipeline_mode=pl.Buffered(1))


def _gelu(x):
    return jax.nn.gelu(x)


def _mod_norm(x, ng, shift, scale):
    ms = jnp.mean(x * x, axis=-1, keepdims=True)
    return (x * lax.rsqrt(ms + EPS) * ng) * (1.0 + scale) + shift


def _ada_body(c_ref, w_ref, b_ref, o_ref):
    c = c_ref[...]
    s = c * jax.nn.sigmoid(c)
    o_ref[...] = jnp.dot(s.astype(BF16), w_ref[...].astype(BF16),
                         preferred_element_type=F32) + b_ref[...]


def _ada_call(c_all, ada_w, ada_b):
    depth = ada_w.shape[0]
    nt = 1536
    return pl.pallas_call(
        _ada_body,
        grid=(depth, 6 * D_MODEL // nt),
        in_specs=[
            pl.BlockSpec((MOD_ROWS, D_MODEL), lambda l, j: (0, 0)),
            pl.BlockSpec((None, D_MODEL, nt), lambda l, j: (l, 0, j)),
            pl.BlockSpec((None, 1, nt), lambda l, j: (l, 0, j)),
        ],
        out_specs=pl.BlockSpec((None, MOD_ROWS, nt), lambda l, j: (l, 0, j)),
        out_shape=jax.ShapeDtypeStruct((depth, MOD_ROWS, 6 * D_MODEL), F32),
        compiler_params=_cparams("arbitrary", "arbitrary"),
        name="ada_mod",
    )(c_all, ada_w, ada_b.reshape(depth, 1, 6 * D_MODEL))


def _even_pre_body(sample, tm, x_ref, mod_ref, ng_ref, win_ref, lng_ref, lnb_ref, wt_ref, bs_ref,
                   xa_ref, bout_ref, *vn_out):
    h = _mod_norm(x_ref[...], ng_ref[...], mod_ref[:, 0:D_MODEL], mod_ref[:, D_MODEL:2 * D_MODEL])
    proj = jnp.dot(h.astype(BF16), win_ref[...], preferred_element_type=F32)
    xa_ref[...] = proj[:, :S5_WIDTH]
    u = _gelu(proj[:, S5_WIDTH:S5_WIDTH + SGU_WIDTH])
    v = _gelu(proj[:, S5_WIDTH + SGU_WIDTH:])
    mu = jnp.mean(v, axis=-1, keepdims=True)
    var = jnp.mean(jnp.square(v - mu), axis=-1, keepdims=True)
    vn = (v - mu) * lax.rsqrt(var + EPS) * lng_ref[...] + lnb_ref[...]
    if sample:
        vn_out[0][...] = vn
        bout_ref[...] = (u * (vn * bs_ref[0:1, :] + bs_ref[1:2, :])).astype(BF16)
    else:
        vnb = vn.astype(BF16)
        hd = SGU_WIDTH // SGU_HEADS
        for ci in range(tm // CHUNK):
            rows = slice(ci * CHUNK, (ci + 1) * CHUNK)
            for hh in range(SGU_HEADS):
                cols = slice(hh * hd, (hh + 1) * hd)
                s = jnp.dot(wt_ref[hh], vnb[rows, cols], preferred_element_type=F32) + bs_ref[:, cols]
                bout_ref[rows, cols] = (u[rows, cols] * s).astype(BF16)


def _even_pre_call(sample, x, mod, ng, win, lng, lnb, wt, bs):
    rows = x.shape[0]
    tm = rows if sample else ROW_TILE
    row_spec = lambda w: pl.BlockSpec((tm, w), lambda i: (i, 0))
    out_shape = [jax.ShapeDtypeStruct((rows, S5_WIDTH), F32),
                 jax.ShapeDtypeStruct((rows, SGU_WIDTH), BF16)]
    out_specs = [row_spec(S5_WIDTH), row_spec(SGU_WIDTH)]
    if sample:
        out_shape.append(jax.ShapeDtypeStruct((rows, SGU_WIDTH), F32))
        out_specs.append(row_spec(SGU_WIDTH))
    return pl.pallas_call(
        functools.partial(_even_pre_body, sample, tm),
        grid=(rows // tm,),
        in_specs=[row_spec(D_MODEL), _const_spec(mod.shape), _const_spec(ng.shape), _weight_spec(win.shape),
                  _const_spec(lng.shape), _const_spec(lnb.shape), _const_spec(wt.shape), _const_spec(bs.shape)],
        out_specs=out_specs,
        out_shape=out_shape,
        compiler_params=_cparams("arbitrary"),
        name="even_pre_sample" if sample else "even_pre_prompt",
    )(x, mod, ng, win, lng, lnb, wt, bs)


def _s5_prompt_body(x_ref, mt_ref, pbr_ref, pbi_ref, pcr_ref, pci_ref, a16_ref, aseg_ref,
                    y_ref, hfin_ref, s_re, s_im, hp_re, hp_im, pw_re, pw_im):
    x = x_ref[...]
    tile = (S5_CL, S5_SEG, S5_STATE)
    s_re[...] = jnp.dot(x, pbr_ref[...], preferred_element_type=F32).reshape(tile)
    s_im[...] = jnp.dot(x, pbi_ref[...], preferred_element_type=F32).reshape(tile)
    a_r = jnp.broadcast_to(a16_ref[0:1, :], (S5_SEG, S5_STATE))
    a_i = jnp.broadcast_to(a16_ref[1:2, :], (S5_SEG, S5_STATE))

    def step(cl, carry):
        hr, hi, pr, pi = carry
        hp_re[cl] = hr
        hp_im[cl] = hi
        pw_re[cl] = pr
        pw_im[cl] = pi
        nhr = a_r * hr - a_i * hi + s_re[cl]
        nhi = a_r * hi + a_i * hr + s_im[cl]
        return nhr, nhi, a_r * pr - a_i * pi, a_r * pi + a_i * pr

    zero = jnp.zeros((S5_SEG, S5_STATE), F32)
    hr, hi, _, _ = lax.fori_loop(0, S5_CL, step, (zero, zero, zero + 1.0, zero))

    g_r = aseg_ref[0:1, :]
    g_i = aseg_ref[1:2, :]
    er = jnp.zeros((1, S5_STATE), F32)
    ei = jnp.zeros((1, S5_STATE), F32)
    ent_r, ent_i = [], []
    for s in range(S5_SEG):
        ent_r.append(er)
        ent_i.append(ei)
        er, ei = (g_r * er - g_i * ei + hr[s:s + 1, :], g_r * ei + g_i * er + hi[s:s + 1, :])
    hfin_ref[0:1, :] = er
    hfin_ref[1:2, :] = ei
    e_r = jnp.concatenate(ent_r, axis=0)[None]
    e_i = jnp.concatenate(ent_i, axis=0)[None]

    pr = pw_re[...]
    pi = pw_im[...]
    rows = S5_CL * S5_SEG
    h_r = (hp_re[...] + pr * e_r - pi * e_i).reshape(rows, S5_STATE).astype(BF16)
    h_i = (hp_im[...] + pr * e_i + pi * e_r).reshape(rows, S5_STATE).astype(BF16)
    y_ref[...] = (jnp.dot(x, mt_ref[...], preferred_element_type=F32)
                  + jnp.dot(h_r, pcr_ref[...], preferred_element_type=F32)
                  + jnp.dot(h_i, pci_ref[...], preferred_element_type=F32))


def _s5_prompt_call(xt, mt, pbr, pbi, pcr, pci, a16, aseg):
    g, rows, width = xt.shape
    grp = lambda a: pl.BlockSpec((None,) + a.shape[1:], lambda i: (i,) + (0,) * (a.ndim - 1))
    scratch = [pltpu.VMEM((S5_CL, S5_SEG, S5_STATE), F32) for _ in range(6)]
    return pl.pallas_call(
        _s5_prompt_body,
        grid=(g,),
        in_specs=[grp(xt), grp(mt), grp(pbr), grp(pbi), grp(pcr), grp(pci), grp(a16), grp(aseg)],
        out_specs=[pl.BlockSpec((None, rows, width), lambda i: (i, 0, 0)),
                   pl.BlockSpec((None, 2, S5_STATE), lambda i: (i, 0, 0))],
        out_shape=[jax.ShapeDtypeStruct((g, rows, width), F32),
                   jax.ShapeDtypeStruct((g, 2, S5_STATE), F32)],
        scratch_shapes=scratch,
        compiler_params=_cparams("arbitrary"),
        name="s5_prompt",
    )(xt, mt, pbr, pbi, pcr, pci, a16, aseg)


def _s5_sample_body(xa_ref, h0r_ref, h0i_ref, ar_ref, ai_ref, bdr_ref, bdi_ref, cdr_ref, cdi_ref,
                    yc_ref, hr_ref, hi_ref):
    u = xa_ref[...].astype(BF16)
    ar = ar_ref[...]
    ai = ai_ref[...]
    h0r = h0r_ref[...]
    h0i = h0i_ref[...]
    hr = ar * h0r - ai * h0i + jnp.dot(u, bdr_ref[...], preferred_element_type=F32)
    hi = ar * h0i + ai * h0r + jnp.dot(u, bdi_ref[...], preferred_element_type=F32)
    hr_ref[...] = hr
    hi_ref[...] = hi
    yc_ref[...] = (jnp.dot(hr.astype(BF16), cdr_ref[...], preferred_element_type=F32)
                   - jnp.dot(hi.astype(BF16), cdi_ref[...], preferred_element_type=F32))


def _s5_sample_call(xa, h0r, h0i, ar, ai, bdr, bdi, cdr, cdi):
    rows = xa.shape[0]
    n = S5_GROUPS * S5_STATE
    args = (xa, h0r, h0i, ar, ai, bdr, bdi, cdr, cdi)
    return pl.pallas_call(
        _s5_sample_body,
        grid=(1,),
        in_specs=[_const_spec(a.shape) for a in args],
        out_specs=[_const_spec((rows, S5_WIDTH)), _const_spec((rows, n)), _const_spec((rows, n))],
        out_shape=[jax.ShapeDtypeStruct((rows, S5_WIDTH), F32),
                   jax.ShapeDtypeStruct((rows, n), F32),
                   jax.ShapeDtypeStruct((rows, n), F32)],
        compiler_params=_cparams("arbitrary"),
        name="s5_sample",
    )(*args)


def _even_post_body(x_ref, yc_ref, xa_ref, bout_ref, mod_ref, d_ref, wglu_ref, bglu_ref, wout_ref, o_ref):
    y = _gelu(yc_ref[...] + d_ref[...] * xa_ref[...])
    gate = jax.nn.sigmoid(jnp.dot(y.astype(BF16), wglu_ref[...], preferred_element_type=F32) + bglu_ref[...])
    a_out = (y * gate).astype(BF16)
    mix = (jnp.dot(a_out, wout_ref[0:S5_WIDTH, :], preferred_element_type=F32)
           + jnp.dot(bout_ref[...], wout_ref[S5_WIDTH:, :], preferred_element_type=F32))
    o_ref[...] = x_ref[...] + mod_ref[:, 2 * D_MODEL:3 * D_MODEL] * mix


def _even_post_call(x, yc, xa, bout, mod, d, wglu, bglu, wout):
    rows = x.shape[0]
    tm = min(rows, ROW_TILE)
    row_spec = lambda w: pl.BlockSpec((tm, w), lambda i: (i, 0))
    return pl.pallas_call(
        _even_post_body,
        grid=(rows // tm,),
        in_specs=[row_spec(D_MODEL), row_spec(S5_WIDTH), row_spec(S5_WIDTH), row_spec(SGU_WIDTH),
                  _const_spec(mod.shape), _const_spec(d.shape), _const_spec(wglu.shape),
                  _const_spec(bglu.shape), _weight_spec(wout.shape)],
        out_specs=row_spec(D_MODEL),
        out_shape=jax.ShapeDtypeStruct((rows, D_MODEL), F32),
        compiler_params=_cparams("arbitrary"),
        name="even_post",
    )(x, yc, xa, bout, mod, d, wglu, bglu, wout)


def _ffn_body(sample, final, tm, *refs):
    refs = list(refs)
    x_ref, mod_ref, ng_ref, wup_ref, cw_ref, cb_ref, wdn_ref = refs[:7]
    pos = 7
    if sample:
        p2_ref, p1_ref = refs[pos:pos + 2]
        pos += 2
    if final:
        fg_ref = refs[pos]
        pos += 1
    o_ref, conv_ref = refs[pos:pos + 2]
    pos += 2
    if not sample:
        carry_ref = refs[pos]

        @pl.when(pl.program_id(0) == 0)
        def _():
            carry_ref[...] = jnp.zeros_like(carry_ref)

    x = x_ref[...]
    h = _mod_norm(x, ng_ref[...], mod_ref[:, 3 * D_MODEL:4 * D_MODEL], mod_ref[:, 4 * D_MODEL:5 * D_MODEL])
    hb = h.astype(BF16)
    acc = jnp.zeros((tm, D_MODEL), F32)
    if not sample:
        row = lax.broadcasted_iota(jnp.int32, (tm, 1), 0)
    for c0, cw in FF_CHUNKS:
        cols = slice(c0, c0 + cw)
        a = jnp.dot(hb, wup_ref[:, cols], preferred_element_type=F32)
        g = jnp.dot(hb, wup_ref[:, D_FF + c0:D_FF + c0 + cw], preferred_element_type=F32)
        if sample:
            am2 = p2_ref[:, cols]
            am1 = p1_ref[:, cols]
            conv_ref[:, cols] = a
        else:
            prev2 = carry_ref[0:1, cols]
            prev1 = carry_ref[1:2, cols]
            am1 = jnp.where(row == 0, prev1, pltpu.roll(a, 1, 0))
            am2 = jnp.where(row == 0, prev2, jnp.where(row == 1, prev1, pltpu.roll(a, 2, 0)))
            carry_ref[0:2, cols] = a[tm - 2:tm, :]
        y = cb_ref[:, cols] + cw_ref[0:1, cols] * am2 + cw_ref[1:2, cols] * am1 + cw_ref[2:3, cols] * a
        act = (_gelu(y) * g).astype(BF16)
        acc = acc + jnp.dot(act, wdn_ref[cols, :], preferred_element_type=F32)
    out = x + mod_ref[:, 5 * D_MODEL:6 * D_MODEL] * acc
    if final:
        ms = jnp.mean(out * out, axis=-1, keepdims=True)
        out = out * lax.rsqrt(ms + EPS) * fg_ref[...]
    o_ref[...] = out
    if not sample:
        conv_ref[...] = carry_ref[0:2, :]


def _ffn_call(sample, x, mod, ng, wup, cw, cb, wdn, prev=None, final_g=None):
    rows = x.shape[0]
    tm = rows if sample else ROW_TILE
    final = final_g is not None
    row_spec = lambda w: pl.BlockSpec((tm, w), lambda i: (i, 0))
    args = [x, mod, ng, wup, cw, cb, wdn]
    in_specs = [row_spec(D_MODEL), _const_spec(mod.shape), _const_spec(ng.shape), _weight_spec(wup.shape),
                _const_spec(cw.shape), _const_spec(cb.shape), _weight_spec(wdn.shape)]
    if sample:
        args += [prev[0], prev[1]]
        in_specs += [_const_spec(prev[0].shape), _const_spec(prev[1].shape)]
    if final:
        args.append(final_g)
        in_specs.append(_const_spec(final_g.shape))
    conv_rows = rows if sample else 2
    return pl.pallas_call(
        functools.partial(_ffn_body, sample, final, tm),
        grid=(rows // tm,),
        in_specs=in_specs,
        out_specs=[row_spec(D_MODEL), _const_spec((conv_rows, D_FF))],
        out_shape=[jax.ShapeDtypeStruct((rows, D_MODEL), F32),
                   jax.ShapeDtypeStruct((conv_rows, D_FF), F32)],
        scratch_shapes=[] if sample else [pltpu.VMEM((8, D_FF), F32)],
        compiler_params=_cparams("arbitrary"),
        name="ffn_sample" if sample else "ffn_prompt",
    )(*args)


def _odd_pre_body(x_ref, mod_ref, ng_ref, wqkv_ref, rc_ref, ra_ref, rb_ref,
                  q_ref, k_ref, v_ref, k32_ref, v32_ref):
    h = _mod_norm(x_ref[...], ng_ref[...], mod_ref[:, 0:D_MODEL], mod_ref[:, D_MODEL:2 * D_MODEL])
    qkv = jnp.dot(h.astype(BF16), wqkv_ref[...], preferred_element_type=F32)
    rc = rc_ref[...]
    ra = ra_ref[...]
    rb = rb_ref[...]
    lanes = rc.shape[1]
    half = ROT_DIM // 2

    def rope(t):
        return t * rc + pltpu.roll(t, half, 1) * ra + pltpu.roll(t, lanes - half, 1) * rb

    for j in range(D_MODEL // lanes):
        cols = slice(j * lanes, (j + 1) * lanes)
        q = rope(qkv[:, j * lanes:(j + 1) * lanes])
        k = rope(qkv[:, D_MODEL + j * lanes:D_MODEL + (j + 1) * lanes])
        q_ref[:, cols] = (q * (HEAD_DIM ** -0.5)).astype(BF16)
        k_ref[:, cols] = k.astype(BF16)
        k32_ref[:, cols] = k
    v = qkv[:, 2 * D_MODEL:]
    v_ref[...] = v.astype(BF16)
    v32_ref[...] = v


def _odd_pre_call(x, mod, ng, wqkv, rc, ra, rb, keep):
    rows = x.shape[0]
    tm = min(rows, ROW_TILE)
    nt = rows // tm
    first_kept = (rows - keep) // tm
    row_spec = lambda w: pl.BlockSpec((tm, w), lambda i: (i, 0))
    keep_spec = pl.BlockSpec((tm, D_MODEL), lambda i: (jnp.maximum(i - first_kept, 0), 0))
    return pl.pallas_call(
        _odd_pre_body,
        grid=(nt,),
        in_specs=[row_spec(D_MODEL), _const_spec(mod.shape), _const_spec(ng.shape), _weight_spec(wqkv.shape),
                  row_spec(rc.shape[1]), row_spec(rc.shape[1]), row_spec(rc.shape[1])],
        out_specs=[row_spec(D_MODEL), row_spec(D_MODEL), row_spec(D_MODEL), keep_spec, keep_spec],
        out_shape=[jax.ShapeDtypeStruct((rows, D_MODEL), BF16)] * 3
        + [jax.ShapeDtypeStruct((keep, D_MODEL), F32)] * 2,
        compiler_params=_cparams("arbitrary"),
        name="odd_pre",
    )(x, mod, ng, wqkv, rc, ra, rb)


def _attn_prompt_body(first, last, *refs):
    refs = list(refs)
    q_ref, kp_ref, kc_ref, vp_ref, vc_ref = refs[:5]
    pos = 5
    if not first:
        oin_ref, sin_ref = refs[pos:pos + 2]
        pos += 2
    o_ref = refs[pos]
    pos += 1
    if not last:
        sout_ref = refs[pos]

    blk = pl.program_id(1)
    qi = lax.broadcasted_iota(jnp.int32, (BAND, 2 * BAND), 0)
    kj = lax.broadcasted_iota(jnp.int32, (BAND, 2 * BAND), 1)
    lo = jnp.maximum(qi, jnp.where(blk == 0, BAND, 0))
    bias = jnp.where((kj >= lo) & (kj <= qi + BAND), 0.0, NEG_INF).astype(F32)
    lane = lax.broadcasted_iota(jnp.int32, (BAND, 2 * HEAD_DIM), 1)
    low_half = lane < HEAD_DIM
    lane_row = lax.broadcasted_iota(jnp.int32, (1, 2 * HEAD_DIM), 1)
    head_keep = [jnp.where(lane_row < HEAD_DIM, 1.0, 0.0).astype(BF16),
                 jnp.where(lane_row < HEAD_DIM, 0.0, 1.0).astype(BF16)]
    if not first:
        st_in = sin_ref[...]
    if not last:
        st_out = jnp.zeros((BAND, 128), F32)
        st_lane = lax.broadcasted_iota(jnp.int32, (BAND, 128), 1)

    for pair in range(N_HEADS // 2):
        cols = slice(pair * 2 * HEAD_DIM, (pair + 1) * 2 * HEAD_DIM)
        qp = q_ref[:, cols]
        kk = jnp.concatenate([kp_ref[:, cols], kc_ref[:, cols]], axis=0)
        vv = jnp.concatenate([vp_ref[:, cols], vc_ref[:, cols]], axis=0)
        halves = []
        for sub in range(2):
            head = 2 * pair + sub
            qm = qp * head_keep[sub]
            s = lax.dot_general(qm, kk, (((1,), (1,)), ((), ())), preferred_element_type=F32) + bias
            m_b = jnp.max(s, axis=-1, keepdims=True)
            p = jnp.exp(s - m_b)
            l_b = jnp.sum(p, axis=-1, keepdims=True)
            pv = jnp.dot(p.astype(BF16), vv, preferred_element_type=F32)
            if first:
                m_new, l_new = m_b, l_b
                o_h = pv / l_b
            else:
                m_run = st_in[:, head:head + 1]
                l_run = st_in[:, N_HEADS + head:N_HEADS + head + 1]
                m_new = jnp.maximum(m_run, m_b)
                w_run = l_run * jnp.exp(m_run - m_new)
                w_b = jnp.exp(m_b - m_new)
                l_new = w_run + w_b * l_b
                o_h = (w_run * oin_ref[:, cols].astype(F32) + w_b * pv) / l_new
            halves.append(o_h)
            if not last:
                st_out = jnp.where(st_lane == head, m_new, st_out)
                st_out = jnp.where(st_lane == N_HEADS + head, l_new, st_out)
        o_ref[:, cols] = jnp.where(low_half, halves[0], halves[1]).astype(o_ref.dtype)
    if not last:
        sout_ref[...] = st_out


def _attn_prompt_call(d, first, last, q, k, v, o_in=None, st_in=None):
    rows = q.shape[0]
    nb = rows // (BAND * d)
    view = lambda a, w: a.reshape(rows // d, d * w)
    cur = lambda w: pl.BlockSpec((BAND, w), lambda r, b: (b, r))
    prev = pl.BlockSpec((BAND, D_MODEL), lambda r, b: (jnp.maximum(b - 1, 0), r))
    args = [view(q, D_MODEL), view(k, D_MODEL), view(k, D_MODEL), view(v, D_MODEL), view(v, D_MODEL)]
    in_specs = [cur(D_MODEL), prev, cur(D_MODEL), prev, cur(D_MODEL)]
    if not first:
        args += [view(o_in, D_MODEL), view(st_in, 128)]
        in_specs += [cur(D_MODEL), cur(128)]
    out_shape = [jax.ShapeDtypeStruct((rows // d, d * D_MODEL), BF16)]
    out_specs = [cur(D_MODEL)]
    if not last:
        out_shape.append(jax.ShapeDtypeStruct((rows // d, d * 128), F32))
        out_specs.append(cur(128))
    outs = pl.pallas_call(
        functools.partial(_attn_prompt_body, first, last),
        grid=(d, nb),
        in_specs=in_specs,
        out_specs=out_specs,
        out_shape=out_shape,
        compiler_params=_cparams("arbitrary", "arbitrary"),
        name="attn_prompt_d%d" % d,
    )(*args)
    o = outs[0].reshape(rows, D_MODEL)
    st = None if last else outs[1].reshape(rows, 128)
    return o, st


def _attn_sample_body(q_ref, kn_ref, vn_ref, *refs):
    caches = refs[:6]
    o_ref = refs[6]
    b = pl.program_id(0)
    q = q_ref[b]
    k_new = kn_ref[b]
    v_new = vn_ref[b]
    s_new = jnp.sum(k_new * q, axis=-1, keepdims=True)[None]
    ms, ls, pvs = [], [], []
    for g in range(3):
        kc = caches[2 * g][...]
        vc = caches[2 * g + 1][...]
        s = jnp.sum(kc * q[None], axis=-1, keepdims=True)
        m = jnp.maximum(jnp.max(s, axis=0, keepdims=True), s_new)
        p = jnp.exp(s - m)
        p_new = jnp.exp(s_new - m)
        ls.append(jnp.sum(p, axis=0, keepdims=True) + p_new)
        pvs.append(jnp.sum(p * vc, axis=0, keepdims=True) + p_new * v_new[None])
        ms.append(m)
    m_all = jnp.maximum(jnp.maximum(ms[0], ms[1]), ms[2])
    ws = [jnp.exp(m - m_all) for m in ms]
    tot = ws[0] * ls[0] + ws[1] * ls[1] + ws[2] * ls[2]
    out = (ws[0] * pvs[0] + ws[1] * pvs[1] + ws[2] * pvs[2]) / tot
    o_ref[b] = out[0]


def _attn_sample_call(q, k_new, v_new, ck, cv):
    bsz, past = ck.shape[0], ck.shape[1]
    args = [q, k_new, v_new]
    in_specs = [_const_spec(a.shape) for a in args]
    for window, d in DIL_BRANCHES:
        blk = (past - window) // (d * BAND)
        spec = pl.BlockSpec((None, BAND, None, N_HEADS, HEAD_DIM), lambda b, blk=blk: (b, blk, 0, 0, 0))
        for cache in (ck, cv):
            args.append(cache.reshape(bsz, past // d, d, N_HEADS, HEAD_DIM))
            in_specs.append(spec)
    return pl.pallas_call(
        _attn_sample_body,
        grid=(bsz,),
        in_specs=in_specs,
        out_specs=_const_spec((bsz, N_HEADS, HEAD_DIM)),
        out_shape=jax.ShapeDtypeStruct((bsz, N_HEADS, HEAD_DIM), F32),
        compiler_params=_cparams("arbitrary"),
        name="attn_sample",
    )(*args)


def _odd_post_body(x_ref, att_ref, mod_ref, wo_ref, o_ref):
    mix = jnp.dot(att_ref[...].astype(BF16), wo_ref[...], preferred_element_type=F32)
    o_ref[...] = x_ref[...] + mod_ref[:, 2 * D_MODEL:3 * D_MODEL] * mix


def _odd_post_call(x, att, mod, wo):
    rows = x.shape[0]
    tm = min(rows, ROW_TILE)
    row_spec = pl.BlockSpec((tm, D_MODEL), lambda i: (i, 0))
    return pl.pallas_call(
        _odd_post_body,
        grid=(rows // tm,),
        in_specs=[row_spec, row_spec, _const_spec(mod.shape), _weight_spec(wo.shape)],
        out_specs=row_spec,
        out_shape=jax.ShapeDtypeStruct((rows, D_MODEL), F32),
        compiler_params=_cparams("arbitrary"),
        name="odd_post",
    )(x, att, mod, wo)


def _s5_tables(lam_re, lam_im, log_dt, b_re, b_im, c_re, c_im):
    hp = lax.Precision.HIGHEST
    dt = jnp.exp(log_dt)[:, None]
    lr, li = lam_re, lam_im
    ks = jnp.arange(S5_T + 1, dtype=F32)[:, None, None]
    mag = jnp.exp(ks * (lr * dt))
    pw_r = mag * jnp.cos(ks * (li * dt))
    pw_i = mag * jnp.sin(ks * (li * dt))
    ar, ai = pw_r[1], pw_i[1]
    den = lr * lr + li * li
    fr = ((ar - 1.0) * lr + ai * li) / den
    fi = (ai * lr - (ar - 1.0) * li) / den
    bbr = fr[..., None] * b_re - fi[..., None] * b_im
    bbi = fr[..., None] * b_im + fi[..., None] * b_re
    ca_r = c_re[None] * pw_r[:, :, None, :] - c_im[None] * pw_i[:, :, None, :]
    ca_i = c_re[None] * pw_i[:, :, None, :] + c_im[None] * pw_r[:, :, None, :]
    kern = (jnp.einsum('kgpn,gnq->kgpq', ca_r[:S5_T], bbr, precision=hp)
            - jnp.einsum('kgpn,gnq->kgpq', ca_i[:S5_T], bbi, precision=hp))
    ti = jnp.arange(S5_T)[:, None]
    to = jnp.arange(S5_T)[None, :]
    lag = to - ti
    toe = jnp.where((lag >= 0)[:, :, None, None, None], kern[jnp.clip(lag, 0, S5_T - 1)], 0.0)
    width = S5_T * S5_GROUP
    mt = toe.transpose(2, 0, 4, 1, 3).reshape(S5_GROUPS, width, width)
    rev_r = pw_r[S5_T - 1::-1][:S5_T]
    rev_i = pw_i[S5_T - 1::-1][:S5_T]
    pb_r = rev_r[..., None] * bbr[None] - rev_i[..., None] * bbi[None]
    pb_i = rev_r[..., None] * bbi[None] + rev_i[..., None] * bbr[None]
    pb_r = pb_r.transpose(1, 0, 3, 2).reshape(S5_GROUPS, width, S5_STATE)
    pb_i = pb_i.transpose(1, 0, 3, 2).reshape(S5_GROUPS, width, S5_STATE)
    pc_r = ca_r[1:].transpose(1, 3, 0, 2).reshape(S5_GROUPS, S5_STATE, width)
    pc_i = (-ca_i[1:]).transpose(1, 3, 0, 2).reshape(S5_GROUPS, S5_STATE, width)
    a16 = jnp.stack([pw_r[S5_T], pw_i[S5_T]], axis=1)
    sr, si = pw_r[S5_T], pw_i[S5_T]
    for _ in range(int(math.log2(S5_CL))):
        sr, si = sr * sr - si * si, 2.0 * sr * si
    aseg = jnp.stack([sr, si], axis=1)
    eye = jnp.eye(S5_GROUPS, dtype=F32)
    n_all = S5_GROUPS * S5_STATE
    bd_r = jnp.einsum('gnp,gh->gphn', bbr, eye).reshape(S5_WIDTH, n_all)
    bd_i = jnp.einsum('gnp,gh->gphn', bbi, eye).reshape(S5_WIDTH, n_all)
    cd_r = jnp.einsum('gpn,gh->gnhp', c_re, eye).reshape(n_all, S5_WIDTH)
    cd_i = jnp.einsum('gpn,gh->gnhp', c_im, eye).reshape(n_all, S5_WIDTH)
    return dict(mt=mt.astype(BF16), pb_r=pb_r.astype(BF16), pb_i=pb_i.astype(BF16),
                pc_r=pc_r.astype(BF16), pc_i=pc_i.astype(BF16), a16=a16, aseg=aseg,
                a_r=ar.reshape(1, n_all), a_i=ai.reshape(1, n_all),
                bd_r=bd_r.astype(BF16), bd_i=bd_i.astype(BF16),
                cd_r=cd_r.astype(BF16), cd_i=cd_i.astype(BF16))


def _rope_tables(pos):
    half = ROT_DIM // 2
    inv = jnp.power(ROPE_THETA, -jnp.arange(half, dtype=F32) * 2.0 / ROT_DIM)
    ang = pos.astype(F32)[:, None] * inv[None, :]
    cos, sin = jnp.cos(ang), jnp.sin(ang)
    rows = pos.shape[0]
    ones = jnp.ones((rows, HEAD_DIM - ROT_DIM), F32)
    zeros = jnp.zeros((rows, HEAD_DIM - ROT_DIM), F32)
    zh = jnp.zeros((rows, half), F32)
    rc = jnp.concatenate([cos, cos, ones], axis=1)
    ra = jnp.concatenate([zh, sin, zeros], axis=1)
    rb = jnp.concatenate([-sin, zh, zeros], axis=1)
    tile2 = lambda t: jnp.concatenate([t, t], axis=1)
    return tile2(rc), tile2(ra), tile2(rb)


def _trunk(sample, x, mods, state, w):
    outs = {}
    mod = mods[0]
    if sample:
        xa, bout, vn = _even_pre_call(True, x, mod, w['ng'][0][0], w['ev_w_in'], w['sg_ln_g'], w['sg_ln_b'],
                                      w['sg_wt'], w['sg_row0'])
        outs['vn'] = vn
        t = w['s5']
        yc, hr, hi = _s5_sample_call(xa, state['s5_re'], state['s5_im'], t['a_r'], t['a_i'],
                                     t['bd_r'], t['bd_i'], t['cd_r'], t['cd_i'])
        outs['s5_re'], outs['s5_im'] = hr, hi
    else:
        xa, bout = _even_pre_call(False, x, mod, w['ng'][0][0], w['ev_w_in'], w['sg_ln_g'], w['sg_ln_b'],
                                  w['sg_wt'], w['sg_bias'])
        t = w['s5']
        xt = (xa.reshape(S5_SEG, S5_CL, S5_T, S5_GROUPS, S5_GROUP).transpose(3, 1, 0, 2, 4)
              .reshape(S5_GROUPS, S5_CL * S5_SEG, S5_T * S5_GROUP).astype(BF16))
        yt, hfin = _s5_prompt_call(xt, t['mt'], t['pb_r'], t['pb_i'], t['pc_r'], t['pc_i'], t['a16'], t['aseg'])
        yc = (yt.reshape(S5_GROUPS, S5_CL, S5_SEG, S5_T, S5_GROUP).transpose(2, 1, 3, 0, 4)
              .reshape(SEQ, S5_WIDTH))
        outs['s5_re'], outs['s5_im'] = hfin[:, 0], hfin[:, 1]
    x = _even_post_call(x, yc, xa, bout, mod, w['s5_d'], w['s5_w_glu'], w['s5_b_glu'], w['ev_w_out'])
    prev = (state['conv'][0][:, 0], state['conv'][0][:, 1]) if sample else None
    x, conv0 = _ffn_call(sample, x, mod, w['ng'][0][1], w['ffn_w_up'][0], w['ffn_conv_w'][0],
                         w['ffn_conv_b'][0], w['ffn_w_down'][0], prev=prev)
    mod = mods[1]
    rows = x.shape[0]
    keep = rows if sample else min(WIN_MAX, rows)
    q, k, v, k32, v32 = _odd_pre_call(x, mod, w['ng'][1][0], w['od_w_qkv'], *w['rope'], keep)
    outs['k'], outs['v'] = k32, v32
    if sample:
        heads = lambda a: a.astype(F32).reshape(rows, N_HEADS, HEAD_DIM)
        att = _attn_sample_call(heads(q), heads(k32), heads(v32), state['ck'], state['cv'])
        att = att.reshape(rows, D_MODEL)
    else:
        att, st = _attn_prompt_call(1, True, False, q, k, v)
        att, st = _attn_prompt_call(4, False, False, q, k, v, att, st)
        att, _ = _attn_prompt_call(16, False, True, q, k, v, att, st)
    x = _odd_post_call(x, att, mod, w['od_w_o'])
    prev = (state['conv'][1][:, 0], state['conv'][1][:, 1]) if sample else None
    y, conv1 = _ffn_call(sample, x, mod, w['ng'][1][1], w['ffn_w_up'][1], w['ffn_conv_w'][1],
                         w['ffn_conv_b'][1], w['ffn_w_down'][1], prev=prev, final_g=w['final_g'])
    outs['y'] = y
    outs['conv'] = (conv0, conv1)
    return outs


def kernel(x_prompt, x_sample, c_prompt, c_sample, state_s5_re, state_s5_im, cache_c_k, cache_c_v,
           state_ffn_conv, ada_w, ada_b, norm_g, final_g, ev_w_in, ev_w_out, s5_lam_re, s5_lam_im,
           s5_log_dt, s5_b_re, s5_b_im, s5_c_re, s5_c_im, s5_d, s5_w_glu, s5_b_glu, sg_ln_g, sg_ln_b,
           sg_w, sg_b, od_w_qkv, od_w_o, ffn_w_up, ffn_conv_w, ffn_conv_b, ffn_w_down):
    bp, seq, _ = x_prompt.shape
    bs = x_sample.shape[0]
    assert bp == 1 and seq == SEQ and bs == DEC_BATCH and x_sample.shape[1] == 1

    c_all = jnp.concatenate([c_sample, c_prompt, jnp.zeros((MOD_ROWS - bs - bp, D_MODEL), F32)], axis=0)
    mod_all = _ada_call(c_all, ada_w, ada_b)
    mods_s = [mod_all[l, :bs] for l in range(2)]
    mods_p = [mod_all[l, bs:bs + 1] for l in range(2)]

    hd = SGU_WIDTH // SGU_HEADS
    causal = jnp.tril(jnp.ones((CHUNK, CHUNK), F32))
    w = dict(
        ng=[[norm_g[l, j].reshape(1, D_MODEL) for j in range(2)] for l in range(2)],
        final_g=final_g.reshape(1, D_MODEL),
        ev_w_in=ev_w_in[0].astype(BF16), ev_w_out=ev_w_out[0].astype(BF16),
        sg_ln_g=sg_ln_g[0].reshape(1, SGU_WIDTH), sg_ln_b=sg_ln_b[0].reshape(1, SGU_WIDTH),
        sg_wt=(sg_w[0] * causal[None]).astype(BF16),
        sg_bias=jnp.repeat(sg_b[0].T, hd, axis=1),
        sg_row0=jnp.stack([jnp.repeat(sg_w[0, :, 0, 0], hd), jnp.repeat(sg_b[0, :, 0], hd)], axis=0),
        s5=_s5_tables(s5_lam_re[0], s5_lam_im[0], s5_log_dt[0], s5_b_re[0], s5_b_im[0], s5_c_re[0], s5_c_im[0]),
        s5_d=s5_d[0].reshape(1, S5_WIDTH), s5_w_glu=s5_w_glu[0].astype(BF16),
        s5_b_glu=s5_b_glu[0].reshape(1, S5_WIDTH),
        od_w_qkv=od_w_qkv[0].astype(BF16), od_w_o=od_w_o[0].astype(BF16),
        ffn_w_up=ffn_w_up.astype(BF16), ffn_w_down=ffn_w_down.astype(BF16),
        ffn_conv_w=ffn_conv_w, ffn_conv_b=ffn_conv_b.reshape(2, 1, D_FF),
    )

    wp = dict(w, rope=_rope_tables(jnp.arange(seq, dtype=jnp.int32)))
    ws = dict(w, rope=_rope_tables(jnp.full((bs,), PAST_LEN, jnp.int32)))

    p = _trunk(False, x_prompt[0], mods_p, None, wp)
    n_all = S5_GROUPS * S5_STATE
    state = dict(s5_re=state_s5_re[0].reshape(bs, n_all), s5_im=state_s5_im[0].reshape(bs, n_all),
                 ck=cache_c_k[0], cv=cache_c_v[0],
                 conv=state_ffn_conv)
    s = _trunk(True, x_sample[:, 0], mods_s, state, ws)

    keep = min(WIN_MAX, seq)
    kv_p = lambda a: a.reshape(1, 1, keep, N_HEADS, HEAD_DIM)
    kv_s = lambda a: a.reshape(1, bs, 1, N_HEADS, HEAD_DIM)
    s5_p = lambda a: a.reshape(1, 1, S5_GROUPS, S5_STATE)
    s5_s = lambda a: a.reshape(1, bs, S5_GROUPS, S5_STATE)
    conv_p = jnp.stack([c.reshape(1, 2, D_FF) for c in p['conv']])
    conv_s = jnp.stack([jnp.stack([state_ffn_conv[l][:, 1], s['conv'][l]], axis=1) for l in range(2)])
    return (p['y'][None], s['y'][:, None], s5_p(p['s5_re']), s5_p(p['s5_im']),
            s5_s(s['s5_re']), s5_s(s['s5_im']), s['vn'].reshape(1, bs, 1, SGU_WIDTH),
            kv_p(p['k']), kv_p(p['v']), kv_s(s['k']), kv_s(s['v']), conv_p, conv_s)
```

```python
import functools
import math

import jax
import jax.numpy as jnp
from jax import lax
from jax.experimental import pallas as pl
from jax.experimental.pallas import tpu as pltpu

F32 = jnp.float32
BF16 = jnp.bfloat16

D_MODEL = 1024
SEQ = 16384
DEC_BATCH = 32
PAST_LEN = 16384
S5_WIDTH = 512
S5_GROUP = 16
S5_GROUPS = 32
S5_STATE = 64
SGU_WIDTH = 512
SGU_HEADS = 4
CHUNK = 128
EVEN_IN = S5_WIDTH + 2 * SGU_WIDTH
HEAD_DIM = 64
N_HEADS = 16
ROT_DIM = 16
ROPE_THETA = 500000.0
DIL_BRANCHES = ((128, 1), (512, 4), (2048, 16))
BAND = 128
WIN_MAX = 2048
D_FF = 2816
EPS = 1e-6
NEG_INF = -1e30

ROW_TILE = 512
MOD_ROWS = 40
S5_T = 16
S5_SEG = 8
S5_CL = SEQ // (S5_T * S5_SEG)
FF_CHUNKS = ((0, 1024), (1024, 1024), (2048, 768))
PLANES = 16
STAT_LANES = 128
VMEM_LIMIT = 56 * 1024 * 1024


def _cparams(*sem):
    return pltpu.CompilerParams(dimension_semantics=sem, vmem_limit_bytes=VMEM_LIMIT)


def _const_spec(shape):
    nd = len(shape)
    return pl.BlockSpec(shape, lambda *_: (0,) * nd)


def _weight_spec(shape):
    nd = len(shape)
    return pl.BlockSpec(shape, lambda *_: (0,) * nd, pipeline_mode=pl.Buffered(1))


def _gelu(x):
    return jax.nn.gelu(x)


def _mod_norm(x, ng, shift, scale):
    ms = jnp.mean(x * x, axis=-1, keepdims=True)
    return (x * lax.rsqrt(ms + EPS) * ng) * (1.0 + scale) + shift


def _ada_body(c_ref, w_ref, b_ref, o_ref):
    c = c_ref[...]
    s = c * jax.nn.sigmoid(c)
    o_ref[...] = jnp.dot(s.astype(BF16), w_ref[...].astype(BF16),
                         preferred_element_type=F32) + b_ref[...]


def _ada_call(c_all, ada_w, ada_b):
    depth = ada_w.shape[0]
    nt = 1536
    return pl.pallas_call(
        _ada_body,
        grid=(depth, 6 * D_MODEL // nt),
        in_specs=[
            pl.BlockSpec((MOD_ROWS, D_MODEL), lambda l, j: (0, 0)),
            pl.BlockSpec((None, D_MODEL, nt), lambda l, j: (l, 0, j)),
            pl.BlockSpec((None, 1, nt), lambda l, j: (l, 0, j)),
        ],
        out_specs=pl.BlockSpec((None, MOD_ROWS, nt), lambda l, j: (l, 0, j)),
        out_shape=jax.ShapeDtypeStruct((depth, MOD_ROWS, 6 * D_MODEL), F32),
        compiler_params=_cparams("arbitrary", "arbitrary"),
        name="ada_mod",
    )(c_all, ada_w, ada_b.reshape(depth, 1, 6 * D_MODEL))


def _even_pre_body(sample, tm, x_ref, mod_ref, ng_ref, win_ref, lng_ref, lnb_ref, wt_ref, bs_ref,
                   xa_ref, bout_ref, *vn_out):
    h = _mod_norm(x_ref[...], ng_ref[...], mod_ref[:, 0:D_MODEL], mod_ref[:, D_MODEL:2 * D_MODEL])
    proj = jnp.dot(h.astype(BF16), win_ref[...], preferred_element_type=F32)
    xa_ref[...] = proj[:, :S5_WIDTH]
    u = _gelu(proj[:, S5_WIDTH:S5_WIDTH + SGU_WIDTH])
    v = _gelu(proj[:, S5_WIDTH + SGU_WIDTH:])
    mu = jnp.mean(v, axis=-1, keepdims=True)
    var = jnp.mean(jnp.square(v - mu), axis=-1, keepdims=True)
    vn = (v - mu) * lax.rsqrt(var + EPS) * lng_ref[...] + lnb_ref[...]
    if sample:
        vn_out[0][...] = vn
        bout_ref[...] = (u * (vn * bs_ref[0:1, :] + bs_ref[1:2, :])).astype(BF16)
    else:
        vnb = vn.astype(BF16)
        hd = SGU_WIDTH // SGU_HEADS
        for ci in range(tm // CHUNK):
            rows = slice(ci * CHUNK, (ci + 1) * CHUNK)
            for hh in range(SGU_HEADS):
                cols = slice(hh * hd, (hh + 1) * hd)
                s = jnp.dot(wt_ref[hh], vnb[rows, cols], preferred_element_type=F32) + bs_ref[:, cols]
                bout_ref[rows, cols] = (u[rows, cols] * s).astype(BF16)


def _even_pre_call(sample, x, mod, ng, win, lng, lnb, wt, bs):
    rows = x.shape[0]
    tm = rows if sample else ROW_TILE
    row_spec = lambda w: pl.BlockSpec((tm, w), lambda i: (i, 0))
    out_shape = [jax.ShapeDtypeStruct((rows, S5_WIDTH), F32),
                 jax.ShapeDtypeStruct((rows, SGU_WIDTH), BF16)]
    out_specs = [row_spec(S5_WIDTH), row_spec(SGU_WIDTH)]
    if sample:
        out_shape.append(jax.ShapeDtypeStruct((rows, SGU_WIDTH), F32))
        out_specs.append(row_spec(SGU_WIDTH))
    return pl.pallas_call(
        functools.partial(_even_pre_body, sample, tm),
        grid=(rows // tm,),
        in_specs=[row_spec(D_MODEL), _const_spec(mod.shape), _const_spec(ng.shape), _weight_spec(win.shape),
                  _const_spec(lng.shape), _const_spec(lnb.shape), _const_spec(wt.shape), _const_spec(bs.shape)],
        out_specs=out_specs,
        out_shape=out_shape,
        compiler_params=_cparams("arbitrary"),
        name="even_pre_sample" if sample else "even_pre_prompt",
    )(x, mod, ng, win, lng, lnb, wt, bs)


def _s5_prompt_body(x_ref, mt_ref, pbr_ref, pbi_ref, pcr_ref, pci_ref, a16_ref, aseg_ref,
                    y_ref, hfin_ref, s_re, s_im, hp_re, hp_im, pw_re, pw_im):
    x = x_ref[...]
    tile = (S5_CL, S5_SEG, S5_STATE)
    s_re[...] = jnp.dot(x, pbr_ref[...], preferred_element_type=F32).reshape(tile)
    s_im[...] = jnp.dot(x, pbi_ref[...], preferred_element_type=F32).reshape(tile)
    a_r = jnp.broadcast_to(a16_ref[0:1, :], (S5_SEG, S5_STATE))
    a_i = jnp.broadcast_to(a16_ref[1:2, :], (S5_SEG, S5_STATE))

    def step(cl, carry):
        hr, hi, pr, pi = carry
        hp_re[cl] = hr
        hp_im[cl] = hi
        pw_re[cl] = pr
        pw_im[cl] = pi
        nhr = a_r * hr - a_i * hi + s_re[cl]
        nhi = a_r * hi + a_i * hr + s_im[cl]
        return nhr, nhi, a_r * pr - a_i * pi, a_r * pi + a_i * pr

    zero = jnp.zeros((S5_SEG, S5_STATE), F32)
    hr, hi, _, _ = lax.fori_loop(0, S5_CL, step, (zero, zero, zero + 1.0, zero))

    g_r = aseg_ref[0:1, :]
    g_i = aseg_ref[1:2, :]
    er = jnp.zeros((1, S5_STATE), F32)
    ei = jnp.zeros((1, S5_STATE), F32)
    ent_r, ent_i = [], []
    for s in range(S5_SEG):
        ent_r.append(er)
        ent_i.append(ei)
        er, ei = (g_r * er - g_i * ei + hr[s:s + 1, :], g_r * ei + g_i * er + hi[s:s + 1, :])
    hfin_ref[0:1, :] = er
    hfin_ref[1:2, :] = ei
    e_r = jnp.concatenate(ent_r, axis=0)[None]
    e_i = jnp.concatenate(ent_i, axis=0)[None]

    pr = pw_re[...]
    pi = pw_im[...]
    rows = S5_CL * S5_SEG
    h_r = (hp_re[...] + pr * e_r - pi * e_i).reshape(rows, S5_STATE).astype(BF16)
    h_i = (hp_im[...] + pr * e_i + pi * e_r).reshape(rows, S5_STATE).astype(BF16)
    y_ref[...] = (jnp.dot(x, mt_ref[...], preferred_element_type=F32)
                  + jnp.dot(h_r, pcr_ref[...], preferred_element_type=F32)
                  + jnp.dot(h_i, pci_ref[...], preferred_element_type=F32))


def _s5_prompt_call(xt, mt, pbr, pbi, pcr, pci, a16, aseg):
    g, rows, width = xt.shape
    grp = lambda a: pl.BlockSpec((None,) + a.shape[1:], lambda i: (i,) + (0,) * (a.ndim - 1))
    scratch = [pltpu.VMEM((S5_CL, S5_SEG, S5_STATE), F32) for _ in range(6)]
    return pl.pallas_call(
        _s5_prompt_body,
        grid=(g,),
        in_specs=[grp(xt), grp(mt), grp(pbr), grp(pbi), grp(pcr), grp(pci), grp(a16), grp(aseg)],
        out_specs=[pl.BlockSpec((None, rows, width), lambda i: (i, 0, 0)),
                   pl.BlockSpec((None, 2, S5_STATE), lambda i: (i, 0, 0))],
        out_shape=[jax.ShapeDtypeStruct((g, rows, width), F32),
                   jax.ShapeDtypeStruct((g, 2, S5_STATE), F32)],
        scratch_shapes=scratch,
        compiler_params=_cparams("arbitrary"),
        name="s5_prompt",
    )(xt, mt, pbr, pbi, pcr, pci, a16, aseg)


def _s5_sample_body(xa_ref, h0r_ref, h0i_ref, ar_ref, ai_ref, bdr_ref, bdi_ref, cdr_ref, cdi_ref,
                    yc_ref, hr_ref, hi_ref):
    u = xa_ref[...].astype(BF16)
    ar = ar_ref[...]
    ai = ai_ref[...]
    h0r = h0r_ref[...]
    h0i = h0i_ref[...]
    hr = ar * h0r - ai * h0i + jnp.dot(u, bdr_ref[...], preferred_element_type=F32)
    hi = ar * h0i + ai * h0r + jnp.dot(u, bdi_ref[...], preferred_element_type=F32)
    hr_ref[...] = hr
    hi_ref[...] = hi
    yc_ref[...] = (jnp.dot(hr.astype(BF16), cdr_ref[...], preferred_element_type=F32)
                   - jnp.dot(hi.astype(BF16), cdi_ref[...], preferred_element_type=F32))


def _s5_sample_call(xa, h0r, h0i, ar, ai, bdr, bdi, cdr, cdi):
    rows = xa.shape[0]
    n = S5_GROUPS * S5_STATE
    args = (xa, h0r, h0i, ar, ai, bdr, bdi, cdr, cdi)
    return pl.pallas_call(
        _s5_sample_body,
        grid=(1,),
        in_specs=[_const_spec(a.shape) for a in args],
        out_specs=[_const_spec((rows, S5_WIDTH)), _const_spec((rows, n)), _const_spec((rows, n))],
        out_shape=[jax.ShapeDtypeStruct((rows, S5_WIDTH), F32),
                   jax.ShapeDtypeStruct((rows, n), F32),
                   jax.ShapeDtypeStruct((rows, n), F32)],
        compiler_params=_cparams("arbitrary"),
        name="s5_sample",
    )(*args)


def _even_post_body(x_ref, yc_ref, xa_ref, bout_ref, mod_ref, d_ref, wglu_ref, bglu_ref, wout_ref, o_ref):
    y = _gelu(yc_ref[...] + d_ref[...] * xa_ref[...])
    gate = jax.nn.sigmoid(jnp.dot(y.astype(BF16), wglu_ref[...], preferred_element_type=F32) + bglu_ref[...])
    a_out = (y * gate).astype(BF16)
    mix = (jnp.dot(a_out, wout_ref[0:S5_WIDTH, :], preferred_element_type=F32)
           + jnp.dot(bout_ref[...], wout_ref[S5_WIDTH:, :], preferred_element_type=F32))
    o_ref[...] = x_ref[...] + mod_ref[:, 2 * D_MODEL:3 * D_MODEL] * mix


def _even_post_call(x, yc, xa, bout, mod, d, wglu, bglu, wout):
    rows = x.shape[0]
    tm = min(rows, ROW_TILE)
    row_spec = lambda w: pl.BlockSpec((tm, w), lambda i: (i, 0))
    return pl.pallas_call(
        _even_post_body,
        grid=(rows // tm,),
        in_specs=[row_spec(D_MODEL), row_spec(S5_WIDTH), row_spec(S5_WIDTH), row_spec(SGU_WIDTH),
                  _const_spec(mod.shape), _const_spec(d.shape), _const_spec(wglu.shape),
                  _const_spec(bglu.shape), _weight_spec(wout.shape)],
        out_specs=row_spec(D_MODEL),
        out_shape=jax.ShapeDtypeStruct((rows, D_MODEL), F32),
        compiler_params=_cparams("arbitrary"),
        name="even_post",
    )(x, yc, xa, bout, mod, d, wglu, bglu, wout)


def _ffn_body(sample, final, tm, *refs):
    refs = list(refs)
    x_ref, mod_ref, ng_ref, wup_ref, cw_ref, cb_ref, wdn_ref = refs[:7]
    pos = 7
    if sample:
        p2_ref, p1_ref = refs[pos:pos + 2]
        pos += 2
    if final:
        fg_ref = refs[pos]
        pos += 1
    o_ref, conv_ref = refs[pos:pos + 2]
    pos += 2
    if not sample:
        carry_ref = refs[pos]

        @pl.when(pl.program_id(0) == 0)
        def _():
            carry_ref[...] = jnp.zeros_like(carry_ref)

    x = x_ref[...]
    h = _mod_norm(x, ng_ref[...], mod_ref[:, 3 * D_MODEL:4 * D_MODEL], mod_ref[:, 4 * D_MODEL:5 * D_MODEL])
    hb = h.astype(BF16)
    acc = jnp.zeros((tm, D_MODEL), F32)
    if not sample:
        row = lax.broadcasted_iota(jnp.int32, (tm, 1), 0)
    for c0, cw in FF_CHUNKS:
        cols = slice(c0, c0 + cw)
        a = jnp.dot(hb, wup_ref[:, cols], preferred_element_type=F32)
        g = jnp.dot(hb, wup_ref[:, D_FF + c0:D_FF + c0 + cw], preferred_element_type=F32)
        if sample:
            am2 = p2_ref[:, cols]
            am1 = p1_ref[:, cols]
            conv_ref[:, cols] = a
        else:
            prev2 = carry_ref[0:1, cols]
            prev1 = carry_ref[1:2, cols]
            am1 = jnp.where(row == 0, prev1, pltpu.roll(a, 1, 0))
            am2 = jnp.where(row == 0, prev2, jnp.where(row == 1, prev1, pltpu.roll(a, 2, 0)))
            carry_ref[0:2, cols] = a[tm - 2:tm, :]
        y = cb_ref[:, cols] + cw_ref[0:1, cols] * am2 + cw_ref[1:2, cols] * am1 + cw_ref[2:3, cols] * a
        act = (_gelu(y) * g).astype(BF16)
        acc = acc + jnp.dot(act, wdn_ref[cols, :], preferred_element_type=F32)
    out = x + mod_ref[:, 5 * D_MODEL:6 * D_MODEL] * acc
    if final:
        ms = jnp.mean(out * out, axis=-1, keepdims=True)
        out = out * lax.rsqrt(ms + EPS) * fg_ref[...]
    o_ref[...] = out
    if not sample:
        conv_ref[...] = carry_ref[0:2, :]


def _ffn_call(sample, x, mod, ng, wup, cw, cb, wdn, prev=None, final_g=None):
    rows = x.shape[0]
    tm = rows if sample else ROW_TILE
    final = final_g is not None
    row_spec = lambda w: pl.BlockSpec((tm, w), lambda i: (i, 0))
    args = [x, mod, ng, wup, cw, cb, wdn]
    in_specs = [row_spec(D_MODEL), _const_spec(mod.shape), _const_spec(ng.shape), _weight_spec(wup.shape),
                _const_spec(cw.shape), _const_spec(cb.shape), _weight_spec(wdn.shape)]
    if sample:
        args += [prev[0], prev[1]]
        in_specs += [_const_spec(prev[0].shape), _const_spec(prev[1].shape)]
    if final:
        args.append(final_g)
        in_specs.append(_const_spec(final_g.shape))
    conv_rows = rows if sample else 2
    return pl.pallas_call(
        functools.partial(_ffn_body, sample, final, tm),
        grid=(rows // tm,),
        in_specs=in_specs,
        out_specs=[row_spec(D_MODEL), _const_spec((conv_rows, D_FF))],
        out_shape=[jax.ShapeDtypeStruct((rows, D_MODEL), F32),
                   jax.ShapeDtypeStruct((conv_rows, D_FF), F32)],
        scratch_shapes=[] if sample else [pltpu.VMEM((8, D_FF), F32)],
        compiler_params=_cparams("arbitrary"),
        name="ffn_sample" if sample else "ffn_prompt",
    )(*args)


def _odd_pre_body(planar, tm, x_ref, mod_ref, ng_ref, wqkv_ref, rc_ref, ra_ref, rb_ref,
                  q_ref, k_ref, v_ref, k32_ref, v32_ref, *rest):
    h = _mod_norm(x_ref[...], ng_ref[...], mod_ref[:, 0:D_MODEL], mod_ref[:, D_MODEL:2 * D_MODEL])
    qkv = jnp.dot(h.astype(BF16), wqkv_ref[...], preferred_element_type=F32)
    rc = rc_ref[...]
    ra = ra_ref[...]
    rb = rb_ref[...]
    lanes = rc.shape[1]
    half = ROT_DIM // 2

    def rope(t):
        return t * rc + pltpu.roll(t, half, 1) * ra + pltpu.roll(t, lanes - half, 1) * rb

    nblk = D_MODEL // lanes
    if planar:
        qpl_ref, kpl_ref, vpl_ref, stage = rest
        per = tm // PLANES

        def to_planes(dst_ref, slot, cols, val):
            stage[slot] = val
            for r in range(PLANES):
                dst_ref[r, :, cols] = stage[slot, pl.ds(r, per, stride=PLANES), :].astype(BF16)

    for j in range(nblk):
        cols = slice(j * lanes, (j + 1) * lanes)
        q = rope(qkv[:, j * lanes:(j + 1) * lanes]) * (HEAD_DIM ** -0.5)
        k = rope(qkv[:, D_MODEL + j * lanes:D_MODEL + (j + 1) * lanes])
        v = qkv[:, 2 * D_MODEL + j * lanes:2 * D_MODEL + (j + 1) * lanes]
        q_ref[:, cols] = q.astype(BF16)
        k_ref[:, cols] = k.astype(BF16)
        v_ref[:, cols] = v.astype(BF16)
        k32_ref[:, cols] = k
        v32_ref[:, cols] = v
        if planar:
            to_planes(qpl_ref, j, cols, q)
            to_planes(kpl_ref, nblk + j, cols, k)
            to_planes(vpl_ref, 2 * nblk + j, cols, v)


def _odd_pre_call(x, mod, ng, wqkv, rc, ra, rb, keep, planar):
    rows = x.shape[0]
    tm = min(rows, ROW_TILE)
    nt = rows // tm
    first_kept = (rows - keep) // tm
    row_spec = lambda w: pl.BlockSpec((tm, w), lambda i: (i, 0))
    keep_spec = pl.BlockSpec((tm, D_MODEL), lambda i: (jnp.maximum(i - first_kept, 0), 0))
    out_specs = [row_spec(D_MODEL), row_spec(D_MODEL), row_spec(D_MODEL), keep_spec, keep_spec]
    out_shape = ([jax.ShapeDtypeStruct((rows, D_MODEL), BF16)] * 3
                 + [jax.ShapeDtypeStruct((keep, D_MODEL), F32)] * 2)
    scratch = []
    if planar:
        plane_spec = pl.BlockSpec((PLANES, tm // PLANES, D_MODEL), lambda i: (0, i, 0))
        out_specs += [plane_spec] * 3
        out_shape += [jax.ShapeDtypeStruct((PLANES, rows // PLANES, D_MODEL), BF16)] * 3
        scratch = [pltpu.VMEM((3 * D_MODEL // rc.shape[1], tm, rc.shape[1]), F32)]
    return pl.pallas_call(
        functools.partial(_odd_pre_body, planar, tm),
        grid=(nt,),
        in_specs=[row_spec(D_MODEL), _const_spec(mod.shape), _const_spec(ng.shape), _weight_spec(wqkv.shape),
                  row_spec(rc.shape[1]), row_spec(rc.shape[1]), row_spec(rc.shape[1])],
        out_specs=out_specs,
        out_shape=out_shape,
        scratch_shapes=scratch,
        compiler_params=_cparams("arbitrary"),
        name="odd_pre",
    )(x, mod, ng, wqkv, rc, ra, rb)


def _attn_prompt_body(n_planes, q_ref, kp_ref, kc_ref, vp_ref, vc_ref, o_ref, st_ref):
    blk = pl.program_id(1)
    per = BAND // n_planes
    qi = lax.broadcasted_iota(jnp.int32, (BAND, 2 * BAND), 0)
    kj = lax.broadcasted_iota(jnp.int32, (BAND, 2 * BAND), 1)
    q_pos = n_planes * (qi % per) + qi // per
    k_half = kj // BAND
    k_pos = n_planes * (per * k_half + kj % per) + (kj % BAND) // per - BAND
    dist = q_pos - k_pos
    valid = (dist >= 0) & (dist <= BAND) & (k_half >= jnp.where(blk == 0, 1, 0))
    bias = jnp.where(valid, 0.0, NEG_INF).astype(F32)
    lane = lax.broadcasted_iota(jnp.int32, (BAND, 2 * HEAD_DIM), 1)
    low_half = lane < HEAD_DIM
    lane_row = lax.broadcasted_iota(jnp.int32, (1, 2 * HEAD_DIM), 1)
    head_keep = [jnp.where(lane_row < HEAD_DIM, 1.0, 0.0).astype(BF16),
                 jnp.where(lane_row < HEAD_DIM, 0.0, 1.0).astype(BF16)]
    st_out = jnp.zeros((BAND, STAT_LANES), F32)
    st_lane = lax.broadcasted_iota(jnp.int32, (BAND, STAT_LANES), 1)

    def rows_of(ref, cols):
        t = ref[:, cols] if n_planes == 1 else ref[:, :, cols]
        return t.reshape(BAND, t.shape[-1])

    for pair in range(N_HEADS // 2):
        cols = slice(pair * 2 * HEAD_DIM, (pair + 1) * 2 * HEAD_DIM)
        qp = rows_of(q_ref, cols)
        kk = jnp.concatenate([rows_of(kp_ref, cols), rows_of(kc_ref, cols)], axis=0)
        vv = jnp.concatenate([rows_of(vp_ref, cols), rows_of(vc_ref, cols)], axis=0)
        halves = []
        for sub in range(2):
            head = 2 * pair + sub
            qm = qp * head_keep[sub]
            s = lax.dot_general(qm, kk, (((1,), (1,)), ((), ())), preferred_element_type=F32) + bias
            m = jnp.max(s, axis=-1, keepdims=True)
            p = jnp.exp(s - m)
            l = jnp.sum(p, axis=-1, keepdims=True)
            halves.append(jnp.dot(p.astype(BF16), vv, preferred_element_type=F32) / l)
            st_out = jnp.where(st_lane == head, m, st_out)
            st_out = jnp.where(st_lane == N_HEADS + head, l, st_out)
        o_pair = jnp.where(low_half, halves[0], halves[1]).astype(o_ref.dtype)
        if n_planes == 1:
            o_ref[:, cols] = o_pair
        else:
            o_ref[:, :, cols] = o_pair.reshape(n_planes, per, 2 * HEAD_DIM)
    st_ref[...] = st_out.reshape(st_ref.shape)


def _attn_prompt_call(d, q, k, v):
    planes, rpp, width = q.shape
    if planes // d == 1 or d == 1:
        n_planes, outer = 1, planes
        view = lambda a: a
        blk = lambda w: (None, BAND, w)
        cur = lambda w: pl.BlockSpec(blk(w), lambda r, b: (r, b, 0))
        prev = lambda w: pl.BlockSpec(blk(w), lambda r, b: (r, jnp.maximum(b - 1, 0), 0))
        nb = rpp // BAND
    else:
        n_planes, outer = planes // d, d
        per = BAND // n_planes
        view = lambda a: a.reshape(n_planes, outer, rpp, a.shape[-1])
        blk = lambda w: (n_planes, None, per, w)
        cur = lambda w: pl.BlockSpec(blk(w), lambda r, b: (0, r, b, 0))
        prev = lambda w: pl.BlockSpec(blk(w), lambda r, b: (0, r, jnp.maximum(b - 1, 0), 0))
        nb = rpp // per
    qv, kv, vv = view(q), view(k), view(v)
    o, st = pl.pallas_call(
        functools.partial(_attn_prompt_body, n_planes),
        grid=(outer, nb),
        in_specs=[cur(width), prev(width), cur(width), prev(width), cur(width)],
        out_specs=[cur(width), cur(STAT_LANES)],
        out_shape=[jax.ShapeDtypeStruct(qv.shape, BF16),
                   jax.ShapeDtypeStruct(qv.shape[:-1] + (STAT_LANES,), F32)],
        compiler_params=_cparams("arbitrary", "arbitrary"),
        name="attn_prompt_d%d" % d,
    )(qv, kv, kv, vv, vv)
    return o.reshape(planes, rpp, width), st.reshape(planes, rpp, STAT_LANES)


def _attn_sample_body(hb, past, q_ref, kn_ref, vn_ref, kt_ref, vt_ref, o_ref):
    rows, news = [], []
    for h in range(hb):
        qc = q_ref[h]
        rows.append(jnp.sum(kt_ref[h] * qc, axis=0, keepdims=True))
        news.append(jnp.sum(kn_ref[h] * qc, axis=0, keepdims=True))
    s = jnp.concatenate(rows, axis=0)
    s_new = jnp.concatenate(news, axis=0)
    r = lax.broadcasted_iota(jnp.int32, (1, past), 1)
    ms, ls, ps, pns = [], [], [], []
    for window, d in DIL_BRANCHES:
        member = (r >= past - window) & ((past - r) % d == 0)
        sg = s + jnp.where(member, 0.0, NEG_INF).astype(F32)
        m = jnp.maximum(jnp.max(sg, axis=1, keepdims=True), s_new)
        p = jnp.exp(sg - m)
        pn = jnp.exp(s_new - m)
        ms.append(m)
        ps.append(p)
        pns.append(pn)
        ls.append(jnp.sum(p, axis=1, keepdims=True) + pn)
    m_all = jnp.maximum(jnp.maximum(ms[0], ms[1]), ms[2])
    cs = [jnp.exp(m - m_all) for m in ms]
    tot = cs[0] * ls[0] + cs[1] * ls[1] + cs[2] * ls[2]
    w = (cs[0] * ps[0] + cs[1] * ps[1] + cs[2] * ps[2]) / tot
    w_new = (cs[0] * pns[0] + cs[1] * pns[1] + cs[2] * pns[2]) / tot
    for h in range(hb):
        o_ref[h] = (jnp.sum(vt_ref[h] * w[h:h + 1, :], axis=1, keepdims=True)
                    + vn_ref[h] * w_new[h:h + 1, :])


def _attn_sample_call(q, k_new, v_new, kt, vt):
    bsz, heads, dim, past = kt.shape
    hb = 8
    col = pl.BlockSpec((None, hb, dim, 1), lambda b, j: (b, j, 0, 0))
    cache = pl.BlockSpec((None, hb, dim, past), lambda b, j: (b, j, 0, 0))
    return pl.pallas_call(
        functools.partial(_attn_sample_body, hb, past),
        grid=(bsz, heads // hb),
        in_specs=[col, col, col, cache, cache],
        out_specs=col,
        out_shape=jax.ShapeDtypeStruct((bsz, heads, dim, 1), F32),
        compiler_params=_cparams("arbitrary", "arbitrary"),
        name="attn_sample",
    )(q, k_new, v_new, kt, vt)


def _odd_post_body(x_ref, att_ref, mod_ref, wo_ref, o_ref):
    mix = jnp.dot(att_ref[...].astype(BF16), wo_ref[...], preferred_element_type=F32)
    o_ref[...] = x_ref[...] + mod_ref[:, 2 * D_MODEL:3 * D_MODEL] * mix


def _odd_post_call(x, att, mod, wo):
    rows = x.shape[0]
    tm = min(rows, ROW_TILE)
    row_spec = pl.BlockSpec((tm, D_MODEL), lambda i: (i, 0))
    return pl.pallas_call(
        _odd_post_body,
        grid=(rows // tm,),
        in_specs=[row_spec, row_spec, _const_spec(mod.shape), _weight_spec(wo.shape)],
        out_specs=row_spec,
        out_shape=jax.ShapeDtypeStruct((rows, D_MODEL), F32),
        compiler_params=_cparams("arbitrary"),
        name="odd_post",
    )(x, att, mod, wo)


def _odd_post_merge_body(tm, x_ref, o1_ref, s1_ref, o4_ref, s4_ref, o16_ref, s16_ref, mod_ref, wo_ref,
                         spread_ref, o_ref, ob4, ob16, sb4, sb16):
    per = tm // PLANES
    nblk = D_MODEL // STAT_LANES
    for r in range(PLANES):
        rows_r = pl.ds(r, per, stride=PLANES)
        for j in range(nblk):
            cols = slice(j * STAT_LANES, (j + 1) * STAT_LANES)
            ob4[j, rows_r, :] = o4_ref[r, :, cols].astype(F32)
            ob16[j, rows_r, :] = o16_ref[r, :, cols].astype(F32)
        sb4[rows_r, :] = s4_ref[r]
        sb16[rows_r, :] = s16_ref[r]
    stats = [s1_ref[...], sb4[...], sb16[...]]
    dens = [pltpu.roll(s, STAT_LANES - N_HEADS, 1) for s in stats]
    m_all = jnp.maximum(jnp.maximum(stats[0], stats[1]), stats[2])
    ws = [d * jnp.exp(s - m_all) for s, d in zip(stats, dens)]
    tot = ws[0] + ws[1] + ws[2]
    lane = lax.broadcasted_iota(jnp.int32, (tm, STAT_LANES), 1)
    spreads = []
    for w in ws:
        coef = jnp.where(lane < N_HEADS, w / tot, 0.0)
        hi = coef.astype(BF16)
        lo = (coef - hi.astype(F32)).astype(BF16)
        spreads.append(jnp.dot(jnp.concatenate([hi, lo], axis=1), spread_ref[...], preferred_element_type=F32))
    pieces = []
    for j in range(nblk):
        cols = slice(j * STAT_LANES, (j + 1) * STAT_LANES)
        pieces.append((spreads[0][:, cols] * o1_ref[:, cols].astype(F32) + spreads[1][:, cols] * ob4[j]
                       + spreads[2][:, cols] * ob16[j]).astype(BF16))
    att = jnp.concatenate(pieces, axis=1)
    mix = jnp.dot(att, wo_ref[...], preferred_element_type=F32)
    o_ref[...] = x_ref[...] + mod_ref[:, 2 * D_MODEL:3 * D_MODEL] * mix


def _odd_post_merge_call(x, branches, mod, wo):
    rows = x.shape[0]
    tm = ROW_TILE
    (o1, s1), (o4, s4), (o16, s16) = branches
    row_spec = pl.BlockSpec((tm, D_MODEL), lambda i: (i, 0))
    nat = lambda w: pl.BlockSpec((None, tm, w), lambda i: (0, i, 0))
    pln = lambda w: pl.BlockSpec((PLANES, tm // PLANES, w), lambda i: (0, i, 0))
    head_of_lane = jnp.arange(D_MODEL) // HEAD_DIM
    spread = (jnp.arange(STAT_LANES)[:, None] == head_of_lane[None, :]).astype(BF16)
    spread = jnp.concatenate([spread, spread], axis=0)
    return pl.pallas_call(
        functools.partial(_odd_post_merge_body, tm),
        grid=(rows // tm,),
        in_specs=[row_spec, nat(D_MODEL), nat(STAT_LANES), pln(D_MODEL), pln(STAT_LANES),
                  pln(D_MODEL), pln(STAT_LANES), _const_spec(mod.shape), _weight_spec(wo.shape),
                  _const_spec(spread.shape)],
        out_specs=row_spec,
        out_shape=jax.ShapeDtypeStruct((rows, D_MODEL), F32),
        scratch_shapes=[pltpu.VMEM((D_MODEL // STAT_LANES, tm, STAT_LANES), F32),
                        pltpu.VMEM((D_MODEL // STAT_LANES, tm, STAT_LANES), F32),
                        pltpu.VMEM((tm, STAT_LANES), F32), pltpu.VMEM((tm, STAT_LANES), F32)],
        compiler_params=_cparams("arbitrary"),
        name="odd_post_merge",
    )(x, o1, s1, o4, s4, o16, s16, mod, wo, spread)


def _s5_tables(lam_re, lam_im, log_dt, b_re, b_im, c_re, c_im):
    hp = lax.Precision.HIGHEST
    dt = jnp.exp(log_dt)[:, None]
    lr, li = lam_re, lam_im
    ks = jnp.arange(S5_T + 1, dtype=F32)[:, None, None]
    mag = jnp.exp(ks * (lr * dt))
    pw_r = mag * jnp.cos(ks * (li * dt))
    pw_i = mag * jnp.sin(ks * (li * dt))
    ar, ai = pw_r[1], pw_i[1]
    den = lr * lr + li * li
    fr = ((ar - 1.0) * lr + ai * li) / den
    fi = (ai * lr - (ar - 1.0) * li) / den
    bbr = fr[..., None] * b_re - fi[..., None] * b_im
    bbi = fr[..., None] * b_im + fi[..., None] * b_re
    ca_r = c_re[None] * pw_r[:, :, None, :] - c_im[None] * pw_i[:, :, None, :]
    ca_i = c_re[None] * pw_i[:, :, None, :] + c_im[None] * pw_r[:, :, None, :]
    kern = (jnp.einsum('kgpn,gnq->kgpq', ca_r[:S5_T], bbr, precision=hp)
            - jnp.einsum('kgpn,gnq->kgpq', ca_i[:S5_T], bbi, precision=hp))
    ti = jnp.arange(S5_T)[:, None]
    to = jnp.arange(S5_T)[None, :]
    lag = to - ti
    toe = jnp.where((lag >= 0)[:, :, None, None, None], kern[jnp.clip(lag, 0, S5_T - 1)], 0.0)
    width = S5_T * S5_GROUP
    mt = toe.transpose(2, 0, 4, 1, 3).reshape(S5_GROUPS, width, width)
    rev_r = pw_r[S5_T - 1::-1][:S5_T]
    rev_i = pw_i[S5_T - 1::-1][:S5_T]
    pb_r = rev_r[..., None] * bbr[None] - rev_i[..., None] * bbi[None]
    pb_i = rev_r[..., None] * bbi[None] + rev_i[..., None] * bbr[None]
    pb_r = pb_r.transpose(1, 0, 3, 2).reshape(S5_GROUPS, width, S5_STATE)
    pb_i = pb_i.transpose(1, 0, 3, 2).reshape(S5_GROUPS, width, S5_STATE)
    pc_r = ca_r[1:].transpose(1, 3, 0, 2).reshape(S5_GROUPS, S5_STATE, width)
    pc_i = (-ca_i[1:]).transpose(1, 3, 0, 2).reshape(S5_GROUPS, S5_STATE, width)
    a16 = jnp.stack([pw_r[S5_T], pw_i[S5_T]], axis=1)
    sr, si = pw_r[S5_T], pw_i[S5_T]
    for _ in range(int(math.log2(S5_CL))):
        sr, si = sr * sr - si * si, 2.0 * sr * si
    aseg = jnp.stack([sr, si], axis=1)
    eye = jnp.eye(S5_GROUPS, dtype=F32)
    n_all = S5_GROUPS * S5_STATE
    bd_r = jnp.einsum('gnp,gh->gphn', bbr, eye).reshape(S5_WIDTH, n_all)
    bd_i = jnp.einsum('gnp,gh->gphn', bbi, eye).reshape(S5_WIDTH, n_all)
    cd_r = jnp.einsum('gpn,gh->gnhp', c_re, eye).reshape(n_all, S5_WIDTH)
    cd_i = jnp.einsum('gpn,gh->gnhp', c_im, eye).reshape(n_all, S5_WIDTH)
    return dict(mt=mt.astype(BF16), pb_r=pb_r.astype(BF16), pb_i=pb_i.astype(BF16),
                pc_r=pc_r.astype(BF16), pc_i=pc_i.astype(BF16), a16=a16, aseg=aseg,
                a_r=ar.reshape(1, n_all), a_i=ai.reshape(1, n_all),
                bd_r=bd_r.astype(BF16), bd_i=bd_i.astype(BF16),
                cd_r=cd_r.astype(BF16), cd_i=cd_i.astype(BF16))


def _rope_tables(pos):
    half = ROT_DIM // 2
    inv = jnp.power(ROPE_THETA, -jnp.arange(half, dtype=F32) * 2.0 / ROT_DIM)
    ang = pos.astype(F32)[:, None] * inv[None, :]
    cos, sin = jnp.cos(ang), jnp.sin(ang)
    rows = pos.shape[0]
    ones = jnp.ones((rows, HEAD_DIM - ROT_DIM), F32)
    zeros = jnp.zeros((rows, HEAD_DIM - ROT_DIM), F32)
    zh = jnp.zeros((rows, half), F32)
    rc = jnp.concatenate([cos, cos, ones], axis=1)
    ra = jnp.concatenate([zh, sin, zeros], axis=1)
    rb = jnp.concatenate([-sin, zh, zeros], axis=1)
    tile2 = lambda t: jnp.concatenate([t, t], axis=1)
    return tile2(rc), tile2(ra), tile2(rb)


def _trunk(sample, x, mods, state, w):
    outs = {}
    mod = mods[0]
    if sample:
        xa, bout, vn = _even_pre_call(True, x, mod, w['ng'][0][0], w['ev_w_in'], w['sg_ln_g'], w['sg_ln_b'],
                                      w['sg_wt'], w['sg_row0'])
        outs['vn'] = vn
        t = w['s5']
        yc, hr, hi = _s5_sample_call(xa, state['s5_re'], state['s5_im'], t['a_r'], t['a_i'],
                                     t['bd_r'], t['bd_i'], t['cd_r'], t['cd_i'])
        outs['s5_re'], outs['s5_im'] = hr, hi
    else:
        xa, bout = _even_pre_call(False, x, mod, w['ng'][0][0], w['ev_w_in'], w['sg_ln_g'], w['sg_ln_b'],
                                  w['sg_wt'], w['sg_bias'])
        t = w['s5']
        xt = (xa.reshape(S5_SEG, S5_CL, S5_T, S5_GROUPS, S5_GROUP).transpose(3, 1, 0, 2, 4)
              .reshape(S5_GROUPS, S5_CL * S5_SEG, S5_T * S5_GROUP).astype(BF16))
        yt, hfin = _s5_prompt_call(xt, t['mt'], t['pb_r'], t['pb_i'], t['pc_r'], t['pc_i'], t['a16'], t['aseg'])
        yc = (yt.reshape(S5_GROUPS, S5_CL, S5_SEG, S5_T, S5_GROUP).transpose(2, 1, 3, 0, 4)
              .reshape(SEQ, S5_WIDTH))
        outs['s5_re'], outs['s5_im'] = hfin[:, 0], hfin[:, 1]
    x = _even_post_call(x, yc, xa, bout, mod, w['s5_d'], w['s5_w_glu'], w['s5_b_glu'], w['ev_w_out'])
    prev = (state['conv'][0][:, 0], state['conv'][0][:, 1]) if sample else None
    x, conv0 = _ffn_call(sample, x, mod, w['ng'][0][1], w['ffn_w_up'][0], w['ffn_conv_w'][0],
                         w['ffn_conv_b'][0], w['ffn_w_down'][0], prev=prev)
    mod = mods[1]
    rows = x.shape[0]
    keep = rows if sample else min(WIN_MAX, rows)
    pre = _odd_pre_call(x, mod, w['ng'][1][0], w['od_w_qkv'], *w['rope'], keep, planar=not sample)
    q, k, v, k32, v32 = pre[:5]
    outs['k'], outs['v'] = k32, v32
    if sample:
        heads = lambda a: a.astype(F32).reshape(rows, N_HEADS, HEAD_DIM, 1)
        att = _attn_sample_call(heads(q), heads(k32), heads(v32), state['ck'], state['cv'])
        x = _odd_post_call(x, att.reshape(rows, D_MODEL), mod, w['od_w_o'])
    else:
        q_pl, k_pl, v_pl = pre[5:]
        branches = (_attn_prompt_call(1, q[None], k[None], v[None]),
                    _attn_prompt_call(4, q_pl, k_pl, v_pl),
                    _attn_prompt_call(16, q_pl, k_pl, v_pl))
        x = _odd_post_merge_call(x, branches, mod, w['od_w_o'])
    prev = (state['conv'][1][:, 0], state['conv'][1][:, 1]) if sample else None
    y, conv1 = _ffn_call(sample, x, mod, w['ng'][1][1], w['ffn_w_up'][1], w['ffn_conv_w'][1],
                         w['ffn_conv_b'][1], w['ffn_w_down'][1], prev=prev, final_g=w['final_g'])
    outs['y'] = y
    outs['conv'] = (conv0, conv1)
    return outs


def kernel(x_prompt, x_sample, c_prompt, c_sample, state_s5_re, state_s5_im, cache_c_k, cache_c_v,
           state_ffn_conv, ada_w, ada_b, norm_g, final_g, ev_w_in, ev_w_out, s5_lam_re, s5_lam_im,
           s5_log_dt, s5_b_re, s5_b_im, s5_c_re, s5_c_im, s5_d, s5_w_glu, s5_b_glu, sg_ln_g, sg_ln_b,
           sg_w, sg_b, od_w_qkv, od_w_o, ffn_w_up, ffn_conv_w, ffn_conv_b, ffn_w_down):
    bp, seq, _ = x_prompt.shape
    bs = x_sample.shape[0]
    assert bp == 1 and seq == SEQ and bs == DEC_BATCH and x_sample.shape[1] == 1

    c_all = jnp.concatenate([c_sample, c_prompt, jnp.zeros((MOD_ROWS - bs - bp, D_MODEL), F32)], axis=0)
    mod_all = _ada_call(c_all, ada_w, ada_b)
    mods_s = [mod_all[l, :bs] for l in range(2)]
    mods_p = [mod_all[l, bs:bs + 1] for l in range(2)]

    hd = SGU_WIDTH // SGU_HEADS
    causal = jnp.tril(jnp.ones((CHUNK, CHUNK), F32))
    w = dict(
        ng=[[norm_g[l, j].reshape(1, D_MODEL) for j in range(2)] for l in range(2)],
        final_g=final_g.reshape(1, D_MODEL),
        ev_w_in=ev_w_in[0].astype(BF16), ev_w_out=ev_w_out[0].astype(BF16),
        sg_ln_g=sg_ln_g[0].reshape(1, SGU_WIDTH), sg_ln_b=sg_ln_b[0].reshape(1, SGU_WIDTH),
        sg_wt=(sg_w[0] * causal[None]).astype(BF16),
        sg_bias=jnp.repeat(sg_b[0].T, hd, axis=1),
        sg_row0=jnp.stack([jnp.repeat(sg_w[0, :, 0, 0], hd), jnp.repeat(sg_b[0, :, 0], hd)], axis=0),
        s5=_s5_tables(s5_lam_re[0], s5_lam_im[0], s5_log_dt[0], s5_b_re[0], s5_b_im[0], s5_c_re[0], s5_c_im[0]),
        s5_d=s5_d[0].reshape(1, S5_WIDTH), s5_w_glu=s5_w_glu[0].astype(BF16),
        s5_b_glu=s5_b_glu[0].reshape(1, S5_WIDTH),
        od_w_qkv=od_w_qkv[0].astype(BF16), od_w_o=od_w_o[0].astype(BF16),
        ffn_w_up=ffn_w_up.astype(BF16), ffn_w_down=ffn_w_down.astype(BF16),
        ffn_conv_w=ffn_conv_w, ffn_conv_b=ffn_conv_b.reshape(2, 1, D_FF),
    )

    wp = dict(w, rope=_rope_tables(jnp.arange(seq, dtype=jnp.int32)))
    ws = dict(w, rope=_rope_tables(jnp.full((bs,), PAST_LEN, jnp.int32)))

    p = _trunk(False, x_prompt[0], mods_p, None, wp)
    n_all = S5_GROUPS * S5_STATE
    state = dict(s5_re=state_s5_re[0].reshape(bs, n_all), s5_im=state_s5_im[0].reshape(bs, n_all),
                 ck=jnp.transpose(cache_c_k[0], (0, 2, 3, 1)), cv=jnp.transpose(cache_c_v[0], (0, 2, 3, 1)),
                 conv=state_ffn_conv)
    s = _trunk(True, x_sample[:, 0], mods_s, state, ws)

    keep = min(WIN_MAX, seq)
    kv_p = lambda a: a.reshape(1, 1, keep, N_HEADS, HEAD_DIM)
    kv_s = lambda a: a.reshape(1, bs, 1, N_HEADS, HEAD_DIM)
    s5_p = lambda a: a.reshape(1, 1, S5_GROUPS, S5_STATE)
    s5_s = lambda a: a.reshape(1, bs, S5_GROUPS, S5_STATE)
    conv_p = jnp.stack([c.reshape(1, 2, D_FF) for c in p['conv']])
    conv_s = jnp.stack([jnp.stack([state_ffn_conv[l][:, 1], s['conv'][l]], axis=1) for l in range(2)])
    return (p['y'][None], s['y'][:, None], s5_p(p['s5_re']), s5_p(p['s5_im']),
            s5_s(s['s5_re']), s5_s(s['s5_im']), s['vn'].reshape(1, bs, 1, SGU_WIDTH),
            kv_p(p['k']), kv_p(p['v']), kv_s(s['k']), kv_s(s['v']), conv_p, conv_s)
```

```python
import functools
import math

import jax
import jax.numpy as jnp
from jax import lax
from jax.experimental import pallas as pl
from jax.experimental.pallas import tpu as pltpu

F32 = jnp.float32
BF16 = jnp.bfloat16

D_MODEL = 1024
SEQ = 16384
DEC_BATCH = 32
PAST_LEN = 16384
S5_WIDTH = 512
S5_GROUP = 16
S5_GROUPS = 32
S5_STATE = 64
SGU_WIDTH = 512
SGU_HEADS = 4
CHUNK = 128
EVEN_IN = S5_WIDTH + 2 * SGU_WIDTH
HEAD_DIM = 64
N_HEADS = 16
ROT_DIM = 16
ROPE_THETA = 500000.0
DIL_BRANCHES = ((128, 1), (512, 4), (2048, 16))
BAND = 128
WIN_MAX = 2048
D_FF = 2816
EPS = 1e-6
NEG_INF = -1e30

ROW_TILE = 512
MOD_ROWS = 40
S5_T = 16
S5_SEG = 8
S5_CL = SEQ // (S5_T * S5_SEG)
S5_OLANES = 128
S5_OCTETS = S5_WIDTH // S5_OLANES
S5_OGROUPS = S5_OLANES // S5_GROUP
S5_OSTATE = S5_OGROUPS * S5_STATE
FF_CHUNKS = ((0, 1024), (1024, 1024), (2048, 768))
ROPE_LANES = 128
PLANES = 16
STAT_LANES = 128
VMEM_LIMIT = 56 * 1024 * 1024


def _cparams(*sem):
    return pltpu.CompilerParams(dimension_semantics=sem, vmem_limit_bytes=VMEM_LIMIT)


def _const_spec(shape):
    nd = len(shape)
    return pl.BlockSpec(shape, lambda *_: (0,) * nd)


def _weight_spec(shape):
    nd = len(shape)
    return pl.BlockSpec(shape, lambda *_: (0,) * nd, pipeline_mode=pl.Buffered(1))


def _gelu(x):
    return jax.nn.gelu(x)


def _mod_norm(x, ng, shift, scale):
    ms = jnp.mean(x * x, axis=-1, keepdims=True)
    return (x * lax.rsqrt(ms + EPS) * ng) * (1.0 + scale) + shift


def _ada_body(c_ref, w_ref, b_ref, o_ref):
    c = c_ref[...]
    s = c * jax.nn.sigmoid(c)
    o_ref[...] = jnp.dot(s.astype(BF16), w_ref[...].astype(BF16),
                         preferred_element_type=F32) + b_ref[...]


def _ada_call(c_all, ada_w, ada_b):
    depth = ada_w.shape[0]
    nt = 1536
    return pl.pallas_call(
        _ada_body,
        grid=(depth, 6 * D_MODEL // nt),
        in_specs=[
            pl.BlockSpec((MOD_ROWS, D_MODEL), lambda l, j: (0, 0)),
            pl.BlockSpec((None, D_MODEL, nt), lambda l, j: (l, 0, j)),
            pl.BlockSpec((None, 1, nt), lambda l, j: (l, 0, j)),
        ],
        out_specs=pl.BlockSpec((None, MOD_ROWS, nt), lambda l, j: (l, 0, j)),
        out_shape=jax.ShapeDtypeStruct((depth, MOD_ROWS, 6 * D_MODEL), F32),
        compiler_params=_cparams("arbitrary", "arbitrary"),
        name="ada_mod",
    )(c_all, ada_w, ada_b.reshape(depth, 1, 6 * D_MODEL))


def _even_pre_body(sample, tm, x_ref, mod_ref, ng_ref, win_ref, lng_ref, lnb_ref, wt_ref, bs_ref,
                   xa_ref, bout_ref, *vn_out):
    h = _mod_norm(x_ref[...], ng_ref[...], mod_ref[:, 0:D_MODEL], mod_ref[:, D_MODEL:2 * D_MODEL])
    proj = jnp.dot(h.astype(BF16), win_ref[...], preferred_element_type=F32)
    xa_ref[...] = proj[:, :S5_WIDTH]
    u = _gelu(proj[:, S5_WIDTH:S5_WIDTH + SGU_WIDTH])
    v = _gelu(proj[:, S5_WIDTH + SGU_WIDTH:])
    mu = jnp.mean(v, axis=-1, keepdims=True)
    var = jnp.mean(jnp.square(v - mu), axis=-1, keepdims=True)
    vn = (v - mu) * lax.rsqrt(var + EPS) * lng_ref[...] + lnb_ref[...]
    if sample:
        vn_out[0][...] = vn
        bout_ref[...] = (u * (vn * bs_ref[0:1, :] + bs_ref[1:2, :])).astype(BF16)
    else:
        vnb = vn.astype(BF16)
        hd = SGU_WIDTH // SGU_HEADS
        for ci in range(tm // CHUNK):
            rows = slice(ci * CHUNK, (ci + 1) * CHUNK)
            for hh in range(SGU_HEADS):
                cols = slice(hh * hd, (hh + 1) * hd)
                s = jnp.dot(wt_ref[hh], vnb[rows, cols], preferred_element_type=F32) + bs_ref[:, cols]
                bout_ref[rows, cols] = (u[rows, cols] * s).astype(BF16)
        xf_ref, stage = vn_out
        per = tm // S5_T
        for ov in range(S5_OCTETS):
            stage[ov] = proj[:, ov * S5_OLANES:(ov + 1) * S5_OLANES]
            for t in range(S5_T):
                c0 = (ov * S5_T + t) * S5_OLANES
                xf_ref[:, c0:c0 + S5_OLANES] = stage[ov, pl.ds(t, per, stride=S5_T), :].astype(BF16)


def _even_pre_call(sample, x, mod, ng, win, lng, lnb, wt, bs):
    rows = x.shape[0]
    tm = rows if sample else ROW_TILE
    row_spec = lambda w: pl.BlockSpec((tm, w), lambda i: (i, 0))
    out_shape = [jax.ShapeDtypeStruct((rows, S5_WIDTH), F32),
                 jax.ShapeDtypeStruct((rows, SGU_WIDTH), BF16)]
    out_specs = [row_spec(S5_WIDTH), row_spec(SGU_WIDTH)]
    scratch = []
    if sample:
        out_shape.append(jax.ShapeDtypeStruct((rows, SGU_WIDTH), F32))
        out_specs.append(row_spec(SGU_WIDTH))
    else:
        out_shape.append(jax.ShapeDtypeStruct((rows // S5_T, S5_T * S5_WIDTH), BF16))
        out_specs.append(pl.BlockSpec((tm // S5_T, S5_T * S5_WIDTH), lambda i: (i, 0)))
        scratch = [pltpu.VMEM((S5_OCTETS, tm, S5_OLANES), F32)]
    return pl.pallas_call(
        functools.partial(_even_pre_body, sample, tm),
        grid=(rows // tm,),
        in_specs=[row_spec(D_MODEL), _const_spec(mod.shape), _const_spec(ng.shape), _weight_spec(win.shape),
                  _const_spec(lng.shape), _const_spec(lnb.shape), _const_spec(wt.shape), _const_spec(bs.shape)],
        out_specs=out_specs,
        out_shape=out_shape,
        scratch_shapes=scratch,
        compiler_params=_cparams("arbitrary"),
        name="even_pre_sample" if sample else "even_pre_prompt",
    )(x, mod, ng, win, lng, lnb, wt, bs)


def _s5_prompt_body(x_ref, mt_ref, pbr_ref, pbi_ref, pcr_ref, pci_ref, a16_ref, aseg_ref,
                    y_ref, hfin_ref, s_re, s_im, hb_re, hb_im):
    x = x_ref[...]
    ns = S5_OSTATE
    seg_rows = lambda s: slice(s * S5_CL, (s + 1) * S5_CL)
    seg_lanes = lambda s: slice(s * ns, (s + 1) * ns)
    sr = jnp.dot(x, pbr_ref[...], preferred_element_type=F32)
    si = jnp.dot(x, pbi_ref[...], preferred_element_type=F32)
    for s in range(S5_SEG):
        s_re[:, seg_lanes(s)] = sr[seg_rows(s), :]
        s_im[:, seg_lanes(s)] = si[seg_rows(s), :]
    wide = lambda t: jnp.concatenate([t] * S5_SEG, axis=1)
    a_r = wide(a16_ref[0:1, :])
    a_i = wide(a16_ref[1:2, :])

    def local_step(cl, carry):
        hr, hi = carry
        row = pl.ds(cl, 1)
        return a_r * hr - a_i * hi + s_re[row, :], a_r * hi + a_i * hr + s_im[row, :]

    zero = jnp.zeros((1, S5_SEG * ns), F32)
    hr, hi = lax.fori_loop(0, S5_CL, local_step, (zero, zero))

    g_r = aseg_ref[0:1, :]
    g_i = aseg_ref[1:2, :]
    er = jnp.zeros((1, ns), F32)
    ei = jnp.zeros((1, ns), F32)
    ent_r, ent_i = [], []
    for s in range(S5_SEG):
        ent_r.append(er)
        ent_i.append(ei)
        er, ei = (g_r * er - g_i * ei + hr[:, seg_lanes(s)], g_r * ei + g_i * er + hi[:, seg_lanes(s)])
    hfin_ref[0:1, :] = er
    hfin_ref[1:2, :] = ei

    def true_step(cl, carry):
        hr, hi = carry
        row = pl.ds(cl, 1)
        nxt = (a_r * hr - a_i * hi + s_re[row, :], a_r * hi + a_i * hr + s_im[row, :])
        s_re[row, :] = hr
        s_im[row, :] = hi
        return nxt

    lax.fori_loop(0, S5_CL, true_step, (jnp.concatenate(ent_r, axis=1), jnp.concatenate(ent_i, axis=1)))
    for s in range(S5_SEG):
        hb_re[seg_rows(s), :] = s_re[:, seg_lanes(s)].astype(BF16)
        hb_im[seg_rows(s), :] = s_im[:, seg_lanes(s)].astype(BF16)
    nq = 4
    qw = x.shape[1] // nq
    for j in range(nq):
        cols = slice(j * qw, (j + 1) * qw)
        y_ref[:, cols] = (jnp.dot(hb_re[...], pcr_ref[:, cols], preferred_element_type=F32)
                          + jnp.dot(hb_im[...], pci_ref[:, cols], preferred_element_type=F32)
                          + jnp.dot(x_ref[:, :(j + 1) * qw], mt_ref[:(j + 1) * qw, cols],
                                    preferred_element_type=F32))


def _s5_prompt_call(xf, mt, pbr, pbi, pcr, pci, a16, aseg):
    rows = xf.shape[0]
    ow = S5_T * S5_OLANES
    grp = lambda a: pl.BlockSpec((None,) + a.shape[1:], lambda i: (i,) + (0,) * (a.ndim - 1),
                                 pipeline_mode=pl.Buffered(1))
    cols = pl.BlockSpec((rows, ow), lambda i: (0, i))
    return pl.pallas_call(
        _s5_prompt_body,
        grid=(S5_OCTETS,),
        in_specs=[cols, grp(mt), grp(pbr), grp(pbi), grp(pcr), grp(pci), grp(a16), grp(aseg)],
        out_specs=[cols, pl.BlockSpec((None, 2, S5_OSTATE), lambda i: (i, 0, 0))],
        out_shape=[jax.ShapeDtypeStruct((rows, S5_OCTETS * ow), F32),
                   jax.ShapeDtypeStruct((S5_OCTETS, 2, S5_OSTATE), F32)],
        scratch_shapes=[pltpu.VMEM((S5_CL, S5_SEG * S5_OSTATE), F32),
                        pltpu.VMEM((S5_CL, S5_SEG * S5_OSTATE), F32),
                        pltpu.VMEM((rows, S5_OSTATE), BF16), pltpu.VMEM((rows, S5_OSTATE), BF16)],
        compiler_params=_cparams("arbitrary"),
        name="s5_prompt",
    )(xf, mt, pbr, pbi, pcr, pci, a16, aseg)


def _s5_sample_body(xa_ref, h0r_ref, h0i_ref, ar_ref, ai_ref, bdr_ref, bdi_ref, cdr_ref, cdi_ref,
                    yc_ref, hr_ref, hi_ref):
    u = xa_ref[...].astype(BF16)
    ar = ar_ref[...]
    ai = ai_ref[...]
    h0r = h0r_ref[...]
    h0i = h0i_ref[...]
    hr = ar * h0r - ai * h0i + jnp.dot(u, bdr_ref[...], preferred_element_type=F32)
    hi = ar * h0i + ai * h0r + jnp.dot(u, bdi_ref[...], preferred_element_type=F32)
    hr_ref[...] = hr
    hi_ref[...] = hi
    yc_ref[...] = (jnp.dot(hr.astype(BF16), cdr_ref[...], preferred_element_type=F32)
                   - jnp.dot(hi.astype(BF16), cdi_ref[...], preferred_element_type=F32))


def _s5_sample_call(xa, h0r, h0i, ar, ai, bdr, bdi, cdr, cdi):
    rows = xa.shape[0]
    n = S5_GROUPS * S5_STATE
    args = (xa, h0r, h0i, ar, ai, bdr, bdi, cdr, cdi)
    return pl.pallas_call(
        _s5_sample_body,
        grid=(1,),
        in_specs=[_const_spec(a.shape) for a in args],
        out_specs=[_const_spec((rows, S5_WIDTH)), _const_spec((rows, n)), _const_spec((rows, n))],
        out_shape=[jax.ShapeDtypeStruct((rows, S5_WIDTH), F32),
                   jax.ShapeDtypeStruct((rows, n), F32),
                   jax.ShapeDtypeStruct((rows, n), F32)],
        compiler_params=_cparams("arbitrary"),
        name="s5_sample",
    )(*args)


def _even_post_body(folded, tm, x_ref, yc_ref, xa_ref, bout_ref, mod_ref, d_ref, wglu_ref, bglu_ref, wout_ref,
                    o_ref, *scratch):
    if folded:
        stage = scratch[0]
        per = tm // S5_T
        for ov in range(S5_OCTETS):
            for t in range(S5_T):
                c0 = (ov * S5_T + t) * S5_OLANES
                stage[ov, pl.ds(t, per, stride=S5_T), :] = yc_ref[:, c0:c0 + S5_OLANES]
        yc = jnp.concatenate([stage[ov] for ov in range(S5_OCTETS)], axis=1)
    else:
        yc = yc_ref[...]
    y = _gelu(yc + d_ref[...] * xa_ref[...])
    gate = jax.nn.sigmoid(jnp.dot(y.astype(BF16), wglu_ref[...], preferred_element_type=F32) + bglu_ref[...])
    a_out = (y * gate).astype(BF16)
    mix = (jnp.dot(a_out, wout_ref[0:S5_WIDTH, :], preferred_element_type=F32)
           + jnp.dot(bout_ref[...], wout_ref[S5_WIDTH:, :], preferred_element_type=F32))
    o_ref[...] = x_ref[...] + mod_ref[:, 2 * D_MODEL:3 * D_MODEL] * mix


def _even_post_call(x, yc, xa, bout, mod, d, wglu, bglu, wout):
    rows = x.shape[0]
    tm = min(rows, ROW_TILE)
    row_spec = lambda w: pl.BlockSpec((tm, w), lambda i: (i, 0))
    folded = yc.shape[0] != rows
    yc_spec = pl.BlockSpec((tm // S5_T, S5_T * S5_WIDTH), lambda i: (i, 0)) if folded else row_spec(S5_WIDTH)
    scratch = [pltpu.VMEM((S5_OCTETS, tm, S5_OLANES), F32)] if folded else []
    return pl.pallas_call(
        functools.partial(_even_post_body, folded, tm),
        grid=(rows // tm,),
        scratch_shapes=scratch,
        in_specs=[row_spec(D_MODEL), yc_spec, row_spec(S5_WIDTH), row_spec(SGU_WIDTH),
                  _const_spec(mod.shape), _const_spec(d.shape), _const_spec(wglu.shape),
                  _const_spec(bglu.shape), _weight_spec(wout.shape)],
        out_specs=row_spec(D_MODEL),
        out_shape=jax.ShapeDtypeStruct((rows, D_MODEL), F32),
        compiler_params=_cparams("arbitrary"),
        name="even_post",
    )(x, yc, xa, bout, mod, d, wglu, bglu, wout)


def _ffn_body(sample, final, tm, *refs):
    refs = list(refs)
    x_ref, mod_ref, ng_ref, wup_ref, cw_ref, cb_ref, wdn_ref = refs[:7]
    pos = 7
    if sample:
        p2_ref, p1_ref = refs[pos:pos + 2]
        pos += 2
    if final:
        fg_ref = refs[pos]
        pos += 1
    o_ref, conv_ref = refs[pos:pos + 2]
    pos += 2
    if not sample:
        carry_ref = refs[pos]

        @pl.when(pl.program_id(0) == 0)
        def _():
            carry_ref[...] = jnp.zeros_like(carry_ref)

    x = x_ref[...]
    h = _mod_norm(x, ng_ref[...], mod_ref[:, 3 * D_MODEL:4 * D_MODEL], mod_ref[:, 4 * D_MODEL:5 * D_MODEL])
    hb = h.astype(BF16)
    acc = jnp.zeros((tm, D_MODEL), F32)
    if not sample:
        row = lax.broadcasted_iota(jnp.int32, (tm, 1), 0)
    for c0, cw in FF_CHUNKS:
        cols = slice(c0, c0 + cw)
        a = jnp.dot(hb, wup_ref[:, cols], preferred_element_type=F32)
        g = jnp.dot(hb, wup_ref[:, D_FF + c0:D_FF + c0 + cw], preferred_element_type=F32)
        if sample:
            am2 = p2_ref[:, cols]
            am1 = p1_ref[:, cols]
            conv_ref[:, cols] = a
        else:
            prev2 = carry_ref[0:1, cols]
            prev1 = carry_ref[1:2, cols]
            am1 = jnp.where(row == 0, prev1, pltpu.roll(a, 1, 0))
            am2 = jnp.where(row == 0, prev2, jnp.where(row == 1, prev1, pltpu.roll(a, 2, 0)))
            carry_ref[0:2, cols] = a[tm - 2:tm, :]
        y = cb_ref[:, cols] + cw_ref[0:1, cols] * am2 + cw_ref[1:2, cols] * am1 + cw_ref[2:3, cols] * a
        act = (_gelu(y) * g).astype(BF16)
        acc = acc + jnp.dot(act, wdn_ref[cols, :], preferred_element_type=F32)
    out = x + mod_ref[:, 5 * D_MODEL:6 * D_MODEL] * acc
    if final:
        ms = jnp.mean(out * out, axis=-1, keepdims=True)
        out = out * lax.rsqrt(ms + EPS) * fg_ref[...]
    o_ref[...] = out
    if not sample:
        conv_ref[...] = carry_ref[0:2, :]


def _ffn_call(sample, x, mod, ng, wup, cw, cb, wdn, prev=None, final_g=None):
    rows = x.shape[0]
    tm = rows if sample else ROW_TILE
    final = final_g is not None
    row_spec = lambda w: pl.BlockSpec((tm, w), lambda i: (i, 0))
    args = [x, mod, ng, wup, cw, cb, wdn]
    in_specs = [row_spec(D_MODEL), _const_spec(mod.shape), _const_spec(ng.shape), _weight_spec(wup.shape),
                _const_spec(cw.shape), _const_spec(cb.shape), _weight_spec(wdn.shape)]
    if sample:
        args += [prev[0], prev[1]]
        in_specs += [_const_spec(prev[0].shape), _const_spec(prev[1].shape)]
    if final:
        args.append(final_g)
        in_specs.append(_const_spec(final_g.shape))
    conv_rows = rows if sample else 2
    return pl.pallas_call(
        functools.partial(_ffn_body, sample, final, tm),
        grid=(rows // tm,),
        in_specs=in_specs,
        out_specs=[row_spec(D_MODEL), _const_spec((conv_rows, D_FF))],
        out_shape=[jax.ShapeDtypeStruct((rows, D_MODEL), F32),
                   jax.ShapeDtypeStruct((conv_rows, D_FF), F32)],
        scratch_shapes=[] if sample else [pltpu.VMEM((8, D_FF), F32)],
        compiler_params=_cparams("arbitrary"),
        name="ffn_sample" if sample else "ffn_prompt",
    )(*args)


def _odd_pre_body(planar, tm, x_ref, mod_ref, ng_ref, wqkv_ref, tile_trig_ref, row_trig_ref,
                  q_ref, k_ref, v_ref, k32_ref, v32_ref, *rest):
    h = _mod_norm(x_ref[...], ng_ref[...], mod_ref[:, 0:D_MODEL], mod_ref[:, D_MODEL:2 * D_MODEL])
    qkv = jnp.dot(h.astype(BF16), wqkv_ref[...], preferred_element_type=F32)
    tt = tile_trig_ref[...]
    cos_sum = lambda f: tt[2 * f:2 * f + 1] * row_trig_ref[2 * f] - tt[2 * f + 1:2 * f + 2] * row_trig_ref[2 * f + 1]
    sin_sum = lambda f: tt[2 * f + 1:2 * f + 2] * row_trig_ref[2 * f] + tt[2 * f:2 * f + 1] * row_trig_ref[2 * f + 1]
    rc = cos_sum(0)
    ra = sin_sum(1)
    rb = -sin_sum(2)
    lanes = ROPE_LANES
    half = ROT_DIM // 2

    def rope(t):
        return t * rc + pltpu.roll(t, half, 1) * ra + pltpu.roll(t, lanes - half, 1) * rb

    nblk = D_MODEL // lanes
    if planar:
        qpl_ref, kpl_ref, vpl_ref, stage = rest
        per = tm // PLANES

        def to_planes(dst_ref, slot, cols, val):
            stage[slot] = val
            for r in range(PLANES):
                dst_ref[r, :, cols] = stage[slot, pl.ds(r, per, stride=PLANES), :].astype(BF16)

    for j in range(nblk):
        cols = slice(j * lanes, (j + 1) * lanes)
        q = rope(qkv[:, j * lanes:(j + 1) * lanes]) * (HEAD_DIM ** -0.5)
        k = rope(qkv[:, D_MODEL + j * lanes:D_MODEL + (j + 1) * lanes])
        v = qkv[:, 2 * D_MODEL + j * lanes:2 * D_MODEL + (j + 1) * lanes]
        q_ref[:, cols] = q.astype(BF16)
        k_ref[:, cols] = k.astype(BF16)
        v_ref[:, cols] = v.astype(BF16)
        k32_ref[:, cols] = k
        v32_ref[:, cols] = v
        if planar:
            to_planes(qpl_ref, j, cols, q)
            to_planes(kpl_ref, nblk + j, cols, k)
            to_planes(vpl_ref, 2 * nblk + j, cols, v)


def _odd_pre_call(x, mod, ng, wqkv, tile_trig, row_trig, keep, planar):
    rows = x.shape[0]
    tm = min(rows, ROW_TILE)
    nt = rows // tm
    first_kept = (rows - keep) // tm
    row_spec = lambda w: pl.BlockSpec((tm, w), lambda i: (i, 0))
    keep_spec = pl.BlockSpec((tm, D_MODEL), lambda i: (jnp.maximum(i - first_kept, 0), 0))
    out_specs = [row_spec(D_MODEL), row_spec(D_MODEL), row_spec(D_MODEL), keep_spec, keep_spec]
    out_shape = ([jax.ShapeDtypeStruct((rows, D_MODEL), BF16)] * 3
                 + [jax.ShapeDtypeStruct((keep, D_MODEL), F32)] * 2)
    scratch = []
    if planar:
        plane_spec = pl.BlockSpec((PLANES, tm // PLANES, D_MODEL), lambda i: (0, i, 0))
        out_specs += [plane_spec] * 3
        out_shape += [jax.ShapeDtypeStruct((PLANES, rows // PLANES, D_MODEL), BF16)] * 3
        scratch = [pltpu.VMEM((3 * D_MODEL // ROPE_LANES, tm, ROPE_LANES), F32)]
    return pl.pallas_call(
        functools.partial(_odd_pre_body, planar, tm),
        grid=(nt,),
        in_specs=[row_spec(D_MODEL), _const_spec(mod.shape), _const_spec(ng.shape), _weight_spec(wqkv.shape),
                  pl.BlockSpec((None,) + tile_trig.shape[1:], lambda i: (i, 0, 0)), _const_spec(row_trig.shape)],
        out_specs=out_specs,
        out_shape=out_shape,
        scratch_shapes=scratch,
        compiler_params=_cparams("arbitrary"),
        name="odd_pre",
    )(x, mod, ng, wqkv, tile_trig, row_trig)


def _attn_prompt_body(n_planes, q_ref, kp_ref, kc_ref, vp_ref, vc_ref, o_ref, st_ref):
    blk = pl.program_id(1)
    per = BAND // n_planes
    qi = lax.broadcasted_iota(jnp.int32, (BAND, 2 * BAND), 0)
    kj = lax.broadcasted_iota(jnp.int32, (BAND, 2 * BAND), 1)
    q_pos = n_planes * (qi % per) + qi // per
    k_half = kj // BAND
    k_pos = n_planes * (per * k_half + kj % per) + (kj % BAND) // per - BAND
    dist = q_pos - k_pos
    valid = (dist >= 0) & (dist <= BAND) & (k_half >= jnp.where(blk == 0, 1, 0))
    bias = jnp.where(valid, 0.0, NEG_INF).astype(F32)
    lane = lax.broadcasted_iota(jnp.int32, (BAND, 2 * HEAD_DIM), 1)
    low_half = lane < HEAD_DIM
    lane_row = lax.broadcasted_iota(jnp.int32, (1, 2 * HEAD_DIM), 1)
    head_keep = [jnp.where(lane_row < HEAD_DIM, 1.0, 0.0).astype(BF16),
                 jnp.where(lane_row < HEAD_DIM, 0.0, 1.0).astype(BF16)]
    st_out = jnp.zeros((BAND, STAT_LANES), F32)
    st_lane = lax.broadcasted_iota(jnp.int32, (BAND, STAT_LANES), 1)

    def rows_of(ref, cols):
        t = ref[:, cols] if n_planes == 1 else ref[:, :, cols]
        return t.reshape(BAND, t.shape[-1])

    for pair in range(N_HEADS // 2):
        cols = slice(pair * 2 * HEAD_DIM, (pair + 1) * 2 * HEAD_DIM)
        qp = rows_of(q_ref, cols)
        kk = jnp.concatenate([rows_of(kp_ref, cols), rows_of(kc_ref, cols)], axis=0)
        vv = jnp.concatenate([rows_of(vp_ref, cols), rows_of(vc_ref, cols)], axis=0)
        halves = []
        for sub in range(2):
            head = 2 * pair + sub
            qm = qp * head_keep[sub]
            s = lax.dot_general(qm, kk, (((1,), (1,)), ((), ())), preferred_element_type=F32) + bias
            m = jnp.max(s, axis=-1, keepdims=True)
            p = jnp.exp(s - m)
            l = jnp.sum(p, axis=-1, keepdims=True)
            halves.append(jnp.dot(p.astype(BF16), vv, preferred_element_type=F32) / l)
            st_out = jnp.where(st_lane == head, m, st_out)
            st_out = jnp.where(st_lane == N_HEADS + head, l, st_out)
        o_pair = jnp.where(low_half, halves[0], halves[1]).astype(o_ref.dtype)
        if n_planes == 1:
            o_ref[:, cols] = o_pair
        else:
            o_ref[:, :, cols] = o_pair.reshape(n_planes, per, 2 * HEAD_DIM)
    st_ref[...] = st_out.reshape(st_ref.shape)


def _attn_prompt_call(d, q, k, v):
    planes, rpp, width = q.shape
    if planes // d == 1 or d == 1:
        n_planes, outer = 1, planes
        view = lambda a: a
        blk = lambda w: (None, BAND, w)
        cur = lambda w: pl.BlockSpec(blk(w), lambda r, b: (r, b, 0))
        prev = lambda w: pl.BlockSpec(blk(w), lambda r, b: (r, jnp.maximum(b - 1, 0), 0))
        nb = rpp // BAND
    else:
        n_planes, outer = planes // d, d
        per = BAND // n_planes
        view = lambda a: a.reshape(n_planes, outer, rpp, a.shape[-1])
        blk = lambda w: (n_planes, None, per, w)
        cur = lambda w: pl.BlockSpec(blk(w), lambda r, b: (0, r, b, 0))
        prev = lambda w: pl.BlockSpec(blk(w), lambda r, b: (0, r, jnp.maximum(b - 1, 0), 0))
        nb = rpp // per
    qv, kv, vv = view(q), view(k), view(v)
    o, st = pl.pallas_call(
        functools.partial(_attn_prompt_body, n_planes),
        grid=(outer, nb),
        in_specs=[cur(width), prev(width), cur(width), prev(width), cur(width)],
        out_specs=[cur(width), cur(STAT_LANES)],
        out_shape=[jax.ShapeDtypeStruct(qv.shape, BF16),
                   jax.ShapeDtypeStruct(qv.shape[:-1] + (STAT_LANES,), F32)],
        compiler_params=_cparams("arbitrary", "arbitrary"),
        name="attn_prompt_d%d" % d,
    )(qv, kv, kv, vv, vv)
    return o.reshape(planes, rpp, width), st.reshape(planes, rpp, STAT_LANES)


def _attn_sample_body(hb, past, q_ref, kn_ref, vn_ref, kt_ref, vt_ref, o_ref):
    rows, news = [], []
    for h in range(hb):
        qc = q_ref[h]
        rows.append(jnp.sum(kt_ref[h] * qc, axis=0, keepdims=True))
        news.append(jnp.sum(kn_ref[h] * qc, axis=0, keepdims=True))
    s = jnp.concatenate(rows, axis=0)
    s_new = jnp.concatenate(news, axis=0)
    r = lax.broadcasted_iota(jnp.int32, (1, past), 1)
    ms, ls, ps, pns = [], [], [], []
    for window, d in DIL_BRANCHES:
        member = (r >= past - window) & ((past - r) % d == 0)
        sg = s + jnp.where(member, 0.0, NEG_INF).astype(F32)
        m = jnp.maximum(jnp.max(sg, axis=1, keepdims=True), s_new)
        p = jnp.exp(sg - m)
        pn = jnp.exp(s_new - m)
        ms.append(m)
        ps.append(p)
        pns.append(pn)
        ls.append(jnp.sum(p, axis=1, keepdims=True) + pn)
    m_all = jnp.maximum(jnp.maximum(ms[0], ms[1]), ms[2])
    cs = [jnp.exp(m - m_all) for m in ms]
    tot = cs[0] * ls[0] + cs[1] * ls[1] + cs[2] * ls[2]
    w = (cs[0] * ps[0] + cs[1] * ps[1] + cs[2] * ps[2]) / tot
    w_new = (cs[0] * pns[0] + cs[1] * pns[1] + cs[2] * pns[2]) / tot
    for h in range(hb):
        o_ref[h] = (jnp.sum(vt_ref[h] * w[h:h + 1, :], axis=1, keepdims=True)
                    + vn_ref[h] * w_new[h:h + 1, :])


def _attn_sample_call(q, k_new, v_new, kt, vt):
    bsz, heads, dim, past = kt.shape
    hb = 8
    col = pl.BlockSpec((None, hb, dim, 1), lambda b, j: (b, j, 0, 0))
    cache = pl.BlockSpec((None, hb, dim, past), lambda b, j: (b, j, 0, 0))
    return pl.pallas_call(
        functools.partial(_attn_sample_body, hb, past),
        grid=(bsz, heads // hb),
        in_specs=[col, col, col, cache, cache],
        out_specs=col,
        out_shape=jax.ShapeDtypeStruct((bsz, heads, dim, 1), F32),
        compiler_params=_cparams("arbitrary", "arbitrary"),
        name="attn_sample",
    )(q, k_new, v_new, kt, vt)


def _odd_post_body(x_ref, att_ref, mod_ref, wo_ref, o_ref):
    mix = jnp.dot(att_ref[...].astype(BF16), wo_ref[...], preferred_element_type=F32)
    o_ref[...] = x_ref[...] + mod_ref[:, 2 * D_MODEL:3 * D_MODEL] * mix


def _odd_post_call(x, att, mod, wo):
    rows = x.shape[0]
    tm = min(rows, ROW_TILE)
    row_spec = pl.BlockSpec((tm, D_MODEL), lambda i: (i, 0))
    return pl.pallas_call(
        _odd_post_body,
        grid=(rows // tm,),
        in_specs=[row_spec, row_spec, _const_spec(mod.shape), _weight_spec(wo.shape)],
        out_specs=row_spec,
        out_shape=jax.ShapeDtypeStruct((rows, D_MODEL), F32),
        compiler_params=_cparams("arbitrary"),
        name="odd_post",
    )(x, att, mod, wo)


def _odd_post_merge_body(tm, x_ref, o1_ref, s1_ref, o4_ref, s4_ref, o16_ref, s16_ref, mod_ref, wo_ref,
                         spread_ref, o_ref, ob4, ob16, sb4, sb16):
    per = tm // PLANES
    nblk = D_MODEL // STAT_LANES
    for r in range(PLANES):
        rows_r = pl.ds(r, per, stride=PLANES)
        for j in range(nblk):
            cols = slice(j * STAT_LANES, (j + 1) * STAT_LANES)
            ob4[j, rows_r, :] = o4_ref[r, :, cols].astype(F32)
            ob16[j, rows_r, :] = o16_ref[r, :, cols].astype(F32)
        sb4[rows_r, :] = s4_ref[r]
        sb16[rows_r, :] = s16_ref[r]
    stats = [s1_ref[...], sb4[...], sb16[...]]
    dens = [pltpu.roll(s, STAT_LANES - N_HEADS, 1) for s in stats]
    m_all = jnp.maximum(jnp.maximum(stats[0], stats[1]), stats[2])
    ws = [d * jnp.exp(s - m_all) for s, d in zip(stats, dens)]
    tot = ws[0] + ws[1] + ws[2]
    lane = lax.broadcasted_iota(jnp.int32, (tm, STAT_LANES), 1)
    spreads = []
    for w in ws:
        coef = jnp.where(lane < N_HEADS, w / tot, 0.0)
        hi = coef.astype(BF16)
        lo = (coef - hi.astype(F32)).astype(BF16)
        spreads.append(jnp.dot(jnp.concatenate([hi, lo], axis=1), spread_ref[...], preferred_element_type=F32))
    pieces = []
    for j in range(nblk):
        cols = slice(j * STAT_LANES, (j + 1) * STAT_LANES)
        pieces.append((spreads[0][:, cols] * o1_ref[:, cols].astype(F32) + spreads[1][:, cols] * ob4[j]
                       + spreads[2][:, cols] * ob16[j]).astype(BF16))
    att = jnp.concatenate(pieces, axis=1)
    mix = jnp.dot(att, wo_ref[...], preferred_element_type=F32)
    o_ref[...] = x_ref[...] + mod_ref[:, 2 * D_MODEL:3 * D_MODEL] * mix


def _odd_post_merge_call(x, branches, mod, wo):
    rows = x.shape[0]
    tm = ROW_TILE
    (o1, s1), (o4, s4), (o16, s16) = branches
    row_spec = pl.BlockSpec((tm, D_MODEL), lambda i: (i, 0))
    nat = lambda w: pl.BlockSpec((None, tm, w), lambda i: (0, i, 0))
    pln = lambda w: pl.BlockSpec((PLANES, tm // PLANES, w), lambda i: (0, i, 0))
    head_of_lane = jnp.arange(D_MODEL) // HEAD_DIM
    spread = (jnp.arange(STAT_LANES)[:, None] == head_of_lane[None, :]).astype(BF16)
    spread = jnp.concatenate([spread, spread], axis=0)
    return pl.pallas_call(
        functools.partial(_odd_post_merge_body, tm),
        grid=(rows // tm,),
        in_specs=[row_spec, nat(D_MODEL), nat(STAT_LANES), pln(D_MODEL), pln(STAT_LANES),
                  pln(D_MODEL), pln(STAT_LANES), _const_spec(mod.shape), _weight_spec(wo.shape),
                  _const_spec(spread.shape)],
        out_specs=row_spec,
        out_shape=jax.ShapeDtypeStruct((rows, D_MODEL), F32),
        scratch_shapes=[pltpu.VMEM((D_MODEL // STAT_LANES, tm, STAT_LANES), F32),
                        pltpu.VMEM((D_MODEL // STAT_LANES, tm, STAT_LANES), F32),
                        pltpu.VMEM((tm, STAT_LANES), F32), pltpu.VMEM((tm, STAT_LANES), F32)],
        compiler_params=_cparams("arbitrary"),
        name="odd_post_merge",
    )(x, o1, s1, o4, s4, o16, s16, mod, wo, spread)


def _s5_tables(lam_re, lam_im, log_dt, b_re, b_im, c_re, c_im):
    hp = lax.Precision.HIGHEST
    dt = jnp.exp(log_dt)[:, None]
    lr, li = lam_re, lam_im
    ks = jnp.arange(S5_T + 1, dtype=F32)[:, None, None]
    mag = jnp.exp(ks * (lr * dt))
    pw_r = mag * jnp.cos(ks * (li * dt))
    pw_i = mag * jnp.sin(ks * (li * dt))
    ar, ai = pw_r[1], pw_i[1]
    den = lr * lr + li * li
    fr = ((ar - 1.0) * lr + ai * li) / den
    fi = (ai * lr - (ar - 1.0) * li) / den
    bbr = fr[..., None] * b_re - fi[..., None] * b_im
    bbi = fr[..., None] * b_im + fi[..., None] * b_re
    ca_r = c_re[None] * pw_r[:, :, None, :] - c_im[None] * pw_i[:, :, None, :]
    ca_i = c_re[None] * pw_i[:, :, None, :] + c_im[None] * pw_r[:, :, None, :]
    kern = (jnp.einsum('kgpn,gnq->kgpq', ca_r[:S5_T], bbr, precision=hp)
            - jnp.einsum('kgpn,gnq->kgpq', ca_i[:S5_T], bbi, precision=hp))
    ti = jnp.arange(S5_T)[:, None]
    to = jnp.arange(S5_T)[None, :]
    lag = to - ti
    toe = jnp.where((lag >= 0)[:, :, None, None, None], kern[jnp.clip(lag, 0, S5_T - 1)], 0.0)
    oc, og = S5_OCTETS, S5_OGROUPS
    eye_o = jnp.eye(og, dtype=F32)
    ow = S5_T * S5_OLANES
    split = lambda a, axis: a.reshape(a.shape[:axis] + (oc, og) + a.shape[axis + 1:])
    mt = (split(toe, 2).transpose(2, 0, 3, 5, 1, 4)[:, :, :, :, :, None, :]
          * eye_o[None, None, :, None, None, :, None]).reshape(oc, ow, ow)
    kr = (S5_T - 1) - jnp.arange(S5_T, dtype=F32)[:, None, None]
    rev_mag = jnp.exp(kr * (lr * dt))
    rev_r = rev_mag * jnp.cos(kr * (li * dt))
    rev_i = rev_mag * jnp.sin(kr * (li * dt))
    pb_r = rev_r[..., None] * bbr[None] - rev_i[..., None] * bbi[None]
    pb_i = rev_r[..., None] * bbi[None] + rev_i[..., None] * bbr[None]
    fold_b = lambda a: (split(a, 1).transpose(1, 0, 2, 4, 3)[:, :, :, :, None, :]
                        * eye_o[None, None, :, None, :, None]).reshape(oc, ow, S5_OSTATE)
    fold_c = lambda a: (split(a, 1).transpose(1, 2, 4, 0, 3)[:, :, :, :, None, :]
                        * eye_o[None, :, None, None, :, None]).reshape(oc, S5_OSTATE, ow)
    pb_r, pb_i = fold_b(pb_r), fold_b(pb_i)
    pc_r, pc_i = fold_c(ca_r[1:]), fold_c(-ca_i[1:])
    per_octet = lambda a: a.reshape(oc, S5_OSTATE)
    a16 = jnp.stack([per_octet(pw_r[S5_T]), per_octet(pw_i[S5_T])], axis=1)
    sr, si = pw_r[S5_T], pw_i[S5_T]
    for _ in range(int(math.log2(S5_CL))):
        sr, si = sr * sr - si * si, 2.0 * sr * si
    aseg = jnp.stack([per_octet(sr), per_octet(si)], axis=1)
    eye = jnp.eye(S5_GROUPS, dtype=F32)
    n_all = S5_GROUPS * S5_STATE
    bd_r = jnp.einsum('gnp,gh->gphn', bbr, eye).reshape(S5_WIDTH, n_all)
    bd_i = jnp.einsum('gnp,gh->gphn', bbi, eye).reshape(S5_WIDTH, n_all)
    cd_r = jnp.einsum('gpn,gh->gnhp', c_re, eye).reshape(n_all, S5_WIDTH)
    cd_i = jnp.einsum('gpn,gh->gnhp', c_im, eye).reshape(n_all, S5_WIDTH)
    return dict(mt=mt.astype(BF16), pb_r=pb_r.astype(BF16), pb_i=pb_i.astype(BF16),
                pc_r=pc_r.astype(BF16), pc_i=pc_i.astype(BF16), a16=a16, aseg=aseg,
                a_r=ar.reshape(1, n_all), a_i=ai.reshape(1, n_all),
                bd_r=bd_r.astype(BF16), bd_i=bd_i.astype(BF16),
                cd_r=cd_r.astype(BF16), cd_i=cd_i.astype(BF16))


def _rope_tables(tile_pos, row_pos):
    half = ROT_DIM // 2
    inv = jnp.power(ROPE_THETA, -jnp.arange(half, dtype=F32) * 2.0 / ROT_DIM)
    e = jnp.arange(ROPE_LANES) % HEAD_DIM
    inv_e = inv[e % half]
    freqs = jnp.stack([jnp.where(e < ROT_DIM, inv_e, 0.0),
                       jnp.where((e >= half) & (e < ROT_DIM), inv_e, 0.0),
                       jnp.where(e < half, inv_e, 0.0)])

    def trig(pos):
        ang = pos.astype(F32)[None, :, None] * freqs[:, None, :]
        return jnp.stack([jnp.cos(ang), jnp.sin(ang)], axis=1).reshape(6, pos.shape[0], ROPE_LANES)

    tile_trig = jnp.concatenate([trig(tile_pos), jnp.zeros((2, tile_pos.shape[0], ROPE_LANES), F32)], axis=0)
    return tile_trig.transpose(1, 0, 2), trig(row_pos)


def _trunk(sample, x, mods, state, w):
    outs = {}
    mod = mods[0]
    if sample:
        xa, bout, vn = _even_pre_call(True, x, mod, w['ng'][0][0], w['ev_w_in'], w['sg_ln_g'], w['sg_ln_b'],
                                      w['sg_wt'], w['sg_row0'])
        outs['vn'] = vn
        t = w['s5']
        yc, hr, hi = _s5_sample_call(xa, state['s5_re'], state['s5_im'], t['a_r'], t['a_i'],
                                     t['bd_r'], t['bd_i'], t['cd_r'], t['cd_i'])
        outs['s5_re'], outs['s5_im'] = hr, hi
    else:
        xa, bout, xf = _even_pre_call(False, x, mod, w['ng'][0][0], w['ev_w_in'], w['sg_ln_g'], w['sg_ln_b'],
                                      w['sg_wt'], w['sg_bias'])
        t = w['s5']
        yc, hfin = _s5_prompt_call(xf, t['mt'], t['pb_r'], t['pb_i'], t['pc_r'], t['pc_i'], t['a16'], t['aseg'])
        outs['s5_re'], outs['s5_im'] = hfin[:, 0], hfin[:, 1]
    x = _even_post_call(x, yc, xa, bout, mod, w['s5_d'], w['s5_w_glu'], w['s5_b_glu'], w['ev_w_out'])
    prev = (state['conv'][0][:, 0], state['conv'][0][:, 1]) if sample else None
    x, conv0 = _ffn_call(sample, x, mod, w['ng'][0][1], w['ffn_w_up'][0], w['ffn_conv_w'][0],
                         w['ffn_conv_b'][0], w['ffn_w_down'][0], prev=prev)
    mod = mods[1]
    rows = x.shape[0]
    keep = rows if sample else min(WIN_MAX, rows)
    pre = _odd_pre_call(x, mod, w['ng'][1][0], w['od_w_qkv'], *w['rope'], keep, planar=not sample)
    q, k, v, k32, v32 = pre[:5]
    outs['k'], outs['v'] = k32, v32
    if sample:
        heads = lambda a: a.astype(F32).reshape(rows, N_HEADS, HEAD_DIM, 1)
        att = _attn_sample_call(heads(q), heads(k32), heads(v32), state['ck'], state['cv'])
        x = _odd_post_call(x, att.reshape(rows, D_MODEL), mod, w['od_w_o'])
    else:
        q_pl, k_pl, v_pl = pre[5:]
        branches = (_attn_prompt_call(1, q[None], k[None], v[None]),
                    _attn_prompt_call(4, q_pl, k_pl, v_pl),
                    _attn_prompt_call(16, q_pl, k_pl, v_pl))
        x = _odd_post_merge_call(x, branches, mod, w['od_w_o'])
    prev = (state['conv'][1][:, 0], state['conv'][1][:, 1]) if sample else None
    y, conv1 = _ffn_call(sample, x, mod, w['ng'][1][1], w['ffn_w_up'][1], w['ffn_conv_w'][1],
                         w['ffn_conv_b'][1], w['ffn_w_down'][1], prev=prev, final_g=w['final_g'])
    outs['y'] = y
    outs['conv'] = (conv0, conv1)
    return outs


def kernel(x_prompt, x_sample, c_prompt, c_sample, state_s5_re, state_s5_im, cache_c_k, cache_c_v,
           state_ffn_conv, ada_w, ada_b, norm_g, final_g, ev_w_in, ev_w_out, s5_lam_re, s5_lam_im,
           s5_log_dt, s5_b_re, s5_b_im, s5_c_re, s5_c_im, s5_d, s5_w_glu, s5_b_glu, sg_ln_g, sg_ln_b,
           sg_w, sg_b, od_w_qkv, od_w_o, ffn_w_up, ffn_conv_w, ffn_conv_b, ffn_w_down):
    bp, seq, _ = x_prompt.shape
    bs = x_sample.shape[0]
    assert bp == 1 and seq == SEQ and bs == DEC_BATCH and x_sample.shape[1] == 1

    c_all = jnp.concatenate([c_sample, c_prompt, jnp.zeros((MOD_ROWS - bs - bp, D_MODEL), F32)], axis=0)
    mod_all = _ada_call(c_all, ada_w, ada_b)
    mods_s = [mod_all[l, :bs] for l in range(2)]
    mods_p = [mod_all[l, bs:bs + 1] for l in range(2)]

    hd = SGU_WIDTH // SGU_HEADS
    causal = jnp.tril(jnp.ones((CHUNK, CHUNK), F32))
    w = dict(
        ng=[[norm_g[l, j].reshape(1, D_MODEL) for j in range(2)] for l in range(2)],
        final_g=final_g.reshape(1, D_MODEL),
        ev_w_in=ev_w_in[0].astype(BF16), ev_w_out=ev_w_out[0].astype(BF16),
        sg_ln_g=sg_ln_g[0].reshape(1, SGU_WIDTH), sg_ln_b=sg_ln_b[0].reshape(1, SGU_WIDTH),
        sg_wt=(sg_w[0] * causal[None]).astype(BF16),
        sg_bias=jnp.repeat(sg_b[0].T, hd, axis=1),
        sg_row0=jnp.stack([jnp.repeat(sg_w[0, :, 0, 0], hd), jnp.repeat(sg_b[0, :, 0], hd)], axis=0),
        s5=_s5_tables(s5_lam_re[0], s5_lam_im[0], s5_log_dt[0], s5_b_re[0], s5_b_im[0], s5_c_re[0], s5_c_im[0]),
        s5_d=s5_d[0].reshape(1, S5_WIDTH), s5_w_glu=s5_w_glu[0].astype(BF16),
        s5_b_glu=s5_b_glu[0].reshape(1, S5_WIDTH),
        od_w_qkv=od_w_qkv[0].astype(BF16), od_w_o=od_w_o[0].astype(BF16),
        ffn_w_up=ffn_w_up.astype(BF16), ffn_w_down=ffn_w_down.astype(BF16),
        ffn_conv_w=ffn_conv_w, ffn_conv_b=ffn_conv_b.reshape(2, 1, D_FF),
    )

    wp = dict(w, rope=_rope_tables(jnp.arange(0, seq, ROW_TILE, dtype=jnp.int32),
                                   jnp.arange(ROW_TILE, dtype=jnp.int32)))
    ws = dict(w, rope=_rope_tables(jnp.full((1,), PAST_LEN, jnp.int32), jnp.zeros((bs,), jnp.int32)))

    p = _trunk(False, x_prompt[0], mods_p, None, wp)
    n_all = S5_GROUPS * S5_STATE
    state = dict(s5_re=state_s5_re[0].reshape(bs, n_all), s5_im=state_s5_im[0].reshape(bs, n_all),
                 ck=jnp.transpose(cache_c_k[0], (0, 2, 3, 1)), cv=jnp.transpose(cache_c_v[0], (0, 2, 3, 1)),
                 conv=state_ffn_conv)
    s = _trunk(True, x_sample[:, 0], mods_s, state, ws)

    keep = min(WIN_MAX, seq)
    kv_p = lambda a: a.reshape(1, 1, keep, N_HEADS, HEAD_DIM)
    kv_s = lambda a: a.reshape(1, bs, 1, N_HEADS, HEAD_DIM)
    s5_p = lambda a: a.reshape(1, 1, S5_GROUPS, S5_STATE)
    s5_s = lambda a: a.reshape(1, bs, S5_GROUPS, S5_STATE)
    conv_p = jnp.stack([c.reshape(1, 2, D_FF) for c in p['conv']])
    conv_s = jnp.stack([jnp.stack([state_ffn_conv[l][:, 1], s['conv'][l]], axis=1) for l in range(2)])
    return (p['y'][None], s['y'][:, None], s5_p(p['s5_re']), s5_p(p['s5_im']),
            s5_s(s['s5_re']), s5_s(s['s5_im']), s['vn'].reshape(1, bs, 1, SGU_WIDTH),
            kv_p(p['k']), kv_p(p['v']), kv_s(s['k']), kv_s(s['v']), conv_p, conv_s)
```

```python
import functools

import jax
import jax.numpy as jnp
from jax import lax
from jax.experimental import pallas as pl
from jax.experimental.pallas import tpu as pltpu

F32 = jnp.float32
BF16 = jnp.bfloat16

D_MODEL = 1024
SEQ = 16384
DEC_BATCH = 32
PAST_LEN = 16384
S5_WIDTH = 512
S5_GROUP = 16
S5_GROUPS = 32
S5_STATE = 64
SGU_WIDTH = 512
SGU_HEADS = 4
CHUNK = 128
EVEN_IN = S5_WIDTH + 2 * SGU_WIDTH
HEAD_DIM = 64
N_HEADS = 16
ROT_DIM = 16
ROPE_THETA = 500000.0
DIL_BRANCHES = ((128, 1), (512, 4), (2048, 16))
BAND = 128
WIN_MAX = 2048
D_FF = 2816
EPS = 1e-6
NEG_INF = -1e30

ROW_TILE = 512
MOD_ROWS = 40
S5_T = 16
S5_SUB = 8
S5_OLANES = 128
S5_OCTETS = S5_WIDTH // S5_OLANES
S5_OGROUPS = S5_OLANES // S5_GROUP
S5_OSTATE = S5_OGROUPS * S5_STATE
FF_CHUNKS = ((0, 1024), (1024, 1024), (2048, 768))
ROPE_LANES = 128
PLANES = 16
STAT_LANES = 128
VMEM_LIMIT = 56 * 1024 * 1024


def _cparams(*sem):
    return pltpu.CompilerParams(dimension_semantics=sem, vmem_limit_bytes=VMEM_LIMIT)


def _const_spec(shape):
    nd = len(shape)
    return pl.BlockSpec(shape, lambda *_: (0,) * nd)


def _weight_spec(shape):
    nd = len(shape)
    return pl.BlockSpec(shape, lambda *_: (0,) * nd, pipeline_mode=pl.Buffered(1))


def _gelu(x):
    return jax.nn.gelu(x)


def _mod_norm(x, ng, shift, scale):
    ms = jnp.mean(x * x, axis=-1, keepdims=True)
    return (x * lax.rsqrt(ms + EPS) * ng) * (1.0 + scale) + shift


def _ada_body(c_ref, w_ref, b_ref, o_ref):
    c = c_ref[...]
    s = c * jax.nn.sigmoid(c)
    o_ref[...] = jnp.dot(s.astype(BF16), w_ref[...].astype(BF16),
                         preferred_element_type=F32) + b_ref[...]


def _ada_call(c_all, ada_w, ada_b):
    depth = ada_w.shape[0]
    nt = 1536
    return pl.pallas_call(
        _ada_body,
        grid=(depth, 6 * D_MODEL // nt),
        in_specs=[
            pl.BlockSpec((MOD_ROWS, D_MODEL), lambda l, j: (0, 0)),
            pl.BlockSpec((None, D_MODEL, nt), lambda l, j: (l, 0, j)),
            pl.BlockSpec((None, 1, nt), lambda l, j: (l, 0, j)),
        ],
        out_specs=pl.BlockSpec((None, MOD_ROWS, nt), lambda l, j: (l, 0, j)),
        out_shape=jax.ShapeDtypeStruct((depth, MOD_ROWS, 6 * D_MODEL), F32),
        compiler_params=_cparams("arbitrary", "arbitrary"),
        name="ada_mod",
    )(c_all, ada_w, ada_b.reshape(depth, 1, 6 * D_MODEL))


def _even_pre_body(sample, tm, x_ref, mod_ref, ng_ref, win_ref, lng_ref, lnb_ref, wt_ref, bs_ref,
                   xa_ref, bout_ref, *vn_out):
    h = _mod_norm(x_ref[...], ng_ref[...], mod_ref[:, 0:D_MODEL], mod_ref[:, D_MODEL:2 * D_MODEL])
    proj = jnp.dot(h.astype(BF16), win_ref[...], preferred_element_type=F32)
    xa_ref[...] = proj[:, :S5_WIDTH]
    u = _gelu(proj[:, S5_WIDTH:S5_WIDTH + SGU_WIDTH])
    v = _gelu(proj[:, S5_WIDTH + SGU_WIDTH:])
    mu = jnp.mean(v, axis=-1, keepdims=True)
    var = jnp.mean(jnp.square(v - mu), axis=-1, keepdims=True)
    vn = (v - mu) * lax.rsqrt(var + EPS) * lng_ref[...] + lnb_ref[...]
    if sample:
        vn_out[0][...] = vn
        bout_ref[...] = (u * (vn * bs_ref[0:1, :] + bs_ref[1:2, :])).astype(BF16)
    else:
        vnb = vn.astype(BF16)
        hd = SGU_WIDTH // SGU_HEADS
        for ci in range(tm // CHUNK):
            rows = slice(ci * CHUNK, (ci + 1) * CHUNK)
            for hh in range(SGU_HEADS):
                cols = slice(hh * hd, (hh + 1) * hd)
                s = jnp.dot(wt_ref[hh], vnb[rows, cols], preferred_element_type=F32) + bs_ref[:, cols]
                bout_ref[rows, cols] = (u[rows, cols] * s).astype(BF16)
        xf_ref, stage = vn_out
        per = tm // S5_T
        for ov in range(S5_OCTETS):
            stage[ov] = proj[:, ov * S5_OLANES:(ov + 1) * S5_OLANES]
            for t in range(S5_T):
                c0 = (ov * S5_T + t) * S5_OLANES
                xf_ref[:, c0:c0 + S5_OLANES] = stage[ov, pl.ds(t, per, stride=S5_T), :].astype(BF16)


def _even_pre_call(sample, x, mod, ng, win, lng, lnb, wt, bs):
    rows = x.shape[0]
    tm = rows if sample else ROW_TILE
    row_spec = lambda w: pl.BlockSpec((tm, w), lambda i: (i, 0))
    out_shape = [jax.ShapeDtypeStruct((rows, S5_WIDTH), F32),
                 jax.ShapeDtypeStruct((rows, SGU_WIDTH), BF16)]
    out_specs = [row_spec(S5_WIDTH), row_spec(SGU_WIDTH)]
    scratch = []
    if sample:
        out_shape.append(jax.ShapeDtypeStruct((rows, SGU_WIDTH), F32))
        out_specs.append(row_spec(SGU_WIDTH))
    else:
        out_shape.append(jax.ShapeDtypeStruct((rows // S5_T, S5_T * S5_WIDTH), BF16))
        out_specs.append(pl.BlockSpec((tm // S5_T, S5_T * S5_WIDTH), lambda i: (i, 0)))
        scratch = [pltpu.VMEM((S5_OCTETS, tm, S5_OLANES), F32)]
    return pl.pallas_call(
        functools.partial(_even_pre_body, sample, tm),
        grid=(rows // tm,),
        in_specs=[row_spec(D_MODEL), _const_spec(mod.shape), _const_spec(ng.shape), _weight_spec(win.shape),
                  _const_spec(lng.shape), _const_spec(lnb.shape), _const_spec(wt.shape), _const_spec(bs.shape)],
        out_specs=out_specs,
        out_shape=out_shape,
        scratch_shapes=scratch,
        compiler_params=_cparams("arbitrary"),
        name="even_pre_sample" if sample else "even_pre_prompt",
    )(x, mod, ng, win, lng, lnb, wt, bs)


def _s5_prompt_body(x_ref, kd_ref, pbr_ref, pbi_ref, pcr_ref, pci_ref, a16_ref,
                    y_ref, hfin_ref, mt, pb_re, pb_im, pc_re, pc_im, s_re, s_im):
    ol, ns = S5_OLANES, S5_OSTATE
    nlb = ns // ol

    @pl.when(pl.program_id(0) == 0)
    def _():
        mt[...] = jnp.zeros_like(mt)

    for ti in range(S5_T):
        for to in range(ti, S5_T):
            mt[ti * ol:(ti + 1) * ol, to * ol:(to + 1) * ol] = kd_ref[to - ti]
    r_grp = lax.broadcasted_iota(jnp.int32, (ol, ns), 0) // S5_GROUP
    c_grp = lax.broadcasted_iota(jnp.int32, (ol, ns), 1) // S5_STATE
    same_group = jnp.where(r_grp == c_grp, 1.0, 0.0).astype(BF16)
    for t in range(S5_T):
        rows_t = slice(t * ol, (t + 1) * ol)
        for src, dst in ((pbr_ref, pb_re), (pbi_ref, pb_im), (pcr_ref, pc_re), (pci_ref, pc_im)):
            dst[rows_t, :] = jnp.concatenate([src[t]] * nlb, axis=1) * same_group

    x = x_ref[...]
    nrows = x.shape[0]
    sr = jnp.dot(x, pb_re[...], preferred_element_type=F32)
    si = jnp.dot(x, pb_im[...], preferred_element_type=F32)
    row_id = lax.broadcasted_iota(jnp.int32, (nrows, 1), 0)

    def shifted(t, k):
        return jnp.where(row_id >= k, pltpu.roll(t, k, 0), 0.0)

    pr, pi = a16_ref[0:1, :], a16_ref[1:2, :]
    k = 1
    while k < S5_SUB:
        tr, ti = shifted(sr, k), shifted(si, k)
        sr, si = sr + pr * tr - pi * ti, si + pr * ti + pi * tr
        pr, pi = pr * pr - pi * pi, 2.0 * pr * pi
        k *= 2
    s_re[...] = sr
    s_im[...] = si
    hr = jnp.zeros((S5_SUB, ns), F32)
    hi = jnp.zeros((S5_SUB, ns), F32)
    for tile in range(nrows // S5_SUB):
        rows_t = slice(tile * S5_SUB, (tile + 1) * S5_SUB)
        hr, hi = pr * hr - pi * hi + s_re[rows_t, :], pr * hi + pi * hr + s_im[rows_t, :]
        s_re[rows_t, :] = hr
        s_im[rows_t, :] = hi
    hfin_ref[0:1, :] = hr[S5_SUB - 1:S5_SUB, :]
    hfin_ref[1:2, :] = hi[S5_SUB - 1:S5_SUB, :]
    hb_re = shifted(s_re[...], 1).astype(BF16)
    hb_im = shifted(s_im[...], 1).astype(BF16)

    nq = 4
    qw = x.shape[1] // nq
    nt_dims = (((1,), (1,)), ((), ()))
    for j in range(nq):
        cols = slice(j * qw, (j + 1) * qw)
        y_ref[:, cols] = (lax.dot_general(hb_re, pc_re[cols, :], nt_dims, preferred_element_type=F32)
                          + lax.dot_general(hb_im, pc_im[cols, :], nt_dims, preferred_element_type=F32)
                          + jnp.dot(x_ref[:, :(j + 1) * qw], mt[:(j + 1) * qw, cols],
                                    preferred_element_type=F32)).astype(y_ref.dtype)


def _s5_prompt_call(xf, kd, pbr, pbi, pcr, pci, a16):
    rows = xf.shape[0]
    ow = S5_T * S5_OLANES
    grp = lambda a: pl.BlockSpec((None,) + a.shape[1:], lambda i: (i,) + (0,) * (a.ndim - 1))
    cols = pl.BlockSpec((rows, ow), lambda i: (0, i))
    return pl.pallas_call(
        _s5_prompt_body,
        grid=(S5_OCTETS,),
        in_specs=[cols, grp(kd), grp(pbr), grp(pbi), grp(pcr), grp(pci), grp(a16)],
        out_specs=[cols, pl.BlockSpec((None, 2, S5_OSTATE), lambda i: (i, 0, 0))],
        out_shape=[jax.ShapeDtypeStruct((rows, S5_OCTETS * ow), BF16),
                   jax.ShapeDtypeStruct((S5_OCTETS, 2, S5_OSTATE), F32)],
        scratch_shapes=[pltpu.VMEM((ow, ow), BF16)]
        + [pltpu.VMEM((ow, S5_OSTATE), BF16)] * 4
        + [pltpu.VMEM((rows, S5_OSTATE), F32)] * 2,
        compiler_params=_cparams("arbitrary"),
        name="s5_prompt",
    )(xf, kd, pbr, pbi, pcr, pci, a16)


def _s5_sample_body(xa_ref, h0r_ref, h0i_ref, ar_ref, ai_ref, bdr_ref, bdi_ref, cdr_ref, cdi_ref,
                    yc_ref, hr_ref, hi_ref):
    u = xa_ref[...].astype(BF16)
    ar = ar_ref[...]
    ai = ai_ref[...]
    h0r = h0r_ref[...]
    h0i = h0i_ref[...]
    hr = ar * h0r - ai * h0i + jnp.dot(u, bdr_ref[...], preferred_element_type=F32)
    hi = ar * h0i + ai * h0r + jnp.dot(u, bdi_ref[...], preferred_element_type=F32)
    hr_ref[...] = hr
    hi_ref[...] = hi
    yc_ref[...] = (jnp.dot(hr.astype(BF16), cdr_ref[...], preferred_element_type=F32)
                   - jnp.dot(hi.astype(BF16), cdi_ref[...], preferred_element_type=F32))


def _s5_sample_call(xa, h0r, h0i, ar, ai, bdr, bdi, cdr, cdi):
    rows = xa.shape[0]
    n = S5_GROUPS * S5_STATE
    args = (xa, h0r, h0i, ar, ai, bdr, bdi, cdr, cdi)
    return pl.pallas_call(
        _s5_sample_body,
        grid=(1,),
        in_specs=[_const_spec(a.shape) for a in args],
        out_specs=[_const_spec((rows, S5_WIDTH)), _const_spec((rows, n)), _const_spec((rows, n))],
        out_shape=[jax.ShapeDtypeStruct((rows, S5_WIDTH), F32),
                   jax.ShapeDtypeStruct((rows, n), F32),
                   jax.ShapeDtypeStruct((rows, n), F32)],
        compiler_params=_cparams("arbitrary"),
        name="s5_sample",
    )(*args)


def _even_post_body(folded, tm, x_ref, yc_ref, xa_ref, bout_ref, mod_ref, d_ref, wglu_ref, bglu_ref, wout_ref,
                    o_ref, *scratch):
    if folded:
        stage = scratch[0]
        per = tm // S5_T
        for ov in range(S5_OCTETS):
            for t in range(S5_T):
                c0 = (ov * S5_T + t) * S5_OLANES
                stage[ov, pl.ds(t, per, stride=S5_T), :] = yc_ref[:, c0:c0 + S5_OLANES].astype(F32)
        yc = jnp.concatenate([stage[ov] for ov in range(S5_OCTETS)], axis=1)
    else:
        yc = yc_ref[...]
    y = _gelu(yc + d_ref[...] * xa_ref[...])
    gate = jax.nn.sigmoid(jnp.dot(y.astype(BF16), wglu_ref[...], preferred_element_type=F32) + bglu_ref[...])
    a_out = (y * gate).astype(BF16)
    mix = (jnp.dot(a_out, wout_ref[0:S5_WIDTH, :], preferred_element_type=F32)
           + jnp.dot(bout_ref[...], wout_ref[S5_WIDTH:, :], preferred_element_type=F32))
    o_ref[...] = x_ref[...] + mod_ref[:, 2 * D_MODEL:3 * D_MODEL] * mix


def _even_post_call(x, yc, xa, bout, mod, d, wglu, bglu, wout):
    rows = x.shape[0]
    tm = min(rows, ROW_TILE)
    row_spec = lambda w: pl.BlockSpec((tm, w), lambda i: (i, 0))
    folded = yc.shape[0] != rows
    yc_spec = pl.BlockSpec((tm // S5_T, S5_T * S5_WIDTH), lambda i: (i, 0)) if folded else row_spec(S5_WIDTH)
    scratch = [pltpu.VMEM((S5_OCTETS, tm, S5_OLANES), F32)] if folded else []
    return pl.pallas_call(
        functools.partial(_even_post_body, folded, tm),
        grid=(rows // tm,),
        scratch_shapes=scratch,
        in_specs=[row_spec(D_MODEL), yc_spec, row_spec(S5_WIDTH), row_spec(SGU_WIDTH),
                  _const_spec(mod.shape), _const_spec(d.shape), _const_spec(wglu.shape),
                  _const_spec(bglu.shape), _weight_spec(wout.shape)],
        out_specs=row_spec(D_MODEL),
        out_shape=jax.ShapeDtypeStruct((rows, D_MODEL), F32),
        compiler_params=_cparams("arbitrary"),
        name="even_post",
    )(x, yc, xa, bout, mod, d, wglu, bglu, wout)


def _ffn_body(sample, final, tm, *refs):
    refs = list(refs)
    x_ref, mod_ref, ng_ref, wup_ref, cw_ref, cb_ref, wdn_ref = refs[:7]
    pos = 7
    if sample:
        p2_ref, p1_ref = refs[pos:pos + 2]
        pos += 2
    if final:
        fg_ref = refs[pos]
        pos += 1
    o_ref, conv_ref = refs[pos:pos + 2]
    pos += 2
    if not sample:
        carry_ref = refs[pos]

        @pl.when(pl.program_id(0) == 0)
        def _():
            carry_ref[...] = jnp.zeros_like(carry_ref)

    x = x_ref[...]
    h = _mod_norm(x, ng_ref[...], mod_ref[:, 3 * D_MODEL:4 * D_MODEL], mod_ref[:, 4 * D_MODEL:5 * D_MODEL])
    hb = h.astype(BF16)
    acc = jnp.zeros((tm, D_MODEL), F32)
    if not sample:
        row = lax.broadcasted_iota(jnp.int32, (tm, 1), 0)
    for c0, cw in FF_CHUNKS:
        cols = slice(c0, c0 + cw)
        a = jnp.dot(hb, wup_ref[:, cols], preferred_element_type=F32)
        g = jnp.dot(hb, wup_ref[:, D_FF + c0:D_FF + c0 + cw], preferred_element_type=F32)
        if sample:
            am2 = p2_ref[:, cols]
            am1 = p1_ref[:, cols]
            conv_ref[:, cols] = a
        else:
            prev2 = carry_ref[0:1, cols]
            prev1 = carry_ref[1:2, cols]
            am1 = jnp.where(row == 0, prev1, pltpu.roll(a, 1, 0))
            am2 = jnp.where(row == 0, prev2, jnp.where(row == 1, prev1, pltpu.roll(a, 2, 0)))
            carry_ref[0:2, cols] = a[tm - 2:tm, :]
        y = cb_ref[:, cols] + cw_ref[0:1, cols] * am2 + cw_ref[1:2, cols] * am1 + cw_ref[2:3, cols] * a
        act = (_gelu(y) * g).astype(BF16)
        acc = acc + jnp.dot(act, wdn_ref[cols, :], preferred_element_type=F32)
    out = x + mod_ref[:, 5 * D_MODEL:6 * D_MODEL] * acc
    if final:
        ms = jnp.mean(out * out, axis=-1, keepdims=True)
        out = out * lax.rsqrt(ms + EPS) * fg_ref[...]
    o_ref[...] = out
    if not sample:
        conv_ref[...] = carry_ref[0:2, :]


def _ffn_call(sample, x, mod, ng, wup, cw, cb, wdn, prev=None, final_g=None):
    rows = x.shape[0]
    tm = rows if sample else ROW_TILE
    final = final_g is not None
    row_spec = lambda w: pl.BlockSpec((tm, w), lambda i: (i, 0))
    args = [x, mod, ng, wup, cw, cb, wdn]
    in_specs = [row_spec(D_MODEL), _const_spec(mod.shape), _const_spec(ng.shape), _weight_spec(wup.shape),
                _const_spec(cw.shape), _const_spec(cb.shape), _weight_spec(wdn.shape)]
    if sample:
        args += [prev[0], prev[1]]
        in_specs += [_const_spec(prev[0].shape), _const_spec(prev[1].shape)]
    if final:
        args.append(final_g)
        in_specs.append(_const_spec(final_g.shape))
    conv_rows = rows if sample else 2
    return pl.pallas_call(
        functools.partial(_ffn_body, sample, final, tm),
        grid=(rows // tm,),
        in_specs=in_specs,
        out_specs=[row_spec(D_MODEL), _const_spec((conv_rows, D_FF))],
        out_shape=[jax.ShapeDtypeStruct((rows, D_MODEL), F32),
                   jax.ShapeDtypeStruct((conv_rows, D_FF), F32)],
        scratch_shapes=[] if sample else [pltpu.VMEM((8, D_FF), F32)],
        compiler_params=_cparams("arbitrary"),
        name="ffn_sample" if sample else "ffn_prompt",
    )(*args)


def _odd_pre_body(planar, tm, x_ref, mod_ref, ng_ref, wqkv_ref, tile_trig_ref, row_trig_ref,
                  q_ref, k_ref, v_ref, k32_ref, v32_ref, *rest):
    h = _mod_norm(x_ref[...], ng_ref[...], mod_ref[:, 0:D_MODEL], mod_ref[:, D_MODEL:2 * D_MODEL])
    qkv = jnp.dot(h.astype(BF16), wqkv_ref[...], preferred_element_type=F32)
    tt = tile_trig_ref[...]
    cos_sum = lambda f: tt[2 * f:2 * f + 1] * row_trig_ref[2 * f] - tt[2 * f + 1:2 * f + 2] * row_trig_ref[2 * f + 1]
    sin_sum = lambda f: tt[2 * f + 1:2 * f + 2] * row_trig_ref[2 * f] + tt[2 * f:2 * f + 1] * row_trig_ref[2 * f + 1]
    rc = cos_sum(0)
    ra = sin_sum(1)
    rb = -sin_sum(2)
    lanes = ROPE_LANES
    half = ROT_DIM // 2

    def rope(t):
        return t * rc + pltpu.roll(t, half, 1) * ra + pltpu.roll(t, lanes - half, 1) * rb

    nblk = D_MODEL // lanes
    if planar:
        qpl_ref, kpl_ref, vpl_ref, stage = rest
        per = tm // PLANES

        def to_planes(dst_ref, slot, cols, val):
            stage[slot] = val
            for r in range(PLANES):
                dst_ref[r, :, cols] = stage[slot, pl.ds(r, per, stride=PLANES), :].astype(BF16)

    for j in range(nblk):
        cols = slice(j * lanes, (j + 1) * lanes)
        q = rope(qkv[:, j * lanes:(j + 1) * lanes]) * (HEAD_DIM ** -0.5)
        k = rope(qkv[:, D_MODEL + j * lanes:D_MODEL + (j + 1) * lanes])
        v = qkv[:, 2 * D_MODEL + j * lanes:2 * D_MODEL + (j + 1) * lanes]
        q_ref[:, cols] = q.astype(BF16)
        k_ref[:, cols] = k.astype(BF16)
        v_ref[:, cols] = v.astype(BF16)
        k32_ref[:, cols] = k
        v32_ref[:, cols] = v
        if planar:
            to_planes(qpl_ref, j, cols, q)
            to_planes(kpl_ref, nblk + j, cols, k)
            to_planes(vpl_ref, 2 * nblk + j, cols, v)


def _odd_pre_call(x, mod, ng, wqkv, tile_trig, row_trig, keep, planar):
    rows = x.shape[0]
    tm = min(rows, ROW_TILE)
    nt = rows // tm
    first_kept = (rows - keep) // tm
    row_spec = lambda w: pl.BlockSpec((tm, w), lambda i: (i, 0))
    keep_spec = pl.BlockSpec((tm, D_MODEL), lambda i: (jnp.maximum(i - first_kept, 0), 0))
    out_specs = [row_spec(D_MODEL), row_spec(D_MODEL), row_spec(D_MODEL), keep_spec, keep_spec]
    out_shape = ([jax.ShapeDtypeStruct((rows, D_MODEL), BF16)] * 3
                 + [jax.ShapeDtypeStruct((keep, D_MODEL), F32)] * 2)
    scratch = []
    if planar:
        plane_spec = pl.BlockSpec((PLANES, tm // PLANES, D_MODEL), lambda i: (0, i, 0))
        out_specs += [plane_spec] * 3
        out_shape += [jax.ShapeDtypeStruct((PLANES, rows // PLANES, D_MODEL), BF16)] * 3
        scratch = [pltpu.VMEM((3 * D_MODEL // ROPE_LANES, tm, ROPE_LANES), F32)]
    return pl.pallas_call(
        functools.partial(_odd_pre_body, planar, tm),
        grid=(nt,),
        in_specs=[row_spec(D_MODEL), _const_spec(mod.shape), _const_spec(ng.shape), _weight_spec(wqkv.shape),
                  pl.BlockSpec((None,) + tile_trig.shape[1:], lambda i: (i, 0, 0)), _const_spec(row_trig.shape)],
        out_specs=out_specs,
        out_shape=out_shape,
        scratch_shapes=scratch,
        compiler_params=_cparams("arbitrary"),
        name="odd_pre",
    )(x, mod, ng, wqkv, tile_trig, row_trig)


def _attn_prompt_body(n_planes, q_ref, kp_ref, kc_ref, vp_ref, vc_ref, o_ref, st_ref):
    blk = pl.program_id(1)
    per = BAND // n_planes
    qi = lax.broadcasted_iota(jnp.int32, (BAND, 2 * BAND), 0)
    kj = lax.broadcasted_iota(jnp.int32, (BAND, 2 * BAND), 1)
    q_pos = n_planes * (qi % per) + qi // per
    k_half = kj // BAND
    k_pos = n_planes * (per * k_half + kj % per) + (kj % BAND) // per - BAND
    dist = q_pos - k_pos
    valid = (dist >= 0) & (dist <= BAND) & (k_half >= jnp.where(blk == 0, 1, 0))
    bias = jnp.where(valid, 0.0, NEG_INF).astype(F32)
    lane = lax.broadcasted_iota(jnp.int32, (BAND, 2 * HEAD_DIM), 1)
    low_half = lane < HEAD_DIM
    lane_row = lax.broadcasted_iota(jnp.int32, (1, 2 * HEAD_DIM), 1)
    head_keep = [jnp.where(lane_row < HEAD_DIM, 1.0, 0.0).astype(BF16),
                 jnp.where(lane_row < HEAD_DIM, 0.0, 1.0).astype(BF16)]
    st_out = jnp.zeros((BAND, STAT_LANES), F32)
    st_lane = lax.broadcasted_iota(jnp.int32, (BAND, STAT_LANES), 1)

    def rows_of(ref, cols):
        t = ref[:, cols] if n_planes == 1 else ref[:, :, cols]
        return t.reshape(BAND, t.shape[-1])

    for pair in range(N_HEADS // 2):
        cols = slice(pair * 2 * HEAD_DIM, (pair + 1) * 2 * HEAD_DIM)
        qp = rows_of(q_ref, cols)
        kk = jnp.concatenate([rows_of(kp_ref, cols), rows_of(kc_ref, cols)], axis=0)
        vv = jnp.concatenate([rows_of(vp_ref, cols), rows_of(vc_ref, cols)], axis=0)
        halves = []
        for sub in range(2):
            head = 2 * pair + sub
            qm = qp * head_keep[sub]
            s = lax.dot_general(qm, kk, (((1,), (1,)), ((), ())), preferred_element_type=F32) + bias
            m = jnp.max(s, axis=-1, keepdims=True)
            p = jnp.exp(s - m)
            l = jnp.sum(p, axis=-1, keepdims=True)
            halves.append(jnp.dot(p.astype(BF16), vv, preferred_element_type=F32) / l)
            st_out = jnp.where(st_lane == head, m, st_out)
            st_out = jnp.where(st_lane == N_HEADS + head, l, st_out)
        o_pair = jnp.where(low_half, halves[0], halves[1]).astype(o_ref.dtype)
        if n_planes == 1:
            o_ref[:, cols] = o_pair
        else:
            o_ref[:, :, cols] = o_pair.reshape(n_planes, per, 2 * HEAD_DIM)
    st_ref[...] = st_out.reshape(st_ref.shape)


def _attn_prompt_call(d, q, k, v):
    planes, rpp, width = q.shape
    if planes // d == 1 or d == 1:
        n_planes, outer = 1, planes
        view = lambda a: a
        blk = lambda w: (None, BAND, w)
        cur = lambda w: pl.BlockSpec(blk(w), lambda r, b: (r, b, 0))
        prev = lambda w: pl.BlockSpec(blk(w), lambda r, b: (r, jnp.maximum(b - 1, 0), 0))
        nb = rpp // BAND
    else:
        n_planes, outer = planes // d, d
        per = BAND // n_planes
        view = lambda a: a.reshape(n_planes, outer, rpp, a.shape[-1])
        blk = lambda w: (n_planes, None, per, w)
        cur = lambda w: pl.BlockSpec(blk(w), lambda r, b: (0, r, b, 0))
        prev = lambda w: pl.BlockSpec(blk(w), lambda r, b: (0, r, jnp.maximum(b - 1, 0), 0))
        nb = rpp // per
    qv, kv, vv = view(q), view(k), view(v)
    o, st = pl.pallas_call(
        functools.partial(_attn_prompt_body, n_planes),
        grid=(outer, nb),
        in_specs=[cur(width), prev(width), cur(width), prev(width), cur(width)],
        out_specs=[cur(width), cur(STAT_LANES)],
        out_shape=[jax.ShapeDtypeStruct(qv.shape, BF16),
                   jax.ShapeDtypeStruct(qv.shape[:-1] + (STAT_LANES,), F32)],
        compiler_params=_cparams("arbitrary", "arbitrary"),
        name="attn_prompt_d%d" % d,
    )(qv, kv, kv, vv, vv)
    return o.reshape(planes, rpp, width), st.reshape(planes, rpp, STAT_LANES)


def _attn_sample_body(hb, past, q_ref, kn_ref, vn_ref, kt_ref, vt_ref, o_ref):
    rows, news = [], []
    for h in range(hb):
        qc = q_ref[h]
        rows.append(jnp.sum(kt_ref[h] * qc, axis=0, keepdims=True))
        news.append(jnp.sum(kn_ref[h] * qc, axis=0, keepdims=True))
    s = jnp.concatenate(rows, axis=0)
    s_new = jnp.concatenate(news, axis=0)
    r = lax.broadcasted_iota(jnp.int32, (1, past), 1)
    ms, ls, ps, pns = [], [], [], []
    for window, d in DIL_BRANCHES:
        member = (r >= past - window) & ((past - r) % d == 0)
        sg = s + jnp.where(member, 0.0, NEG_INF).astype(F32)
        m = jnp.maximum(jnp.max(sg, axis=1, keepdims=True), s_new)
        p = jnp.exp(sg - m)
        pn = jnp.exp(s_new - m)
        ms.append(m)
        ps.append(p)
        pns.append(pn)
        ls.append(jnp.sum(p, axis=1, keepdims=True) + pn)
    m_all = jnp.maximum(jnp.maximum(ms[0], ms[1]), ms[2])
    cs = [jnp.exp(m - m_all) for m in ms]
    tot = cs[0] * ls[0] + cs[1] * ls[1] + cs[2] * ls[2]
    w = (cs[0] * ps[0] + cs[1] * ps[1] + cs[2] * ps[2]) / tot
    w_new = (cs[0] * pns[0] + cs[1] * pns[1] + cs[2] * pns[2]) / tot
    for h in range(hb):
        o_ref[h] = (jnp.sum(vt_ref[h] * w[h:h + 1, :], axis=1, keepdims=True)
                    + vn_ref[h] * w_new[h:h + 1, :])


def _attn_sample_call(q, k_new, v_new, kt, vt):
    bsz, heads, dim, past = kt.shape
    hb = 8
    col = pl.BlockSpec((None, hb, dim, 1), lambda b, j: (b, j, 0, 0))
    cache = pl.BlockSpec((None, hb, dim, past), lambda b, j: (b, j, 0, 0))
    return pl.pallas_call(
        functools.partial(_attn_sample_body, hb, past),
        grid=(bsz, heads // hb),
        in_specs=[col, col, col, cache, cache],
        out_specs=col,
        out_shape=jax.ShapeDtypeStruct((bsz, heads, dim, 1), F32),
        compiler_params=_cparams("arbitrary", "arbitrary"),
        name="attn_sample",
    )(q, k_new, v_new, kt, vt)


def _odd_post_body(x_ref, att_ref, mod_ref, wo_ref, o_ref):
    mix = jnp.dot(att_ref[...].astype(BF16), wo_ref[...], preferred_element_type=F32)
    o_ref[...] = x_ref[...] + mod_ref[:, 2 * D_MODEL:3 * D_MODEL] * mix


def _odd_post_call(x, att, mod, wo):
    rows = x.shape[0]
    tm = min(rows, ROW_TILE)
    row_spec = pl.BlockSpec((tm, D_MODEL), lambda i: (i, 0))
    return pl.pallas_call(
        _odd_post_body,
        grid=(rows // tm,),
        in_specs=[row_spec, row_spec, _const_spec(mod.shape), _weight_spec(wo.shape)],
        out_specs=row_spec,
        out_shape=jax.ShapeDtypeStruct((rows, D_MODEL), F32),
        compiler_params=_cparams("arbitrary"),
        name="odd_post",
    )(x, att, mod, wo)


def _odd_post_merge_body(tm, x_ref, o1_ref, s1_ref, o4_ref, s4_ref, o16_ref, s16_ref, mod_ref, wo_ref,
                         spread_ref, o_ref, ob4, ob16, sb4, sb16):
    per = tm // PLANES
    nblk = D_MODEL // STAT_LANES
    for r in range(PLANES):
        rows_r = pl.ds(r, per, stride=PLANES)
        for j in range(nblk):
            cols = slice(j * STAT_LANES, (j + 1) * STAT_LANES)
            ob4[j, rows_r, :] = o4_ref[r, :, cols].astype(F32)
            ob16[j, rows_r, :] = o16_ref[r, :, cols].astype(F32)
        sb4[rows_r, :] = s4_ref[r]
        sb16[rows_r, :] = s16_ref[r]
    stats = [s1_ref[...], sb4[...], sb16[...]]
    dens = [pltpu.roll(s, STAT_LANES - N_HEADS, 1) for s in stats]
    m_all = jnp.maximum(jnp.maximum(stats[0], stats[1]), stats[2])
    ws = [d * jnp.exp(s - m_all) for s, d in zip(stats, dens)]
    tot = ws[0] + ws[1] + ws[2]
    lane = lax.broadcasted_iota(jnp.int32, (tm, STAT_LANES), 1)
    spreads = []
    for w in ws:
        coef = jnp.where(lane < N_HEADS, w / tot, 0.0)
        hi = coef.astype(BF16)
        lo = (coef - hi.astype(F32)).astype(BF16)
        spreads.append(jnp.dot(jnp.concatenate([hi, lo], axis=1), spread_ref[...], preferred_element_type=F32))
    pieces = []
    for j in range(nblk):
        cols = slice(j * STAT_LANES, (j + 1) * STAT_LANES)
        pieces.append((spreads[0][:, cols] * o1_ref[:, cols].astype(F32) + spreads[1][:, cols] * ob4[j]
                       + spreads[2][:, cols] * ob16[j]).astype(BF16))
    att = jnp.concatenate(pieces, axis=1)
    mix = jnp.dot(att, wo_ref[...], preferred_element_type=F32)
    o_ref[...] = x_ref[...] + mod_ref[:, 2 * D_MODEL:3 * D_MODEL] * mix


def _odd_post_merge_call(x, branches, mod, wo):
    rows = x.shape[0]
    tm = ROW_TILE
    (o1, s1), (o4, s4), (o16, s16) = branches
    row_spec = pl.BlockSpec((tm, D_MODEL), lambda i: (i, 0))
    nat = lambda w: pl.BlockSpec((None, tm, w), lambda i: (0, i, 0))
    pln = lambda w: pl.BlockSpec((PLANES, tm // PLANES, w), lambda i: (0, i, 0))
    head_of_lane = jnp.arange(D_MODEL) // HEAD_DIM
    spread = (jnp.arange(STAT_LANES)[:, None] == head_of_lane[None, :]).astype(BF16)
    spread = jnp.concatenate([spread, spread], axis=0)
    return pl.pallas_call(
        functools.partial(_odd_post_merge_body, tm),
        grid=(rows // tm,),
        in_specs=[row_spec, nat(D_MODEL), nat(STAT_LANES), pln(D_MODEL), pln(STAT_LANES),
                  pln(D_MODEL), pln(STAT_LANES), _const_spec(mod.shape), _weight_spec(wo.shape),
                  _const_spec(spread.shape)],
        out_specs=row_spec,
        out_shape=jax.ShapeDtypeStruct((rows, D_MODEL), F32),
        scratch_shapes=[pltpu.VMEM((D_MODEL // STAT_LANES, tm, STAT_LANES), F32),
                        pltpu.VMEM((D_MODEL // STAT_LANES, tm, STAT_LANES), F32),
                        pltpu.VMEM((tm, STAT_LANES), F32), pltpu.VMEM((tm, STAT_LANES), F32)],
        compiler_params=_cparams("arbitrary"),
        name="odd_post_merge",
    )(x, o1, s1, o4, s4, o16, s16, mod, wo, spread)


def _s5_tables(lam_re, lam_im, log_dt, b_re, b_im, c_re, c_im):
    hp = lax.Precision.HIGHEST
    dt = jnp.exp(log_dt)[:, None]
    lr, li = lam_re, lam_im
    ks = jnp.arange(S5_T + 1, dtype=F32)[:, None, None]
    mag = jnp.exp(ks * (lr * dt))
    pw_r = mag * jnp.cos(ks * (li * dt))
    pw_i = mag * jnp.sin(ks * (li * dt))
    ar, ai = pw_r[1], pw_i[1]
    den = lr * lr + li * li
    fr = ((ar - 1.0) * lr + ai * li) / den
    fi = (ai * lr - (ar - 1.0) * li) / den
    bbr = fr[..., None] * b_re - fi[..., None] * b_im
    bbi = fr[..., None] * b_im + fi[..., None] * b_re
    ca_r = c_re[None] * pw_r[:, :, None, :] - c_im[None] * pw_i[:, :, None, :]
    ca_i = c_re[None] * pw_i[:, :, None, :] + c_im[None] * pw_r[:, :, None, :]
    kern = (jnp.einsum('kgpn,gnq->kgpq', ca_r[:S5_T], bbr, precision=hp)
            - jnp.einsum('kgpn,gnq->kgpq', ca_i[:S5_T], bbi, precision=hp))
    oc, og = S5_OCTETS, S5_OGROUPS
    eye_o = jnp.eye(og, dtype=F32)
    split = lambda a, axis: a.reshape(a.shape[:axis] + (oc, og) + a.shape[axis + 1:])
    kd = (split(kern, 1).transpose(1, 0, 2, 4, 3)[:, :, :, :, None, :]
          * eye_o[None, None, :, None, :, None]).reshape(oc, S5_T, S5_OLANES, S5_OLANES)
    kr = (S5_T - 1) - jnp.arange(S5_T, dtype=F32)[:, None, None]
    rev_mag = jnp.exp(kr * (lr * dt))
    rev_r = rev_mag * jnp.cos(kr * (li * dt))
    rev_i = rev_mag * jnp.sin(kr * (li * dt))
    pb_r = rev_r[..., None] * bbr[None] - rev_i[..., None] * bbi[None]
    pb_i = rev_r[..., None] * bbi[None] + rev_i[..., None] * bbr[None]
    twice = lambda a: jnp.concatenate([a, a], axis=-1)
    fold_b = lambda a: twice(split(a, 1).transpose(1, 0, 2, 4, 3).reshape(oc, S5_T, S5_OLANES, S5_STATE))
    fold_c = lambda a: twice(split(a, 1).transpose(1, 0, 2, 3, 4).reshape(oc, S5_T, S5_OLANES, S5_STATE))
    pb_r, pb_i = fold_b(pb_r), fold_b(pb_i)
    pc_r, pc_i = fold_c(ca_r[1:]), fold_c(-ca_i[1:])
    per_octet = lambda a: a.reshape(oc, S5_OSTATE)
    a16 = jnp.stack([per_octet(pw_r[S5_T]), per_octet(pw_i[S5_T])], axis=1)
    eye = jnp.eye(S5_GROUPS, dtype=F32)
    n_all = S5_GROUPS * S5_STATE
    bd_r = jnp.einsum('gnp,gh->gphn', bbr, eye).reshape(S5_WIDTH, n_all)
    bd_i = jnp.einsum('gnp,gh->gphn', bbi, eye).reshape(S5_WIDTH, n_all)
    cd_r = jnp.einsum('gpn,gh->gnhp', c_re, eye).reshape(n_all, S5_WIDTH)
    cd_i = jnp.einsum('gpn,gh->gnhp', c_im, eye).reshape(n_all, S5_WIDTH)
    return dict(kd=kd.astype(BF16), pb_r=pb_r.astype(BF16), pb_i=pb_i.astype(BF16),
                pc_r=pc_r.astype(BF16), pc_i=pc_i.astype(BF16), a16=a16,
                a_r=ar.reshape(1, n_all), a_i=ai.reshape(1, n_all),
                bd_r=bd_r.astype(BF16), bd_i=bd_i.astype(BF16),
                cd_r=cd_r.astype(BF16), cd_i=cd_i.astype(BF16))


def _rope_tables(tile_pos, row_pos):
    half = ROT_DIM // 2
    inv = jnp.power(ROPE_THETA, -jnp.arange(half, dtype=F32) * 2.0 / ROT_DIM)
    e = jnp.arange(ROPE_LANES) % HEAD_DIM
    inv_e = inv[e % half]
    freqs = jnp.stack([jnp.where(e < ROT_DIM, inv_e, 0.0),
                       jnp.where((e >= half) & (e < ROT_DIM), inv_e, 0.0),
                       jnp.where(e < half, inv_e, 0.0)])

    def trig(pos):
        ang = pos.astype(F32)[None, :, None] * freqs[:, None, :]
        return jnp.stack([jnp.cos(ang), jnp.sin(ang)], axis=1).reshape(6, pos.shape[0], ROPE_LANES)

    tile_trig = jnp.concatenate([trig(tile_pos), jnp.zeros((2, tile_pos.shape[0], ROPE_LANES), F32)], axis=0)
    return tile_trig.transpose(1, 0, 2), trig(row_pos)


def _trunk(sample, x, mods, state, w):
    outs = {}
    mod = mods[0]
    if sample:
        xa, bout, vn = _even_pre_call(True, x, mod, w['ng'][0][0], w['ev_w_in'], w['sg_ln_g'], w['sg_ln_b'],
                                      w['sg_wt'], w['sg_row0'])
        outs['vn'] = vn
        t = w['s5']
        yc, hr, hi = _s5_sample_call(xa, state['s5_re'], state['s5_im'], t['a_r'], t['a_i'],
                                     t['bd_r'], t['bd_i'], t['cd_r'], t['cd_i'])
        outs['s5_re'], outs['s5_im'] = hr, hi
    else:
        xa, bout, xf = _even_pre_call(False, x, mod, w['ng'][0][0], w['ev_w_in'], w['sg_ln_g'], w['sg_ln_b'],
                                      w['sg_wt'], w['sg_bias'])
        t = w['s5']
        yc, hfin = _s5_prompt_call(xf, t['kd'],t['pb_r'], t['pb_i'], t['pc_r'], t['pc_i'], t['a16'])
        outs['s5_re'], outs['s5_im'] = hfin[:, 0], hfin[:, 1]
    x = _even_post_call(x, yc, xa, bout, mod, w['s5_d'], w['s5_w_glu'], w['s5_b_glu'], w['ev_w_out'])
    prev = (state['conv'][0][:, 0], state['conv'][0][:, 1]) if sample else None
    x, conv0 = _ffn_call(sample, x, mod, w['ng'][0][1], w['ffn_w_up'][0], w['ffn_conv_w'][0],
                         w['ffn_conv_b'][0], w['ffn_w_down'][0], prev=prev)
    mod = mods[1]
    rows = x.shape[0]
    keep = rows if sample else min(WIN_MAX, rows)
    pre = _odd_pre_call(x, mod, w['ng'][1][0], w['od_w_qkv'], *w['rope'], keep, planar=not sample)
    q, k, v, k32, v32 = pre[:5]
    outs['k'], outs['v'] = k32, v32
    if sample:
        heads = lambda a: a.astype(F32).reshape(rows, N_HEADS, HEAD_DIM, 1)
        att = _attn_sample_call(heads(q), heads(k32), heads(v32), state['ck'], state['cv'])
        x = _odd_post_call(x, att.reshape(rows, D_MODEL), mod, w['od_w_o'])
    else:
        q_pl, k_pl, v_pl = pre[5:]
        branches = (_attn_prompt_call(1, q[None], k[None], v[None]),
                    _attn_prompt_call(4, q_pl, k_pl, v_pl),
                    _attn_prompt_call(16, q_pl, k_pl, v_pl))
        x = _odd_post_merge_call(x, branches, mod, w['od_w_o'])
    prev = (state['conv'][1][:, 0], state['conv'][1][:, 1]) if sample else None
    y, conv1 = _ffn_call(sample, x, mod, w['ng'][1][1], w['ffn_w_up'][1], w['ffn_conv_w'][1],
                         w['ffn_conv_b'][1], w['ffn_w_down'][1], prev=prev, final_g=w['final_g'])
    outs['y'] = y
    outs['conv'] = (conv0, conv1)
    return outs


def kernel(x_prompt, x_sample, c_prompt, c_sample, state_s5_re, state_s5_im, cache_c_k, cache_c_v,
           state_ffn_conv, ada_w, ada_b, norm_g, final_g, ev_w_in, ev_w_out, s5_lam_re, s5_lam_im,
           s5_log_dt, s5_b_re, s5_b_im, s5_c_re, s5_c_im, s5_d, s5_w_glu, s5_b_glu, sg_ln_g, sg_ln_b,
           sg_w, sg_b, od_w_qkv, od_w_o, ffn_w_up, ffn_conv_w, ffn_conv_b, ffn_w_down):
    bp, seq, _ = x_prompt.shape
    bs = x_sample.shape[0]
    assert bp == 1 and seq == SEQ and bs == DEC_BATCH and x_sample.shape[1] == 1

    c_all = jnp.concatenate([c_sample, c_prompt, jnp.zeros((MOD_ROWS - bs - bp, D_MODEL), F32)], axis=0)
    mod_all = _ada_call(c_all, ada_w, ada_b)
    mods_s = [mod_all[l, :bs] for l in range(2)]
    mods_p = [mod_all[l, bs:bs + 1] for l in range(2)]

    hd = SGU_WIDTH // SGU_HEADS
    causal = jnp.tril(jnp.ones((CHUNK, CHUNK), F32))
    w = dict(
        ng=[[norm_g[l, j].reshape(1, D_MODEL) for j in range(2)] for l in range(2)],
        final_g=final_g.reshape(1, D_MODEL),
        ev_w_in=ev_w_in[0].astype(BF16), ev_w_out=ev_w_out[0].astype(BF16),
        sg_ln_g=sg_ln_g[0].reshape(1, SGU_WIDTH), sg_ln_b=sg_ln_b[0].reshape(1, SGU_WIDTH),
        sg_wt=(sg_w[0] * causal[None]).astype(BF16),
        sg_bias=jnp.repeat(sg_b[0].T, hd, axis=1),
        sg_row0=jnp.stack([jnp.repeat(sg_w[0, :, 0, 0], hd), jnp.repeat(sg_b[0, :, 0], hd)], axis=0),
        s5=_s5_tables(s5_lam_re[0], s5_lam_im[0], s5_log_dt[0], s5_b_re[0], s5_b_im[0], s5_c_re[0], s5_c_im[0]),
        s5_d=s5_d[0].reshape(1, S5_WIDTH), s5_w_glu=s5_w_glu[0].astype(BF16),
        s5_b_glu=s5_b_glu[0].reshape(1, S5_WIDTH),
        od_w_qkv=od_w_qkv[0].astype(BF16), od_w_o=od_w_o[0].astype(BF16),
        ffn_w_up=ffn_w_up.astype(BF16), ffn_w_down=ffn_w_down.astype(BF16),
        ffn_conv_w=ffn_conv_w, ffn_conv_b=ffn_conv_b.reshape(2, 1, D_FF),
    )

    wp = dict(w, rope=_rope_tables(jnp.arange(0, seq, ROW_TILE, dtype=jnp.int32),
                                   jnp.arange(ROW_TILE, dtype=jnp.int32)))
    ws = dict(w, rope=_rope_tables(jnp.full((1,), PAST_LEN, jnp.int32), jnp.zeros((bs,), jnp.int32)))

    p = _trunk(False, x_prompt[0], mods_p, None, wp)
    n_all = S5_GROUPS * S5_STATE
    state = dict(s5_re=state_s5_re[0].reshape(bs, n_all), s5_im=state_s5_im[0].reshape(bs, n_all),
                 ck=jnp.transpose(cache_c_k[0], (0, 2, 3, 1)), cv=jnp.transpose(cache_c_v[0], (0, 2, 3, 1)),
                 conv=state_ffn_conv)
    s = _trunk(True, x_sample[:, 0], mods_s, state, ws)

    keep = min(WIN_MAX, seq)
    kv_p = lambda a: a.reshape(1, 1, keep, N_HEADS, HEAD_DIM)
    kv_s = lambda a: a.reshape(1, bs, 1, N_HEADS, HEAD_DIM)
    s5_p = lambda a: a.reshape(1, 1, S5_GROUPS, S5_STATE)
    s5_s = lambda a: a.reshape(1, bs, S5_GROUPS, S5_STATE)
    conv_p = jnp.stack([c.reshape(1, 2, D_FF) for c in p['conv']])
    conv_s = jnp.stack([jnp.stack([state_ffn_conv[l][:, 1], s['conv'][l]], axis=1) for l in range(2)])
    return (p['y'][None], s['y'][:, None], s5_p(p['s5_re']), s5_p(p['s5_im']),
            s5_s(s['s5_re']), s5_s(s['s5_im']), s['vn'].reshape(1, bs, 1, SGU_WIDTH),
            kv_p(p['k']), kv_p(p['v']), kv_s(s['k']), kv_s(s['v']), conv_p, conv_s)
```

```python
import functools

import jax
import jax.numpy as jnp
from jax import lax
from jax.experimental import pallas as pl
from jax.experimental.pallas import tpu as pltpu

F32 = jnp.float32
BF16 = jnp.bfloat16

D_MODEL = 1024
SEQ = 16384
DEC_BATCH = 32
PAST_LEN = 16384
S5_WIDTH = 512
S5_GROUP = 16
S5_GROUPS = 32
S5_STATE = 64
SGU_WIDTH = 512
SGU_HEADS = 4
CHUNK = 128
EVEN_IN = S5_WIDTH + 2 * SGU_WIDTH
HEAD_DIM = 64
N_HEADS = 16
ROT_DIM = 16
ROPE_THETA = 500000.0
DIL_BRANCHES = ((128, 1), (512, 4), (2048, 16))
BAND = 128
WIN_MAX = 2048
D_FF = 2816
EPS = 1e-6
NEG_INF = -1e30

ROW_TILE = 512
MOD_ROWS = 40
S5_T = 16
S5_SUB = 8
S5_OLANES = 128
S5_OCTETS = S5_WIDTH // S5_OLANES
S5_OGROUPS = S5_OLANES // S5_GROUP
S5_OSTATE = S5_OGROUPS * S5_STATE
FF_CHUNKS = ((0, 1024), (1024, 1024), (2048, 768))
ROPE_LANES = 128
PLANES = 16
STAT_LANES = 128
VMEM_LIMIT = 56 * 1024 * 1024


def _cparams(*sem):
    return pltpu.CompilerParams(dimension_semantics=sem, vmem_limit_bytes=VMEM_LIMIT)


def _const_spec(shape):
    nd = len(shape)
    return pl.BlockSpec(shape, lambda *_: (0,) * nd)


def _weight_spec(shape):
    nd = len(shape)
    return pl.BlockSpec(shape, lambda *_: (0,) * nd, pipeline_mode=pl.Buffered(1))


def _gelu(x):
    return jax.nn.gelu(x)


def _mod_norm(x, ng, shift, scale):
    ms = jnp.mean(x * x, axis=-1, keepdims=True)
    return (x * lax.rsqrt(ms + EPS) * ng) * (1.0 + scale) + shift


def _ada_body(c_ref, w_ref, b_ref, o_ref):
    c = c_ref[...]
    s = c * jax.nn.sigmoid(c)
    o_ref[...] = jnp.dot(s.astype(BF16), w_ref[...].astype(BF16),
                         preferred_element_type=F32) + b_ref[...]


def _ada_call(c_all, ada_w, ada_b):
    depth = ada_w.shape[0]
    nt = 1536
    return pl.pallas_call(
        _ada_body,
        grid=(depth, 6 * D_MODEL // nt),
        in_specs=[
            pl.BlockSpec((MOD_ROWS, D_MODEL), lambda l, j: (0, 0)),
            pl.BlockSpec((None, D_MODEL, nt), lambda l, j: (l, 0, j)),
            pl.BlockSpec((None, 1, nt), lambda l, j: (l, 0, j)),
        ],
        out_specs=pl.BlockSpec((None, MOD_ROWS, nt), lambda l, j: (l, 0, j)),
        out_shape=jax.ShapeDtypeStruct((depth, MOD_ROWS, 6 * D_MODEL), F32),
        compiler_params=_cparams("arbitrary", "arbitrary"),
        name="ada_mod",
    )(c_all, ada_w, ada_b.reshape(depth, 1, 6 * D_MODEL))


def _even_pre_body(sample, tm, x_ref, mod_ref, ng_ref, win_ref, lng_ref, lnb_ref, wt_ref, bs_ref,
                   xa_ref, bout_ref, *vn_out):
    h = _mod_norm(x_ref[...], ng_ref[...], mod_ref[:, 0:D_MODEL], mod_ref[:, D_MODEL:2 * D_MODEL])
    proj = jnp.dot(h.astype(BF16), win_ref[...], preferred_element_type=F32)
    xa_ref[...] = proj[:, :S5_WIDTH]
    u = _gelu(proj[:, S5_WIDTH:S5_WIDTH + SGU_WIDTH])
    v = _gelu(proj[:, S5_WIDTH + SGU_WIDTH:])
    mu = jnp.mean(v, axis=-1, keepdims=True)
    var = jnp.mean(jnp.square(v - mu), axis=-1, keepdims=True)
    vn = (v - mu) * lax.rsqrt(var + EPS) * lng_ref[...] + lnb_ref[...]
    if sample:
        vn_out[0][...] = vn
        bout_ref[...] = (u * (vn * bs_ref[0:1, :] + bs_ref[1:2, :])).astype(BF16)
    else:
        vnb = vn.astype(BF16)
        hd = SGU_WIDTH // SGU_HEADS
        for ci in range(tm // CHUNK):
            rows = slice(ci * CHUNK, (ci + 1) * CHUNK)
            for hh in range(SGU_HEADS):
                cols = slice(hh * hd, (hh + 1) * hd)
                s = jnp.dot(wt_ref[hh], vnb[rows, cols], preferred_element_type=F32) + bs_ref[:, cols]
                bout_ref[rows, cols] = (u[rows, cols] * s).astype(BF16)
        xf_ref, stage = vn_out
        per = tm // S5_T
        for ov in range(S5_OCTETS):
            stage[ov] = proj[:, ov * S5_OLANES:(ov + 1) * S5_OLANES]
            for t in range(S5_T):
                c0 = (ov * S5_T + t) * S5_OLANES
                xf_ref[:, c0:c0 + S5_OLANES] = stage[ov, pl.ds(t, per, stride=S5_T), :].astype(BF16)


def _even_pre_call(sample, x, mod, ng, win, lng, lnb, wt, bs):
    rows = x.shape[0]
    tm = rows if sample else ROW_TILE
    row_spec = lambda w: pl.BlockSpec((tm, w), lambda i: (i, 0))
    out_shape = [jax.ShapeDtypeStruct((rows, S5_WIDTH), F32),
                 jax.ShapeDtypeStruct((rows, SGU_WIDTH), BF16)]
    out_specs = [row_spec(S5_WIDTH), row_spec(SGU_WIDTH)]
    scratch = []
    if sample:
        out_shape.append(jax.ShapeDtypeStruct((rows, SGU_WIDTH), F32))
        out_specs.append(row_spec(SGU_WIDTH))
    else:
        out_shape.append(jax.ShapeDtypeStruct((rows // S5_T, S5_T * S5_WIDTH), BF16))
        out_specs.append(pl.BlockSpec((tm // S5_T, S5_T * S5_WIDTH), lambda i: (i, 0)))
        scratch = [pltpu.VMEM((S5_OCTETS, tm, S5_OLANES), F32)]
    return pl.pallas_call(
        functools.partial(_even_pre_body, sample, tm),
        grid=(rows // tm,),
        in_specs=[row_spec(D_MODEL), _const_spec(mod.shape), _const_spec(ng.shape), _weight_spec(win.shape),
                  _const_spec(lng.shape), _const_spec(lnb.shape), _const_spec(wt.shape), _const_spec(bs.shape)],
        out_specs=out_specs,
        out_shape=out_shape,
        scratch_shapes=scratch,
        compiler_params=_cparams("arbitrary"),
        name="even_pre_sample" if sample else "even_pre_prompt",
    )(x, mod, ng, win, lng, lnb, wt, bs)


def _s5_prompt_body(x_ref, pbr_ref, pbi_ref, pcr_ref, pci_ref, a16_ref,
                    y_ref, hfin_ref, mtt, pb_re, pb_im, pc_re, pc_im, s_re, s_im):
    ol, ns = S5_OLANES, S5_OSTATE
    nlb = ns // ol
    nt_dims = (((1,), (1,)), ((), ()))

    r_grp = lax.broadcasted_iota(jnp.int32, (ol, ns), 0) // S5_GROUP
    c_grp = lax.broadcasted_iota(jnp.int32, (ol, ns), 1) // S5_STATE
    same_group = jnp.where(r_grp == c_grp, 1.0, 0.0).astype(BF16)
    for src, dst in ((pbr_ref, pb_re), (pbi_ref, pb_im), (pcr_ref, pc_re), (pci_ref, pc_im)):
        for t in range(src.shape[0]):
            dst[t * ol:(t + 1) * ol, :] = jnp.concatenate([src[t]] * nlb, axis=1) * same_group

    last = slice((S5_T - 1) * ol, S5_T * ol)
    lagk = (lax.dot_general(pc_re[0:S5_T * ol, :], pb_re[last, :], nt_dims, preferred_element_type=F32)
            + lax.dot_general(pc_im[0:S5_T * ol, :], pb_im[last, :], nt_dims, preferred_element_type=F32)
            ).astype(BF16)

    @pl.when(pl.program_id(0) == 0)
    def _():
        mtt[...] = jnp.zeros_like(mtt)

    for to in range(S5_T):
        for ti in range(to + 1):
            mtt[to * ol:(to + 1) * ol, ti * ol:(ti + 1) * ol] = lagk[(to - ti) * ol:(to - ti + 1) * ol, :]

    x = x_ref[...]
    nrows = x.shape[0]
    sr = jnp.dot(x, pb_re[...], preferred_element_type=F32)
    si = jnp.dot(x, pb_im[...], preferred_element_type=F32)
    row_id = lax.broadcasted_iota(jnp.int32, (nrows, 1), 0)

    def shifted(t, k):
        return jnp.where(row_id >= k, pltpu.roll(t, k, 0), 0.0)

    pr, pi = a16_ref[0:1, :], a16_ref[1:2, :]
    k = 1
    while k < S5_SUB:
        tr, ti = shifted(sr, k), shifted(si, k)
        sr, si = sr + pr * tr - pi * ti, si + pr * ti + pi * tr
        pr, pi = pr * pr - pi * pi, 2.0 * pr * pi
        k *= 2
    s_re[...] = sr
    s_im[...] = si
    hr = jnp.zeros((S5_SUB, ns), F32)
    hi = jnp.zeros((S5_SUB, ns), F32)
    for tile in range(nrows // S5_SUB):
        rows_t = slice(tile * S5_SUB, (tile + 1) * S5_SUB)
        hr, hi = pr * hr - pi * hi + s_re[rows_t, :], pr * hi + pi * hr + s_im[rows_t, :]
        s_re[rows_t, :] = hr
        s_im[rows_t, :] = hi
    hfin_ref[0:1, :] = hr[S5_SUB - 1:S5_SUB, :]
    hfin_ref[1:2, :] = hi[S5_SUB - 1:S5_SUB, :]
    hb_re = shifted(s_re[...], 1).astype(BF16)
    hb_im = shifted(s_im[...], 1).astype(BF16)

    nq = 4
    qw = x.shape[1] // nq
    for j in range(nq):
        cols = slice(j * qw, (j + 1) * qw)
        carry_rows = slice(j * qw + ol, (j + 1) * qw + ol)
        y_ref[:, cols] = (lax.dot_general(hb_re, pc_re[carry_rows, :], nt_dims, preferred_element_type=F32)
                          + lax.dot_general(hb_im, pc_im[carry_rows, :], nt_dims, preferred_element_type=F32)
                          + lax.dot_general(x_ref[:, :(j + 1) * qw], mtt[cols, :(j + 1) * qw], nt_dims,
                                            preferred_element_type=F32)).astype(y_ref.dtype)


def _s5_prompt_call(xf, pbr, pbi, pcr, pci, a16):
    rows = xf.shape[0]
    ow = S5_T * S5_OLANES
    grp = lambda a: pl.BlockSpec((None,) + a.shape[1:], lambda i: (i,) + (0,) * (a.ndim - 1))
    cols = pl.BlockSpec((rows, ow), lambda i: (0, i))
    return pl.pallas_call(
        _s5_prompt_body,
        grid=(S5_OCTETS,),
        in_specs=[cols, grp(pbr), grp(pbi), grp(pcr), grp(pci), grp(a16)],
        out_specs=[cols, pl.BlockSpec((None, 2, S5_OSTATE), lambda i: (i, 0, 0))],
        out_shape=[jax.ShapeDtypeStruct((rows, S5_OCTETS * ow), BF16),
                   jax.ShapeDtypeStruct((S5_OCTETS, 2, S5_OSTATE), F32)],
        scratch_shapes=[pltpu.VMEM((ow, ow), BF16)]
        + [pltpu.VMEM((ow, S5_OSTATE), BF16)] * 2
        + [pltpu.VMEM((ow + S5_OLANES, S5_OSTATE), BF16)] * 2
        + [pltpu.VMEM((rows, S5_OSTATE), F32)] * 2,
        compiler_params=_cparams("arbitrary"),
        name="s5_prompt",
    )(xf, pbr, pbi, pcr, pci, a16)


def _s5_sample_body(xa_ref, h0r_ref, h0i_ref, ar_ref, ai_ref, bdr_ref, bdi_ref, cdr_ref, cdi_ref,
                    yc_ref, hr_ref, hi_ref):
    u = xa_ref[...].astype(BF16)
    ar = ar_ref[...]
    ai = ai_ref[...]
    h0r = h0r_ref[...]
    h0i = h0i_ref[...]
    hr = ar * h0r - ai * h0i + jnp.dot(u, bdr_ref[...], preferred_element_type=F32)
    hi = ar * h0i + ai * h0r + jnp.dot(u, bdi_ref[...], preferred_element_type=F32)
    hr_ref[...] = hr
    hi_ref[...] = hi
    yc_ref[...] = (jnp.dot(hr.astype(BF16), cdr_ref[...], preferred_element_type=F32)
                   - jnp.dot(hi.astype(BF16), cdi_ref[...], preferred_element_type=F32))


def _s5_sample_call(xa, h0r, h0i, ar, ai, bdr, bdi, cdr, cdi):
    rows = xa.shape[0]
    n = S5_GROUPS * S5_STATE
    args = (xa, h0r, h0i, ar, ai, bdr, bdi, cdr, cdi)
    return pl.pallas_call(
        _s5_sample_body,
        grid=(1,),
        in_specs=[_const_spec(a.shape) for a in args],
        out_specs=[_const_spec((rows, S5_WIDTH)), _const_spec((rows, n)), _const_spec((rows, n))],
        out_shape=[jax.ShapeDtypeStruct((rows, S5_WIDTH), F32),
                   jax.ShapeDtypeStruct((rows, n), F32),
                   jax.ShapeDtypeStruct((rows, n), F32)],
        compiler_params=_cparams("arbitrary"),
        name="s5_sample",
    )(*args)


def _even_post_body(folded, tm, x_ref, yc_ref, xa_ref, bout_ref, mod_ref, d_ref, wglu_ref, bglu_ref, wout_ref,
                    o_ref, *scratch):
    if folded:
        stage = scratch[0]
        per = tm // S5_T
        for ov in range(S5_OCTETS):
            for t in range(S5_T):
                c0 = (ov * S5_T + t) * S5_OLANES
                stage[ov, pl.ds(t, per, stride=S5_T), :] = yc_ref[:, c0:c0 + S5_OLANES].astype(F32)
        yc = jnp.concatenate([stage[ov] for ov in range(S5_OCTETS)], axis=1)
    else:
        yc = yc_ref[...]
    y = _gelu(yc + d_ref[...] * xa_ref[...])
    gate = jax.nn.sigmoid(jnp.dot(y.astype(BF16), wglu_ref[...], preferred_element_type=F32) + bglu_ref[...])
    a_out = (y * gate).astype(BF16)
    mix = (jnp.dot(a_out, wout_ref[0:S5_WIDTH, :], preferred_element_type=F32)
           + jnp.dot(bout_ref[...], wout_ref[S5_WIDTH:, :], preferred_element_type=F32))
    o_ref[...] = x_ref[...] + mod_ref[:, 2 * D_MODEL:3 * D_MODEL] * mix


def _even_post_call(x, yc, xa, bout, mod, d, wglu, bglu, wout):
    rows = x.shape[0]
    tm = min(rows, ROW_TILE)
    row_spec = lambda w: pl.BlockSpec((tm, w), lambda i: (i, 0))
    folded = yc.shape[0] != rows
    yc_spec = pl.BlockSpec((tm // S5_T, S5_T * S5_WIDTH), lambda i: (i, 0)) if folded else row_spec(S5_WIDTH)
    scratch = [pltpu.VMEM((S5_OCTETS, tm, S5_OLANES), F32)] if folded else []
    return pl.pallas_call(
        functools.partial(_even_post_body, folded, tm),
        grid=(rows // tm,),
        scratch_shapes=scratch,
        in_specs=[row_spec(D_MODEL), yc_spec, row_spec(S5_WIDTH), row_spec(SGU_WIDTH),
                  _const_spec(mod.shape), _const_spec(d.shape), _const_spec(wglu.shape),
                  _const_spec(bglu.shape), _weight_spec(wout.shape)],
        out_specs=row_spec(D_MODEL),
        out_shape=jax.ShapeDtypeStruct((rows, D_MODEL), F32),
        compiler_params=_cparams("arbitrary"),
        name="even_post",
    )(x, yc, xa, bout, mod, d, wglu, bglu, wout)


def _ffn_body(sample, final, tm, *refs):
    refs = list(refs)
    x_ref, mod_ref, ng_ref, wup_ref, cw_ref, cb_ref, wdn_ref = refs[:7]
    pos = 7
    if sample:
        p2_ref, p1_ref = refs[pos:pos + 2]
        pos += 2
    if final:
        fg_ref = refs[pos]
        pos += 1
    o_ref, conv_ref = refs[pos:pos + 2]
    pos += 2
    if not sample:
        carry_ref = refs[pos]

        @pl.when(pl.program_id(0) == 0)
        def _():
            carry_ref[...] = jnp.zeros_like(carry_ref)

    x = x_ref[...]
    h = _mod_norm(x, ng_ref[...], mod_ref[:, 3 * D_MODEL:4 * D_MODEL], mod_ref[:, 4 * D_MODEL:5 * D_MODEL])
    hb = h.astype(BF16)
    acc = jnp.zeros((tm, D_MODEL), F32)
    if not sample:
        row = lax.broadcasted_iota(jnp.int32, (tm, 1), 0)
    for c0, cw in FF_CHUNKS:
        cols = slice(c0, c0 + cw)
        a = jnp.dot(hb, wup_ref[:, cols], preferred_element_type=F32)
        g = jnp.dot(hb, wup_ref[:, D_FF + c0:D_FF + c0 + cw], preferred_element_type=F32)
        if sample:
            am2 = p2_ref[:, cols]
            am1 = p1_ref[:, cols]
            conv_ref[:, cols] = a
        else:
            prev2 = carry_ref[0:1, cols]
            prev1 = carry_ref[1:2, cols]
            am1 = jnp.where(row == 0, prev1, pltpu.roll(a, 1, 0))
            am2 = jnp.where(row == 0, prev2, jnp.where(row == 1, prev1, pltpu.roll(a, 2, 0)))
            carry_ref[0:2, cols] = a[tm - 2:tm, :]
        y = cb_ref[:, cols] + cw_ref[0:1, cols] * am2 + cw_ref[1:2, cols] * am1 + cw_ref[2:3, cols] * a
        act = (_gelu(y) * g).astype(BF16)
        acc = acc + jnp.dot(act, wdn_ref[cols, :], preferred_element_type=F32)
    out = x + mod_ref[:, 5 * D_MODEL:6 * D_MODEL] * acc
    if final:
        ms = jnp.mean(out * out, axis=-1, keepdims=True)
        out = out * lax.rsqrt(ms + EPS) * fg_ref[...]
    o_ref[...] = out
    if not sample:
        conv_ref[...] = carry_ref[0:2, :]


def _ffn_call(sample, layer, x, mod, ng, wup, cw, cb, wdn, prev=None, final_g=None):
    rows = x.shape[0]
    tm = rows if sample else ROW_TILE
    final = final_g is not None
    row_spec = lambda w: pl.BlockSpec((tm, w), lambda i: (i, 0))
    of_layer = lambda a, **kw: pl.BlockSpec((None,) + a.shape[1:], lambda i: (layer,) + (0,) * (a.ndim - 1), **kw)
    once = dict(pipeline_mode=pl.Buffered(1))
    args = [x, mod, ng, wup, cw, cb, wdn]
    in_specs = [row_spec(D_MODEL), _const_spec(mod.shape), _const_spec(ng.shape), of_layer(wup, **once),
                of_layer(cw), of_layer(cb), of_layer(wdn, **once)]
    if sample:
        args += [prev[0], prev[1]]
        in_specs += [_const_spec(prev[0].shape), _const_spec(prev[1].shape)]
    if final:
        args.append(final_g)
        in_specs.append(_const_spec(final_g.shape))
    conv_rows = rows if sample else 2
    return pl.pallas_call(
        functools.partial(_ffn_body, sample, final, tm),
        grid=(rows // tm,),
        in_specs=in_specs,
        out_specs=[row_spec(D_MODEL), _const_spec((conv_rows, D_FF))],
        out_shape=[jax.ShapeDtypeStruct((rows, D_MODEL), F32),
                   jax.ShapeDtypeStruct((conv_rows, D_FF), F32)],
        scratch_shapes=[] if sample else [pltpu.VMEM((8, D_FF), F32)],
        compiler_params=_cparams("arbitrary"),
        name="ffn_sample" if sample else "ffn_prompt",
    )(*args)


def _odd_pre_body(planar, tm, x_ref, mod_ref, ng_ref, wqkv_ref, tile_trig_ref, row_trig_ref,
                  q_ref, k_ref, v_ref, k32_ref, v32_ref, *rest):
    h = _mod_norm(x_ref[...], ng_ref[...], mod_ref[:, 0:D_MODEL], mod_ref[:, D_MODEL:2 * D_MODEL])
    qkv = jnp.dot(h.astype(BF16), wqkv_ref[...], preferred_element_type=F32)
    tt = tile_trig_ref[...]
    cos_sum = lambda f: tt[2 * f:2 * f + 1] * row_trig_ref[2 * f] - tt[2 * f + 1:2 * f + 2] * row_trig_ref[2 * f + 1]
    sin_sum = lambda f: tt[2 * f + 1:2 * f + 2] * row_trig_ref[2 * f] + tt[2 * f:2 * f + 1] * row_trig_ref[2 * f + 1]
    rc = cos_sum(0)
    ra = sin_sum(1)
    rb = -sin_sum(2)
    lanes = ROPE_LANES
    half = ROT_DIM // 2

    def rope(t):
        return t * rc + pltpu.roll(t, half, 1) * ra + pltpu.roll(t, lanes - half, 1) * rb

    nblk = D_MODEL // lanes
    if planar:
        qpl_ref, kpl_ref, vpl_ref, stage = rest
        per = tm // PLANES

        def to_planes(dst_ref, slot, cols, val):
            stage[slot] = val
            for r in range(PLANES):
                dst_ref[r, :, cols] = stage[slot, pl.ds(r, per, stride=PLANES), :].astype(BF16)

    for j in range(nblk):
        cols = slice(j * lanes, (j + 1) * lanes)
        q = rope(qkv[:, j * lanes:(j + 1) * lanes]) * (HEAD_DIM ** -0.5)
        k = rope(qkv[:, D_MODEL + j * lanes:D_MODEL + (j + 1) * lanes])
        v = qkv[:, 2 * D_MODEL + j * lanes:2 * D_MODEL + (j + 1) * lanes]
        q_ref[:, cols] = q.astype(BF16)
        k_ref[:, cols] = k.astype(BF16)
        v_ref[:, cols] = v.astype(BF16)
        k32_ref[:, cols] = k
        v32_ref[:, cols] = v
        if planar:
            to_planes(qpl_ref, j, cols, q)
            to_planes(kpl_ref, nblk + j, cols, k)
            to_planes(vpl_ref, 2 * nblk + j, cols, v)


def _odd_pre_call(x, mod, ng, wqkv, tile_trig, row_trig, keep, planar):
    rows = x.shape[0]
    tm = min(rows, ROW_TILE)
    nt = rows // tm
    first_kept = (rows - keep) // tm
    row_spec = lambda w: pl.BlockSpec((tm, w), lambda i: (i, 0))
    keep_spec = pl.BlockSpec((tm, D_MODEL), lambda i: (jnp.maximum(i - first_kept, 0), 0))
    out_specs = [row_spec(D_MODEL), row_spec(D_MODEL), row_spec(D_MODEL), keep_spec, keep_spec]
    out_shape = ([jax.ShapeDtypeStruct((rows, D_MODEL), BF16)] * 3
                 + [jax.ShapeDtypeStruct((keep, D_MODEL), F32)] * 2)
    scratch = []
    if planar:
        plane_spec = pl.BlockSpec((PLANES, tm // PLANES, D_MODEL), lambda i: (0, i, 0))
        out_specs += [plane_spec] * 3
        out_shape += [jax.ShapeDtypeStruct((PLANES, rows // PLANES, D_MODEL), BF16)] * 3
        scratch = [pltpu.VMEM((3 * D_MODEL // ROPE_LANES, tm, ROPE_LANES), F32)]
    return pl.pallas_call(
        functools.partial(_odd_pre_body, planar, tm),
        grid=(nt,),
        in_specs=[row_spec(D_MODEL), _const_spec(mod.shape), _const_spec(ng.shape), _weight_spec(wqkv.shape),
                  pl.BlockSpec((None,) + tile_trig.shape[1:], lambda i: (i, 0, 0)), _const_spec(row_trig.shape)],
        out_specs=out_specs,
        out_shape=out_shape,
        scratch_shapes=scratch,
        compiler_params=_cparams("arbitrary"),
        name="odd_pre",
    )(x, mod, ng, wqkv, tile_trig, row_trig)


def _attn_prompt_body(n_planes, q_ref, kp_ref, kc_ref, vp_ref, vc_ref, o_ref, st_ref):
    blk = pl.program_id(1)
    per = BAND // n_planes
    qi = lax.broadcasted_iota(jnp.int32, (BAND, 2 * BAND), 0)
    kj = lax.broadcasted_iota(jnp.int32, (BAND, 2 * BAND), 1)
    q_pos = n_planes * (qi % per) + qi // per
    k_half = kj // BAND
    k_pos = n_planes * (per * k_half + kj % per) + (kj % BAND) // per - BAND
    dist = q_pos - k_pos
    valid = (dist >= 0) & (dist <= BAND) & (k_half >= jnp.where(blk == 0, 1, 0))
    bias = jnp.where(valid, 0.0, NEG_INF).astype(F32)
    lane = lax.broadcasted_iota(jnp.int32, (BAND, 2 * HEAD_DIM), 1)
    low_half = lane < HEAD_DIM
    lane_row = lax.broadcasted_iota(jnp.int32, (1, 2 * HEAD_DIM), 1)
    head_keep = [jnp.where(lane_row < HEAD_DIM, 1.0, 0.0).astype(BF16),
                 jnp.where(lane_row < HEAD_DIM, 0.0, 1.0).astype(BF16)]
    st_out = jnp.zeros((BAND, STAT_LANES), F32)
    st_lane = lax.broadcasted_iota(jnp.int32, (BAND, STAT_LANES), 1)

    def rows_of(ref, cols):
        t = ref[:, cols] if n_planes == 1 else ref[:, :, cols]
        return t.reshape(BAND, t.shape[-1])

    for pair in range(N_HEADS // 2):
        cols = slice(pair * 2 * HEAD_DIM, (pair + 1) * 2 * HEAD_DIM)
        qp = rows_of(q_ref, cols)
        kk = jnp.concatenate([rows_of(kp_ref, cols), rows_of(kc_ref, cols)], axis=0)
        vv = jnp.concatenate([rows_of(vp_ref, cols), rows_of(vc_ref, cols)], axis=0)
        halves = []
        for sub in range(2):
            head = 2 * pair + sub
            qm = qp * head_keep[sub]
            s = lax.dot_general(qm, kk, (((1,), (1,)), ((), ())), preferred_element_type=F32) + bias
            m = jnp.max(s, axis=-1, keepdims=True)
            p = jnp.exp(s - m)
            l = jnp.sum(p, axis=-1, keepdims=True)
            halves.append(jnp.dot(p.astype(BF16), vv, preferred_element_type=F32) / l)
            st_out = jnp.where(st_lane == head, m, st_out)
            st_out = jnp.where(st_lane == N_HEADS + head, l, st_out)
        o_pair = jnp.where(low_half, halves[0], halves[1]).astype(o_ref.dtype)
        if n_planes == 1:
            o_ref[:, cols] = o_pair
        else:
            o_ref[:, :, cols] = o_pair.reshape(n_planes, per, 2 * HEAD_DIM)
    st_ref[...] = st_out.reshape(st_ref.shape)


def _attn_prompt_call(d, q, k, v):
    planes, rpp, width = q.shape
    if planes // d == 1 or d == 1:
        n_planes, outer = 1, planes
        view = lambda a: a
        blk = lambda w: (None, BAND, w)
        cur = lambda w: pl.BlockSpec(blk(w), lambda r, b: (r, b, 0))
        prev = lambda w: pl.BlockSpec(blk(w), lambda r, b: (r, jnp.maximum(b - 1, 0), 0))
        nb = rpp // BAND
    else:
        n_planes, outer = planes // d, d
        per = BAND // n_planes
        view = lambda a: a.reshape(n_planes, outer, rpp, a.shape[-1])
        blk = lambda w: (n_planes, None, per, w)
        cur = lambda w: pl.BlockSpec(blk(w), lambda r, b: (0, r, b, 0))
        prev = lambda w: pl.BlockSpec(blk(w), lambda r, b: (0, r, jnp.maximum(b - 1, 0), 0))
        nb = rpp // per
    qv, kv, vv = view(q), view(k), view(v)
    o, st = pl.pallas_call(
        functools.partial(_attn_prompt_body, n_planes),
        grid=(outer, nb),
        in_specs=[cur(width), prev(width), cur(width), prev(width), cur(width)],
        out_specs=[cur(width), cur(STAT_LANES)],
        out_shape=[jax.ShapeDtypeStruct(qv.shape, BF16),
                   jax.ShapeDtypeStruct(qv.shape[:-1] + (STAT_LANES,), F32)],
        compiler_params=_cparams("arbitrary", "arbitrary"),
        name="attn_prompt_d%d" % d,
    )(qv, kv, kv, vv, vv)
    return o.reshape(planes, rpp, width), st.reshape(planes, rpp, STAT_LANES)


def _attn_sample_body(hb, past, q_ref, kn_ref, vn_ref, kt_ref, vt_ref, o_ref):
    b = pl.program_id(1)
    dim = kt_ref.shape[1]
    q_row = q_ref[pl.ds(b, 1), :]
    kn_row = kn_ref[pl.ds(b, 1), :]
    vn_row = vn_ref[pl.ds(b, 1), :]
    on_diag = (lax.broadcasted_iota(jnp.int32, (dim, dim), 0)
               == lax.broadcasted_iota(jnp.int32, (dim, dim), 1))

    def to_col(row):
        return jnp.sum(jnp.where(on_diag, jnp.broadcast_to(row, (dim, dim)), 0.0), axis=1, keepdims=True)

    def to_row(col):
        return jnp.sum(jnp.where(on_diag, jnp.broadcast_to(col, (dim, dim)), 0.0), axis=0, keepdims=True)

    rows, news = [], []
    for h in range(hb):
        lanes = slice(h * dim, (h + 1) * dim)
        rows.append(jnp.sum(kt_ref[h] * to_col(q_row[:, lanes]), axis=0, keepdims=True))
        news.append(jnp.sum(kn_row[:, lanes] * q_row[:, lanes], axis=1, keepdims=True))
    s = jnp.concatenate(rows, axis=0)
    s_new = jnp.concatenate(news, axis=0)
    r = lax.broadcasted_iota(jnp.int32, (1, past), 1)
    ms, ls, ps, pns = [], [], [], []
    for window, d in DIL_BRANCHES:
        member = (r >= past - window) & ((past - r) % d == 0)
        sg = s + jnp.where(member, 0.0, NEG_INF).astype(F32)
        m = jnp.maximum(jnp.max(sg, axis=1, keepdims=True), s_new)
        p = jnp.exp(sg - m)
        pn = jnp.exp(s_new - m)
        ms.append(m)
        ps.append(p)
        pns.append(pn)
        ls.append(jnp.sum(p, axis=1, keepdims=True) + pn)
    m_all = jnp.maximum(jnp.maximum(ms[0], ms[1]), ms[2])
    cs = [jnp.exp(m - m_all) for m in ms]
    tot = cs[0] * ls[0] + cs[1] * ls[1] + cs[2] * ls[2]
    w = (cs[0] * ps[0] + cs[1] * ps[1] + cs[2] * ps[2]) / tot
    w_new = (cs[0] * pns[0] + cs[1] * pns[1] + cs[2] * pns[2]) / tot
    outs = []
    for h in range(hb):
        lanes = slice(h * dim, (h + 1) * dim)
        from_cache = jnp.sum(vt_ref[h] * w[h:h + 1, :], axis=1, keepdims=True)
        outs.append(to_row(from_cache) + vn_row[:, lanes] * w_new[h:h + 1, :])
    o_ref[pl.ds(b, 1), :] = jnp.concatenate(outs, axis=1)


def _attn_sample_call(q, k_new, v_new, kt, vt):
    bsz, heads, dim, past = kt.shape
    hb = 8
    rows = pl.BlockSpec((bsz, hb * dim), lambda j, b: (0, j))
    cache = pl.BlockSpec((None, hb, dim, past), lambda j, b: (b, j, 0, 0))
    return pl.pallas_call(
        functools.partial(_attn_sample_body, hb, past),
        grid=(heads // hb, bsz),
        in_specs=[rows, rows, rows, cache, cache],
        out_specs=rows,
        out_shape=jax.ShapeDtypeStruct((bsz, heads * dim), F32),
        compiler_params=_cparams("arbitrary", "arbitrary"),
        name="attn_sample",
    )(q, k_new, v_new, kt, vt)


def _odd_post_body(x_ref, att_ref, mod_ref, wo_ref, o_ref):
    mix = jnp.dot(att_ref[...].astype(BF16), wo_ref[...], preferred_element_type=F32)
    o_ref[...] = x_ref[...] + mod_ref[:, 2 * D_MODEL:3 * D_MODEL] * mix


def _odd_post_call(x, att, mod, wo):
    rows = x.shape[0]
    tm = min(rows, ROW_TILE)
    row_spec = pl.BlockSpec((tm, D_MODEL), lambda i: (i, 0))
    return pl.pallas_call(
        _odd_post_body,
        grid=(rows // tm,),
        in_specs=[row_spec, row_spec, _const_spec(mod.shape), _weight_spec(wo.shape)],
        out_specs=row_spec,
        out_shape=jax.ShapeDtypeStruct((rows, D_MODEL), F32),
        compiler_params=_cparams("arbitrary"),
        name="odd_post",
    )(x, att, mod, wo)


def _odd_post_merge_body(tm, x_ref, o1_ref, s1_ref, o4_ref, s4_ref, o16_ref, s16_ref, mod_ref, wo_ref,
                         spread_ref, o_ref, ob4, ob16, sb4, sb16):
    per = tm // PLANES
    nblk = D_MODEL // STAT_LANES
    for r in range(PLANES):
        rows_r = pl.ds(r, per, stride=PLANES)
        for j in range(nblk):
            cols = slice(j * STAT_LANES, (j + 1) * STAT_LANES)
            ob4[j, rows_r, :] = o4_ref[r, :, cols].astype(F32)
            ob16[j, rows_r, :] = o16_ref[r, :, cols].astype(F32)
        sb4[rows_r, :] = s4_ref[r]
        sb16[rows_r, :] = s16_ref[r]
    stats = [s1_ref[...], sb4[...], sb16[...]]
    dens = [pltpu.roll(s, STAT_LANES - N_HEADS, 1) for s in stats]
    m_all = jnp.maximum(jnp.maximum(stats[0], stats[1]), stats[2])
    ws = [d * jnp.exp(s - m_all) for s, d in zip(stats, dens)]
    tot = ws[0] + ws[1] + ws[2]
    lane = lax.broadcasted_iota(jnp.int32, (tm, STAT_LANES), 1)
    spreads = []
    for w in ws:
        coef = jnp.where(lane < N_HEADS, w / tot, 0.0)
        hi = coef.astype(BF16)
        lo = (coef - hi.astype(F32)).astype(BF16)
        spreads.append(jnp.dot(jnp.concatenate([hi, lo], axis=1), spread_ref[...], preferred_element_type=F32))
    pieces = []
    for j in range(nblk):
        cols = slice(j * STAT_LANES, (j + 1) * STAT_LANES)
        pieces.append((spreads[0][:, cols] * o1_ref[:, cols].astype(F32) + spreads[1][:, cols] * ob4[j]
                       + spreads[2][:, cols] * ob16[j]).astype(BF16))
    att = jnp.concatenate(pieces, axis=1)
    mix = jnp.dot(att, wo_ref[...], preferred_element_type=F32)
    o_ref[...] = x_ref[...] + mod_ref[:, 2 * D_MODEL:3 * D_MODEL] * mix


def _odd_post_merge_call(x, branches, mod, wo):
    rows = x.shape[0]
    tm = ROW_TILE
    (o1, s1), (o4, s4), (o16, s16) = branches
    row_spec = pl.BlockSpec((tm, D_MODEL), lambda i: (i, 0))
    nat = lambda w: pl.BlockSpec((None, tm, w), lambda i: (0, i, 0))
    pln = lambda w: pl.BlockSpec((PLANES, tm // PLANES, w), lambda i: (0, i, 0))
    head_of_lane = jnp.arange(D_MODEL) // HEAD_DIM
    spread = (jnp.arange(STAT_LANES)[:, None] == head_of_lane[None, :]).astype(BF16)
    spread = jnp.concatenate([spread, spread], axis=0)
    return pl.pallas_call(
        functools.partial(_odd_post_merge_body, tm),
        grid=(rows // tm,),
        in_specs=[row_spec, nat(D_MODEL), nat(STAT_LANES), pln(D_MODEL), pln(STAT_LANES),
                  pln(D_MODEL), pln(STAT_LANES), _const_spec(mod.shape), _weight_spec(wo.shape),
                  _const_spec(spread.shape)],
        out_specs=row_spec,
        out_shape=jax.ShapeDtypeStruct((rows, D_MODEL), F32),
        scratch_shapes=[pltpu.VMEM((D_MODEL // STAT_LANES, tm, STAT_LANES), F32),
                        pltpu.VMEM((D_MODEL // STAT_LANES, tm, STAT_LANES), F32),
                        pltpu.VMEM((tm, STAT_LANES), F32), pltpu.VMEM((tm, STAT_LANES), F32)],
        compiler_params=_cparams("arbitrary"),
        name="odd_post_merge",
    )(x, o1, s1, o4, s4, o16, s16, mod, wo, spread)


def _s5_tables(lam_re, lam_im, log_dt, b_re, b_im, c_re, c_im):
    dt = jnp.exp(log_dt)[:, None]
    lr, li = lam_re, lam_im
    ks = jnp.arange(S5_T + 1, dtype=F32)[:, None, None]
    mag = jnp.exp(ks * (lr * dt))
    pw_r = mag * jnp.cos(ks * (li * dt))
    pw_i = mag * jnp.sin(ks * (li * dt))
    ar, ai = pw_r[1], pw_i[1]
    den = lr * lr + li * li
    fr = ((ar - 1.0) * lr + ai * li) / den
    fi = (ai * lr - (ar - 1.0) * li) / den
    bbr = fr[..., None] * b_re - fi[..., None] * b_im
    bbi = fr[..., None] * b_im + fi[..., None] * b_re
    ca_r = c_re[None] * pw_r[:, :, None, :] - c_im[None] * pw_i[:, :, None, :]
    ca_i = c_re[None] * pw_i[:, :, None, :] + c_im[None] * pw_r[:, :, None, :]
    oc, og = S5_OCTETS, S5_OGROUPS
    split = lambda a, axis: a.reshape(a.shape[:axis] + (oc, og) + a.shape[axis + 1:])
    kr =(S5_T - 1) - jnp.arange(S5_T, dtype=F32)[:, None, None]
    rev_mag = jnp.exp(kr * (lr * dt))
    rev_r = rev_mag * jnp.cos(kr * (li * dt))
    rev_i = rev_mag * jnp.sin(kr * (li * dt))
    pb_r = rev_r[..., None] * bbr[None] - rev_i[..., None] * bbi[None]
    pb_i = rev_r[..., None] * bbi[None] + rev_i[..., None] * bbr[None]
    twice = lambda a: jnp.concatenate([a, a], axis=-1)
    fold_b = lambda a: twice(split(a, 1).transpose(1, 0, 2, 4, 3).reshape(oc, -1, S5_OLANES, S5_STATE))
    fold_c = lambda a: twice(split(a, 1).transpose(1, 0, 2, 3, 4).reshape(oc, -1, S5_OLANES, S5_STATE))
    pb_r, pb_i = fold_b(pb_r), fold_b(pb_i)
    pc_r, pc_i = fold_c(ca_r), fold_c(-ca_i)
    per_octet = lambda a: a.reshape(oc, S5_OSTATE)
    a16 = jnp.stack([per_octet(pw_r[S5_T]), per_octet(pw_i[S5_T])], axis=1)
    eye = jnp.eye(S5_GROUPS, dtype=F32)
    n_all = S5_GROUPS * S5_STATE
    bd_r = jnp.einsum('gnp,gh->gphn', bbr, eye).reshape(S5_WIDTH, n_all)
    bd_i = jnp.einsum('gnp,gh->gphn', bbi, eye).reshape(S5_WIDTH, n_all)
    cd_r = jnp.einsum('gpn,gh->gnhp', c_re, eye).reshape(n_all, S5_WIDTH)
    cd_i = jnp.einsum('gpn,gh->gnhp', c_im, eye).reshape(n_all, S5_WIDTH)
    return dict(pb_r=pb_r.astype(BF16), pb_i=pb_i.astype(BF16),
                pc_r=pc_r.astype(BF16), pc_i=pc_i.astype(BF16), a16=a16,
                a_r=ar.reshape(1, n_all), a_i=ai.reshape(1, n_all),
                bd_r=bd_r.astype(BF16), bd_i=bd_i.astype(BF16),
                cd_r=cd_r.astype(BF16), cd_i=cd_i.astype(BF16))


def _rope_tables(tile_pos, row_pos):
    half = ROT_DIM // 2
    inv = jnp.power(ROPE_THETA, -jnp.arange(half, dtype=F32) * 2.0 / ROT_DIM)
    e = jnp.arange(ROPE_LANES) % HEAD_DIM
    inv_e = inv[e % half]
    freqs = jnp.stack([jnp.where(e < ROT_DIM, inv_e, 0.0),
                       jnp.where((e >= half) & (e < ROT_DIM), inv_e, 0.0),
                       jnp.where(e < half, inv_e, 0.0)])

    def trig(pos):
        ang = pos.astype(F32)[None, :, None] * freqs[:, None, :]
        return jnp.stack([jnp.cos(ang), jnp.sin(ang)], axis=1).reshape(6, pos.shape[0], ROPE_LANES)

    tile_trig = jnp.concatenate([trig(tile_pos), jnp.zeros((2, tile_pos.shape[0], ROPE_LANES), F32)], axis=0)
    return tile_trig.transpose(1, 0, 2), trig(row_pos)


def _trunk(sample, x, mods, state, w):
    outs = {}
    mod = mods[0]
    if sample:
        xa, bout, vn = _even_pre_call(True, x, mod, w['ng'][0][0], w['ev_w_in'], w['sg_ln_g'], w['sg_ln_b'],
                                      w['sg_wt'], w['sg_row0'])
        outs['vn'] = vn
        t = w['s5']
        yc, hr, hi = _s5_sample_call(xa, state['s5_re'], state['s5_im'], t['a_r'], t['a_i'],
                                     t['bd_r'], t['bd_i'], t['cd_r'], t['cd_i'])
        outs['s5_re'], outs['s5_im'] = hr, hi
    else:
        xa, bout, xf = _even_pre_call(False, x, mod, w['ng'][0][0], w['ev_w_in'], w['sg_ln_g'], w['sg_ln_b'],
                                      w['sg_wt'], w['sg_bias'])
        t = w['s5']
        yc, hfin = _s5_prompt_call(xf, t['pb_r'], t['pb_i'], t['pc_r'], t['pc_i'], t['a16'])
        outs['s5_re'], outs['s5_im'] = hfin[:, 0], hfin[:, 1]
    x = _even_post_call(x, yc, xa, bout, mod, w['s5_d'], w['s5_w_glu'], w['s5_b_glu'], w['ev_w_out'])
    prev = (state['conv'][0][:, 0], state['conv'][0][:, 1]) if sample else None
    x, conv0 = _ffn_call(sample, 0, x, mod, w['ng'][0][1], w['ffn_w_up'], w['ffn_conv_w'],
                         w['ffn_conv_b'], w['ffn_w_down'], prev=prev)
    mod = mods[1]
    rows = x.shape[0]
    keep = rows if sample else min(WIN_MAX, rows)
    pre = _odd_pre_call(x, mod, w['ng'][1][0], w['od_w_qkv'], *w['rope'], keep, planar=not sample)
    q, k, v, k32, v32 = pre[:5]
    outs['k'], outs['v'] = k32, v32
    if sample:
        att = _attn_sample_call(q.astype(F32), k32, v32, state['ck'], state['cv'])
        x = _odd_post_call(x, att, mod, w['od_w_o'])
    else:
        q_pl, k_pl, v_pl = pre[5:]
        branches = (_attn_prompt_call(1, q[None], k[None], v[None]),
                    _attn_prompt_call(4, q_pl, k_pl, v_pl),
                    _attn_prompt_call(16, q_pl, k_pl, v_pl))
        x = _odd_post_merge_call(x, branches, mod, w['od_w_o'])
    prev = (state['conv'][1][:, 0], state['conv'][1][:, 1]) if sample else None
    y, conv1 = _ffn_call(sample, 1, x, mod, w['ng'][1][1], w['ffn_w_up'], w['ffn_conv_w'],
                         w['ffn_conv_b'], w['ffn_w_down'], prev=prev, final_g=w['final_g'])
    outs['y'] = y
    outs['conv'] = (conv0, conv1)
    return outs


def kernel(x_prompt, x_sample, c_prompt, c_sample, state_s5_re, state_s5_im, cache_c_k, cache_c_v,
           state_ffn_conv, ada_w, ada_b, norm_g, final_g, ev_w_in, ev_w_out, s5_lam_re, s5_lam_im,
           s5_log_dt, s5_b_re, s5_b_im, s5_c_re, s5_c_im, s5_d, s5_w_glu, s5_b_glu, sg_ln_g, sg_ln_b,
           sg_w, sg_b, od_w_qkv, od_w_o, ffn_w_up, ffn_conv_w, ffn_conv_b, ffn_w_down):
    bp, seq, _ = x_prompt.shape
    bs = x_sample.shape[0]
    assert bp == 1 and seq == SEQ and bs == DEC_BATCH and x_sample.shape[1] == 1

    c_all = jnp.concatenate([c_sample, c_prompt, jnp.zeros((MOD_ROWS - bs - bp, D_MODEL), F32)], axis=0)
    mod_all = _ada_call(c_all, ada_w, ada_b)
    mods_s = [mod_all[l, :bs] for l in range(2)]
    mods_p = [mod_all[l, bs:bs + 1] for l in range(2)]

    hd = SGU_WIDTH // SGU_HEADS
    causal = jnp.tril(jnp.ones((CHUNK, CHUNK), F32))
    w = dict(
        ng=[[norm_g[l, j].reshape(1, D_MODEL) for j in range(2)] for l in range(2)],
        final_g=final_g.reshape(1, D_MODEL),
        ev_w_in=ev_w_in[0].astype(BF16), ev_w_out=ev_w_out[0].astype(BF16),
        sg_ln_g=sg_ln_g[0].reshape(1, SGU_WIDTH), sg_ln_b=sg_ln_b[0].reshape(1, SGU_WIDTH),
        sg_wt=(sg_w[0] * causal[None]).astype(BF16),
        sg_bias=jnp.repeat(sg_b[0].T, hd, axis=1),
        sg_row0=jnp.stack([jnp.repeat(sg_w[0, :, 0, 0], hd), jnp.repeat(sg_b[0, :, 0], hd)], axis=0),
        s5=_s5_tables(s5_lam_re[0], s5_lam_im[0], s5_log_dt[0], s5_b_re[0], s5_b_im[0], s5_c_re[0], s5_c_im[0]),
        s5_d=s5_d[0].reshape(1, S5_WIDTH), s5_w_glu=s5_w_glu[0].astype(BF16),
        s5_b_glu=s5_b_glu[0].reshape(1, S5_WIDTH),
        od_w_qkv=od_w_qkv[0].astype(BF16), od_w_o=od_w_o[0].astype(BF16),
        ffn_w_up=ffn_w_up.astype(BF16), ffn_w_down=ffn_w_down.astype(BF16),
        ffn_conv_w=ffn_conv_w, ffn_conv_b=ffn_conv_b.reshape(2, 1, D_FF),
    )

    wp = dict(w, rope=_rope_tables(jnp.arange(0, seq, ROW_TILE, dtype=jnp.int32),
                                   jnp.arange(ROW_TILE, dtype=jnp.int32)))
    ws = dict(w, rope=_rope_tables(jnp.full((1,), PAST_LEN, jnp.int32), jnp.zeros((bs,), jnp.int32)))

    p = _trunk(False, x_prompt[0], mods_p, None, wp)
    n_all = S5_GROUPS * S5_STATE
    state = dict(s5_re=state_s5_re[0].reshape(bs, n_all), s5_im=state_s5_im[0].reshape(bs, n_all),
                 ck=jnp.transpose(cache_c_k[0], (0, 2, 3, 1)), cv=jnp.transpose(cache_c_v[0], (0, 2, 3, 1)),
                 conv=state_ffn_conv)
    s = _trunk(True, x_sample[:, 0], mods_s, state, ws)

    keep = min(WIN_MAX, seq)
    kv_p = lambda a: a.reshape(1, 1, keep, N_HEADS, HEAD_DIM)
    kv_s = lambda a: a.reshape(1, bs, 1, N_HEADS, HEAD_DIM)
    s5_p = lambda a: a.reshape(1, 1, S5_GROUPS, S5_STATE)
    s5_s = lambda a: a.reshape(1, bs, S5_GROUPS, S5_STATE)
    conv_p = jnp.stack([c.reshape(1, 2, D_FF) for c in p['conv']])
    conv_s = jnp.stack([jnp.stack([state_ffn_conv[l][:, 1], s['conv'][l]], axis=1) for l in range(2)])
    return (p['y'][None], s['y'][:, None], s5_p(p['s5_re']), s5_p(p['s5_im']),
            s5_s(s['s5_re']), s5_s(s['s5_im']), s['vn'].reshape(1, bs, 1, SGU_WIDTH),
            kv_p(p['k']), kv_p(p['v']), kv_s(s['k']), kv_s(s['v']), conv_p, conv_s)
```

```python
import functools

import jax
import jax.numpy as jnp
from jax import lax
from jax.experimental import pallas as pl
from jax.experimental.pallas import tpu as pltpu

F32 = jnp.float32
BF16 = jnp.bfloat16

D_MODEL = 1024
SEQ = 16384
DEC_BATCH = 32
PAST_LEN = 16384
S5_WIDTH = 512
S5_GROUP = 16
S5_GROUPS = 32
S5_STATE = 64
SGU_WIDTH = 512
SGU_HEADS = 4
CHUNK = 128
EVEN_IN = S5_WIDTH + 2 * SGU_WIDTH
HEAD_DIM = 64
N_HEADS = 16
ROT_DIM = 16
ROPE_THETA = 500000.0
DIL_BRANCHES = ((128, 1), (512, 4), (2048, 16))
BAND = 128
WIN_MAX = 2048
D_FF = 2816
EPS = 1e-6
NEG_INF = -1e30

ROW_TILE = 512
MOD_ROWS = 40
S5_T = 16
S5_SUB = 8
S5_OLANES = 128
S5_OCTETS = S5_WIDTH // S5_OLANES
S5_OGROUPS = S5_OLANES // S5_GROUP
S5_OSTATE = S5_OGROUPS * S5_STATE
FF_CHUNKS = ((0, 1024), (1024, 1024), (2048, 768))
ROPE_LANES = 128
PLANES = 16
LOG2_E = 1.4426950408889634
ATT_BLOCKS = 4
STAT_LANES = 128
VMEM_LIMIT = 56 * 1024 * 1024


def _cparams(*sem):
    return pltpu.CompilerParams(dimension_semantics=sem, vmem_limit_bytes=VMEM_LIMIT)


def _const_spec(shape):
    nd = len(shape)
    return pl.BlockSpec(shape, lambda *_: (0,) * nd)


def _weight_spec(shape):
    nd = len(shape)
    return pl.BlockSpec(shape, lambda *_: (0,) * nd, pipeline_mode=pl.Buffered(1))


def _gelu(x):
    return jax.nn.gelu(x)


def _mod_norm(x, ng, shift, scale):
    ms = jnp.mean(x * x, axis=-1, keepdims=True)
    return (x * lax.rsqrt(ms + EPS) * ng) * (1.0 + scale) + shift


def _ada_body(c_ref, w_ref, b_ref, o_ref):
    c = c_ref[...]
    s = c * jax.nn.sigmoid(c)
    o_ref[...] = jnp.dot(s.astype(BF16), w_ref[...].astype(BF16),
                         preferred_element_type=F32) + b_ref[...]


def _ada_call(c_all, ada_w, ada_b):
    depth = ada_w.shape[0]
    nt = 1536
    return pl.pallas_call(
        _ada_body,
        grid=(depth, 6 * D_MODEL // nt),
        in_specs=[
            pl.BlockSpec((MOD_ROWS, D_MODEL), lambda l, j: (0, 0)),
            pl.BlockSpec((None, D_MODEL, nt), lambda l, j: (l, 0, j)),
            pl.BlockSpec((None, 1, nt), lambda l, j: (l, 0, j)),
        ],
        out_specs=pl.BlockSpec((None, MOD_ROWS, nt), lambda l, j: (l, 0, j)),
        out_shape=jax.ShapeDtypeStruct((depth, MOD_ROWS, 6 * D_MODEL), F32),
        compiler_params=_cparams("arbitrary", "arbitrary"),
        name="ada_mod",
    )(c_all, ada_w, ada_b.reshape(depth, 1, 6 * D_MODEL))


def _even_pre_body(sample, tm, x_ref, mod_ref, ng_ref, win_ref, lng_ref, lnb_ref, wt_ref, bs_ref,
                   xa_ref, bout_ref, *vn_out):
    h = _mod_norm(x_ref[...], ng_ref[...], mod_ref[:, 0:D_MODEL], mod_ref[:, D_MODEL:2 * D_MODEL])
    proj = jnp.dot(h.astype(BF16), win_ref[...], preferred_element_type=F32)
    xa_ref[...] = proj[:, :S5_WIDTH]
    u = _gelu(proj[:, S5_WIDTH:S5_WIDTH + SGU_WIDTH])
    v = _gelu(proj[:, S5_WIDTH + SGU_WIDTH:])
    mu = jnp.mean(v, axis=-1, keepdims=True)
    var = jnp.mean(jnp.square(v - mu), axis=-1, keepdims=True)
    vn = (v - mu) * lax.rsqrt(var + EPS) * lng_ref[...] + lnb_ref[...]
    if sample:
        vn_out[0][...] = vn
        bout_ref[...] = (u * (vn * bs_ref[0:1, :] + bs_ref[1:2, :])).astype(BF16)
    else:
        vnb = vn.astype(BF16)
        hd = SGU_WIDTH // SGU_HEADS
        for ci in range(tm // CHUNK):
            rows = slice(ci * CHUNK, (ci + 1) * CHUNK)
            for hh in range(SGU_HEADS):
                cols = slice(hh * hd, (hh + 1) * hd)
                s = jnp.dot(wt_ref[hh], vnb[rows, cols], preferred_element_type=F32) + bs_ref[:, cols]
                bout_ref[rows, cols] = (u[rows, cols] * s).astype(BF16)
        xf_ref, stage = vn_out
        per = tm // S5_T
        for ov in range(S5_OCTETS):
            stage[ov] = proj[:, ov * S5_OLANES:(ov + 1) * S5_OLANES]
            for t in range(S5_T):
                c0 = (ov * S5_T + t) * S5_OLANES
                xf_ref[:, c0:c0 + S5_OLANES] = stage[ov, pl.ds(t, per, stride=S5_T), :].astype(BF16)


def _even_pre_call(sample, x, mod, ng, win, lng, lnb, wt, bs):
    rows = x.shape[0]
    tm = rows if sample else ROW_TILE
    row_spec = lambda w: pl.BlockSpec((tm, w), lambda i: (i, 0))
    out_shape = [jax.ShapeDtypeStruct((rows, S5_WIDTH), F32),
                 jax.ShapeDtypeStruct((rows, SGU_WIDTH), BF16)]
    out_specs = [row_spec(S5_WIDTH), row_spec(SGU_WIDTH)]
    scratch = []
    if sample:
        out_shape.append(jax.ShapeDtypeStruct((rows, SGU_WIDTH), F32))
        out_specs.append(row_spec(SGU_WIDTH))
    else:
        out_shape.append(jax.ShapeDtypeStruct((rows // S5_T, S5_T * S5_WIDTH), BF16))
        out_specs.append(pl.BlockSpec((tm // S5_T, S5_T * S5_WIDTH), lambda i: (i, 0)))
        scratch = [pltpu.VMEM((S5_OCTETS, tm, S5_OLANES), F32)]
    return pl.pallas_call(
        functools.partial(_even_pre_body, sample, tm),
        grid=(rows // tm,),
        in_specs=[row_spec(D_MODEL), _const_spec(mod.shape), _const_spec(ng.shape), _weight_spec(win.shape),
                  _const_spec(lng.shape), _const_spec(lnb.shape), _const_spec(wt.shape), _const_spec(bs.shape)],
        out_specs=out_specs,
        out_shape=out_shape,
        scratch_shapes=scratch,
        compiler_params=_cparams("arbitrary"),
        name="even_pre_sample" if sample else "even_pre_prompt",
    )(x, mod, ng, win, lng, lnb, wt, bs)


def _s5_prompt_body(x_ref, pbr_ref, pbi_ref, pcr_ref, pci_ref, a16_ref,
                    y_ref, hfin_ref, mtt, pb_re, pb_im, pc_re, pc_im, s_re, s_im):
    ol, ns = S5_OLANES, S5_OSTATE
    nlb = ns // ol
    nt_dims = (((1,), (1,)), ((), ()))

    r_grp = lax.broadcasted_iota(jnp.int32, (ol, ns), 0) // S5_GROUP
    c_grp = lax.broadcasted_iota(jnp.int32, (ol, ns), 1) // S5_STATE
    same_group = jnp.where(r_grp == c_grp, 1.0, 0.0).astype(BF16)
    for src, dst in ((pbr_ref, pb_re), (pbi_ref, pb_im), (pcr_ref, pc_re), (pci_ref, pc_im)):
        for t in range(src.shape[0]):
            dst[t * ol:(t + 1) * ol, :] = jnp.concatenate([src[t]] * nlb, axis=1) * same_group

    last = slice((S5_T - 1) * ol, S5_T * ol)
    lagk = (lax.dot_general(pc_re[0:S5_T * ol, :], pb_re[last, :], nt_dims, preferred_element_type=F32)
            + lax.dot_general(pc_im[0:S5_T * ol, :], pb_im[last, :], nt_dims, preferred_element_type=F32)
            ).astype(BF16)

    @pl.when(pl.program_id(0) == 0)
    def _():
        mtt[...] = jnp.zeros_like(mtt)

    for to in range(S5_T):
        for ti in range(to + 1):
            mtt[to * ol:(to + 1) * ol, ti * ol:(ti + 1) * ol] = lagk[(to - ti) * ol:(to - ti + 1) * ol, :]

    x = x_ref[...]
    nrows = x.shape[0]
    sr = jnp.dot(x, pb_re[...], preferred_element_type=F32)
    si = jnp.dot(x, pb_im[...], preferred_element_type=F32)
    row_id = lax.broadcasted_iota(jnp.int32, (nrows, 1), 0)

    def shifted(t, k):
        return jnp.where(row_id >= k, pltpu.roll(t, k, 0), 0.0)

    pr, pi = a16_ref[0:1, :], a16_ref[1:2, :]
    k = 1
    while k < S5_SUB:
        tr, ti = shifted(sr, k), shifted(si, k)
        sr, si = sr + pr * tr - pi * ti, si + pr * ti + pi * tr
        pr, pi = pr * pr - pi * pi, 2.0 * pr * pi
        k *= 2
    s_re[...] = sr
    s_im[...] = si
    hr = jnp.zeros((S5_SUB, ns), F32)
    hi = jnp.zeros((S5_SUB, ns), F32)
    for tile in range(nrows // S5_SUB):
        rows_t = slice(tile * S5_SUB, (tile + 1) * S5_SUB)
        hr, hi = pr * hr - pi * hi + s_re[rows_t, :], pr * hi + pi * hr + s_im[rows_t, :]
        s_re[rows_t, :] = hr
        s_im[rows_t, :] = hi
    hfin_ref[0:1, :] = hr[S5_SUB - 1:S5_SUB, :]
    hfin_ref[1:2, :] = hi[S5_SUB - 1:S5_SUB, :]
    hb_re = shifted(s_re[...], 1).astype(BF16)
    hb_im = shifted(s_im[...], 1).astype(BF16)

    nq = 4
    qw = x.shape[1] // nq
    for j in range(nq):
        cols = slice(j * qw, (j + 1) * qw)
        carry_rows = slice(j * qw + ol, (j + 1) * qw + ol)
        y_ref[:, cols] = (lax.dot_general(hb_re, pc_re[carry_rows, :], nt_dims, preferred_element_type=F32)
                          + lax.dot_general(hb_im, pc_im[carry_rows, :], nt_dims, preferred_element_type=F32)
                          + lax.dot_general(x_ref[:, :(j + 1) * qw], mtt[cols, :(j + 1) * qw], nt_dims,
                                            preferred_element_type=F32)).astype(y_ref.dtype)


def _s5_prompt_call(xf, pbr, pbi, pcr, pci, a16):
    rows = xf.shape[0]
    ow = S5_T * S5_OLANES
    grp = lambda a: pl.BlockSpec((None,) + a.shape[1:], lambda i: (i,) + (0,) * (a.ndim - 1))
    cols = pl.BlockSpec((rows, ow), lambda i: (0, i))
    return pl.pallas_call(
        _s5_prompt_body,
        grid=(S5_OCTETS,),
        in_specs=[cols, grp(pbr), grp(pbi), grp(pcr), grp(pci), grp(a16)],
        out_specs=[cols, pl.BlockSpec((None, 2, S5_OSTATE), lambda i: (i, 0, 0))],
        out_shape=[jax.ShapeDtypeStruct((rows, S5_OCTETS * ow), BF16),
                   jax.ShapeDtypeStruct((S5_OCTETS, 2, S5_OSTATE), F32)],
        scratch_shapes=[pltpu.VMEM((ow, ow), BF16)]
        + [pltpu.VMEM((ow, S5_OSTATE), BF16)] * 2
        + [pltpu.VMEM((ow + S5_OLANES, S5_OSTATE), BF16)] * 2
        + [pltpu.VMEM((rows, S5_OSTATE), F32)] * 2,
        compiler_params=_cparams("arbitrary"),
        name="s5_prompt",
    )(xf, pbr, pbi, pcr, pci, a16)


def _s5_sample_body(xa_ref, h0r_ref, h0i_ref, ar_ref, ai_ref, bdr_ref, bdi_ref, cdr_ref, cdi_ref,
                    yc_ref, hr_ref, hi_ref):
    u = xa_ref[...].astype(BF16)
    ar = ar_ref[...]
    ai = ai_ref[...]
    h0r = h0r_ref[...]
    h0i = h0i_ref[...]
    hr = ar * h0r - ai * h0i + jnp.dot(u, bdr_ref[...], preferred_element_type=F32)
    hi = ar * h0i + ai * h0r + jnp.dot(u, bdi_ref[...], preferred_element_type=F32)
    hr_ref[...] = hr
    hi_ref[...] = hi
    yc_ref[...] = (jnp.dot(hr.astype(BF16), cdr_ref[...], preferred_element_type=F32)
                   - jnp.dot(hi.astype(BF16), cdi_ref[...], preferred_element_type=F32))


def _s5_sample_call(xa, h0r, h0i, ar, ai, bdr, bdi, cdr, cdi):
    rows = xa.shape[0]
    n = S5_GROUPS * S5_STATE
    args = (xa, h0r, h0i, ar, ai, bdr, bdi, cdr, cdi)
    return pl.pallas_call(
        _s5_sample_body,
        grid=(1,),
        in_specs=[_const_spec(a.shape) for a in args],
        out_specs=[_const_spec((rows, S5_WIDTH)), _const_spec((rows, n)), _const_spec((rows, n))],
        out_shape=[jax.ShapeDtypeStruct((rows, S5_WIDTH), F32),
                   jax.ShapeDtypeStruct((rows, n), F32),
                   jax.ShapeDtypeStruct((rows, n), F32)],
        compiler_params=_cparams("arbitrary"),
        name="s5_sample",
    )(*args)


def _even_post_body(folded, tm, x_ref, yc_ref, xa_ref, bout_ref, mod_ref, d_ref, wglu_ref, bglu_ref, wout_ref,
                    o_ref, *scratch):
    if folded:
        stage = scratch[0]
        per = tm // S5_T
        for ov in range(S5_OCTETS):
            for t in range(S5_T):
                c0 = (ov * S5_T + t) * S5_OLANES
                stage[ov, pl.ds(t, per, stride=S5_T), :] = yc_ref[:, c0:c0 + S5_OLANES].astype(F32)
        yc = jnp.concatenate([stage[ov] for ov in range(S5_OCTETS)], axis=1)
    else:
        yc = yc_ref[...]
    y = _gelu(yc + d_ref[...] * xa_ref[...])
    gate = jax.nn.sigmoid(jnp.dot(y.astype(BF16), wglu_ref[...], preferred_element_type=F32) + bglu_ref[...])
    a_out = (y * gate).astype(BF16)
    mix = (jnp.dot(a_out, wout_ref[0:S5_WIDTH, :], preferred_element_type=F32)
           + jnp.dot(bout_ref[...], wout_ref[S5_WIDTH:, :], preferred_element_type=F32))
    o_ref[...] = x_ref[...] + mod_ref[:, 2 * D_MODEL:3 * D_MODEL] * mix


def _even_post_call(x, yc, xa, bout, mod, d, wglu, bglu, wout):
    rows = x.shape[0]
    tm = min(rows, ROW_TILE)
    row_spec = lambda w: pl.BlockSpec((tm, w), lambda i: (i, 0))
    folded = yc.shape[0] != rows
    yc_spec = pl.BlockSpec((tm // S5_T, S5_T * S5_WIDTH), lambda i: (i, 0)) if folded else row_spec(S5_WIDTH)
    scratch = [pltpu.VMEM((S5_OCTETS, tm, S5_OLANES), F32)] if folded else []
    return pl.pallas_call(
        functools.partial(_even_post_body, folded, tm),
        grid=(rows // tm,),
        scratch_shapes=scratch,
        in_specs=[row_spec(D_MODEL), yc_spec, row_spec(S5_WIDTH), row_spec(SGU_WIDTH),
                  _const_spec(mod.shape), _const_spec(d.shape), _const_spec(wglu.shape),
                  _const_spec(bglu.shape), _weight_spec(wout.shape)],
        out_specs=row_spec(D_MODEL),
        out_shape=jax.ShapeDtypeStruct((rows, D_MODEL), F32),
        compiler_params=_cparams("arbitrary"),
        name="even_post",
    )(x, yc, xa, bout, mod, d, wglu, bglu, wout)


def _ffn_body(sample, final, tm, *refs):
    refs = list(refs)
    x_ref, mod_ref, ng_ref, wup_ref, cw_ref, cb_ref, wdn_ref = refs[:7]
    pos = 7
    if sample:
        p2_ref, p1_ref = refs[pos:pos + 2]
        pos += 2
    if final:
        fg_ref = refs[pos]
        pos += 1
    o_ref, conv_ref = refs[pos:pos + 2]
    pos += 2
    if not sample:
        carry_ref = refs[pos]

        @pl.when(pl.program_id(0) == 0)
        def _():
            carry_ref[...] = jnp.zeros_like(carry_ref)

    x = x_ref[...]
    h = _mod_norm(x, ng_ref[...], mod_ref[:, 3 * D_MODEL:4 * D_MODEL], mod_ref[:, 4 * D_MODEL:5 * D_MODEL])
    hb = h.astype(BF16)
    acc = jnp.zeros((tm, D_MODEL), F32)
    if not sample:
        row = lax.broadcasted_iota(jnp.int32, (tm, 1), 0)
    for c0, cw in FF_CHUNKS:
        cols = slice(c0, c0 + cw)
        a = jnp.dot(hb, wup_ref[:, cols], preferred_element_type=F32)
        g = jnp.dot(hb, wup_ref[:, D_FF + c0:D_FF + c0 + cw], preferred_element_type=F32)
        if sample:
            am2 = p2_ref[:, cols]
            am1 = p1_ref[:, cols]
            conv_ref[:, cols] = a
        else:
            prev2 = carry_ref[0:1, cols]
            prev1 = carry_ref[1:2, cols]
            am1 = jnp.where(row == 0, prev1, pltpu.roll(a, 1, 0))
            am2 = jnp.where(row == 0, prev2, jnp.where(row == 1, prev1, pltpu.roll(a, 2, 0)))
            carry_ref[0:2, cols] = a[tm - 2:tm, :]
        y = cb_ref[:, cols] + cw_ref[0:1, cols] * am2 + cw_ref[1:2, cols] * am1 + cw_ref[2:3, cols] * a
        act = (_gelu(y) * g).astype(BF16)
        acc = acc + jnp.dot(act, wdn_ref[cols, :], preferred_element_type=F32)
    out = x + mod_ref[:, 5 * D_MODEL:6 * D_MODEL] * acc
    if final:
        ms = jnp.mean(out * out, axis=-1, keepdims=True)
        out = out * lax.rsqrt(ms + EPS) * fg_ref[...]
    o_ref[...] = out
    if not sample:
        conv_ref[...] = carry_ref[0:2, :]


def _ffn_call(sample, layer, x, mod, ng, wup, cw, cb, wdn, prev=None, final_g=None):
    rows = x.shape[0]
    tm = rows if sample else ROW_TILE
    final = final_g is not None
    row_spec = lambda w: pl.BlockSpec((tm, w), lambda i: (i, 0))
    of_layer = lambda a, **kw: pl.BlockSpec((None,) + a.shape[1:], lambda i: (layer,) + (0,) * (a.ndim - 1), **kw)
    once = dict(pipeline_mode=pl.Buffered(1))
    args = [x, mod, ng, wup, cw, cb, wdn]
    in_specs = [row_spec(D_MODEL), _const_spec(mod.shape), _const_spec(ng.shape), of_layer(wup, **once),
                of_layer(cw), of_layer(cb), of_layer(wdn, **once)]
    if sample:
        args += [prev[0], prev[1]]
        in_specs += [_const_spec(prev[0].shape), _const_spec(prev[1].shape)]
    if final:
        args.append(final_g)
        in_specs.append(_const_spec(final_g.shape))
    conv_rows = rows if sample else 2
    return pl.pallas_call(
        functools.partial(_ffn_body, sample, final, tm),
        grid=(rows // tm,),
        in_specs=in_specs,
        out_specs=[row_spec(D_MODEL), _const_spec((conv_rows, D_FF))],
        out_shape=[jax.ShapeDtypeStruct((rows, D_MODEL), F32),
                   jax.ShapeDtypeStruct((conv_rows, D_FF), F32)],
        scratch_shapes=[] if sample else [pltpu.VMEM((8, D_FF), F32)],
        compiler_params=_cparams("arbitrary"),
        name="ffn_sample" if sample else "ffn_prompt",
    )(*args)


def _odd_pre_body(planar, tm, x_ref, mod_ref, ng_ref, wqkv_ref, tile_trig_ref, row_trig_ref,
                  q_ref, k_ref, v_ref, k32_ref, v32_ref, *rest):
    h = _mod_norm(x_ref[...], ng_ref[...], mod_ref[:, 0:D_MODEL], mod_ref[:, D_MODEL:2 * D_MODEL])
    qkv = jnp.dot(h.astype(BF16), wqkv_ref[...], preferred_element_type=F32)
    tt = tile_trig_ref[...]
    cos_sum = lambda f: tt[2 * f:2 * f + 1] * row_trig_ref[2 * f] - tt[2 * f + 1:2 * f + 2] * row_trig_ref[2 * f + 1]
    sin_sum = lambda f: tt[2 * f + 1:2 * f + 2] * row_trig_ref[2 * f] + tt[2 * f:2 * f + 1] * row_trig_ref[2 * f + 1]
    rc = cos_sum(0)
    ra = sin_sum(1)
    rb = -sin_sum(2)
    lanes = ROPE_LANES
    half = ROT_DIM // 2

    def rope(t):
        return t * rc + pltpu.roll(t, half, 1) * ra + pltpu.roll(t, lanes - half, 1) * rb

    nblk = D_MODEL // lanes
    if planar:
        qpl_ref, kpl_ref, vpl_ref, stage = rest
        per = tm // PLANES

        def to_planes(dst_ref, slot, cols, val):
            stage[slot] = val
            for r in range(PLANES):
                dst_ref[r, :, cols] = stage[slot, pl.ds(r, per, stride=PLANES), :].astype(BF16)

    for j in range(nblk):
        cols = slice(j * lanes, (j + 1) * lanes)
        q = rope(qkv[:, j * lanes:(j + 1) * lanes]) * (HEAD_DIM ** -0.5 * (LOG2_E if planar else 1.0))
        k = rope(qkv[:, D_MODEL + j * lanes:D_MODEL + (j + 1) * lanes])
        v = qkv[:, 2 * D_MODEL + j * lanes:2 * D_MODEL + (j + 1) * lanes]
        q_ref[:, cols] = q.astype(BF16)
        k_ref[:, cols] = k.astype(BF16)
        v_ref[:, cols] = v.astype(BF16)
        k32_ref[:, cols] = k
        v32_ref[:, cols] = v
        if planar:
            to_planes(qpl_ref, j, cols, q)
            to_planes(kpl_ref, nblk + j, cols, k)
            to_planes(vpl_ref, 2 * nblk + j, cols, v)


def _odd_pre_call(x, mod, ng, wqkv, tile_trig, row_trig, keep, planar):
    rows = x.shape[0]
    tm = min(rows, ROW_TILE)
    nt = rows // tm
    first_kept = (rows - keep) // tm
    row_spec = lambda w: pl.BlockSpec((tm, w), lambda i: (i, 0))
    keep_spec = pl.BlockSpec((tm, D_MODEL), lambda i: (jnp.maximum(i - first_kept, 0), 0))
    out_specs = [row_spec(D_MODEL), row_spec(D_MODEL), row_spec(D_MODEL), keep_spec, keep_spec]
    out_shape = ([jax.ShapeDtypeStruct((rows, D_MODEL), BF16)] * 3
                 + [jax.ShapeDtypeStruct((keep, D_MODEL), F32)] * 2)
    scratch = []
    if planar:
        plane_spec = pl.BlockSpec((PLANES, tm // PLANES, D_MODEL), lambda i: (0, i, 0))
        out_specs += [plane_spec] * 3
        out_shape += [jax.ShapeDtypeStruct((PLANES, rows // PLANES, D_MODEL), BF16)] * 3
        scratch = [pltpu.VMEM((3 * D_MODEL // ROPE_LANES, tm, ROPE_LANES), F32)]
    return pl.pallas_call(
        functools.partial(_odd_pre_body, planar, tm),
        grid=(nt,),
        in_specs=[row_spec(D_MODEL), _const_spec(mod.shape), _const_spec(ng.shape), _weight_spec(wqkv.shape),
                  pl.BlockSpec((None,) + tile_trig.shape[1:], lambda i: (i, 0, 0)), _const_spec(row_trig.shape)],
        out_specs=out_specs,
        out_shape=out_shape,
        scratch_shapes=scratch,
        compiler_params=_cparams("arbitrary"),
        name="odd_pre",
    )(x, mod, ng, wqkv, tile_trig, row_trig)


def _attn_prompt_body(n_planes, n_blocks, q_ref, kp_ref, kc_ref, vp_ref, vc_ref, o_ref, st_ref):
    per = BAND // n_planes
    qi = lax.broadcasted_iota(jnp.int32, (BAND, 2 * BAND), 0)
    kj = lax.broadcasted_iota(jnp.int32, (BAND, 2 * BAND), 1)
    q_pos = n_planes * (qi % per) + qi // per
    k_half = kj // BAND
    k_pos = n_planes * (per * k_half + kj % per) + (kj % BAND) // per - BAND
    dist = q_pos - k_pos
    in_band = (dist >= 0) & (dist <= BAND)
    no_prev = jnp.where(pl.program_id(1) == 0, 1, 0)
    bias_any = jnp.where(in_band, 0.0, NEG_INF).astype(F32)
    bias_first = jnp.where(in_band & (k_half >= no_prev), 0.0, NEG_INF).astype(F32)
    lane = lax.broadcasted_iota(jnp.int32, (BAND, 2 * HEAD_DIM), 1)
    low_half = lane < HEAD_DIM
    lane_row = lax.broadcasted_iota(jnp.int32, (1, 2 * HEAD_DIM), 1)
    head_keep = [jnp.where(lane_row < HEAD_DIM, 1.0, 0.0).astype(BF16),
                 jnp.where(lane_row < HEAD_DIM, 0.0, 1.0).astype(BF16)]
    st_lane = lax.broadcasted_iota(jnp.int32, (BAND, STAT_LANES), 1)
    st_out = [jnp.zeros((BAND, STAT_LANES), F32)] * n_blocks

    def rows_of(ref, u, cols):
        t = ref[u * BAND:(u + 1) * BAND, cols] if n_planes == 1 else ref[:, u * per:(u + 1) * per, cols]
        return t.reshape(BAND, t.shape[-1])

    for pair in range(N_HEADS // 2):
        cols = slice(pair * 2 * HEAD_DIM, (pair + 1) * 2 * HEAD_DIM)
        for u in range(n_blocks):
            qp = rows_of(q_ref, u, cols)
            k_before = rows_of(kp_ref, 0, cols) if u == 0 else rows_of(kc_ref, u - 1, cols)
            v_before = rows_of(vp_ref, 0, cols) if u == 0 else rows_of(vc_ref, u - 1, cols)
            kk = jnp.concatenate([k_before, rows_of(kc_ref, u, cols)], axis=0)
            vv = jnp.concatenate([v_before, rows_of(vc_ref, u, cols)], axis=0)
            bias = bias_first if u == 0 else bias_any
            q2 = jnp.concatenate([qp * head_keep[0], qp * head_keep[1]], axis=0)
            s = (lax.dot_general(q2, kk, (((1,), (1,)), ((), ())), preferred_element_type=F32)
                 + jnp.concatenate([bias, bias], axis=0))
            m = jnp.max(s, axis=-1, keepdims=True)
            p = jnp.exp2(s - m)
            l = jnp.sum(p, axis=-1, keepdims=True)
            pv = jnp.dot(p.astype(BF16), vv, preferred_element_type=F32) / l
            for sub in range(2):
                head = 2 * pair + sub
                st_out[u] = jnp.where(st_lane == head, m[sub * BAND:(sub + 1) * BAND], st_out[u])
                st_out[u] = jnp.where(st_lane == N_HEADS + head, l[sub * BAND:(sub + 1) * BAND], st_out[u])
            o_pair = jnp.where(low_half, pv[:BAND], pv[BAND:]).astype(o_ref.dtype)
            if n_planes == 1:
                o_ref[u * BAND:(u + 1) * BAND, cols] = o_pair
            else:
                o_ref[:, u * per:(u + 1) * per, cols] = o_pair.reshape(n_planes, per, 2 * HEAD_DIM)
    for u in range(n_blocks):
        if n_planes == 1:
            st_ref[u * BAND:(u + 1) * BAND, :] = st_out[u]
        else:
            st_ref[:, u * per:(u + 1) * per, :] = st_out[u].reshape(n_planes, per, STAT_LANES)


def _attn_prompt_call(d, q, k, v):
    planes, rpp, width = q.shape
    nu = ATT_BLOCKS
    before = lambda b: jnp.maximum(nu * b - 1, 0)
    if planes // d == 1 or d == 1:
        n_planes, outer = 1, planes
        view = lambda a: a
        cur = lambda w: pl.BlockSpec((None, nu * BAND, w), lambda r, b: (r, b, 0))
        prev = lambda w: pl.BlockSpec((None, BAND, w), lambda r, b: (r, before(b), 0))
        nb = rpp // (nu * BAND)
    else:
        n_planes, outer = planes // d, d
        per = BAND // n_planes
        view = lambda a: a.reshape(n_planes, outer, rpp, a.shape[-1])
        cur = lambda w: pl.BlockSpec((n_planes, None, nu * per, w), lambda r, b: (0, r, b, 0))
        prev = lambda w: pl.BlockSpec((n_planes, None, per, w), lambda r, b: (0, r, before(b), 0))
        nb = rpp // (nu * per)
    qv, kv, vv = view(q), view(k), view(v)
    o, st = pl.pallas_call(
        functools.partial(_attn_prompt_body, n_planes, nu),
        grid=(outer, nb),
        in_specs=[cur(width), prev(width), cur(width), prev(width), cur(width)],
        out_specs=[cur(width), cur(STAT_LANES)],
        out_shape=[jax.ShapeDtypeStruct(qv.shape, BF16),
                   jax.ShapeDtypeStruct(qv.shape[:-1] + (STAT_LANES,), F32)],
        compiler_params=_cparams("arbitrary", "arbitrary"),
        name="attn_prompt_d%d" % d,
    )(qv, kv, kv, vv, vv)
    return o.reshape(planes, rpp, width), st.reshape(planes, rpp, STAT_LANES)


def _attn_sample_body(hb, past, q_ref, kn_ref, vn_ref, kt_ref, vt_ref, o_ref):
    b = pl.program_id(1)
    dim = kt_ref.shape[1]
    q_row = q_ref[pl.ds(b, 1), :]
    kn_row = kn_ref[pl.ds(b, 1), :]
    vn_row = vn_ref[pl.ds(b, 1), :]
    on_diag = (lax.broadcasted_iota(jnp.int32, (dim, dim), 0)
               == lax.broadcasted_iota(jnp.int32, (dim, dim), 1))

    def to_col(row):
        return jnp.sum(jnp.where(on_diag, jnp.broadcast_to(row, (dim, dim)), 0.0), axis=1, keepdims=True)

    def to_row(col):
        return jnp.sum(jnp.where(on_diag, jnp.broadcast_to(col, (dim, dim)), 0.0), axis=0, keepdims=True)

    rows, news = [], []
    for h in range(hb):
        lanes = slice(h * dim, (h + 1) * dim)
        rows.append(jnp.sum(kt_ref[h] * to_col(q_row[:, lanes]), axis=0, keepdims=True))
        news.append(jnp.sum(kn_row[:, lanes] * q_row[:, lanes], axis=1, keepdims=True))
    s = jnp.concatenate(rows, axis=0)
    s_new = jnp.concatenate(news, axis=0)
    r = lax.broadcasted_iota(jnp.int32, (1, past), 1)
    ms, ls, ps, pns = [], [], [], []
    for window, d in DIL_BRANCHES:
        member = (r >= past - window) & ((past - r) % d == 0)
        sg = s + jnp.where(member, 0.0, NEG_INF).astype(F32)
        m = jnp.maximum(jnp.max(sg, axis=1, keepdims=True), s_new)
        p = jnp.exp(sg - m)
        pn = jnp.exp(s_new - m)
        ms.append(m)
        ps.append(p)
        pns.append(pn)
        ls.append(jnp.sum(p, axis=1, keepdims=True) + pn)
    m_all = jnp.maximum(jnp.maximum(ms[0], ms[1]), ms[2])
    cs = [jnp.exp(m - m_all) for m in ms]
    tot = cs[0] * ls[0] + cs[1] * ls[1] + cs[2] * ls[2]
    w = (cs[0] * ps[0] + cs[1] * ps[1] + cs[2] * ps[2]) / tot
    w_new = (cs[0] * pns[0] + cs[1] * pns[1] + cs[2] * pns[2]) / tot
    outs = []
    for h in range(hb):
        lanes = slice(h * dim, (h + 1) * dim)
        from_cache = jnp.sum(vt_ref[h] * w[h:h + 1, :], axis=1, keepdims=True)
        outs.append(to_row(from_cache) + vn_row[:, lanes] * w_new[h:h + 1, :])
    o_ref[pl.ds(b, 1), :] = jnp.concatenate(outs, axis=1)


def _attn_sample_call(q, k_new, v_new, kt, vt):
    bsz, heads, dim, past = kt.shape
    hb = 8
    rows = pl.BlockSpec((bsz, hb * dim), lambda j, b: (0, j))
    cache = pl.BlockSpec((None, hb, dim, past), lambda j, b: (b, j, 0, 0))
    return pl.pallas_call(
        functools.partial(_attn_sample_body, hb, past),
        grid=(heads // hb, bsz),
        in_specs=[rows, rows, rows, cache, cache],
        out_specs=rows,
        out_shape=jax.ShapeDtypeStruct((bsz, heads * dim), F32),
        compiler_params=_cparams("arbitrary", "arbitrary"),
        name="attn_sample",
    )(q, k_new, v_new, kt, vt)


def _odd_post_body(x_ref, att_ref, mod_ref, wo_ref, o_ref):
    mix = jnp.dot(att_ref[...].astype(BF16), wo_ref[...], preferred_element_type=F32)
    o_ref[...] = x_ref[...] + mod_ref[:, 2 * D_MODEL:3 * D_MODEL] * mix


def _odd_post_call(x, att, mod, wo):
    rows = x.shape[0]
    tm = min(rows, ROW_TILE)
    row_spec = pl.BlockSpec((tm, D_MODEL), lambda i: (i, 0))
    return pl.pallas_call(
        _odd_post_body,
        grid=(rows // tm,),
        in_specs=[row_spec, row_spec, _const_spec(mod.shape), _weight_spec(wo.shape)],
        out_specs=row_spec,
        out_shape=jax.ShapeDtypeStruct((rows, D_MODEL), F32),
        compiler_params=_cparams("arbitrary"),
        name="odd_post",
    )(x, att, mod, wo)


def _odd_post_merge_body(tm, x_ref, o1_ref, s1_ref, o4_ref, s4_ref, o16_ref, s16_ref, mod_ref, wo_ref,
                         spread_ref, o_ref, ob4, ob16, sb4, sb16):
    per = tm // PLANES
    nblk = D_MODEL // STAT_LANES
    for r in range(PLANES):
        rows_r = pl.ds(r, per, stride=PLANES)
        for j in range(nblk):
            cols = slice(j * STAT_LANES, (j + 1) * STAT_LANES)
            ob4[j, rows_r, :] = o4_ref[r, :, cols].astype(F32)
            ob16[j, rows_r, :] = o16_ref[r, :, cols].astype(F32)
        sb4[rows_r, :] = s4_ref[r]
        sb16[rows_r, :] = s16_ref[r]
    stats = [s1_ref[...], sb4[...], sb16[...]]
    dens = [pltpu.roll(s, STAT_LANES - N_HEADS, 1) for s in stats]
    m_all = jnp.maximum(jnp.maximum(stats[0], stats[1]), stats[2])
    ws = [d * jnp.exp2(s - m_all) for s, d in zip(stats, dens)]
    tot = ws[0] + ws[1] + ws[2]
    lane = lax.broadcasted_iota(jnp.int32, (tm, STAT_LANES), 1)
    spreads = []
    for w in ws[:2]:
        coef = jnp.where(lane < N_HEADS, w / tot, 0.0)
        hi = coef.astype(BF16)
        lo = (coef - hi.astype(F32)).astype(BF16)
        spreads.append(jnp.dot(jnp.concatenate([hi, lo], axis=1), spread_ref[...], preferred_element_type=F32))
    pieces = []
    for j in range(nblk):
        cols = slice(j * STAT_LANES, (j + 1) * STAT_LANES)
        last = ob16[j]
        pieces.append((last + spreads[0][:, cols] * (o1_ref[:, cols].astype(F32) - last)
                       + spreads[1][:, cols] * (ob4[j] - last)).astype(BF16))
    att = jnp.concatenate(pieces, axis=1)
    mix = jnp.dot(att, wo_ref[...], preferred_element_type=F32)
    o_ref[...] = x_ref[...] + mod_ref[:, 2 * D_MODEL:3 * D_MODEL] * mix


def _odd_post_merge_call(x, branches, mod, wo):
    rows = x.shape[0]
    tm = ROW_TILE
    (o1, s1), (o4, s4), (o16, s16) = branches
    row_spec = pl.BlockSpec((tm, D_MODEL), lambda i: (i, 0))
    nat = lambda w: pl.BlockSpec((None, tm, w), lambda i: (0, i, 0))
    pln = lambda w: pl.BlockSpec((PLANES, tm // PLANES, w), lambda i: (0, i, 0))
    head_of_lane = jnp.arange(D_MODEL) // HEAD_DIM
    spread = (jnp.arange(STAT_LANES)[:, None] == head_of_lane[None, :]).astype(BF16)
    spread = jnp.concatenate([spread, spread], axis=0)
    return pl.pallas_call(
        functools.partial(_odd_post_merge_body, tm),
        grid=(rows // tm,),
        in_specs=[row_spec, nat(D_MODEL), nat(STAT_LANES), pln(D_MODEL), pln(STAT_LANES),
                  pln(D_MODEL), pln(STAT_LANES), _const_spec(mod.shape), _weight_spec(wo.shape),
                  _const_spec(spread.shape)],
        out_specs=row_spec,
        out_shape=jax.ShapeDtypeStruct((rows, D_MODEL), F32),
        scratch_shapes=[pltpu.VMEM((D_MODEL // STAT_LANES, tm, STAT_LANES), F32),
                        pltpu.VMEM((D_MODEL // STAT_LANES, tm, STAT_LANES), F32),
                        pltpu.VMEM((tm, STAT_LANES), F32), pltpu.VMEM((tm, STAT_LANES), F32)],
        compiler_params=_cparams("arbitrary"),
        name="odd_post_merge",
    )(x, o1, s1, o4, s4, o16, s16, mod, wo, spread)


def _s5_tables(lam_re, lam_im, log_dt, b_re, b_im, c_re, c_im):
    dt = jnp.exp(log_dt)[:, None]
    lr, li = lam_re, lam_im
    ks = jnp.arange(S5_T + 1, dtype=F32)[:, None, None]
    mag = jnp.exp(ks * (lr * dt))
    pw_r = mag * jnp.cos(ks * (li * dt))
    pw_i = mag * jnp.sin(ks * (li * dt))
    ar, ai = pw_r[1], pw_i[1]
    den = lr * lr + li * li
    fr = ((ar - 1.0) * lr + ai * li) / den
    fi = (ai * lr - (ar - 1.0) * li) / den
    bbr = fr[..., None] * b_re - fi[..., None] * b_im
    bbi = fr[..., None] * b_im + fi[..., None] * b_re
    ca_r = c_re[None] * pw_r[:, :, None, :] - c_im[None] * pw_i[:, :, None, :]
    ca_i = c_re[None] * pw_i[:, :, None, :] + c_im[None] * pw_r[:, :, None, :]
    oc, og = S5_OCTETS, S5_OGROUPS
    split = lambda a, axis: a.reshape(a.shape[:axis] + (oc, og) + a.shape[axis + 1:])
    kr =(S5_T - 1) - jnp.arange(S5_T, dtype=F32)[:, None, None]
    rev_mag = jnp.exp(kr * (lr * dt))
    rev_r = rev_mag * jnp.cos(kr * (li * dt))
    rev_i = rev_mag * jnp.sin(kr * (li * dt))
    pb_r = rev_r[..., None] * bbr[None] - rev_i[..., None] * bbi[None]
    pb_i = rev_r[..., None] * bbi[None] + rev_i[..., None] * bbr[None]
    twice = lambda a: jnp.concatenate([a, a], axis=-1)
    fold_b = lambda a: twice(split(a, 1).transpose(1, 0, 2, 4, 3).reshape(oc, -1, S5_OLANES, S5_STATE))
    fold_c = lambda a: twice(split(a, 1).transpose(1, 0, 2, 3, 4).reshape(oc, -1, S5_OLANES, S5_STATE))
    pb_r, pb_i = fold_b(pb_r), fold_b(pb_i)
    pc_r, pc_i = fold_c(ca_r), fold_c(-ca_i)
    per_octet = lambda a: a.reshape(oc, S5_OSTATE)
    a16 = jnp.stack([per_octet(pw_r[S5_T]), per_octet(pw_i[S5_T])], axis=1)
    eye = jnp.eye(S5_GROUPS, dtype=F32)
    n_all = S5_GROUPS * S5_STATE
    bd_r = jnp.einsum('gnp,gh->gphn', bbr, eye).reshape(S5_WIDTH, n_all)
    bd_i = jnp.einsum('gnp,gh->gphn', bbi, eye).reshape(S5_WIDTH, n_all)
    cd_r = jnp.einsum('gpn,gh->gnhp', c_re, eye).reshape(n_all, S5_WIDTH)
    cd_i = jnp.einsum('gpn,gh->gnhp', c_im, eye).reshape(n_all, S5_WIDTH)
    return dict(pb_r=pb_r.astype(BF16), pb_i=pb_i.astype(BF16),
                pc_r=pc_r.astype(BF16), pc_i=pc_i.astype(BF16), a16=a16,
                a_r=ar.reshape(1, n_all), a_i=ai.reshape(1, n_all),
                bd_r=bd_r.astype(BF16), bd_i=bd_i.astype(BF16),
                cd_r=cd_r.astype(BF16), cd_i=cd_i.astype(BF16))


def _rope_tables(tile_pos, row_pos):
    half = ROT_DIM // 2
    inv = jnp.power(ROPE_THETA, -jnp.arange(half, dtype=F32) * 2.0 / ROT_DIM)
    e = jnp.arange(ROPE_LANES) % HEAD_DIM
    inv_e = inv[e % half]
    freqs = jnp.stack([jnp.where(e < ROT_DIM, inv_e, 0.0),
                       jnp.where((e >= half) & (e < ROT_DIM), inv_e, 0.0),
                       jnp.where(e < half, inv_e, 0.0)])

    def trig(pos):
        ang = pos.astype(F32)[None, :, None] * freqs[:, None, :]
        return jnp.stack([jnp.cos(ang), jnp.sin(ang)], axis=1).reshape(6, pos.shape[0], ROPE_LANES)

    tile_trig = jnp.concatenate([trig(tile_pos), jnp.zeros((2, tile_pos.shape[0], ROPE_LANES), F32)], axis=0)
    return tile_trig.transpose(1, 0, 2), trig(row_pos)


def _trunk(sample, x, mods, state, w):
    outs = {}
    mod = mods[0]
    if sample:
        xa, bout, vn = _even_pre_call(True, x, mod, w['ng'][0][0], w['ev_w_in'], w['sg_ln_g'], w['sg_ln_b'],
                                      w['sg_wt'], w['sg_row0'])
        outs['vn'] = vn
        t = w['s5']
        yc, hr, hi = _s5_sample_call(xa, state['s5_re'], state['s5_im'], t['a_r'], t['a_i'],
                                     t['bd_r'], t['bd_i'], t['cd_r'], t['cd_i'])
        outs['s5_re'], outs['s5_im'] = hr, hi
    else:
        xa, bout, xf = _even_pre_call(False, x, mod, w['ng'][0][0], w['ev_w_in'], w['sg_ln_g'], w['sg_ln_b'],
                                      w['sg_wt'], w['sg_bias'])
        t = w['s5']
        yc, hfin = _s5_prompt_call(xf, t['pb_r'], t['pb_i'], t['pc_r'], t['pc_i'], t['a16'])
        outs['s5_re'], outs['s5_im'] = hfin[:, 0], hfin[:, 1]
    x = _even_post_call(x, yc, xa, bout, mod, w['s5_d'], w['s5_w_glu'], w['s5_b_glu'], w['ev_w_out'])
    prev = (state['conv'][0][:, 0], state['conv'][0][:, 1]) if sample else None
    x, conv0 = _ffn_call(sample, 0, x, mod, w['ng'][0][1], w['ffn_w_up'], w['ffn_conv_w'],
                         w['ffn_conv_b'], w['ffn_w_down'], prev=prev)
    mod = mods[1]
    rows = x.shape[0]
    keep = rows if sample else min(WIN_MAX, rows)
    pre = _odd_pre_call(x, mod, w['ng'][1][0], w['od_w_qkv'], *w['rope'], keep, planar=not sample)
    q, k, v, k32, v32 = pre[:5]
    outs['k'], outs['v'] = k32, v32
    if sample:
        att = _attn_sample_call(q.astype(F32), k32, v32, state['ck'], state['cv'])
        x = _odd_post_call(x, att, mod, w['od_w_o'])
    else:
        q_pl, k_pl, v_pl = pre[5:]
        branches = (_attn_prompt_call(1, q[None], k[None], v[None]),
                    _attn_prompt_call(4, q_pl, k_pl, v_pl),
                    _attn_prompt_call(16, q_pl, k_pl, v_pl))
        x = _odd_post_merge_call(x, branches, mod, w['od_w_o'])
    prev = (state['conv'][1][:, 0], state['conv'][1][:, 1]) if sample else None
    y, conv1 = _ffn_call(sample, 1, x, mod, w['ng'][1][1], w['ffn_w_up'], w['ffn_conv_w'],
                         w['ffn_conv_b'], w['ffn_w_down'], prev=prev, final_g=w['final_g'])
    outs['y'] = y
    outs['conv'] = (conv0, conv1)
    return outs


def kernel(x_prompt, x_sample, c_prompt, c_sample, state_s5_re, state_s5_im, cache_c_k, cache_c_v,
           state_ffn_conv, ada_w, ada_b, norm_g, final_g, ev_w_in, ev_w_out, s5_lam_re, s5_lam_im,
           s5_log_dt, s5_b_re, s5_b_im, s5_c_re, s5_c_im, s5_d, s5_w_glu, s5_b_glu, sg_ln_g, sg_ln_b,
           sg_w, sg_b, od_w_qkv, od_w_o, ffn_w_up, ffn_conv_w, ffn_conv_b, ffn_w_down):
    bp, seq, _ = x_prompt.shape
    bs = x_sample.shape[0]
    assert bp == 1 and seq == SEQ and bs == DEC_BATCH and x_sample.shape[1] == 1

    c_all = jnp.concatenate([c_sample, c_prompt, jnp.zeros((MOD_ROWS - bs - bp, D_MODEL), F32)], axis=0)
    mod_all = _ada_call(c_all, ada_w, ada_b)
    mods_s = [mod_all[l, :bs] for l in range(2)]
    mods_p = [mod_all[l, bs:bs + 1] for l in range(2)]

    hd = SGU_WIDTH // SGU_HEADS
    causal = jnp.tril(jnp.ones((CHUNK, CHUNK), F32))
    w = dict(
        ng=[[norm_g[l, j].reshape(1, D_MODEL) for j in range(2)] for l in range(2)],
        final_g=final_g.reshape(1, D_MODEL),
        ev_w_in=ev_w_in[0].astype(BF16), ev_w_out=ev_w_out[0].astype(BF16),
        sg_ln_g=sg_ln_g[0].reshape(1, SGU_WIDTH), sg_ln_b=sg_ln_b[0].reshape(1, SGU_WIDTH),
        sg_wt=(sg_w[0] * causal[None]).astype(BF16),
        sg_bias=jnp.repeat(sg_b[0].T, hd, axis=1),
        sg_row0=jnp.stack([jnp.repeat(sg_w[0, :, 0, 0], hd), jnp.repeat(sg_b[0, :, 0], hd)], axis=0),
        s5=_s5_tables(s5_lam_re[0], s5_lam_im[0], s5_log_dt[0], s5_b_re[0], s5_b_im[0], s5_c_re[0], s5_c_im[0]),
        s5_d=s5_d[0].reshape(1, S5_WIDTH), s5_w_glu=s5_w_glu[0].astype(BF16),
        s5_b_glu=s5_b_glu[0].reshape(1, S5_WIDTH),
        od_w_qkv=od_w_qkv[0].astype(BF16), od_w_o=od_w_o[0].astype(BF16),
        ffn_w_up=ffn_w_up.astype(BF16), ffn_w_down=ffn_w_down.astype(BF16),
        ffn_conv_w=ffn_conv_w, ffn_conv_b=ffn_conv_b.reshape(2, 1, D_FF),
    )

    wp = dict(w, rope=_rope_tables(jnp.arange(0, seq, ROW_TILE, dtype=jnp.int32),
                                   jnp.arange(ROW_TILE, dtype=jnp.int32)))
    ws = dict(w, rope=_rope_tables(jnp.full((1,), PAST_LEN, jnp.int32), jnp.zeros((bs,), jnp.int32)))

    p = _trunk(False, x_prompt[0], mods_p, None, wp)
    n_all = S5_GROUPS * S5_STATE
    state = dict(s5_re=state_s5_re[0].reshape(bs, n_all), s5_im=state_s5_im[0].reshape(bs, n_all),
                 ck=jnp.transpose(cache_c_k[0], (0, 2, 3, 1)), cv=jnp.transpose(cache_c_v[0], (0, 2, 3, 1)),
                 conv=state_ffn_conv)
    s = _trunk(True, x_sample[:, 0], mods_s, state, ws)

    keep = min(WIN_MAX, seq)
    kv_p = lambda a: a.reshape(1, 1, keep, N_HEADS, HEAD_DIM)
    kv_s = lambda a: a.reshape(1, bs, 1, N_HEADS, HEAD_DIM)
    s5_p = lambda a: a.reshape(1, 1, S5_GROUPS, S5_STATE)
    s5_s = lambda a: a.reshape(1, bs, S5_GROUPS, S5_STATE)
    conv_p = jnp.stack([c.reshape(1, 2, D_FF) for c in p['conv']])
    conv_s = jnp.stack([jnp.stack([state_ffn_conv[l][:, 1], s['conv'][l]], axis=1) for l in range(2)])
    return (p['y'][None], s['y'][:, None], s5_p(p['s5_re']), s5_p(p['s5_im']),
            s5_s(s['s5_re']), s5_s(s['s5_im']), s['vn'].reshape(1, bs, 1, SGU_WIDTH),
            kv_p(p['k']), kv_p(p['v']), kv_s(s['k']), kv_s(s['v']), conv_p, conv_s)
```

```python
import functools

import jax
import jax.numpy as jnp
from jax import lax
from jax.experimental import pallas as pl
from jax.experimental.pallas import tpu as pltpu

F32 = jnp.float32
BF16 = jnp.bfloat16

D_MODEL = 1024
SEQ = 16384
DEC_BATCH = 32
PAST_LEN = 16384
S5_WIDTH = 512
S5_GROUP = 16
S5_GROUPS = 32
S5_STATE = 64
SGU_WIDTH = 512
SGU_HEADS = 4
CHUNK = 128
EVEN_IN = S5_WIDTH + 2 * SGU_WIDTH
HEAD_DIM = 64
N_HEADS = 16
ROT_DIM = 16
ROPE_THETA = 500000.0
DIL_BRANCHES = ((128, 1), (512, 4), (2048, 16))
BAND = 128
WIN_MAX = 2048
D_FF = 2816
EPS = 1e-6
NEG_INF = -1e30

ROW_TILE = 512
MOD_ROWS = 40
S5_T = 16
S5_SUB = 8
S5_OLANES = 128
S5_OCTETS = S5_WIDTH // S5_OLANES
S5_OGROUPS = S5_OLANES // S5_GROUP
S5_OSTATE = S5_OGROUPS * S5_STATE
FF_CHUNKS = ((0, D_FF),)
ROPE_LANES = 128
PLANES = 16
LOG2_E = 1.4426950408889634
ATT_BLOCKS = 4
STAT_LANES = 128
VMEM_LIMIT = 56 * 1024 * 1024


def _cparams(*sem):
    return pltpu.CompilerParams(dimension_semantics=sem, vmem_limit_bytes=VMEM_LIMIT)


def _const_spec(shape):
    nd = len(shape)
    return pl.BlockSpec(shape, lambda *_: (0,) * nd)


def _weight_spec(shape):
    nd = len(shape)
    return pl.BlockSpec(shape, lambda *_: (0,) * nd, pipeline_mode=pl.Buffered(1))


def _gelu(x):
    return jax.nn.gelu(x)


def _mod_norm(x, ng, shift, scale):
    ms = jnp.mean(x * x, axis=-1, keepdims=True)
    return (x * lax.rsqrt(ms + EPS) * ng) * (1.0 + scale) + shift


def _ada_body(c_ref, w_ref, b_ref, o_ref):
    c = c_ref[...]
    s = c * jax.nn.sigmoid(c)
    o_ref[...] = jnp.dot(s.astype(BF16), w_ref[...].astype(BF16),
                         preferred_element_type=F32) + b_ref[...]


def _ada_call(c_all, ada_w, ada_b):
    depth = ada_w.shape[0]
    nt = 1536
    return pl.pallas_call(
        _ada_body,
        grid=(depth, 6 * D_MODEL // nt),
        in_specs=[
            pl.BlockSpec((MOD_ROWS, D_MODEL), lambda l, j: (0, 0)),
            pl.BlockSpec((None, D_MODEL, nt), lambda l, j: (l, 0, j)),
            pl.BlockSpec((None, 1, nt), lambda l, j: (l, 0, j)),
        ],
        out_specs=pl.BlockSpec((None, MOD_ROWS, nt), lambda l, j: (l, 0, j)),
        out_shape=jax.ShapeDtypeStruct((depth, MOD_ROWS, 6 * D_MODEL), F32),
        compiler_params=_cparams("arbitrary", "arbitrary"),
        name="ada_mod",
    )(c_all, ada_w, ada_b.reshape(depth, 1, 6 * D_MODEL))


def _even_pre_body(sample, tm, x_ref, mod_ref, ng_ref, win_ref, lng_ref, lnb_ref, wt_ref, bs_ref,
                   xa_ref, bout_ref, *vn_out):
    h = _mod_norm(x_ref[...], ng_ref[...], mod_ref[:, 0:D_MODEL], mod_ref[:, D_MODEL:2 * D_MODEL])
    proj = jnp.dot(h.astype(BF16), win_ref[...], preferred_element_type=F32)
    xa_ref[...] = proj[:, :S5_WIDTH]
    u = _gelu(proj[:, S5_WIDTH:S5_WIDTH + SGU_WIDTH])
    v = _gelu(proj[:, S5_WIDTH + SGU_WIDTH:])
    mu = jnp.mean(v, axis=-1, keepdims=True)
    var = jnp.mean(jnp.square(v - mu), axis=-1, keepdims=True)
    vn = (v - mu) * lax.rsqrt(var + EPS) * lng_ref[...] + lnb_ref[...]
    if sample:
        vn_out[0][...] = vn
        bout_ref[...] = (u * (vn * bs_ref[0:1, :] + bs_ref[1:2, :])).astype(BF16)
    else:
        vnb = vn.astype(BF16)
        hd = SGU_WIDTH // SGU_HEADS
        for ci in range(tm // CHUNK):
            rows = slice(ci * CHUNK, (ci + 1) * CHUNK)
            for hh in range(SGU_HEADS):
                cols = slice(hh * hd, (hh + 1) * hd)
                s = jnp.dot(wt_ref[hh], vnb[rows, cols], preferred_element_type=F32) + bs_ref[:, cols]
                bout_ref[rows, cols] = (u[rows, cols] * s).astype(BF16)
        xf_ref, stage = vn_out
        per = tm // S5_T
        for ov in range(S5_OCTETS):
            stage[ov] = proj[:, ov * S5_OLANES:(ov + 1) * S5_OLANES]
            for t in range(S5_T):
                c0 = (ov * S5_T + t) * S5_OLANES
                xf_ref[:, c0:c0 + S5_OLANES] = stage[ov, pl.ds(t, per, stride=S5_T), :].astype(BF16)


def _even_pre_call(sample, x, mod, ng, win, lng, lnb, wt, bs):
    rows = x.shape[0]
    tm = rows if sample else ROW_TILE
    row_spec = lambda w: pl.BlockSpec((tm, w), lambda i: (i, 0))
    out_shape = [jax.ShapeDtypeStruct((rows, S5_WIDTH), F32),
                 jax.ShapeDtypeStruct((rows, SGU_WIDTH), BF16)]
    out_specs = [row_spec(S5_WIDTH), row_spec(SGU_WIDTH)]
    scratch = []
    if sample:
        out_shape.append(jax.ShapeDtypeStruct((rows, SGU_WIDTH), F32))
        out_specs.append(row_spec(SGU_WIDTH))
    else:
        out_shape.append(jax.ShapeDtypeStruct((rows // S5_T, S5_T * S5_WIDTH), BF16))
        out_specs.append(pl.BlockSpec((tm // S5_T, S5_T * S5_WIDTH), lambda i: (i, 0)))
        scratch = [pltpu.VMEM((S5_OCTETS, tm, S5_OLANES), F32)]
    return pl.pallas_call(
        functools.partial(_even_pre_body, sample, tm),
        grid=(rows // tm,),
        in_specs=[row_spec(D_MODEL), _const_spec(mod.shape), _const_spec(ng.shape), _weight_spec(win.shape),
                  _const_spec(lng.shape), _const_spec(lnb.shape), _const_spec(wt.shape), _const_spec(bs.shape)],
        out_specs=out_specs,
        out_shape=out_shape,
        scratch_shapes=scratch,
        compiler_params=_cparams("arbitrary"),
        name="even_pre_sample" if sample else "even_pre_prompt",
    )(x, mod, ng, win, lng, lnb, wt, bs)


def _s5_prompt_body(x_ref, pbr_ref, pbi_ref, pcr_ref, pci_ref, a16_ref,
                    y_ref, hfin_ref, mtt, pb_re, pb_im, pc_re, pc_im, s_re, s_im):
    ol, ns = S5_OLANES, S5_OSTATE
    nlb = ns // ol
    nt_dims = (((1,), (1,)), ((), ()))

    r_grp = lax.broadcasted_iota(jnp.int32, (ol, ns), 0) // S5_GROUP
    c_grp = lax.broadcasted_iota(jnp.int32, (ol, ns), 1) // S5_STATE
    same_group = jnp.where(r_grp == c_grp, 1.0, 0.0).astype(BF16)
    for src, dst in ((pbr_ref, pb_re), (pbi_ref, pb_im), (pcr_ref, pc_re), (pci_ref, pc_im)):
        for t in range(src.shape[0]):
            dst[t * ol:(t + 1) * ol, :] = jnp.concatenate([src[t]] * nlb, axis=1) * same_group

    last = slice((S5_T - 1) * ol, S5_T * ol)
    lagk = (lax.dot_general(pc_re[0:S5_T * ol, :], pb_re[last, :], nt_dims, preferred_element_type=F32)
            + lax.dot_general(pc_im[0:S5_T * ol, :], pb_im[last, :], nt_dims, preferred_element_type=F32)
            ).astype(BF16)

    @pl.when(pl.program_id(0) == 0)
    def _():
        mtt[...] = jnp.zeros_like(mtt)

    for to in range(S5_T):
        for ti in range(to + 1):
            mtt[to * ol:(to + 1) * ol, ti * ol:(ti + 1) * ol] = lagk[(to - ti) * ol:(to - ti + 1) * ol, :]

    x = x_ref[...]
    nrows = x.shape[0]
    sr = jnp.dot(x, pb_re[...], preferred_element_type=F32)
    si = jnp.dot(x, pb_im[...], preferred_element_type=F32)
    row_id = lax.broadcasted_iota(jnp.int32, (nrows, 1), 0)

    def shifted(t, k):
        return jnp.where(row_id >= k, pltpu.roll(t, k, 0), 0.0)

    pr, pi = a16_ref[0:1, :], a16_ref[1:2, :]
    k = 1
    while k < S5_SUB:
        tr, ti = shifted(sr, k), shifted(si, k)
        sr, si = sr + pr * tr - pi * ti, si + pr * ti + pi * tr
        pr, pi = pr * pr - pi * pi, 2.0 * pr * pi
        k *= 2
    s_re[...] = sr
    s_im[...] = si
    hr = jnp.zeros((S5_SUB, ns), F32)
    hi = jnp.zeros((S5_SUB, ns), F32)
    for tile in range(nrows // S5_SUB):
        rows_t = slice(tile * S5_SUB, (tile + 1) * S5_SUB)
        hr, hi = pr * hr - pi * hi + s_re[rows_t, :], pr * hi + pi * hr + s_im[rows_t, :]
        s_re[rows_t, :] = hr
        s_im[rows_t, :] = hi
    hfin_ref[0:1, :] = hr[S5_SUB - 1:S5_SUB, :]
    hfin_ref[1:2, :] = hi[S5_SUB - 1:S5_SUB, :]
    hb_re = shifted(s_re[...], 1).astype(BF16)
    hb_im = shifted(s_im[...], 1).astype(BF16)

    nq = 4
    qw = x.shape[1] // nq
    for j in range(nq):
        cols = slice(j * qw, (j + 1) * qw)
        carry_rows = slice(j * qw + ol, (j + 1) * qw + ol)
        y_ref[:, cols] = (lax.dot_general(hb_re, pc_re[carry_rows, :], nt_dims, preferred_element_type=F32)
                          + lax.dot_general(hb_im, pc_im[carry_rows, :], nt_dims, preferred_element_type=F32)
                          + lax.dot_general(x_ref[:, :(j + 1) * qw], mtt[cols, :(j + 1) * qw], nt_dims,
                                            preferred_element_type=F32)).astype(y_ref.dtype)


def _s5_prompt_call(xf, pbr, pbi, pcr, pci, a16):
    rows = xf.shape[0]
    ow = S5_T * S5_OLANES
    grp = lambda a: pl.BlockSpec((None,) + a.shape[1:], lambda i: (i,) + (0,) * (a.ndim - 1))
    cols = pl.BlockSpec((rows, ow), lambda i: (0, i))
    return pl.pallas_call(
        _s5_prompt_body,
        grid=(S5_OCTETS,),
        in_specs=[cols, grp(pbr), grp(pbi), grp(pcr), grp(pci), grp(a16)],
        out_specs=[cols, pl.BlockSpec((None, 2, S5_OSTATE), lambda i: (i, 0, 0))],
        out_shape=[jax.ShapeDtypeStruct((rows, S5_OCTETS * ow), BF16),
                   jax.ShapeDtypeStruct((S5_OCTETS, 2, S5_OSTATE), F32)],
        scratch_shapes=[pltpu.VMEM((ow, ow), BF16)]
        + [pltpu.VMEM((ow, S5_OSTATE), BF16)] * 2
        + [pltpu.VMEM((ow + S5_OLANES, S5_OSTATE), BF16)] * 2
        + [pltpu.VMEM((rows, S5_OSTATE), F32)] * 2,
        compiler_params=_cparams("arbitrary"),
        name="s5_prompt",
    )(xf, pbr, pbi, pcr, pci, a16)


def _s5_sample_body(xa_ref, h0r_ref, h0i_ref, ar_ref, ai_ref, bdr_ref, bdi_ref, cdr_ref, cdi_ref,
                    yc_ref, hr_ref, hi_ref):
    u = xa_ref[...].astype(BF16)
    ar = ar_ref[...]
    ai = ai_ref[...]
    h0r = h0r_ref[...]
    h0i = h0i_ref[...]
    hr = ar * h0r - ai * h0i + jnp.dot(u, bdr_ref[...], preferred_element_type=F32)
    hi = ar * h0i + ai * h0r + jnp.dot(u, bdi_ref[...], preferred_element_type=F32)
    hr_ref[...] = hr
    hi_ref[...] = hi
    yc_ref[...] = (jnp.dot(hr.astype(BF16), cdr_ref[...], preferred_element_type=F32)
                   - jnp.dot(hi.astype(BF16), cdi_ref[...], preferred_element_type=F32))


def _s5_sample_call(xa, h0r, h0i, ar, ai, bdr, bdi, cdr, cdi):
    rows = xa.shape[0]
    n = S5_GROUPS * S5_STATE
    args = (xa, h0r, h0i, ar, ai, bdr, bdi, cdr, cdi)
    return pl.pallas_call(
        _s5_sample_body,
        grid=(1,),
        in_specs=[_const_spec(a.shape) for a in args],
        out_specs=[_const_spec((rows, S5_WIDTH)), _const_spec((rows, n)), _const_spec((rows, n))],
        out_shape=[jax.ShapeDtypeStruct((rows, S5_WIDTH), F32),
                   jax.ShapeDtypeStruct((rows, n), F32),
                   jax.ShapeDtypeStruct((rows, n), F32)],
        compiler_params=_cparams("arbitrary"),
        name="s5_sample",
    )(*args)


def _even_post_value(folded, tm, in_refs, scratch):
    x_ref, yc_ref, xa_ref, bout_ref, mod_ref, d_ref, wglu_ref, bglu_ref, wout_ref = in_refs
    if folded:
        stage = scratch[0]
        per = tm // S5_T
        for ov in range(S5_OCTETS):
            for t in range(S5_T):
                c0 = (ov * S5_T + t) * S5_OLANES
                stage[ov, pl.ds(t, per, stride=S5_T), :] = yc_ref[:, c0:c0 + S5_OLANES].astype(F32)
        yc = jnp.concatenate([stage[ov] for ov in range(S5_OCTETS)], axis=1)
    else:
        yc = yc_ref[...]
    y = _gelu(yc + d_ref[...] * xa_ref[...])
    gate = jax.nn.sigmoid(jnp.dot(y.astype(BF16), wglu_ref[...], preferred_element_type=F32) + bglu_ref[...])
    a_out = (y * gate).astype(BF16)
    mix = (jnp.dot(a_out, wout_ref[0:S5_WIDTH, :], preferred_element_type=F32)
           + jnp.dot(bout_ref[...], wout_ref[S5_WIDTH:, :], preferred_element_type=F32))
    return x_ref[...] + mod_ref[:, 2 * D_MODEL:3 * D_MODEL] * mix


def _even_post_stage(x, yc, xa, bout, mod, d, wglu, bglu, wout):
    rows = x.shape[0]
    tm = min(rows, ROW_TILE)
    row_spec = lambda w: pl.BlockSpec((tm, w), lambda i: (i, 0))
    folded = yc.shape[0] != rows
    yc_spec = pl.BlockSpec((tm // S5_T, S5_T * S5_WIDTH), lambda i: (i, 0)) if folded else row_spec(S5_WIDTH)
    scratch = [pltpu.VMEM((S5_OCTETS, tm, S5_OLANES), F32)] if folded else []
    in_specs = [row_spec(D_MODEL), yc_spec, row_spec(S5_WIDTH), row_spec(SGU_WIDTH),
                _const_spec(mod.shape), _const_spec(d.shape), _weight_spec(wglu.shape),
                _const_spec(bglu.shape), _weight_spec(wout.shape)]
    return (functools.partial(_even_post_value, folded, tm), [x, yc, xa, bout, mod, d, wglu, bglu, wout],
            in_specs, scratch)


def _ffn_body(sample, final, tm, pre, *refs):
    refs = list(refs)
    if pre is None:
        x_in = refs[0]
        refs = refs[1:]
    else:
        pre_fn, n_pre_in, n_pre_scratch = pre
        pre_in = refs[:n_pre_in]
        pre_scratch = refs[len(refs) - n_pre_scratch:]
        refs = refs[n_pre_in:len(refs) - n_pre_scratch]
    mod_ref, ng_ref, wup_ref, cw_ref, cb_ref, wdn_ref = refs[:6]
    pos = 6
    if sample:
        p2_ref, p1_ref = refs[pos:pos + 2]
        pos += 2
    if final:
        fg_ref = refs[pos]
        pos += 1
    o_ref, conv_ref = refs[pos:pos + 2]
    pos += 2
    if not sample:
        carry_ref = refs[pos]

        @pl.when(pl.program_id(0) == 0)
        def _():
            carry_ref[...] = jnp.zeros_like(carry_ref)

    x = x_in[...] if pre is None else pre_fn(pre_in, pre_scratch)
    h = _mod_norm(x, ng_ref[...], mod_ref[:, 3 * D_MODEL:4 * D_MODEL], mod_ref[:, 4 * D_MODEL:5 * D_MODEL])
    hb = h.astype(BF16)
    acc = jnp.zeros((tm, D_MODEL), F32)
    if not sample:
        row = lax.broadcasted_iota(jnp.int32, (tm, 1), 0)
    for c0, cw in FF_CHUNKS:
        cols = slice(c0, c0 + cw)
        a = jnp.dot(hb, wup_ref[:, cols], preferred_element_type=F32)
        g = jnp.dot(hb, wup_ref[:, D_FF + c0:D_FF + c0 + cw], preferred_element_type=F32)
        if sample:
            am2 = p2_ref[:, cols]
            am1 = p1_ref[:, cols]
            conv_ref[:, cols] = a
        else:
            prev2 = carry_ref[0:1, cols]
            prev1 = carry_ref[1:2, cols]
            am1 = jnp.where(row == 0, prev1, pltpu.roll(a, 1, 0))
            am2 = jnp.where(row == 0, prev2, jnp.where(row == 1, prev1, pltpu.roll(a, 2, 0)))
            carry_ref[0:2, cols] = a[tm - 2:tm, :]
        y = cb_ref[:, cols] + cw_ref[0:1, cols] * am2 + cw_ref[1:2, cols] * am1 + cw_ref[2:3, cols] * a
        act = (_gelu(y) * g).astype(BF16)
        acc = acc + jnp.dot(act, wdn_ref[cols, :], preferred_element_type=F32)
    out = x + mod_ref[:, 5 * D_MODEL:6 * D_MODEL] * acc
    if final:
        ms = jnp.mean(out * out, axis=-1, keepdims=True)
        out = out * lax.rsqrt(ms + EPS) * fg_ref[...]
    o_ref[...] = out
    if not sample:
        conv_ref[...] = carry_ref[0:2, :]


def _ffn_call(sample, layer, x, mod, ng, wup, cw, cb, wdn, prev=None, final_g=None, pre=None):
    rows = (x if pre is None else pre[1][0]).shape[0]
    tm = rows if sample else ROW_TILE
    final = final_g is not None
    row_spec = lambda w: pl.BlockSpec((tm, w), lambda i: (i, 0))
    of_layer = lambda a, **kw: pl.BlockSpec((None,) + a.shape[1:], lambda i: (layer,) + (0,) * (a.ndim - 1), **kw)
    once = dict(pipeline_mode=pl.Buffered(1))
    args = [mod, ng, wup, cw, cb, wdn]
    in_specs = [_const_spec(mod.shape), _const_spec(ng.shape), of_layer(wup, **once),
                of_layer(cw), of_layer(cb), of_layer(wdn, **once)]
    pre_static, pre_scratch = None, []
    if pre is None:
        args.insert(0, x)
        in_specs.insert(0, row_spec(D_MODEL))
    else:
        pre_fn, pre_args, pre_specs, pre_scratch = pre
        args = list(pre_args) + args
        in_specs = list(pre_specs) + in_specs
        pre_static = (pre_fn, len(pre_args), len(pre_scratch))
    if sample:
        args += [prev[0], prev[1]]
        in_specs += [_const_spec(prev[0].shape), _const_spec(prev[1].shape)]
    if final:
        args.append(final_g)
        in_specs.append(_const_spec(final_g.shape))
    conv_rows = rows if sample else 2
    return pl.pallas_call(
        functools.partial(_ffn_body, sample, final, tm, pre_static),
        grid=(rows // tm,),
        in_specs=in_specs,
        out_specs=[row_spec(D_MODEL), _const_spec((conv_rows, D_FF))],
        out_shape=[jax.ShapeDtypeStruct((rows, D_MODEL), F32),
                   jax.ShapeDtypeStruct((conv_rows, D_FF), F32)],
        scratch_shapes=([] if sample else [pltpu.VMEM((8, D_FF), F32)]) + list(pre_scratch),
        compiler_params=_cparams("arbitrary"),
        name="ffn_sample" if sample else "ffn_prompt",
    )(*args)


def _odd_pre_body(planar, tm, x_ref, mod_ref, ng_ref, wqkv_ref, tile_trig_ref, row_trig_ref,
                  q_ref, k_ref, v_ref, k32_ref, v32_ref, *rest):
    h = _mod_norm(x_ref[...], ng_ref[...], mod_ref[:, 0:D_MODEL], mod_ref[:, D_MODEL:2 * D_MODEL])
    qkv = jnp.dot(h.astype(BF16), wqkv_ref[...], preferred_element_type=F32)
    tt = tile_trig_ref[...]
    cos_sum = lambda f: tt[2 * f:2 * f + 1] * row_trig_ref[2 * f] - tt[2 * f + 1:2 * f + 2] * row_trig_ref[2 * f + 1]
    sin_sum = lambda f: tt[2 * f + 1:2 * f + 2] * row_trig_ref[2 * f] + tt[2 * f:2 * f + 1] * row_trig_ref[2 * f + 1]
    rc = cos_sum(0)
    ra = sin_sum(1)
    rb = -sin_sum(2)
    lanes = ROPE_LANES
    half = ROT_DIM // 2

    def rope(t):
        return t * rc + pltpu.roll(t, half, 1) * ra + pltpu.roll(t, lanes - half, 1) * rb

    nblk = D_MODEL // lanes
    if planar:
        qpl_ref, kpl_ref, vpl_ref, stage = rest
        per = tm // PLANES

        def to_planes(dst_ref, slot, cols, val):
            stage[slot] = val
            for r in range(PLANES):
                dst_ref[r, :, cols] = stage[slot, pl.ds(r, per, stride=PLANES), :].astype(BF16)

    for j in range(nblk):
        cols = slice(j * lanes, (j + 1) * lanes)
        q = rope(qkv[:, j * lanes:(j + 1) * lanes]) * (HEAD_DIM ** -0.5 * (LOG2_E if planar else 1.0))
        k = rope(qkv[:, D_MODEL + j * lanes:D_MODEL + (j + 1) * lanes])
        v = qkv[:, 2 * D_MODEL + j * lanes:2 * D_MODEL + (j + 1) * lanes]
        q_ref[:, cols] = q.astype(BF16)
        k_ref[:, cols] = k.astype(BF16)
        v_ref[:, cols] = v.astype(BF16)
        k32_ref[:, cols] = k
        v32_ref[:, cols] = v
        if planar:
            to_planes(qpl_ref, j, cols, q)
            to_planes(kpl_ref, nblk + j, cols, k)
            to_planes(vpl_ref, 2 * nblk + j, cols, v)


def _odd_pre_call(x, mod, ng, wqkv, tile_trig, row_trig, keep, planar):
    rows = x.shape[0]
    tm = min(rows, ROW_TILE)
    nt = rows // tm
    first_kept = (rows - keep) // tm
    row_spec = lambda w: pl.BlockSpec((tm, w), lambda i: (i, 0))
    keep_spec = pl.BlockSpec((tm, D_MODEL), lambda i: (jnp.maximum(i - first_kept, 0), 0))
    out_specs = [row_spec(D_MODEL), row_spec(D_MODEL), row_spec(D_MODEL), keep_spec, keep_spec]
    out_shape = ([jax.ShapeDtypeStruct((rows, D_MODEL), BF16)] * 3
                 + [jax.ShapeDtypeStruct((keep, D_MODEL), F32)] * 2)
    scratch = []
    if planar:
        plane_spec = pl.BlockSpec((PLANES, tm // PLANES, D_MODEL), lambda i: (0, i, 0))
        out_specs += [plane_spec] * 3
        out_shape += [jax.ShapeDtypeStruct((PLANES, rows // PLANES, D_MODEL), BF16)] * 3
        scratch = [pltpu.VMEM((3 * D_MODEL // ROPE_LANES, tm, ROPE_LANES), F32)]
    return pl.pallas_call(
        functools.partial(_odd_pre_body, planar, tm),
        grid=(nt,),
        in_specs=[row_spec(D_MODEL), _const_spec(mod.shape), _const_spec(ng.shape), _weight_spec(wqkv.shape),
                  pl.BlockSpec((None,) + tile_trig.shape[1:], lambda i: (i, 0, 0)), _const_spec(row_trig.shape)],
        out_specs=out_specs,
        out_shape=out_shape,
        scratch_shapes=scratch,
        compiler_params=_cparams("arbitrary"),
        name="odd_pre",
    )(x, mod, ng, wqkv, tile_trig, row_trig)


def _attn_prompt_body(n_planes, n_blocks, q_ref, kp_ref, kc_ref, vp_ref, vc_ref, o_ref, st_ref):
    per = BAND // n_planes
    qi = lax.broadcasted_iota(jnp.int32, (BAND, 2 * BAND), 0)
    kj = lax.broadcasted_iota(jnp.int32, (BAND, 2 * BAND), 1)
    q_pos = n_planes * (qi % per) + qi // per
    k_half = kj // BAND
    k_pos = n_planes * (per * k_half + kj % per) + (kj % BAND) // per - BAND
    dist = q_pos - k_pos
    in_band = (dist >= 0) & (dist <= BAND)
    no_prev = jnp.where(pl.program_id(1) == 0, 1, 0)
    bias_any = jnp.where(in_band, 0.0, NEG_INF).astype(F32)
    bias_first = jnp.where(in_band & (k_half >= no_prev), 0.0, NEG_INF).astype(F32)
    lane = lax.broadcasted_iota(jnp.int32, (BAND, 2 * HEAD_DIM), 1)
    low_half = lane < HEAD_DIM
    lane_row = lax.broadcasted_iota(jnp.int32, (1, 2 * HEAD_DIM), 1)
    head_keep = [jnp.where(lane_row < HEAD_DIM, 1.0, 0.0).astype(BF16),
                 jnp.where(lane_row < HEAD_DIM, 0.0, 1.0).astype(BF16)]
    st_lane = lax.broadcasted_iota(jnp.int32, (BAND, STAT_LANES), 1)
    st_out = [jnp.zeros((BAND, STAT_LANES), F32)] * n_blocks

    def rows_of(ref, u, cols):
        t = ref[u * BAND:(u + 1) * BAND, cols] if n_planes == 1 else ref[:, u * per:(u + 1) * per, cols]
        return t.reshape(BAND, t.shape[-1])

    for pair in range(N_HEADS // 2):
        cols = slice(pair * 2 * HEAD_DIM, (pair + 1) * 2 * HEAD_DIM)
        for u in range(n_blocks):
            qp = rows_of(q_ref, u, cols)
            k_before = rows_of(kp_ref, 0, cols) if u == 0 else rows_of(kc_ref, u - 1, cols)
            v_before = rows_of(vp_ref, 0, cols) if u == 0 else rows_of(vc_ref, u - 1, cols)
            kk = jnp.concatenate([k_before, rows_of(kc_ref, u, cols)], axis=0)
            vv = jnp.concatenate([v_before, rows_of(vc_ref, u, cols)], axis=0)
            bias = bias_first if u == 0 else bias_any
            q2 = jnp.concatenate([qp * head_keep[0], qp * head_keep[1]], axis=0)
            s = (lax.dot_general(q2, kk, (((1,), (1,)), ((), ())), preferred_element_type=F32)
                 + jnp.concatenate([bias, bias], axis=0))
            m = jnp.max(s, axis=-1, keepdims=True)
            p = jnp.exp2(s - m)
            l = jnp.sum(p, axis=-1, keepdims=True)
            pv = jnp.dot(p.astype(BF16), vv, preferred_element_type=F32) / l
            for sub in range(2):
                head = 2 * pair + sub
                st_out[u] = jnp.where(st_lane == head, m[sub * BAND:(sub + 1) * BAND], st_out[u])
                st_out[u] = jnp.where(st_lane == N_HEADS + head, l[sub * BAND:(sub + 1) * BAND], st_out[u])
            o_pair = jnp.where(low_half, pv[:BAND], pv[BAND:]).astype(o_ref.dtype)
            if n_planes == 1:
                o_ref[u * BAND:(u + 1) * BAND, cols] = o_pair
            else:
                o_ref[:, u * per:(u + 1) * per, cols] = o_pair.reshape(n_planes, per, 2 * HEAD_DIM)
    for u in range(n_blocks):
        if n_planes == 1:
            st_ref[u * BAND:(u + 1) * BAND, :] = st_out[u]
        else:
            st_ref[:, u * per:(u + 1) * per, :] = st_out[u].reshape(n_planes, per, STAT_LANES)


def _attn_prompt_call(d, q, k, v):
    planes, rpp, width = q.shape
    nu = ATT_BLOCKS
    before = lambda b: jnp.maximum(nu * b - 1, 0)
    if planes // d == 1 or d == 1:
        n_planes, outer = 1, planes
        view = lambda a: a
        cur = lambda w: pl.BlockSpec((None, nu * BAND, w), lambda r, b: (r, b, 0))
        prev = lambda w: pl.BlockSpec((None, BAND, w), lambda r, b: (r, before(b), 0))
        nb = rpp // (nu * BAND)
    else:
        n_planes, outer = planes // d, d
        per = BAND // n_planes
        view = lambda a: a.reshape(n_planes, outer, rpp, a.shape[-1])
        cur = lambda w: pl.BlockSpec((n_planes, None, nu * per, w), lambda r, b: (0, r, b, 0))
        prev = lambda w: pl.BlockSpec((n_planes, None, per, w), lambda r, b: (0, r, before(b), 0))
        nb = rpp // (nu * per)
    qv, kv, vv = view(q), view(k), view(v)
    o, st = pl.pallas_call(
        functools.partial(_attn_prompt_body, n_planes, nu),
        grid=(outer, nb),
        in_specs=[cur(width), prev(width), cur(width), prev(width), cur(width)],
        out_specs=[cur(width), cur(STAT_LANES)],
        out_shape=[jax.ShapeDtypeStruct(qv.shape, BF16),
                   jax.ShapeDtypeStruct(qv.shape[:-1] + (STAT_LANES,), F32)],
        compiler_params=_cparams("arbitrary", "arbitrary"),
        name="attn_prompt_d%d" % d,
    )(qv, kv, kv, vv, vv)
    return o.reshape(planes, rpp, width), st.reshape(planes, rpp, STAT_LANES)


def _attn_sample_body(hb, past, q_ref, kn_ref, vn_ref, kt_ref, vt_ref, o_ref):
    b = pl.program_id(1)
    dim = kt_ref.shape[1]
    q_row = q_ref[pl.ds(b, 1), :]
    kn_row = kn_ref[pl.ds(b, 1), :]
    vn_row = vn_ref[pl.ds(b, 1), :]
    on_diag = (lax.broadcasted_iota(jnp.int32, (dim, dim), 0)
               == lax.broadcasted_iota(jnp.int32, (dim, dim), 1))

    def to_col(row):
        return jnp.sum(jnp.where(on_diag, jnp.broadcast_to(row, (dim, dim)), 0.0), axis=1, keepdims=True)

    def to_row(col):
        return jnp.sum(jnp.where(on_diag, jnp.broadcast_to(col, (dim, dim)), 0.0), axis=0, keepdims=True)

    rows, news = [], []
    for h in range(hb):
        lanes = slice(h * dim, (h + 1) * dim)
        rows.append(jnp.sum(kt_ref[h] * to_col(q_row[:, lanes]), axis=0, keepdims=True))
        news.append(jnp.sum(kn_row[:, lanes] * q_row[:, lanes], axis=1, keepdims=True))
    s = jnp.concatenate(rows, axis=0)
    s_new = jnp.concatenate(news, axis=0)
    r = lax.broadcasted_iota(jnp.int32, (1, past), 1)
    ms, ls, ps, pns = [], [], [], []
    for window, d in DIL_BRANCHES:
        member = (r >= past - window) & ((past - r) % d == 0)
        sg = s + jnp.where(member, 0.0, NEG_INF).astype(F32)
        m = jnp.maximum(jnp.max(sg, axis=1, keepdims=True), s_new)
        p = jnp.exp(sg - m)
        pn = jnp.exp(s_new - m)
        ms.append(m)
        ps.append(p)
        pns.append(pn)
        ls.append(jnp.sum(p, axis=1, keepdims=True) + pn)
    m_all = jnp.maximum(jnp.maximum(ms[0], ms[1]), ms[2])
    cs = [jnp.exp(m - m_all) for m in ms]
    tot = cs[0] * ls[0] + cs[1] * ls[1] + cs[2] * ls[2]
    w = (cs[0] * ps[0] + cs[1] * ps[1] + cs[2] * ps[2]) / tot
    w_new = (cs[0] * pns[0] + cs[1] * pns[1] + cs[2] * pns[2]) / tot
    outs = []
    for h in range(hb):
        lanes = slice(h * dim, (h + 1) * dim)
        from_cache = jnp.sum(vt_ref[h] * w[h:h + 1, :], axis=1, keepdims=True)
        outs.append(to_row(from_cache) + vn_row[:, lanes] * w_new[h:h + 1, :])
    o_ref[pl.ds(b, 1), :] = jnp.concatenate(outs, axis=1)


def _attn_sample_call(q, k_new, v_new, kt, vt):
    bsz, heads, dim, past = kt.shape
    hb = heads
    rows = pl.BlockSpec((bsz, hb * dim), lambda j, b: (0, j))
    cache = pl.BlockSpec((None, hb, dim, past), lambda j, b: (b, j, 0, 0))
    return pl.pallas_call(
        functools.partial(_attn_sample_body, hb, past),
        grid=(heads // hb, bsz),
        in_specs=[rows, rows, rows, cache, cache],
        out_specs=rows,
        out_shape=jax.ShapeDtypeStruct((bsz, heads * dim), F32),
        compiler_params=_cparams("arbitrary", "arbitrary"),
        name="attn_sample",
    )(q, k_new, v_new, kt, vt)


def _odd_post_value(in_refs, scratch):
    x_ref, att_ref, mod_ref, wo_ref = in_refs
    mix = jnp.dot(att_ref[...].astype(BF16), wo_ref[...], preferred_element_type=F32)
    return x_ref[...] + mod_ref[:, 2 * D_MODEL:3 * D_MODEL] * mix


def _odd_post_stage(x, att, mod, wo):
    rows = x.shape[0]
    tm = min(rows, ROW_TILE)
    row_spec = pl.BlockSpec((tm, D_MODEL), lambda i: (i, 0))
    return (_odd_post_value, [x, att, mod, wo],
            [row_spec, row_spec, _const_spec(mod.shape), _weight_spec(wo.shape)], [])


def _odd_post_merge_value(tm, in_refs, scratch):
    x_ref, o1_ref, s1_ref, o4_ref, s4_ref, o16_ref, s16_ref, mod_ref, wo_ref, spread_ref = in_refs
    ob4, ob16, sb4, sb16 = scratch
    per = tm // PLANES
    nblk = D_MODEL // STAT_LANES
    for r in range(PLANES):
        rows_r = pl.ds(r, per, stride=PLANES)
        for j in range(nblk):
            cols = slice(j * STAT_LANES, (j + 1) * STAT_LANES)
            ob4[j, rows_r, :] = o4_ref[r, :, cols].astype(F32)
            ob16[j, rows_r, :] = o16_ref[r, :, cols].astype(F32)
        sb4[rows_r, :] = s4_ref[r]
        sb16[rows_r, :] = s16_ref[r]
    stats = [s1_ref[...], sb4[...], sb16[...]]
    dens = [pltpu.roll(s, STAT_LANES - N_HEADS, 1) for s in stats]
    m_all = jnp.maximum(jnp.maximum(stats[0], stats[1]), stats[2])
    ws = [d * jnp.exp2(s - m_all) for s, d in zip(stats, dens)]
    tot = ws[0] + ws[1] + ws[2]
    lane = lax.broadcasted_iota(jnp.int32, (tm, STAT_LANES), 1)
    spreads = []
    for w in ws[:2]:
        coef = jnp.where(lane < N_HEADS, w / tot, 0.0)
        hi = coef.astype(BF16)
        lo = (coef - hi.astype(F32)).astype(BF16)
        spreads.append(jnp.dot(jnp.concatenate([hi, lo], axis=1), spread_ref[...], preferred_element_type=F32))
    pieces = []
    for j in range(nblk):
        cols = slice(j * STAT_LANES, (j + 1) * STAT_LANES)
        last = ob16[j]
        pieces.append((last + spreads[0][:, cols] * (o1_ref[:, cols].astype(F32) - last)
                       + spreads[1][:, cols] * (ob4[j] - last)).astype(BF16))
    att = jnp.concatenate(pieces, axis=1)
    mix = jnp.dot(att, wo_ref[...], preferred_element_type=F32)
    return x_ref[...] + mod_ref[:, 2 * D_MODEL:3 * D_MODEL] * mix


def _odd_post_merge_stage(x, branches, mod, wo):
    tm = ROW_TILE
    (o1, s1), (o4, s4), (o16, s16) = branches
    row_spec = pl.BlockSpec((tm, D_MODEL), lambda i: (i, 0))
    nat = lambda w: pl.BlockSpec((None, tm, w), lambda i: (0, i, 0))
    pln = lambda w: pl.BlockSpec((PLANES, tm // PLANES, w), lambda i: (0, i, 0))
    head_of_lane = jnp.arange(D_MODEL) // HEAD_DIM
    spread = (jnp.arange(STAT_LANES)[:, None] == head_of_lane[None, :]).astype(BF16)
    spread = jnp.concatenate([spread, spread], axis=0)
    in_specs = [row_spec, nat(D_MODEL), nat(STAT_LANES), pln(D_MODEL), pln(STAT_LANES),
                pln(D_MODEL), pln(STAT_LANES), _const_spec(mod.shape), _weight_spec(wo.shape),
                _const_spec(spread.shape)]
    scratch = [pltpu.VMEM((D_MODEL // STAT_LANES, tm, STAT_LANES), F32),
               pltpu.VMEM((D_MODEL // STAT_LANES, tm, STAT_LANES), F32),
               pltpu.VMEM((tm, STAT_LANES), F32), pltpu.VMEM((tm, STAT_LANES), F32)]
    return (functools.partial(_odd_post_merge_value, tm), [x, o1, s1, o4, s4, o16, s16, mod, wo, spread],
            in_specs, scratch)


def _s5_tables(lam_re, lam_im, log_dt, b_re, b_im, c_re, c_im):
    dt = jnp.exp(log_dt)[:, None]
    lr, li = lam_re, lam_im
    ks = jnp.arange(S5_T + 1, dtype=F32)[:, None, None]
    mag = jnp.exp(ks * (lr * dt))
    pw_r = mag * jnp.cos(ks * (li * dt))
    pw_i = mag * jnp.sin(ks * (li * dt))
    ar, ai = pw_r[1], pw_i[1]
    den = lr * lr + li * li
    fr = ((ar - 1.0) * lr + ai * li) / den
    fi = (ai * lr - (ar - 1.0) * li) / den
    bbr = fr[..., None] * b_re - fi[..., None] * b_im
    bbi = fr[..., None] * b_im + fi[..., None] * b_re
    ca_r = c_re[None] * pw_r[:, :, None, :] - c_im[None] * pw_i[:, :, None, :]
    ca_i = c_re[None] * pw_i[:, :, None, :] + c_im[None] * pw_r[:, :, None, :]
    oc, og = S5_OCTETS, S5_OGROUPS
    split = lambda a, axis: a.reshape(a.shape[:axis] + (oc, og) + a.shape[axis + 1:])
    kr = (S5_T - 1) - jnp.arange(S5_T, dtype=F32)[:, None, None]
    rev_mag = jnp.exp(kr * (lr * dt))
    rev_r = rev_mag * jnp.cos(kr * (li * dt))
    rev_i = rev_mag * jnp.sin(kr * (li * dt))
    pb_r = rev_r[..., None] * bbr[None] - rev_i[..., None] * bbi[None]
    pb_i = rev_r[..., None] * bbi[None] + rev_i[..., None] * bbr[None]
    twice = lambda a: jnp.concatenate([a, a], axis=-1)
    fold_b = lambda a: twice(split(a, 1).transpose(1, 0, 2, 4, 3).reshape(oc, -1, S5_OLANES, S5_STATE))
    fold_c = lambda a: twice(split(a, 1).transpose(1, 0, 2, 3, 4).reshape(oc, -1, S5_OLANES, S5_STATE))
    pb_r, pb_i = fold_b(pb_r), fold_b(pb_i)
    pc_r, pc_i = fold_c(ca_r), fold_c(-ca_i)
    per_octet = lambda a: a.reshape(oc, S5_OSTATE)
    a16 = jnp.stack([per_octet(pw_r[S5_T]), per_octet(pw_i[S5_T])], axis=1)
    eye = jnp.eye(S5_GROUPS, dtype=F32)
    n_all = S5_GROUPS * S5_STATE
    bd_r = jnp.einsum('gnp,gh->gphn', bbr, eye).reshape(S5_WIDTH, n_all)
    bd_i = jnp.einsum('gnp,gh->gphn', bbi, eye).reshape(S5_WIDTH, n_all)
    cd_r = jnp.einsum('gpn,gh->gnhp', c_re, eye).reshape(n_all, S5_WIDTH)
    cd_i = jnp.einsum('gpn,gh->gnhp', c_im, eye).reshape(n_all, S5_WIDTH)
    return dict(pb_r=pb_r.astype(BF16), pb_i=pb_i.astype(BF16),
                pc_r=pc_r.astype(BF16), pc_i=pc_i.astype(BF16), a16=a16,
                a_r=ar.reshape(1, n_all), a_i=ai.reshape(1, n_all),
                bd_r=bd_r.astype(BF16), bd_i=bd_i.astype(BF16),
                cd_r=cd_r.astype(BF16), cd_i=cd_i.astype(BF16))


def _rope_tables(tile_pos, row_pos):
    half = ROT_DIM // 2
    inv = jnp.power(ROPE_THETA, -jnp.arange(half, dtype=F32) * 2.0 / ROT_DIM)
    e = jnp.arange(ROPE_LANES) % HEAD_DIM
    inv_e = inv[e % half]
    freqs = jnp.stack([jnp.where(e < ROT_DIM, inv_e, 0.0),
                       jnp.where((e >= half) & (e < ROT_DIM), inv_e, 0.0),
                       jnp.where(e < half, inv_e, 0.0)])

    def trig(pos):
        ang = pos.astype(F32)[None, :, None] * freqs[:, None, :]
        return jnp.stack([jnp.cos(ang), jnp.sin(ang)], axis=1).reshape(6, pos.shape[0], ROPE_LANES)

    tile_trig = jnp.concatenate([trig(tile_pos), jnp.zeros((2, tile_pos.shape[0], ROPE_LANES), F32)], axis=0)
    return tile_trig.transpose(1, 0, 2), trig(row_pos)


def _trunk(sample, x, mods, state, w):
    outs = {}
    mod = mods[0]
    if sample:
        xa, bout, vn = _even_pre_call(True, x, mod, w['ng'][0][0], w['ev_w_in'], w['sg_ln_g'], w['sg_ln_b'],
                                      w['sg_wt'], w['sg_row0'])
        outs['vn'] = vn
        t = w['s5']
        yc, hr, hi = _s5_sample_call(xa, state['s5_re'], state['s5_im'], t['a_r'], t['a_i'],
                                     t['bd_r'], t['bd_i'], t['cd_r'], t['cd_i'])
        outs['s5_re'], outs['s5_im'] = hr, hi
    else:
        xa, bout, xf = _even_pre_call(False, x, mod, w['ng'][0][0], w['ev_w_in'], w['sg_ln_g'], w['sg_ln_b'],
                                      w['sg_wt'], w['sg_bias'])
        t = w['s5']
        yc, hfin = _s5_prompt_call(xf, t['pb_r'], t['pb_i'], t['pc_r'], t['pc_i'], t['a16'])
        outs['s5_re'], outs['s5_im'] = hfin[:, 0], hfin[:, 1]
    mix0 = _even_post_stage(x, yc, xa, bout, mod, w['s5_d'], w['s5_w_glu'], w['s5_b_glu'], w['ev_w_out'])
    prev = (state['conv'][0][:, 0], state['conv'][0][:, 1]) if sample else None
    x, conv0 = _ffn_call(sample, 0, None, mod, w['ng'][0][1], w['ffn_w_up'], w['ffn_conv_w'],
                         w['ffn_conv_b'], w['ffn_w_down'], prev=prev, pre=mix0)
    mod = mods[1]
    rows = x.shape[0]
    keep = rows if sample else min(WIN_MAX, rows)
    pre = _odd_pre_call(x, mod, w['ng'][1][0], w['od_w_qkv'], *w['rope'], keep, planar=not sample)
    q, k, v, k32, v32 = pre[:5]
    outs['k'], outs['v'] = k32, v32
    if sample:
        att = _attn_sample_call(q.astype(F32), k32, v32, state['ck'], state['cv'])
        mix1 = _odd_post_stage(x, att, mod, w['od_w_o'])
    else:
        q_pl, k_pl, v_pl = pre[5:]
        branches = (_attn_prompt_call(1, q[None], k[None], v[None]),
                    _attn_prompt_call(4, q_pl, k_pl, v_pl),
                    _attn_prompt_call(16, q_pl, k_pl, v_pl))
        mix1 = _odd_post_merge_stage(x, branches, mod, w['od_w_o'])
    prev = (state['conv'][1][:, 0], state['conv'][1][:, 1]) if sample else None
    y, conv1 = _ffn_call(sample, 1, None, mod, w['ng'][1][1], w['ffn_w_up'], w['ffn_conv_w'],
                         w['ffn_conv_b'], w['ffn_w_down'], prev=prev, final_g=w['final_g'], pre=mix1)
    outs['y'] = y
    outs['conv'] = (conv0, conv1)
    return outs


def kernel(x_prompt, x_sample, c_prompt, c_sample, state_s5_re, state_s5_im, cache_c_k, cache_c_v,
           state_ffn_conv, ada_w, ada_b, norm_g, final_g, ev_w_in, ev_w_out, s5_lam_re, s5_lam_im,
           s5_log_dt, s5_b_re, s5_b_im, s5_c_re, s5_c_im, s5_d, s5_w_glu, s5_b_glu, sg_ln_g, sg_ln_b,
           sg_w, sg_b, od_w_qkv, od_w_o, ffn_w_up, ffn_conv_w, ffn_conv_b, ffn_w_down):
    bp, seq, _ = x_prompt.shape
    bs = x_sample.shape[0]
    assert bp == 1 and seq == SEQ and bs == DEC_BATCH and x_sample.shape[1] == 1

    c_all = jnp.concatenate([c_sample, c_prompt, jnp.zeros((MOD_ROWS - bs - bp, D_MODEL), F32)], axis=0)
    mod_all = _ada_call(c_all, ada_w, ada_b)
    mods_s = [mod_all[l, :bs] for l in range(2)]
    mods_p = [mod_all[l, bs:bs + 1] for l in range(2)]

    hd = SGU_WIDTH // SGU_HEADS
    causal = jnp.tril(jnp.ones((CHUNK, CHUNK), F32))
    w = dict(
        ng=[[norm_g[l, j].reshape(1, D_MODEL) for j in range(2)] for l in range(2)],
        final_g=final_g.reshape(1, D_MODEL),
        ev_w_in=ev_w_in[0].astype(BF16), ev_w_out=ev_w_out[0].astype(BF16),
        sg_ln_g=sg_ln_g[0].reshape(1, SGU_WIDTH), sg_ln_b=sg_ln_b[0].reshape(1, SGU_WIDTH),
        sg_wt=(sg_w[0] * causal[None]).astype(BF16),
        sg_bias=jnp.repeat(sg_b[0].T, hd, axis=1),
        sg_row0=jnp.stack([jnp.repeat(sg_w[0, :, 0, 0], hd), jnp.repeat(sg_b[0, :, 0], hd)], axis=0),
        s5=_s5_tables(s5_lam_re[0], s5_lam_im[0], s5_log_dt[0], s5_b_re[0], s5_b_im[0], s5_c_re[0], s5_c_im[0]),
        s5_d=s5_d[0].reshape(1, S5_WIDTH), s5_w_glu=s5_w_glu[0].astype(BF16),
        s5_b_glu=s5_b_glu[0].reshape(1, S5_WIDTH),
        od_w_qkv=od_w_qkv[0].astype(BF16), od_w_o=od_w_o[0].astype(BF16),
        ffn_w_up=ffn_w_up.astype(BF16), ffn_w_down=ffn_w_down.astype(BF16),
        ffn_conv_w=ffn_conv_w, ffn_conv_b=ffn_conv_b.reshape(2, 1, D_FF),
    )

    wp = dict(w, rope=_rope_tables(jnp.arange(0, seq, ROW_TILE, dtype=jnp.int32),
                                   jnp.arange(ROW_TILE, dtype=jnp.int32)))
    ws = dict(w, rope=_rope_tables(jnp.full((1,), PAST_LEN, jnp.int32), jnp.zeros((bs,), jnp.int32)))

    p = _trunk(False, x_prompt[0], mods_p, None, wp)
    n_all = S5_GROUPS * S5_STATE
    state = dict(s5_re=state_s5_re[0].reshape(bs, n_all), s5_im=state_s5_im[0].reshape(bs, n_all),
                 ck=jnp.transpose(cache_c_k[0], (0, 2, 3, 1)), cv=jnp.transpose(cache_c_v[0], (0, 2, 3, 1)),
                 conv=state_ffn_conv)
    s = _trunk(True, x_sample[:, 0], mods_s, state, ws)

    keep = min(WIN_MAX, seq)
    kv_p = lambda a: a.reshape(1, 1, keep, N_HEADS, HEAD_DIM)
    kv_s = lambda a: a.reshape(1, bs, 1, N_HEADS, HEAD_DIM)
    s5_p = lambda a: a.reshape(1, 1, S5_GROUPS, S5_STATE)
    s5_s = lambda a: a.reshape(1, bs, S5_GROUPS, S5_STATE)
    conv_p = jnp.stack([c.reshape(1, 2, D_FF) for c in p['conv']])
    conv_s = jnp.stack([jnp.stack([state_ffn_conv[l][:, 1], s['conv'][l]], axis=1) for l in range(2)])
    return (p['y'][None], s['y'][:, None], s5_p(p['s5_re']), s5_p(p['s5_im']),
            s5_s(s['s5_re']), s5_s(s['s5_im']), s['vn'].reshape(1, bs, 1, SGU_WIDTH),
            kv_p(p['k']), kv_p(p['v']), kv_s(s['k']), kv_s(s['v']), conv_p, conv_s)
```

```python
import functools

import jax
import jax.numpy as jnp
from jax import lax
from jax.experimental import pallas as pl
from jax.experimental.pallas import tpu as pltpu

F32 = jnp.float32
BF16 = jnp.bfloat16

D_MODEL = 1024
SEQ = 16384
DEC_BATCH = 32
PAST_LEN = 16384
S5_WIDTH = 512
S5_GROUP = 16
S5_GROUPS = 32
S5_STATE = 64
SGU_WIDTH = 512
SGU_HEADS = 4
CHUNK = 128
EVEN_IN = S5_WIDTH + 2 * SGU_WIDTH
HEAD_DIM = 64
N_HEADS = 16
ROT_DIM = 16
ROPE_THETA = 500000.0
DIL_BRANCHES = ((128, 1), (512, 4), (2048, 16))
BAND = 128
WIN_MAX = 2048
D_FF = 2816
EPS = 1e-6
NEG_INF = -1e30

ROW_TILE = 512
MOD_ROWS = 40
S5_T = 16
S5_SUB = 8
S5_OLANES = 128
S5_OCTETS = S5_WIDTH // S5_OLANES
S5_OGROUPS = S5_OLANES // S5_GROUP
S5_OSTATE = S5_OGROUPS * S5_STATE
FF_CHUNKS = ((0, D_FF),)
ROPE_LANES = 128
PLANES = 16
LOG2_E = 1.4426950408889634
ATT_BLOCKS = 8
STAT_LANES = 128
VMEM_LIMIT = 56 * 1024 * 1024


def _cparams(*sem):
    return pltpu.CompilerParams(dimension_semantics=sem, vmem_limit_bytes=VMEM_LIMIT)


def _const_spec(shape):
    nd = len(shape)
    return pl.BlockSpec(shape, lambda *_: (0,) * nd)


def _weight_spec(shape):
    nd = len(shape)
    return pl.BlockSpec(shape, lambda *_: (0,) * nd, pipeline_mode=pl.Buffered(1))


def _gelu(x):
    return jax.nn.gelu(x)


def _mod_norm(x, ng, shift, scale):
    ms = jnp.mean(x * x, axis=-1, keepdims=True)
    return (x * lax.rsqrt(ms + EPS) * ng) * (1.0 + scale) + shift


def _ada_body(c_ref, w_ref, b_ref, o_ref):
    c = c_ref[...]
    s = c * jax.nn.sigmoid(c)
    o_ref[...] = jnp.dot(s.astype(BF16), w_ref[...].astype(BF16),
                         preferred_element_type=F32) + b_ref[...]


def _ada_call(c_all, ada_w, ada_b):
    depth = ada_w.shape[0]
    nt = 1536
    return pl.pallas_call(
        _ada_body,
        grid=(depth, 6 * D_MODEL // nt),
        in_specs=[
            pl.BlockSpec((MOD_ROWS, D_MODEL), lambda l, j: (0, 0)),
            pl.BlockSpec((None, D_MODEL, nt), lambda l, j: (l, 0, j)),
            pl.BlockSpec((None, 1, nt), lambda l, j: (l, 0, j)),
        ],
        out_specs=pl.BlockSpec((None, MOD_ROWS, nt), lambda l, j: (l, 0, j)),
        out_shape=jax.ShapeDtypeStruct((depth, MOD_ROWS, 6 * D_MODEL), F32),
        compiler_params=_cparams("arbitrary", "arbitrary"),
        name="ada_mod",
    )(c_all, ada_w, ada_b.reshape(depth, 1, 6 * D_MODEL))


def _even_pre_body(sample, tm, x_ref, mod_ref, ng_ref, win_ref, lng_ref, lnb_ref, wt_ref, bs_ref,
                   xa_ref, bout_ref, *vn_out):
    h = _mod_norm(x_ref[...], ng_ref[...], mod_ref[:, 0:D_MODEL], mod_ref[:, D_MODEL:2 * D_MODEL])
    proj = jnp.dot(h.astype(BF16), win_ref[...], preferred_element_type=F32)
    xa_ref[...] = proj[:, :S5_WIDTH]
    u = _gelu(proj[:, S5_WIDTH:S5_WIDTH + SGU_WIDTH])
    v = _gelu(proj[:, S5_WIDTH + SGU_WIDTH:])
    mu = jnp.mean(v, axis=-1, keepdims=True)
    var = jnp.mean(jnp.square(v - mu), axis=-1, keepdims=True)
    vn = (v - mu) * lax.rsqrt(var + EPS) * lng_ref[...] + lnb_ref[...]
    if sample:
        vn_out[0][...] = vn
        bout_ref[...] = (u * (vn * bs_ref[0:1, :] + bs_ref[1:2, :])).astype(BF16)
    else:
        vnb = vn.astype(BF16)
        hd = SGU_WIDTH // SGU_HEADS
        for ci in range(tm // CHUNK):
            rows = slice(ci * CHUNK, (ci + 1) * CHUNK)
            for hh in range(SGU_HEADS):
                cols = slice(hh * hd, (hh + 1) * hd)
                s = jnp.dot(wt_ref[hh], vnb[rows, cols], preferred_element_type=F32) + bs_ref[:, cols]
                bout_ref[rows, cols] = (u[rows, cols] * s).astype(BF16)
        xf_ref, stage = vn_out
        per = tm // S5_T
        for ov in range(S5_OCTETS):
            stage[ov] = proj[:, ov * S5_OLANES:(ov + 1) * S5_OLANES]
            for t in range(S5_T):
                c0 = (ov * S5_T + t) * S5_OLANES
                xf_ref[:, c0:c0 + S5_OLANES] = stage[ov, pl.ds(t, per, stride=S5_T), :].astype(BF16)


def _even_pre_call(sample, x, mod, ng, win, lng, lnb, wt, bs):
    rows = x.shape[0]
    tm = rows if sample else ROW_TILE
    row_spec = lambda w: pl.BlockSpec((tm, w), lambda i: (i, 0))
    out_shape = [jax.ShapeDtypeStruct((rows, S5_WIDTH), F32),
                 jax.ShapeDtypeStruct((rows, SGU_WIDTH), BF16)]
    out_specs = [row_spec(S5_WIDTH), row_spec(SGU_WIDTH)]
    scratch = []
    if sample:
        out_shape.append(jax.ShapeDtypeStruct((rows, SGU_WIDTH), F32))
        out_specs.append(row_spec(SGU_WIDTH))
    else:
        out_shape.append(jax.ShapeDtypeStruct((rows // S5_T, S5_T * S5_WIDTH), BF16))
        out_specs.append(pl.BlockSpec((tm // S5_T, S5_T * S5_WIDTH), lambda i: (i, 0)))
        scratch = [pltpu.VMEM((S5_OCTETS, tm, S5_OLANES), F32)]
    return pl.pallas_call(
        functools.partial(_even_pre_body, sample, tm),
        grid=(rows // tm,),
        in_specs=[row_spec(D_MODEL), _const_spec(mod.shape), _const_spec(ng.shape), _weight_spec(win.shape),
                  _const_spec(lng.shape), _const_spec(lnb.shape), _const_spec(wt.shape), _const_spec(bs.shape)],
        out_specs=out_specs,
        out_shape=out_shape,
        scratch_shapes=scratch,
        compiler_params=_cparams("arbitrary"),
        name="even_pre_sample" if sample else "even_pre_prompt",
    )(x, mod, ng, win, lng, lnb, wt, bs)


def _s5_prompt_body(x_ref, pbr_ref, pbi_ref, pcr_ref, pci_ref, a16_ref,
                    y_ref, hfin_ref, mtt, pb_re, pb_im, pc_re, pc_im, s_re, s_im):
    ol, ns = S5_OLANES, S5_OSTATE
    nlb = ns // ol
    nt_dims = (((1,), (1,)), ((), ()))

    r_grp = lax.broadcasted_iota(jnp.int32, (ol, ns), 0) // S5_GROUP
    c_grp = lax.broadcasted_iota(jnp.int32, (ol, ns), 1) // S5_STATE
    same_group = jnp.where(r_grp == c_grp, 1.0, 0.0).astype(BF16)
    for src, dst in ((pbr_ref, pb_re), (pbi_ref, pb_im), (pcr_ref, pc_re), (pci_ref, pc_im)):
        for t in range(src.shape[0]):
            dst[t * ol:(t + 1) * ol, :] = jnp.concatenate([src[t]] * nlb, axis=1) * same_group

    last = slice((S5_T - 1) * ol, S5_T * ol)
    lagk = (lax.dot_general(pc_re[0:S5_T * ol, :], pb_re[last, :], nt_dims, preferred_element_type=F32)
            + lax.dot_general(pc_im[0:S5_T * ol, :], pb_im[last, :], nt_dims, preferred_element_type=F32)
            ).astype(BF16)

    @pl.when(pl.program_id(0) == 0)
    def _():
        mtt[...] = jnp.zeros_like(mtt)

    for to in range(S5_T):
        for ti in range(to + 1):
            mtt[to * ol:(to + 1) * ol, ti * ol:(ti + 1) * ol] = lagk[(to - ti) * ol:(to - ti + 1) * ol, :]

    x = x_ref[...]
    nrows = x.shape[0]
    sr = jnp.dot(x, pb_re[...], preferred_element_type=F32)
    si = jnp.dot(x, pb_im[...], preferred_element_type=F32)
    row_id = lax.broadcasted_iota(jnp.int32, (nrows, 1), 0)

    def shifted(t, k):
        return jnp.where(row_id >= k, pltpu.roll(t, k, 0), 0.0)

    pr, pi = a16_ref[0:1, :], a16_ref[1:2, :]
    k = 1
    while k < S5_SUB:
        tr, ti = shifted(sr, k), shifted(si, k)
        sr, si = sr + pr * tr - pi * ti, si + pr * ti + pi * tr
        pr, pi = pr * pr - pi * pi, 2.0 * pr * pi
        k *= 2
    s_re[...] = sr
    s_im[...] = si
    hr = jnp.zeros((S5_SUB, ns), F32)
    hi = jnp.zeros((S5_SUB, ns), F32)
    for tile in range(nrows // S5_SUB):
        rows_t = slice(tile * S5_SUB, (tile + 1) * S5_SUB)
        hr, hi = pr * hr - pi * hi + s_re[rows_t, :], pr * hi + pi * hr + s_im[rows_t, :]
        s_re[rows_t, :] = hr
        s_im[rows_t, :] = hi
    hfin_ref[0:1, :] = hr[S5_SUB - 1:S5_SUB, :]
    hfin_ref[1:2, :] = hi[S5_SUB - 1:S5_SUB, :]
    hb_re = shifted(s_re[...], 1).astype(BF16)
    hb_im = shifted(s_im[...], 1).astype(BF16)

    nq = 4
    qw = x.shape[1] // nq
    for j in range(nq):
        cols = slice(j * qw, (j + 1) * qw)
        carry_rows = slice(j * qw + ol, (j + 1) * qw + ol)
        y_ref[:, cols] = (lax.dot_general(hb_re, pc_re[carry_rows, :], nt_dims, preferred_element_type=F32)
                          + lax.dot_general(hb_im, pc_im[carry_rows, :], nt_dims, preferred_element_type=F32)
                          + lax.dot_general(x_ref[:, :(j + 1) * qw], mtt[cols, :(j + 1) * qw], nt_dims,
                                            preferred_element_type=F32)).astype(y_ref.dtype)


def _s5_prompt_call(xf, pbr, pbi, pcr, pci, a16):
    rows = xf.shape[0]
    ow = S5_T * S5_OLANES
    grp = lambda a: pl.BlockSpec((None,) + a.shape[1:], lambda i: (i,) + (0,) * (a.ndim - 1))
    cols = pl.BlockSpec((rows, ow), lambda i: (0, i))
    return pl.pallas_call(
        _s5_prompt_body,
        grid=(S5_OCTETS,),
        in_specs=[cols, grp(pbr), grp(pbi), grp(pcr), grp(pci), grp(a16)],
        out_specs=[cols, pl.BlockSpec((None, 2, S5_OSTATE), lambda i: (i, 0, 0))],
        out_shape=[jax.ShapeDtypeStruct((rows, S5_OCTETS * ow), BF16),
                   jax.ShapeDtypeStruct((S5_OCTETS, 2, S5_OSTATE), F32)],
        scratch_shapes=[pltpu.VMEM((ow, ow), BF16)]
        + [pltpu.VMEM((ow, S5_OSTATE), BF16)] * 2
        + [pltpu.VMEM((ow + S5_OLANES, S5_OSTATE), BF16)] * 2
        + [pltpu.VMEM((rows, S5_OSTATE), F32)] * 2,
        compiler_params=_cparams("arbitrary"),
        name="s5_prompt",
    )(xf, pbr, pbi, pcr, pci, a16)


def _s5_sample_body(xa_ref, h0r_ref, h0i_ref, ar_ref, ai_ref, bdr_ref, bdi_ref, cdr_ref, cdi_ref,
                    yc_ref, hr_ref, hi_ref):
    u = xa_ref[...].astype(BF16)
    ar = ar_ref[...]
    ai = ai_ref[...]
    h0r = h0r_ref[...]
    h0i = h0i_ref[...]
    hr = ar * h0r - ai * h0i + jnp.dot(u, bdr_ref[...], preferred_element_type=F32)
    hi = ar * h0i + ai * h0r + jnp.dot(u, bdi_ref[...], preferred_element_type=F32)
    hr_ref[...] = hr
    hi_ref[...] = hi
    yc_ref[...] = (jnp.dot(hr.astype(BF16), cdr_ref[...], preferred_element_type=F32)
                   - jnp.dot(hi.astype(BF16), cdi_ref[...], preferred_element_type=F32))


def _s5_sample_call(xa, h0r, h0i, ar, ai, bdr, bdi, cdr, cdi):
    rows = xa.shape[0]
    n = S5_GROUPS * S5_STATE
    args = (xa, h0r, h0i, ar, ai, bdr, bdi, cdr, cdi)
    return pl.pallas_call(
        _s5_sample_body,
        grid=(1,),
        in_specs=[_const_spec(a.shape) for a in args],
        out_specs=[_const_spec((rows, S5_WIDTH)), _const_spec((rows, n)), _const_spec((rows, n))],
        out_shape=[jax.ShapeDtypeStruct((rows, S5_WIDTH), F32),
                   jax.ShapeDtypeStruct((rows, n), F32),
                   jax.ShapeDtypeStruct((rows, n), F32)],
        compiler_params=_cparams("arbitrary"),
        name="s5_sample",
    )(*args)


def _even_post_value(folded, tm, in_refs, scratch):
    x_ref, yc_ref, xa_ref, bout_ref, mod_ref, d_ref, wglu_ref, bglu_ref, wout_ref = in_refs
    if folded:
        stage = scratch[0]
        per = tm // S5_T
        for ov in range(S5_OCTETS):
            for t in range(S5_T):
                c0 = (ov * S5_T + t) * S5_OLANES
                stage[ov, pl.ds(t, per, stride=S5_T), :] = yc_ref[:, c0:c0 + S5_OLANES].astype(F32)
        yc = jnp.concatenate([stage[ov] for ov in range(S5_OCTETS)], axis=1)
    else:
        yc = yc_ref[...]
    y = _gelu(yc + d_ref[...] * xa_ref[...])
    gate = jax.nn.sigmoid(jnp.dot(y.astype(BF16), wglu_ref[...], preferred_element_type=F32) + bglu_ref[...])
    a_out = (y * gate).astype(BF16)
    mix = (jnp.dot(a_out, wout_ref[0:S5_WIDTH, :], preferred_element_type=F32)
           + jnp.dot(bout_ref[...], wout_ref[S5_WIDTH:, :], preferred_element_type=F32))
    return x_ref[...] + mod_ref[:, 2 * D_MODEL:3 * D_MODEL] * mix


def _even_post_stage(x, yc, xa, bout, mod, d, wglu, bglu, wout):
    rows = x.shape[0]
    tm = min(rows, ROW_TILE)
    row_spec = lambda w: pl.BlockSpec((tm, w), lambda i: (i, 0))
    folded = yc.shape[0] != rows
    yc_spec = pl.BlockSpec((tm // S5_T, S5_T * S5_WIDTH), lambda i: (i, 0)) if folded else row_spec(S5_WIDTH)
    scratch = [pltpu.VMEM((S5_OCTETS, tm, S5_OLANES), F32)] if folded else []
    in_specs = [row_spec(D_MODEL), yc_spec, row_spec(S5_WIDTH), row_spec(SGU_WIDTH),
                _const_spec(mod.shape), _const_spec(d.shape), _weight_spec(wglu.shape),
                _const_spec(bglu.shape), _weight_spec(wout.shape)]
    return (functools.partial(_even_post_value, folded, tm), [x, yc, xa, bout, mod, d, wglu, bglu, wout],
            in_specs, scratch)


def _ffn_body(sample, final, tm, pre, *refs):
    refs = list(refs)
    if pre is None:
        x_in = refs[0]
        refs = refs[1:]
    else:
        pre_fn, n_pre_in, n_pre_scratch = pre
        pre_in = refs[:n_pre_in]
        pre_scratch = refs[len(refs) - n_pre_scratch:]
        refs = refs[n_pre_in:len(refs) - n_pre_scratch]
    mod_ref, ng_ref, wup_ref, cw_ref, cb_ref, wdn_ref = refs[:6]
    pos = 6
    if sample:
        p2_ref, p1_ref = refs[pos:pos + 2]
        pos += 2
    if final:
        fg_ref = refs[pos]
        pos += 1
    o_ref, conv_ref = refs[pos:pos + 2]
    pos += 2
    if not sample:
        carry_ref = refs[pos]

        @pl.when(pl.program_id(0) == 0)
        def _():
            carry_ref[...] = jnp.zeros_like(carry_ref)

    x = x_in[...] if pre is None else pre_fn(pre_in, pre_scratch)
    h = _mod_norm(x, ng_ref[...], mod_ref[:, 3 * D_MODEL:4 * D_MODEL], mod_ref[:, 4 * D_MODEL:5 * D_MODEL])
    hb = h.astype(BF16)
    acc = jnp.zeros((tm, D_MODEL), F32)
    if not sample:
        row = lax.broadcasted_iota(jnp.int32, (tm, 1), 0)
    for c0, cw in FF_CHUNKS:
        cols = slice(c0, c0 + cw)
        a = jnp.dot(hb, wup_ref[:, cols], preferred_element_type=F32)
        g = jnp.dot(hb, wup_ref[:, D_FF + c0:D_FF + c0 + cw], preferred_element_type=F32)
        if sample:
            am2 = p2_ref[:, cols]
            am1 = p1_ref[:, cols]
            conv_ref[:, cols] = a
        else:
            prev2 = carry_ref[0:1, cols]
            prev1 = carry_ref[1:2, cols]
            am1 = jnp.where(row == 0, prev1, pltpu.roll(a, 1, 0))
            am2 = jnp.where(row == 0, prev2, jnp.where(row == 1, prev1, pltpu.roll(a, 2, 0)))
            carry_ref[0:2, cols] = a[tm - 2:tm, :]
        y = cb_ref[:, cols] + cw_ref[0:1, cols] * am2 + cw_ref[1:2, cols] * am1 + cw_ref[2:3, cols] * a
        act = (_gelu(y) * g).astype(BF16)
        acc = acc + jnp.dot(act, wdn_ref[cols, :], preferred_element_type=F32)
    out = x + mod_ref[:, 5 * D_MODEL:6 * D_MODEL] * acc
    if final:
        ms = jnp.mean(out * out, axis=-1, keepdims=True)
        out = out * lax.rsqrt(ms + EPS) * fg_ref[...]
    o_ref[...] = out
    if not sample:
        conv_ref[...] = carry_ref[0:2, :]


def _ffn_call(sample, layer, x, mod, ng, wup, cw, cb, wdn, prev=None, final_g=None, pre=None):
    rows = (x if pre is None else pre[1][0]).shape[0]
    tm = rows if sample else ROW_TILE
    final = final_g is not None
    row_spec = lambda w: pl.BlockSpec((tm, w), lambda i: (i, 0))
    of_layer = lambda a, **kw: pl.BlockSpec((None,) + a.shape[1:], lambda i: (layer,) + (0,) * (a.ndim - 1), **kw)
    once = dict(pipeline_mode=pl.Buffered(1))
    args = [mod, ng, wup, cw, cb, wdn]
    in_specs = [_const_spec(mod.shape), _const_spec(ng.shape), of_layer(wup, **once),
                of_layer(cw), of_layer(cb), of_layer(wdn, **once)]
    pre_static, pre_scratch = None, []
    if pre is None:
        args.insert(0, x)
        in_specs.insert(0, row_spec(D_MODEL))
    else:
        pre_fn, pre_args, pre_specs, pre_scratch = pre
        args = list(pre_args) + args
        in_specs = list(pre_specs) + in_specs
        pre_static = (pre_fn, len(pre_args), len(pre_scratch))
    if sample:
        args += [prev[0], prev[1]]
        in_specs += [_const_spec(prev[0].shape), _const_spec(prev[1].shape)]
    if final:
        args.append(final_g)
        in_specs.append(_const_spec(final_g.shape))
    conv_rows = rows if sample else 2
    return pl.pallas_call(
        functools.partial(_ffn_body, sample, final, tm, pre_static),
        grid=(rows // tm,),
        in_specs=in_specs,
        out_specs=[row_spec(D_MODEL), _const_spec((conv_rows, D_FF))],
        out_shape=[jax.ShapeDtypeStruct((rows, D_MODEL), F32),
                   jax.ShapeDtypeStruct((conv_rows, D_FF), F32)],
        scratch_shapes=([] if sample else [pltpu.VMEM((8, D_FF), F32)]) + list(pre_scratch),
        compiler_params=_cparams("arbitrary"),
        name="ffn_sample" if sample else "ffn_prompt",
    )(*args)


def _odd_pre_body(planar, tm, first_kept, x_ref, mod_ref, ng_ref, wqkv_ref, tile_trig_ref, row_trig_ref,
                  q_ref, k_ref, v_ref, k32_ref, v32_ref, *rest):
    h = _mod_norm(x_ref[...], ng_ref[...], mod_ref[:, 0:D_MODEL], mod_ref[:, D_MODEL:2 * D_MODEL])
    qkv = jnp.dot(h.astype(BF16), wqkv_ref[...], preferred_element_type=F32)
    tt = tile_trig_ref[...]
    cos_sum = lambda f: tt[2 * f:2 * f + 1] * row_trig_ref[2 * f] - tt[2 * f + 1:2 * f + 2] * row_trig_ref[2 * f + 1]
    sin_sum = lambda f: tt[2 * f + 1:2 * f + 2] * row_trig_ref[2 * f] + tt[2 * f:2 * f + 1] * row_trig_ref[2 * f + 1]
    rc = cos_sum(0)
    ra = sin_sum(1)
    rb = -sin_sum(2)
    lanes = ROPE_LANES
    half = ROT_DIM // 2

    def rope(t):
        return t * rc + pltpu.roll(t, half, 1) * ra + pltpu.roll(t, lanes - half, 1) * rb

    nblk = D_MODEL // lanes
    if planar:
        qpl_ref, kpl_ref, vpl_ref, stage = rest
        per = tm // PLANES

        def to_planes(dst_ref, slot, cols, val):
            stage[slot] = val
            for r in range(PLANES):
                dst_ref[r, :, cols] = stage[slot, pl.ds(r, per, stride=PLANES), :].astype(BF16)

    for j in range(nblk):
        cols = slice(j * lanes, (j + 1) * lanes)
        q = rope(qkv[:, j * lanes:(j + 1) * lanes]) * (HEAD_DIM ** -0.5 * (LOG2_E if planar else 1.0))
        k = rope(qkv[:, D_MODEL + j * lanes:D_MODEL + (j + 1) * lanes])
        v = qkv[:, 2 * D_MODEL + j * lanes:2 * D_MODEL + (j + 1) * lanes]
        q_ref[:, cols] = q.astype(BF16)
        k_ref[:, cols] = k.astype(BF16)
        v_ref[:, cols] = v.astype(BF16)
        if planar:
            to_planes(qpl_ref, j, cols, q)
            to_planes(kpl_ref, nblk + j, cols, k)
            to_planes(vpl_ref, 2 * nblk + j, cols, v)
        else:
            k32_ref[:, cols] = k
            v32_ref[:, cols] = v
    if planar:
        @pl.when(pl.program_id(0) >= first_kept)
        def _():
            for j in range(nblk):
                cols = slice(j * lanes, (j + 1) * lanes)
                k32_ref[:, cols] = stage[nblk + j]
                v32_ref[:, cols] = stage[2 * nblk + j]


def _odd_pre_call(x, mod, ng, wqkv, tile_trig, row_trig, keep, planar):
    rows = x.shape[0]
    tm = min(rows, ROW_TILE)
    nt = rows // tm
    first_kept = (rows - keep) // tm
    row_spec = lambda w: pl.BlockSpec((tm, w), lambda i: (i, 0))
    keep_spec = pl.BlockSpec((tm, D_MODEL), lambda i: (jnp.maximum(i - first_kept, 0), 0))
    out_specs = [row_spec(D_MODEL), row_spec(D_MODEL), row_spec(D_MODEL), keep_spec, keep_spec]
    out_shape = ([jax.ShapeDtypeStruct((rows, D_MODEL), BF16)] * 3
                 + [jax.ShapeDtypeStruct((keep, D_MODEL), F32)] * 2)
    scratch = []
    if planar:
        plane_spec = pl.BlockSpec((PLANES, tm // PLANES, D_MODEL), lambda i: (0, i, 0))
        out_specs += [plane_spec] * 3
        out_shape += [jax.ShapeDtypeStruct((PLANES, rows // PLANES, D_MODEL), BF16)] * 3
        scratch = [pltpu.VMEM((3 * D_MODEL // ROPE_LANES, tm, ROPE_LANES), F32)]
    return pl.pallas_call(
        functools.partial(_odd_pre_body, planar, tm, first_kept),
        grid=(nt,),
        in_specs=[row_spec(D_MODEL), _const_spec(mod.shape), _const_spec(ng.shape), _weight_spec(wqkv.shape),
                  pl.BlockSpec((None,) + tile_trig.shape[1:], lambda i: (i, 0, 0)), _const_spec(row_trig.shape)],
        out_specs=out_specs,
        out_shape=out_shape,
        scratch_shapes=scratch,
        compiler_params=_cparams("arbitrary"),
        name="odd_pre",
    )(x, mod, ng, wqkv, tile_trig, row_trig)


def _attn_prompt_body(n_planes, n_blocks, q_ref, kp_ref, kc_ref, vp_ref, vc_ref, o_ref, st_ref):
    per = BAND // n_planes
    qi = lax.broadcasted_iota(jnp.int32, (BAND, 2 * BAND), 0)
    kj = lax.broadcasted_iota(jnp.int32, (BAND, 2 * BAND), 1)
    q_pos = n_planes * (qi % per) + qi // per
    k_half = kj // BAND
    k_pos = n_planes * (per * k_half + kj % per) + (kj % BAND) // per - BAND
    dist = q_pos - k_pos
    in_band = (dist >= 0) & (dist <= BAND)
    no_prev = jnp.where(pl.program_id(1) == 0, 1, 0)
    bias_any = jnp.where(in_band, 0.0, NEG_INF).astype(F32)
    bias_first = jnp.where(in_band & (k_half >= no_prev), 0.0, NEG_INF).astype(F32)
    lane = lax.broadcasted_iota(jnp.int32, (BAND, 2 * HEAD_DIM), 1)
    low_half = lane < HEAD_DIM
    lane_row = lax.broadcasted_iota(jnp.int32, (1, 2 * HEAD_DIM), 1)
    head_keep = [jnp.where(lane_row < HEAD_DIM, 1.0, 0.0).astype(BF16),
                 jnp.where(lane_row < HEAD_DIM, 0.0, 1.0).astype(BF16)]
    st_lane = lax.broadcasted_iota(jnp.int32, (BAND, STAT_LANES), 1)
    st_out = [jnp.zeros((BAND, STAT_LANES), F32)] * n_blocks

    def rows_of(ref, u, cols):
        t = ref[u * BAND:(u + 1) * BAND, cols] if n_planes == 1 else ref[:, u * per:(u + 1) * per, cols]
        return t.reshape(BAND, t.shape[-1])

    for pair in range(N_HEADS // 2):
        cols = slice(pair * 2 * HEAD_DIM, (pair + 1) * 2 * HEAD_DIM)
        for u in range(n_blocks):
            qp = rows_of(q_ref, u, cols)
            k_before = rows_of(kp_ref, 0, cols) if u == 0 else rows_of(kc_ref, u - 1, cols)
            v_before = rows_of(vp_ref, 0, cols) if u == 0 else rows_of(vc_ref, u - 1, cols)
            kk = jnp.concatenate([k_before, rows_of(kc_ref, u, cols)], axis=0)
            vv = jnp.concatenate([v_before, rows_of(vc_ref, u, cols)], axis=0)
            bias = bias_first if u == 0 else bias_any
            q2 = jnp.concatenate([qp * head_keep[0], qp * head_keep[1]], axis=0)
            s = (lax.dot_general(q2, kk, (((1,), (1,)), ((), ())), preferred_element_type=F32)
                 + jnp.concatenate([bias, bias], axis=0))
            m = jnp.max(s, axis=-1, keepdims=True)
            p = jnp.exp2(s - m)
            l = jnp.sum(p, axis=-1, keepdims=True)
            pv = jnp.dot(p.astype(BF16), vv, preferred_element_type=F32) / l
            for sub in range(2):
                head = 2 * pair + sub
                st_out[u] = jnp.where(st_lane == head, m[sub * BAND:(sub + 1) * BAND], st_out[u])
                st_out[u] = jnp.where(st_lane == N_HEADS + head, l[sub * BAND:(sub + 1) * BAND], st_out[u])
            o_pair = jnp.where(low_half, pv[:BAND], pv[BAND:]).astype(o_ref.dtype)
            if n_planes == 1:
                o_ref[u * BAND:(u + 1) * BAND, cols] = o_pair
            else:
                o_ref[:, u * per:(u + 1) * per, cols] = o_pair.reshape(n_planes, per, 2 * HEAD_DIM)
    for u in range(n_blocks):
        if n_planes == 1:
            st_ref[u * BAND:(u + 1) * BAND, :] = st_out[u]
        else:
            st_ref[:, u * per:(u + 1) * per, :] = st_out[u].reshape(n_planes, per, STAT_LANES)


def _attn_prompt_call(d, q, k, v):
    planes, rpp, width = q.shape
    nu = ATT_BLOCKS
    before = lambda b: jnp.maximum(nu * b - 1, 0)
    if planes // d == 1 or d == 1:
        n_planes, outer = 1, planes
        view = lambda a: a
        cur = lambda w: pl.BlockSpec((None, nu * BAND, w), lambda r, b: (r, b, 0))
        prev = lambda w: pl.BlockSpec((None, BAND, w), lambda r, b: (r, before(b), 0))
        nb = rpp // (nu * BAND)
    else:
        n_planes, outer = planes // d, d
        per = BAND // n_planes
        view = lambda a: a.reshape(n_planes, outer, rpp, a.shape[-1])
        cur = lambda w: pl.BlockSpec((n_planes, None, nu * per, w), lambda r, b: (0, r, b, 0))
        prev = lambda w: pl.BlockSpec((n_planes, None, per, w), lambda r, b: (0, r, before(b), 0))
        nb = rpp // (nu * per)
    qv, kv, vv = view(q), view(k), view(v)
    o, st = pl.pallas_call(
        functools.partial(_attn_prompt_body, n_planes, nu),
        grid=(outer, nb),
        in_specs=[cur(width), prev(width), cur(width), prev(width), cur(width)],
        out_specs=[cur(width), cur(STAT_LANES)],
        out_shape=[jax.ShapeDtypeStruct(qv.shape, BF16),
                   jax.ShapeDtypeStruct(qv.shape[:-1] + (STAT_LANES,), F32)],
        compiler_params=_cparams("arbitrary", "arbitrary"),
        name="attn_prompt_d%d" % d,
    )(qv, kv, kv, vv, vv)
    return o.reshape(planes, rpp, width), st.reshape(planes, rpp, STAT_LANES)


def _attn_sample_body(hb, past, q_ref, kn_ref, vn_ref, kt_ref, vt_ref, o_ref):
    b = pl.program_id(1)
    dim = kt_ref.shape[1]
    q_row = q_ref[pl.ds(b, 1), :]
    kn_row = kn_ref[pl.ds(b, 1), :]
    vn_row = vn_ref[pl.ds(b, 1), :]
    on_diag = (lax.broadcasted_iota(jnp.int32, (dim, dim), 0)
               == lax.broadcasted_iota(jnp.int32, (dim, dim), 1))

    def to_col(row):
        return jnp.sum(jnp.where(on_diag, jnp.broadcast_to(row, (dim, dim)), 0.0), axis=1, keepdims=True)

    def to_row(col):
        return jnp.sum(jnp.where(on_diag, jnp.broadcast_to(col, (dim, dim)), 0.0), axis=0, keepdims=True)

    rows, news = [], []
    for h in range(hb):
        lanes = slice(h * dim, (h + 1) * dim)
        rows.append(jnp.sum(kt_ref[h] * to_col(q_row[:, lanes]), axis=0, keepdims=True))
        news.append(jnp.sum(kn_row[:, lanes] * q_row[:, lanes], axis=1, keepdims=True))
    s = jnp.concatenate(rows, axis=0)
    s_new = jnp.concatenate(news, axis=0)
    r = lax.broadcasted_iota(jnp.int32, (1, past), 1)
    ms, ls, ps, pns = [], [], [], []
    for window, d in DIL_BRANCHES:
        member = (r >= past - window) & ((past - r) % d == 0)
        sg = s + jnp.where(member, 0.0, NEG_INF).astype(F32)
        m = jnp.maximum(jnp.max(sg, axis=1, keepdims=True), s_new)
        p = jnp.exp(sg - m)
        pn = jnp.exp(s_new - m)
        ms.append(m)
        ps.append(p)
        pns.append(pn)
        ls.append(jnp.sum(p, axis=1, keepdims=True) + pn)
    m_all = jnp.maximum(jnp.maximum(ms[0], ms[1]), ms[2])
    cs = [jnp.exp(m - m_all) for m in ms]
    tot = cs[0] * ls[0] + cs[1] * ls[1] + cs[2] * ls[2]
    w = (cs[0] * ps[0] + cs[1] * ps[1] + cs[2] * ps[2]) / tot
    w_new = (cs[0] * pns[0] + cs[1] * pns[1] + cs[2] * pns[2]) / tot
    outs = []
    for h in range(hb):
        lanes = slice(h * dim, (h + 1) * dim)
        from_cache = jnp.sum(vt_ref[h] * w[h:h + 1, :], axis=1, keepdims=True)
        outs.append(to_row(from_cache) + vn_row[:, lanes] * w_new[h:h + 1, :])
    o_ref[pl.ds(b, 1), :] = jnp.concatenate(outs, axis=1)


def _attn_sample_call(q, k_new, v_new, kt, vt):
    bsz, heads, dim, past = kt.shape
    hb = heads
    rows = pl.BlockSpec((bsz, hb * dim), lambda j, b: (0, j))
    cache = pl.BlockSpec((None, hb, dim, past), lambda j, b: (b, j, 0, 0))
    return pl.pallas_call(
        functools.partial(_attn_sample_body, hb, past),
        grid=(heads // hb, bsz),
        in_specs=[rows, rows, rows, cache, cache],
        out_specs=rows,
        out_shape=jax.ShapeDtypeStruct((bsz, heads * dim), F32),
        compiler_params=_cparams("arbitrary", "arbitrary"),
        name="attn_sample",
    )(q, k_new, v_new, kt, vt)


def _odd_post_value(in_refs, scratch):
    x_ref, att_ref, mod_ref, wo_ref = in_refs
    mix = jnp.dot(att_ref[...].astype(BF16), wo_ref[...], preferred_element_type=F32)
    return x_ref[...] + mod_ref[:, 2 * D_MODEL:3 * D_MODEL] * mix


def _odd_post_stage(x, att, mod, wo):
    rows = x.shape[0]
    tm = min(rows, ROW_TILE)
    row_spec = pl.BlockSpec((tm, D_MODEL), lambda i: (i, 0))
    return (_odd_post_value, [x, att, mod, wo],
            [row_spec, row_spec, _const_spec(mod.shape), _weight_spec(wo.shape)], [])


def _odd_post_merge_value(tm, in_refs, scratch):
    x_ref, o1_ref, s1_ref, o4_ref, s4_ref, o16_ref, s16_ref, mod_ref, wo_ref, spread_ref = in_refs
    ob4, ob16, sb4, sb16 = scratch
    per = tm // PLANES
    nblk = D_MODEL // STAT_LANES
    for r in range(PLANES):
        rows_r = pl.ds(r, per, stride=PLANES)
        for j in range(nblk):
            cols = slice(j * STAT_LANES, (j + 1) * STAT_LANES)
            ob4[j, rows_r, :] = o4_ref[r, :, cols].astype(F32)
            ob16[j, rows_r, :] = o16_ref[r, :, cols].astype(F32)
        sb4[rows_r, :] = s4_ref[r]
        sb16[rows_r, :] = s16_ref[r]
    stats = [s1_ref[...], sb4[...], sb16[...]]
    dens = [pltpu.roll(s, STAT_LANES - N_HEADS, 1) for s in stats]
    m_all = jnp.maximum(jnp.maximum(stats[0], stats[1]), stats[2])
    ws = [d * jnp.exp2(s - m_all) for s, d in zip(stats, dens)]
    tot = ws[0] + ws[1] + ws[2]
    lane = lax.broadcasted_iota(jnp.int32, (tm, STAT_LANES), 1)
    spreads = []
    for w in ws[:2]:
        coef = jnp.where(lane < N_HEADS, w / tot, 0.0)
        hi = coef.astype(BF16)
        lo = (coef - hi.astype(F32)).astype(BF16)
        spreads.append(jnp.dot(jnp.concatenate([hi, lo], axis=1), spread_ref[...], preferred_element_type=F32))
    pieces = []
    for j in range(nblk):
        cols = slice(j * STAT_LANES, (j + 1) * STAT_LANES)
        last = ob16[j]
        pieces.append((last + spreads[0][:, cols] * (o1_ref[:, cols].astype(F32) - last)
                       + spreads[1][:, cols] * (ob4[j] - last)).astype(BF16))
    att = jnp.concatenate(pieces, axis=1)
    mix = jnp.dot(att, wo_ref[...], preferred_element_type=F32)
    return x_ref[...] + mod_ref[:, 2 * D_MODEL:3 * D_MODEL] * mix


def _odd_post_merge_stage(x, branches, mod, wo):
    tm = ROW_TILE
    (o1, s1), (o4, s4), (o16, s16) = branches
    row_spec = pl.BlockSpec((tm, D_MODEL), lambda i: (i, 0))
    nat = lambda w: pl.BlockSpec((None, tm, w), lambda i: (0, i, 0))
    pln = lambda w: pl.BlockSpec((PLANES, tm // PLANES, w), lambda i: (0, i, 0))
    head_of_lane = jnp.arange(D_MODEL) // HEAD_DIM
    spread = (jnp.arange(STAT_LANES)[:, None] == head_of_lane[None, :]).astype(BF16)
    spread = jnp.concatenate([spread, spread], axis=0)
    in_specs = [row_spec, nat(D_MODEL), nat(STAT_LANES), pln(D_MODEL), pln(STAT_LANES),
                pln(D_MODEL), pln(STAT_LANES), _const_spec(mod.shape), _weight_spec(wo.shape),
                _const_spec(spread.shape)]
    scratch = [pltpu.VMEM((D_MODEL // STAT_LANES, tm, STAT_LANES), F32),
               pltpu.VMEM((D_MODEL // STAT_LANES, tm, STAT_LANES), F32),
               pltpu.VMEM((tm, STAT_LANES), F32), pltpu.VMEM((tm, STAT_LANES), F32)]
    return (functools.partial(_odd_post_merge_value, tm), [x, o1, s1, o4, s4, o16, s16, mod, wo, spread],
            in_specs, scratch)


def _s5_tables(lam_re, lam_im, log_dt, b_re, b_im, c_re, c_im):
    dt = jnp.exp(log_dt)[:, None]
    lr, li = lam_re, lam_im
    ks = jnp.arange(S5_T + 1, dtype=F32)[:, None, None]
    mag = jnp.exp(ks * (lr * dt))
    pw_r = mag * jnp.cos(ks * (li * dt))
    pw_i = mag * jnp.sin(ks * (li * dt))
    ar, ai = pw_r[1], pw_i[1]
    den = lr * lr + li * li
    fr = ((ar - 1.0) * lr + ai * li) / den
    fi = (ai * lr - (ar - 1.0) * li) / den
    bbr = fr[..., None] * b_re - fi[..., None] * b_im
    bbi = fr[..., None] * b_im + fi[..., None] * b_re
    ca_r = c_re[None] * pw_r[:, :, None, :] - c_im[None] * pw_i[:, :, None, :]
    ca_i = c_re[None] * pw_i[:, :, None, :] + c_im[None] * pw_r[:, :, None, :]
    oc, og = S5_OCTETS, S5_OGROUPS
    split = lambda a, axis: a.reshape(a.shape[:axis] + (oc, og) + a.shape[axis + 1:])
    kr = (S5_T - 1) - jnp.arange(S5_T, dtype=F32)[:, None, None]
    rev_mag = jnp.exp(kr * (lr * dt))
    rev_r = rev_mag * jnp.cos(kr * (li * dt))
    rev_i = rev_mag * jnp.sin(kr * (li * dt))
    pb_r = rev_r[..., None] * bbr[None] - rev_i[..., None] * bbi[None]
    pb_i = rev_r[..., None] * bbi[None] + rev_i[..., None] * bbr[None]
    twice = lambda a: jnp.concatenate([a, a], axis=-1)
    fold_b = lambda a: twice(split(a, 1).transpose(1, 0, 2, 4, 3).reshape(oc, -1, S5_OLANES, S5_STATE))
    fold_c = lambda a: twice(split(a, 1).transpose(1, 0, 2, 3, 4).reshape(oc, -1, S5_OLANES, S5_STATE))
    pb_r, pb_i = fold_b(pb_r), fold_b(pb_i)
    pc_r, pc_i = fold_c(ca_r), fold_c(-ca_i)
    per_octet = lambda a: a.reshape(oc, S5_OSTATE)
    a16 = jnp.stack([per_octet(pw_r[S5_T]), per_octet(pw_i[S5_T])], axis=1)
    eye = jnp.eye(S5_GROUPS, dtype=F32)
    n_all = S5_GROUPS * S5_STATE
    bd_r = jnp.einsum('gnp,gh->gphn', bbr, eye).reshape(S5_WIDTH, n_all)
    bd_i = jnp.einsum('gnp,gh->gphn', bbi, eye).reshape(S5_WIDTH, n_all)
    cd_r = jnp.einsum('gpn,gh->gnhp', c_re, eye).reshape(n_all, S5_WIDTH)
    cd_i = jnp.einsum('gpn,gh->gnhp', c_im, eye).reshape(n_all, S5_WIDTH)
    return dict(pb_r=pb_r.astype(BF16), pb_i=pb_i.astype(BF16),
                pc_r=pc_r.astype(BF16), pc_i=pc_i.astype(BF16), a16=a16,
                a_r=ar.reshape(1, n_all), a_i=ai.reshape(1, n_all),
                bd_r=bd_r.astype(BF16), bd_i=bd_i.astype(BF16),
                cd_r=cd_r.astype(BF16), cd_i=cd_i.astype(BF16))


def _rope_tables(tile_pos, row_pos):
    half = ROT_DIM // 2
    inv = jnp.power(ROPE_THETA, -jnp.arange(half, dtype=F32) * 2.0 / ROT_DIM)
    e = jnp.arange(ROPE_LANES) % HEAD_DIM
    inv_e = inv[e % half]
    freqs = jnp.stack([jnp.where(e < ROT_DIM, inv_e, 0.0),
                       jnp.where((e >= half) & (e < ROT_DIM), inv_e, 0.0),
                       jnp.where(e < half, inv_e, 0.0)])

    def trig(pos):
        ang = pos.astype(F32)[None, :, None] * freqs[:, None, :]
        return jnp.stack([jnp.cos(ang), jnp.sin(ang)], axis=1).reshape(6, pos.shape[0], ROPE_LANES)

    tile_trig = jnp.concatenate([trig(tile_pos), jnp.zeros((2, tile_pos.shape[0], ROPE_LANES), F32)], axis=0)
    return tile_trig.transpose(1, 0, 2), trig(row_pos)


def _trunk(sample, x, mods, state, w):
    outs = {}
    mod = mods[0]
    if sample:
        xa, bout, vn = _even_pre_call(True, x, mod, w['ng'][0][0], w['ev_w_in'], w['sg_ln_g'], w['sg_ln_b'],
                                      w['sg_wt'], w['sg_row0'])
        outs['vn'] = vn
        t = w['s5']
        yc, hr, hi = _s5_sample_call(xa, state['s5_re'], state['s5_im'], t['a_r'], t['a_i'],
                                     t['bd_r'], t['bd_i'], t['cd_r'], t['cd_i'])
        outs['s5_re'], outs['s5_im'] = hr, hi
    else:
        xa, bout, xf = _even_pre_call(False, x, mod, w['ng'][0][0], w['ev_w_in'], w['sg_ln_g'], w['sg_ln_b'],
                                      w['sg_wt'], w['sg_bias'])
        t = w['s5']
        yc, hfin = _s5_prompt_call(xf, t['pb_r'], t['pb_i'], t['pc_r'], t['pc_i'], t['a16'])
        outs['s5_re'], outs['s5_im'] = hfin[:, 0], hfin[:, 1]
    mix0 = _even_post_stage(x, yc, xa, bout, mod, w['s5_d'], w['s5_w_glu'], w['s5_b_glu'], w['ev_w_out'])
    prev = (state['conv'][0][:, 0], state['conv'][0][:, 1]) if sample else None
    x, conv0 = _ffn_call(sample, 0, None, mod, w['ng'][0][1], w['ffn_w_up'], w['ffn_conv_w'],
                         w['ffn_conv_b'], w['ffn_w_down'], prev=prev, pre=mix0)
    mod = mods[1]
    rows = x.shape[0]
    keep = rows if sample else min(WIN_MAX, rows)
    pre = _odd_pre_call(x, mod, w['ng'][1][0], w['od_w_qkv'], *w['rope'], keep, planar=not sample)
    q, k, v, k32, v32 = pre[:5]
    outs['k'], outs['v'] = k32, v32
    if sample:
        att = _attn_sample_call(q.astype(F32), k32, v32, state['ck'], state['cv'])
        mix1 = _odd_post_stage(x, att, mod, w['od_w_o'])
    else:
        q_pl, k_pl, v_pl = pre[5:]
        branches = (_attn_prompt_call(1, q[None], k[None], v[None]),
                    _attn_prompt_call(4, q_pl, k_pl, v_pl),
                    _attn_prompt_call(16, q_pl, k_pl, v_pl))
        mix1 = _odd_post_merge_stage(x, branches, mod, w['od_w_o'])
    prev = (state['conv'][1][:, 0], state['conv'][1][:, 1]) if sample else None
    y, conv1 = _ffn_call(sample, 1, None, mod, w['ng'][1][1], w['ffn_w_up'], w['ffn_conv_w'],
                         w['ffn_conv_b'], w['ffn_w_down'], prev=prev, final_g=w['final_g'], pre=mix1)
    outs['y'] = y
    outs['conv'] = (conv0, conv1)
    return outs


def kernel(x_prompt, x_sample, c_prompt, c_sample, state_s5_re, state_s5_im, cache_c_k, cache_c_v,
           state_ffn_conv, ada_w, ada_b, norm_g, final_g, ev_w_in, ev_w_out, s5_lam_re, s5_lam_im,
           s5_log_dt, s5_b_re, s5_b_im, s5_c_re, s5_c_im, s5_d, s5_w_glu, s5_b_glu, sg_ln_g, sg_ln_b,
           sg_w, sg_b, od_w_qkv, od_w_o, ffn_w_up, ffn_conv_w, ffn_conv_b, ffn_w_down):
    bp, seq, _ = x_prompt.shape
    bs = x_sample.shape[0]
    assert bp == 1 and seq == SEQ and bs == DEC_BATCH and x_sample.shape[1] == 1

    c_all = jnp.concatenate([c_sample, c_prompt, jnp.zeros((MOD_ROWS - bs - bp, D_MODEL), F32)], axis=0)
    mod_all = _ada_call(c_all, ada_w, ada_b)
    mods_s = [mod_all[l, :bs] for l in range(2)]
    mods_p = [mod_all[l, bs:bs + 1] for l in range(2)]

    hd = SGU_WIDTH // SGU_HEADS
    causal = jnp.tril(jnp.ones((CHUNK, CHUNK), F32))
    w = dict(
        ng=[[norm_g[l, j].reshape(1, D_MODEL) for j in range(2)] for l in range(2)],
        final_g=final_g.reshape(1, D_MODEL),
        ev_w_in=ev_w_in[0].astype(BF16), ev_w_out=ev_w_out[0].astype(BF16),
        sg_ln_g=sg_ln_g[0].reshape(1, SGU_WIDTH), sg_ln_b=sg_ln_b[0].reshape(1, SGU_WIDTH),
        sg_wt=(sg_w[0] * causal[None]).astype(BF16),
        sg_bias=jnp.repeat(sg_b[0].T, hd, axis=1),
        sg_row0=jnp.stack([jnp.repeat(sg_w[0, :, 0, 0], hd), jnp.repeat(sg_b[0, :, 0], hd)], axis=0),
        s5=_s5_tables(s5_lam_re[0], s5_lam_im[0], s5_log_dt[0], s5_b_re[0], s5_b_im[0], s5_c_re[0], s5_c_im[0]),
        s5_d=s5_d[0].reshape(1, S5_WIDTH), s5_w_glu=s5_w_glu[0].astype(BF16),
        s5_b_glu=s5_b_glu[0].reshape(1, S5_WIDTH),
        od_w_qkv=od_w_qkv[0].astype(BF16), od_w_o=od_w_o[0].astype(BF16),
        ffn_w_up=ffn_w_up.astype(BF16), ffn_w_down=ffn_w_down.astype(BF16),
        ffn_conv_w=ffn_conv_w, ffn_conv_b=ffn_conv_b.reshape(2, 1, D_FF),
    )

    wp = dict(w, rope=_rope_tables(jnp.arange(0, seq, ROW_TILE, dtype=jnp.int32),
                                   jnp.arange(ROW_TILE, dtype=jnp.int32)))
    ws = dict(w, rope=_rope_tables(jnp.full((1,), PAST_LEN, jnp.int32), jnp.zeros((bs,), jnp.int32)))

    p = _trunk(False, x_prompt[0], mods_p, None, wp)
    n_all = S5_GROUPS * S5_STATE
    state = dict(s5_re=state_s5_re[0].reshape(bs, n_all), s5_im=state_s5_im[0].reshape(bs, n_all),
                 ck=jnp.transpose(cache_c_k[0], (0, 2, 3, 1)), cv=jnp.transpose(cache_c_v[0], (0, 2, 3, 1)),
                 conv=state_ffn_conv)
    s = _trunk(True, x_sample[:, 0], mods_s, state, ws)

    keep = min(WIN_MAX, seq)
    kv_p = lambda a: a.reshape(1, 1, keep, N_HEADS, HEAD_DIM)
    kv_s = lambda a: a.reshape(1, bs, 1, N_HEADS, HEAD_DIM)
    s5_p = lambda a: a.reshape(1, 1, S5_GROUPS, S5_STATE)
    s5_s = lambda a: a.reshape(1, bs, S5_GROUPS, S5_STATE)
    conv_p = jnp.stack([c.reshape(1, 2, D_FF) for c in p['conv']])
    conv_s = jnp.stack([jnp.stack([state_ffn_conv[l][:, 1], s['conv'][l]], axis=1) for l in range(2)])
    return (p['y'][None], s['y'][:, None], s5_p(p['s5_re']), s5_p(p['s5_im']),
            s5_s(s['s5_re']), s5_s(s['s5_im']), s['vn'].reshape(1, bs, 1, SGU_WIDTH),
            kv_p(p['k']), kv_p(p['v']), kv_s(s['k']), kv_s(s['v']), conv_p, conv_s)
```

```python
import functools

import jax
import jax.numpy as jnp
from jax import lax
from jax.experimental import pallas as pl
from jax.experimental.pallas import tpu as pltpu

F32 = jnp.float32
BF16 = jnp.bfloat16

D_MODEL = 1024
SEQ = 16384
DEC_BATCH = 32
PAST_LEN = 16384
S5_WIDTH = 512
S5_GROUP = 16
S5_GROUPS = 32
S5_STATE = 64
SGU_WIDTH = 512
SGU_HEADS = 4
CHUNK = 128
EVEN_IN = S5_WIDTH + 2 * SGU_WIDTH
HEAD_DIM = 64
N_HEADS = 16
ROT_DIM = 16
ROPE_THETA = 500000.0
DIL_BRANCHES = ((128, 1), (512, 4), (2048, 16))
BAND = 128
WIN_MAX = 2048
D_FF = 2816
EPS = 1e-6
NEG_INF = -1e30

ROW_TILE = 512
MOD_ROWS = 40
S5_T = 16
S5_SUB = 8
S5_OLANES = 128
S5_OCTETS = S5_WIDTH // S5_OLANES
S5_OGROUPS = S5_OLANES // S5_GROUP
S5_OSTATE = S5_OGROUPS * S5_STATE
FF_CHUNKS = ((0, D_FF),)
ROPE_LANES = 128
PLANES = 16
LOG2_E = 1.4426950408889634
ATT_BLOCKS = 8
STAT_LANES = 128
VMEM_LIMIT = 56 * 1024 * 1024


def _cparams(*sem):
    return pltpu.CompilerParams(dimension_semantics=sem, vmem_limit_bytes=VMEM_LIMIT)


def _const_spec(shape):
    nd = len(shape)
    return pl.BlockSpec(shape, lambda *_: (0,) * nd)


def _weight_spec(shape):
    nd = len(shape)
    return pl.BlockSpec(shape, lambda *_: (0,) * nd, pipeline_mode=pl.Buffered(1))


def _gelu(x):
    return jax.nn.gelu(x)


def _mod_norm(x, ng, shift, scale):
    ms = jnp.mean(x * x, axis=-1, keepdims=True)
    return (x * lax.rsqrt(ms + EPS) * ng) * (1.0 + scale) + shift


def _ada_body(c_ref, w_ref, b_ref, o_ref):
    c = c_ref[...]
    s = c * jax.nn.sigmoid(c)
    o_ref[...] = jnp.dot(s.astype(BF16), w_ref[...].astype(BF16),
                         preferred_element_type=F32) + b_ref[...]


def _ada_call(c_all, ada_w, ada_b):
    depth = ada_w.shape[0]
    nt = 1536
    return pl.pallas_call(
        _ada_body,
        grid=(depth, 6 * D_MODEL // nt),
        in_specs=[
            pl.BlockSpec((MOD_ROWS, D_MODEL), lambda l, j: (0, 0)),
            pl.BlockSpec((None, D_MODEL, nt), lambda l, j: (l, 0, j)),
            pl.BlockSpec((None, 1, nt), lambda l, j: (l, 0, j)),
        ],
        out_specs=pl.BlockSpec((None, MOD_ROWS, nt), lambda l, j: (l, 0, j)),
        out_shape=jax.ShapeDtypeStruct((depth, MOD_ROWS, 6 * D_MODEL), F32),
        compiler_params=_cparams("arbitrary", "arbitrary"),
        name="ada_mod",
    )(c_all, ada_w, ada_b.reshape(depth, 1, 6 * D_MODEL))


def _even_pre_body(sample, tm, x_ref, mod_ref, ng_ref, win_ref, lng_ref, lnb_ref, wt_ref, bs_ref,
                   xa_ref, bout_ref, *vn_out):
    h = _mod_norm(x_ref[...], ng_ref[...], mod_ref[:, 0:D_MODEL], mod_ref[:, D_MODEL:2 * D_MODEL])
    proj = jnp.dot(h.astype(BF16), win_ref[...], preferred_element_type=F32)
    xa_ref[...] = proj[:, :S5_WIDTH]
    u = _gelu(proj[:, S5_WIDTH:S5_WIDTH + SGU_WIDTH])
    v = _gelu(proj[:, S5_WIDTH + SGU_WIDTH:])
    mu = jnp.mean(v, axis=-1, keepdims=True)
    var = jnp.mean(jnp.square(v - mu), axis=-1, keepdims=True)
    vn = (v - mu) * lax.rsqrt(var + EPS) * lng_ref[...] + lnb_ref[...]
    if sample:
        vn_out[0][...] = vn
        bout_ref[...] = (u * (vn * bs_ref[0:1, :] + bs_ref[1:2, :])).astype(BF16)
    else:
        vnb = vn.astype(BF16)
        hd = SGU_WIDTH // SGU_HEADS
        for ci in range(tm // CHUNK):
            rows = slice(ci * CHUNK, (ci + 1) * CHUNK)
            for hh in range(SGU_HEADS):
                cols = slice(hh * hd, (hh + 1) * hd)
                s = jnp.dot(wt_ref[hh], vnb[rows, cols], preferred_element_type=F32) + bs_ref[:, cols]
                bout_ref[rows, cols] = (u[rows, cols] * s).astype(BF16)
        xf_ref, stage = vn_out
        per = tm // S5_T
        for ov in range(S5_OCTETS):
            stage[ov] = proj[:, ov * S5_OLANES:(ov + 1) * S5_OLANES]
            for t in range(S5_T):
                c0 = (ov * S5_T + t) * S5_OLANES
                xf_ref[:, c0:c0 + S5_OLANES] = stage[ov, pl.ds(t, per, stride=S5_T), :].astype(BF16)


def _even_pre_call(sample, x, mod, ng, win, lng, lnb, wt, bs):
    rows = x.shape[0]
    tm = rows if sample else ROW_TILE
    row_spec = lambda w: pl.BlockSpec((tm, w), lambda i: (i, 0))
    out_shape = [jax.ShapeDtypeStruct((rows, S5_WIDTH), F32),
                 jax.ShapeDtypeStruct((rows, SGU_WIDTH), BF16)]
    out_specs = [row_spec(S5_WIDTH), row_spec(SGU_WIDTH)]
    scratch = []
    if sample:
        out_shape.append(jax.ShapeDtypeStruct((rows, SGU_WIDTH), F32))
        out_specs.append(row_spec(SGU_WIDTH))
    else:
        out_shape.append(jax.ShapeDtypeStruct((rows // S5_T, S5_T * S5_WIDTH), BF16))
        out_specs.append(pl.BlockSpec((tm // S5_T, S5_T * S5_WIDTH), lambda i: (i, 0)))
        scratch = [pltpu.VMEM((S5_OCTETS, tm, S5_OLANES), F32)]
    return pl.pallas_call(
        functools.partial(_even_pre_body, sample, tm),
        grid=(rows // tm,),
        in_specs=[row_spec(D_MODEL), _const_spec(mod.shape), _const_spec(ng.shape), _weight_spec(win.shape),
                  _const_spec(lng.shape), _const_spec(lnb.shape), _const_spec(wt.shape), _const_spec(bs.shape)],
        out_specs=out_specs,
        out_shape=out_shape,
        scratch_shapes=scratch,
        compiler_params=_cparams("arbitrary"),
        name="even_pre_sample" if sample else "even_pre_prompt",
    )(x, mod, ng, win, lng, lnb, wt, bs)


def _s5_prompt_body(x_ref, pbr_ref, pbi_ref, pcr_ref, pci_ref, a16_ref,
                    y_ref, hfin_ref, mtt, pb_re, pb_im, pc_re, pc_im, s_re, s_im):
    ol, ns = S5_OLANES, S5_OSTATE
    nlb = ns // ol
    nt_dims = (((1,), (1,)), ((), ()))

    r_grp = lax.broadcasted_iota(jnp.int32, (ol, ns), 0) // S5_GROUP
    c_grp = lax.broadcasted_iota(jnp.int32, (ol, ns), 1) // S5_STATE
    same_group = jnp.where(r_grp == c_grp, 1.0, 0.0).astype(BF16)
    for src, dst in ((pbr_ref, pb_re), (pbi_ref, pb_im), (pcr_ref, pc_re), (pci_ref, pc_im)):
        for t in range(src.shape[0]):
            dst[t * ol:(t + 1) * ol, :] = jnp.concatenate([src[t]] * nlb, axis=1) * same_group

    last = slice((S5_T - 1) * ol, S5_T * ol)
    lagk = (lax.dot_general(pc_re[0:S5_T * ol, :], pb_re[last, :], nt_dims, preferred_element_type=F32)
            + lax.dot_general(pc_im[0:S5_T * ol, :], pb_im[last, :], nt_dims, preferred_element_type=F32)
            ).astype(BF16)

    @pl.when(pl.program_id(0) == 0)
    def _():
        mtt[...] = jnp.zeros_like(mtt)

    for to in range(S5_T):
        for ti in range(to + 1):
            mtt[to * ol:(to + 1) * ol, ti * ol:(ti + 1) * ol] = lagk[(to - ti) * ol:(to - ti + 1) * ol, :]

    x = x_ref[...]
    nrows = x.shape[0]
    sr = jnp.dot(x, pb_re[...], preferred_element_type=F32)
    si = jnp.dot(x, pb_im[...], preferred_element_type=F32)
    row_id = lax.broadcasted_iota(jnp.int32, (nrows, 1), 0)

    def shifted(t, k):
        return jnp.where(row_id >= k, pltpu.roll(t, k, 0), 0.0)

    pr, pi = a16_ref[0:1, :], a16_ref[1:2, :]
    k = 1
    while k < S5_SUB:
        tr, ti = shifted(sr, k), shifted(si, k)
        sr, si = sr + pr * tr - pi * ti, si + pr * ti + pi * tr
        pr, pi = pr * pr - pi * pi, 2.0 * pr * pi
        k *= 2
    s_re[...] = sr
    s_im[...] = si
    hr = jnp.zeros((S5_SUB, ns), F32)
    hi = jnp.zeros((S5_SUB, ns), F32)
    for tile in range(nrows // S5_SUB):
        rows_t = slice(tile * S5_SUB, (tile + 1) * S5_SUB)
        hr, hi = pr * hr - pi * hi + s_re[rows_t, :], pr * hi + pi * hr + s_im[rows_t, :]
        s_re[rows_t, :] = hr
        s_im[rows_t, :] = hi
    hfin_ref[0:1, :] = hr[S5_SUB - 1:S5_SUB, :]
    hfin_ref[1:2, :] = hi[S5_SUB - 1:S5_SUB, :]
    hb_re = shifted(s_re[...], 1).astype(BF16)
    hb_im = shifted(s_im[...], 1).astype(BF16)

    nq = 4
    qw = x.shape[1] // nq
    for j in range(nq):
        cols = slice(j * qw, (j + 1) * qw)
        carry_rows = slice(j * qw + ol, (j + 1) * qw + ol)
        y_ref[:, cols] = (lax.dot_general(hb_re, pc_re[carry_rows, :], nt_dims, preferred_element_type=F32)
                          + lax.dot_general(hb_im, pc_im[carry_rows, :], nt_dims, preferred_element_type=F32)
                          + lax.dot_general(x_ref[:, :(j + 1) * qw], mtt[cols, :(j + 1) * qw], nt_dims,
                                            preferred_element_type=F32)).astype(y_ref.dtype)


def _s5_prompt_call(xf, pbr, pbi, pcr, pci, a16):
    rows = xf.shape[0]
    ow = S5_T * S5_OLANES
    grp = lambda a: pl.BlockSpec((None,) + a.shape[1:], lambda i: (i,) + (0,) * (a.ndim - 1))
    cols = pl.BlockSpec((rows, ow), lambda i: (0, i))
    return pl.pallas_call(
        _s5_prompt_body,
        grid=(S5_OCTETS,),
        in_specs=[cols, grp(pbr), grp(pbi), grp(pcr), grp(pci), grp(a16)],
        out_specs=[cols, pl.BlockSpec((None, 2, S5_OSTATE), lambda i: (i, 0, 0))],
        out_shape=[jax.ShapeDtypeStruct((rows, S5_OCTETS * ow), BF16),
                   jax.ShapeDtypeStruct((S5_OCTETS, 2, S5_OSTATE), F32)],
        scratch_shapes=[pltpu.VMEM((ow, ow), BF16)]
        + [pltpu.VMEM((ow, S5_OSTATE), BF16)] * 2
        + [pltpu.VMEM((ow + S5_OLANES, S5_OSTATE), BF16)] * 2
        + [pltpu.VMEM((rows, S5_OSTATE), F32)] * 2,
        compiler_params=_cparams("arbitrary"),
        name="s5_prompt",
    )(xf, pbr, pbi, pcr, pci, a16)


def _s5_sample_body(xa_ref, h0r_ref, h0i_ref, ar_ref, ai_ref, b_r_ref, b_i_ref, c_r_ref, c_i_ref,
                    yc_ref, hr_ref, hi_ref):
    ol, ns = S5_OLANES, S5_OSTATE
    r_grp = lax.broadcasted_iota(jnp.int32, (ol, ns), 0) // S5_GROUP
    c_grp = lax.broadcasted_iota(jnp.int32, (ol, ns), 1) // S5_STATE
    same_group = jnp.where(r_grp == c_grp, 1.0, 0.0).astype(BF16)
    expand = lambda ref: jnp.concatenate([ref[...]] * (ns // ol), axis=1) * same_group
    nt_dims = (((1,), (1,)), ((), ()))
    u = xa_ref[...].astype(BF16)
    ar = ar_ref[...]
    ai = ai_ref[...]
    h0r = h0r_ref[...]
    h0i = h0i_ref[...]
    hr = ar * h0r - ai * h0i + jnp.dot(u, expand(b_r_ref), preferred_element_type=F32)
    hi = ar * h0i + ai * h0r + jnp.dot(u, expand(b_i_ref), preferred_element_type=F32)
    hr_ref[...] = hr
    hi_ref[...] = hi
    yc_ref[...] = (lax.dot_general(hr.astype(BF16), expand(c_r_ref), nt_dims, preferred_element_type=F32)
                   + lax.dot_general(hi.astype(BF16), expand(c_i_ref), nt_dims, preferred_element_type=F32))


def _s5_sample_call(xa, h0r, h0i, ar, ai, pbr, pbi, pcr, pci):
    rows = xa.shape[0]
    n = S5_GROUPS * S5_STATE
    octet_cols = lambda w: pl.BlockSpec((rows, w), lambda v: (0, v))
    table = lambda a, entry: pl.BlockSpec((None, None) + a.shape[2:], lambda v: (v, entry, 0, 0))
    return pl.pallas_call(
        _s5_sample_body,
        grid=(S5_OCTETS,),
        in_specs=[octet_cols(S5_OLANES), octet_cols(S5_OSTATE), octet_cols(S5_OSTATE),
                  pl.BlockSpec((1, S5_OSTATE), lambda v: (0, v)), pl.BlockSpec((1, S5_OSTATE), lambda v: (0, v)),
                  table(pbr, S5_T - 1), table(pbi, S5_T - 1), table(pcr, 0), table(pci, 0)],
        out_specs=[octet_cols(S5_OLANES), octet_cols(S5_OSTATE), octet_cols(S5_OSTATE)],
        out_shape=[jax.ShapeDtypeStruct((rows, S5_WIDTH), F32),
                   jax.ShapeDtypeStruct((rows, n), F32),
                   jax.ShapeDtypeStruct((rows, n), F32)],
        compiler_params=_cparams("arbitrary"),
        name="s5_sample",
    )(xa, h0r, h0i, ar, ai, pbr, pbi, pcr, pci)


def _even_post_value(folded, tm, in_refs, scratch):
    x_ref, yc_ref, xa_ref, bout_ref, mod_ref, d_ref, wglu_ref, bglu_ref, wout_ref = in_refs
    if folded:
        stage = scratch[0]
        per = tm // S5_T
        for ov in range(S5_OCTETS):
            for t in range(S5_T):
                c0 = (ov * S5_T + t) * S5_OLANES
                stage[ov, pl.ds(t, per, stride=S5_T), :] = yc_ref[:, c0:c0 + S5_OLANES].astype(F32)
        yc = jnp.concatenate([stage[ov] for ov in range(S5_OCTETS)], axis=1)
    else:
        yc = yc_ref[...]
    y = _gelu(yc + d_ref[...] * xa_ref[...])
    gate = jax.nn.sigmoid(jnp.dot(y.astype(BF16), wglu_ref[...], preferred_element_type=F32) + bglu_ref[...])
    a_out = (y * gate).astype(BF16)
    mix = (jnp.dot(a_out, wout_ref[0:S5_WIDTH, :], preferred_element_type=F32)
           + jnp.dot(bout_ref[...], wout_ref[S5_WIDTH:, :], preferred_element_type=F32))
    return x_ref[...] + mod_ref[:, 2 * D_MODEL:3 * D_MODEL] * mix


def _even_post_stage(x, yc, xa, bout, mod, d, wglu, bglu, wout):
    rows = x.shape[0]
    tm = min(rows, ROW_TILE)
    row_spec = lambda w: pl.BlockSpec((tm, w), lambda i: (i, 0))
    folded = yc.shape[0] != rows
    yc_spec = pl.BlockSpec((tm // S5_T, S5_T * S5_WIDTH), lambda i: (i, 0)) if folded else row_spec(S5_WIDTH)
    scratch = [pltpu.VMEM((S5_OCTETS, tm, S5_OLANES), F32)] if folded else []
    in_specs = [row_spec(D_MODEL), yc_spec, row_spec(S5_WIDTH), row_spec(SGU_WIDTH),
                _const_spec(mod.shape), _const_spec(d.shape), _weight_spec(wglu.shape),
                _const_spec(bglu.shape), _weight_spec(wout.shape)]
    return (functools.partial(_even_post_value, folded, tm), [x, yc, xa, bout, mod, d, wglu, bglu, wout],
            in_specs, scratch)


def _ffn_body(sample, final, tm, pre, *refs):
    refs = list(refs)
    if pre is None:
        x_in = refs[0]
        refs = refs[1:]
    else:
        pre_fn, n_pre_in, n_pre_scratch = pre
        pre_in = refs[:n_pre_in]
        pre_scratch = refs[len(refs) - n_pre_scratch:]
        refs = refs[n_pre_in:len(refs) - n_pre_scratch]
    mod_ref, ng_ref, wup_ref, cw_ref, cb_ref, wdn_ref = refs[:6]
    pos = 6
    if sample:
        p2_ref, p1_ref = refs[pos:pos + 2]
        pos += 2
    if final:
        fg_ref = refs[pos]
        pos += 1
    o_ref, conv_ref = refs[pos:pos + 2]
    pos += 2
    if not sample:
        carry_ref = refs[pos]

        @pl.when(pl.program_id(0) == 0)
        def _():
            carry_ref[...] = jnp.zeros_like(carry_ref)

    x = x_in[...] if pre is None else pre_fn(pre_in, pre_scratch)
    h = _mod_norm(x, ng_ref[...], mod_ref[:, 3 * D_MODEL:4 * D_MODEL], mod_ref[:, 4 * D_MODEL:5 * D_MODEL])
    hb = h.astype(BF16)
    acc = jnp.zeros((tm, D_MODEL), F32)
    if not sample:
        row = lax.broadcasted_iota(jnp.int32, (tm, 1), 0)
    for c0, cw in FF_CHUNKS:
        cols = slice(c0, c0 + cw)
        a = jnp.dot(hb, wup_ref[:, cols], preferred_element_type=F32)
        g = jnp.dot(hb, wup_ref[:, D_FF + c0:D_FF + c0 + cw], preferred_element_type=F32)
        if sample:
            am2 = p2_ref[:, cols]
            am1 = p1_ref[:, cols]
            conv_ref[:, cols] = a
        else:
            prev2 = carry_ref[0:1, cols]
            prev1 = carry_ref[1:2, cols]
            am1 = jnp.where(row == 0, prev1, pltpu.roll(a, 1, 0))
            am2 = jnp.where(row == 0, prev2, jnp.where(row == 1, prev1, pltpu.roll(a, 2, 0)))
            carry_ref[0:2, cols] = a[tm - 2:tm, :]
        y = cb_ref[:, cols] + cw_ref[0:1, cols] * am2 + cw_ref[1:2, cols] * am1 + cw_ref[2:3, cols] * a
        act = (_gelu(y) * g).astype(BF16)
        acc = acc + jnp.dot(act, wdn_ref[cols, :], preferred_element_type=F32)
    out = x + mod_ref[:, 5 * D_MODEL:6 * D_MODEL] * acc
    if final:
        ms = jnp.mean(out * out, axis=-1, keepdims=True)
        out = out * lax.rsqrt(ms + EPS) * fg_ref[...]
    o_ref[...] = out
    if not sample:
        conv_ref[...] = carry_ref[0:2, :]


def _ffn_call(sample, layer, x, mod, ng, wup, cw, cb, wdn, prev=None, final_g=None, pre=None):
    rows = (x if pre is None else pre[1][0]).shape[0]
    tm = rows if sample else ROW_TILE
    final = final_g is not None
    row_spec = lambda w: pl.BlockSpec((tm, w), lambda i: (i, 0))
    of_layer = lambda a, **kw: pl.BlockSpec((None,) + a.shape[1:], lambda i: (layer,) + (0,) * (a.ndim - 1), **kw)
    once = dict(pipeline_mode=pl.Buffered(1))
    args = [mod, ng, wup, cw, cb, wdn]
    in_specs = [_const_spec(mod.shape), _const_spec(ng.shape), of_layer(wup, **once),
                of_layer(cw), of_layer(cb), of_layer(wdn, **once)]
    pre_static, pre_scratch = None, []
    if pre is None:
        args.insert(0, x)
        in_specs.insert(0, row_spec(D_MODEL))
    else:
        pre_fn, pre_args, pre_specs, pre_scratch = pre
        args = list(pre_args) + args
        in_specs = list(pre_specs) + in_specs
        pre_static = (pre_fn, len(pre_args), len(pre_scratch))
    if sample:
        args += [prev[0], prev[1]]
        in_specs += [_const_spec(prev[0].shape), _const_spec(prev[1].shape)]
    if final:
        args.append(final_g)
        in_specs.append(_const_spec(final_g.shape))
    conv_rows = rows if sample else 2
    return pl.pallas_call(
        functools.partial(_ffn_body, sample, final, tm, pre_static),
        grid=(rows // tm,),
        in_specs=in_specs,
        out_specs=[row_spec(D_MODEL), _const_spec((conv_rows, D_FF))],
        out_shape=[jax.ShapeDtypeStruct((rows, D_MODEL), F32),
                   jax.ShapeDtypeStruct((conv_rows, D_FF), F32)],
        scratch_shapes=([] if sample else [pltpu.VMEM((8, D_FF), F32)]) + list(pre_scratch),
        compiler_params=_cparams("arbitrary"),
        name="ffn_sample" if sample else "ffn_prompt",
    )(*args)


def _odd_pre_body(planar, tm, first_kept, x_ref, mod_ref, ng_ref, wqkv_ref, tile_trig_ref, row_trig_ref,
                  q_ref, k_ref, v_ref, k32_ref, v32_ref, *rest):
    h = _mod_norm(x_ref[...], ng_ref[...], mod_ref[:, 0:D_MODEL], mod_ref[:, D_MODEL:2 * D_MODEL])
    qkv = jnp.dot(h.astype(BF16), wqkv_ref[...], preferred_element_type=F32)
    tt = tile_trig_ref[...]
    cos_sum = lambda f: tt[2 * f:2 * f + 1] * row_trig_ref[2 * f] - tt[2 * f + 1:2 * f + 2] * row_trig_ref[2 * f + 1]
    sin_sum = lambda f: tt[2 * f + 1:2 * f + 2] * row_trig_ref[2 * f] + tt[2 * f:2 * f + 1] * row_trig_ref[2 * f + 1]
    rc = cos_sum(0)
    ra = sin_sum(1)
    rb = -sin_sum(2)
    lanes = ROPE_LANES
    half = ROT_DIM // 2

    def rope(t):
        return t * rc + pltpu.roll(t, half, 1) * ra + pltpu.roll(t, lanes - half, 1) * rb

    nblk = D_MODEL // lanes
    if planar:
        qpl_ref, kpl_ref, vpl_ref, stage = rest
        per = tm // PLANES

        def to_planes(dst_ref, slot, cols, val):
            stage[slot] = val
            for r in range(PLANES):
                dst_ref[r, :, cols] = stage[slot, pl.ds(r, per, stride=PLANES), :].astype(BF16)

    for j in range(nblk):
        cols = slice(j * lanes, (j + 1) * lanes)
        q = rope(qkv[:, j * lanes:(j + 1) * lanes]) * (HEAD_DIM ** -0.5 * (LOG2_E if planar else 1.0))
        k = rope(qkv[:, D_MODEL + j * lanes:D_MODEL + (j + 1) * lanes])
        v = qkv[:, 2 * D_MODEL + j * lanes:2 * D_MODEL + (j + 1) * lanes]
        q_ref[:, cols] = q.astype(BF16)
        k_ref[:, cols] = k.astype(BF16)
        v_ref[:, cols] = v.astype(BF16)
        if planar:
            to_planes(qpl_ref, j, cols, q)
            to_planes(kpl_ref, nblk + j, cols, k)
            to_planes(vpl_ref, 2 * nblk + j, cols, v)
        else:
            k32_ref[:, cols] = k
            v32_ref[:, cols] = v
    if planar:
        @pl.when(pl.program_id(0) >= first_kept)
        def _():
            for j in range(nblk):
                cols = slice(j * lanes, (j + 1) * lanes)
                k32_ref[:, cols] = stage[nblk + j]
                v32_ref[:, cols] = stage[2 * nblk + j]


def _odd_pre_call(x, mod, ng, wqkv, tile_trig, row_trig, keep, planar):
    rows = x.shape[0]
    tm = min(rows, ROW_TILE)
    nt = rows // tm
    first_kept = (rows - keep) // tm
    row_spec = lambda w: pl.BlockSpec((tm, w), lambda i: (i, 0))
    keep_spec = pl.BlockSpec((tm, D_MODEL), lambda i: (jnp.maximum(i - first_kept, 0), 0))
    out_specs = [row_spec(D_MODEL), row_spec(D_MODEL), row_spec(D_MODEL), keep_spec, keep_spec]
    out_shape = ([jax.ShapeDtypeStruct((rows, D_MODEL), BF16)] * 3
                 + [jax.ShapeDtypeStruct((keep, D_MODEL), F32)] * 2)
    scratch = []
    if planar:
        plane_spec = pl.BlockSpec((PLANES, tm // PLANES, D_MODEL), lambda i: (0, i, 0))
        out_specs += [plane_spec] * 3
        out_shape += [jax.ShapeDtypeStruct((PLANES, rows // PLANES, D_MODEL), BF16)] * 3
        scratch = [pltpu.VMEM((3 * D_MODEL // ROPE_LANES, tm, ROPE_LANES), F32)]
    return pl.pallas_call(
        functools.partial(_odd_pre_body, planar, tm, first_kept),
        grid=(nt,),
        in_specs=[row_spec(D_MODEL), _const_spec(mod.shape), _const_spec(ng.shape), _weight_spec(wqkv.shape),
                  pl.BlockSpec((None,) + tile_trig.shape[1:], lambda i: (i, 0, 0)), _const_spec(row_trig.shape)],
        out_specs=out_specs,
        out_shape=out_shape,
        scratch_shapes=scratch,
        compiler_params=_cparams("arbitrary"),
        name="odd_pre",
    )(x, mod, ng, wqkv, tile_trig, row_trig)


def _attn_prompt_body(n_planes, n_blocks, q_ref, kp_ref, kc_ref, vp_ref, vc_ref, o_ref, st_ref):
    per = BAND // n_planes
    qi = lax.broadcasted_iota(jnp.int32, (BAND, 2 * BAND), 0)
    kj = lax.broadcasted_iota(jnp.int32, (BAND, 2 * BAND), 1)
    q_pos = n_planes * (qi % per) + qi // per
    k_half = kj // BAND
    k_pos = n_planes * (per * k_half + kj % per) + (kj % BAND) // per - BAND
    dist = q_pos - k_pos
    in_band = (dist >= 0) & (dist <= BAND)
    no_prev = jnp.where(pl.program_id(1) == 0, 1, 0)
    bias_any = jnp.where(in_band, 0.0, NEG_INF).astype(F32)
    bias_first = jnp.where(in_band & (k_half >= no_prev), 0.0, NEG_INF).astype(F32)
    lane = lax.broadcasted_iota(jnp.int32, (BAND, 2 * HEAD_DIM), 1)
    low_half = lane < HEAD_DIM
    lane_row = lax.broadcasted_iota(jnp.int32, (1, 2 * HEAD_DIM), 1)
    head_keep = [jnp.where(lane_row < HEAD_DIM, 1.0, 0.0).astype(BF16),
                 jnp.where(lane_row < HEAD_DIM, 0.0, 1.0).astype(BF16)]
    st_lane = lax.broadcasted_iota(jnp.int32, (BAND, STAT_LANES), 1)
    st_out = [jnp.zeros((BAND, STAT_LANES), F32)] * n_blocks

    def rows_of(ref, u, cols):
        t = ref[u * BAND:(u + 1) * BAND, cols] if n_planes == 1 else ref[:, u * per:(u + 1) * per, cols]
        return t.reshape(BAND, t.shape[-1])

    for pair in range(N_HEADS // 2):
        cols = slice(pair * 2 * HEAD_DIM, (pair + 1) * 2 * HEAD_DIM)
        for u in range(n_blocks):
            qp = rows_of(q_ref, u, cols)
            k_before = rows_of(kp_ref, 0, cols) if u == 0 else rows_of(kc_ref, u - 1, cols)
            v_before = rows_of(vp_ref, 0, cols) if u == 0 else rows_of(vc_ref, u - 1, cols)
            kk = jnp.concatenate([k_before, rows_of(kc_ref, u, cols)], axis=0)
            vv = jnp.concatenate([v_before, rows_of(vc_ref, u, cols)], axis=0)
            bias = bias_first if u == 0 else bias_any
            q2 = jnp.concatenate([qp * head_keep[0], qp * head_keep[1]], axis=0)
            s = (lax.dot_general(q2, kk, (((1,), (1,)), ((), ())), preferred_element_type=F32)
                 + jnp.concatenate([bias, bias], axis=0))
            m = jnp.max(s, axis=-1, keepdims=True)
            p = jnp.exp2(s - m)
            l = jnp.sum(p, axis=-1, keepdims=True)
            pv = jnp.dot(p.astype(BF16), vv, preferred_element_type=F32) / l
            for sub in range(2):
                head = 2 * pair + sub
                st_out[u] = jnp.where(st_lane == head, m[sub * BAND:(sub + 1) * BAND], st_out[u])
                st_out[u] = jnp.where(st_lane == N_HEADS + head, l[sub * BAND:(sub + 1) * BAND], st_out[u])
            o_pair = jnp.where(low_half, pv[:BAND], pv[BAND:]).astype(o_ref.dtype)
            if n_planes == 1:
                o_ref[u * BAND:(u + 1) * BAND, cols] = o_pair
            else:
                o_ref[:, u * per:(u + 1) * per, cols] = o_pair.reshape(n_planes, per, 2 * HEAD_DIM)
    for u in range(n_blocks):
        if n_planes == 1:
            st_ref[u * BAND:(u + 1) * BAND, :] = st_out[u]
        else:
            st_ref[:, u * per:(u + 1) * per, :] = st_out[u].reshape(n_planes, per, STAT_LANES)


def _attn_prompt_call(d, q, k, v):
    planes, rpp, width = q.shape
    nu = ATT_BLOCKS
    before = lambda b: jnp.maximum(nu * b - 1, 0)
    if planes // d == 1 or d == 1:
        n_planes, outer = 1, planes
        view = lambda a: a
        cur = lambda w: pl.BlockSpec((None, nu * BAND, w), lambda r, b: (r, b, 0))
        prev = lambda w: pl.BlockSpec((None, BAND, w), lambda r, b: (r, before(b), 0))
        nb = rpp // (nu * BAND)
    else:
        n_planes, outer = planes // d, d
        per = BAND // n_planes
        view = lambda a: a.reshape(n_planes, outer, rpp, a.shape[-1])
        cur = lambda w: pl.BlockSpec((n_planes, None, nu * per, w), lambda r, b: (0, r, b, 0))
        prev = lambda w: pl.BlockSpec((n_planes, None, per, w), lambda r, b: (0, r, before(b), 0))
        nb = rpp // (nu * per)
    qv, kv, vv = view(q), view(k), view(v)
    o, st = pl.pallas_call(
        functools.partial(_attn_prompt_body, n_planes, nu),
        grid=(outer, nb),
        in_specs=[cur(width), prev(width), cur(width), prev(width), cur(width)],
        out_specs=[cur(width), cur(STAT_LANES)],
        out_shape=[jax.ShapeDtypeStruct(qv.shape, BF16),
                   jax.ShapeDtypeStruct(qv.shape[:-1] + (STAT_LANES,), F32)],
        compiler_params=_cparams("arbitrary", "arbitrary"),
        name="attn_prompt_d%d" % d,
    )(qv, kv, kv, vv, vv)
    return o.reshape(planes, rpp, width), st.reshape(planes, rpp, STAT_LANES)


def _attn_sample_body(hb, past, q_ref, kn_ref, vn_ref, kt_ref, vt_ref, o_ref):
    b = pl.program_id(1)
    dim = kt_ref.shape[1]
    q_row = q_ref[pl.ds(b, 1), :]
    kn_row = kn_ref[pl.ds(b, 1), :]
    vn_row = vn_ref[pl.ds(b, 1), :]
    on_diag = (lax.broadcasted_iota(jnp.int32, (dim, dim), 0)
               == lax.broadcasted_iota(jnp.int32, (dim, dim), 1))

    def to_col(row):
        return jnp.sum(jnp.where(on_diag, jnp.broadcast_to(row, (dim, dim)), 0.0), axis=1, keepdims=True)

    def to_row(col):
        return jnp.sum(jnp.where(on_diag, jnp.broadcast_to(col, (dim, dim)), 0.0), axis=0, keepdims=True)

    rows, news = [], []
    for h in range(hb):
        lanes = slice(h * dim, (h + 1) * dim)
        rows.append(jnp.sum(kt_ref[h] * to_col(q_row[:, lanes]), axis=0, keepdims=True))
        news.append(jnp.sum(kn_row[:, lanes] * q_row[:, lanes], axis=1, keepdims=True))
    s = jnp.concatenate(rows, axis=0)
    s_new = jnp.concatenate(news, axis=0)
    r = lax.broadcasted_iota(jnp.int32, (1, past), 1)
    ms, ls, ps, pns = [], [], [], []
    for window, d in DIL_BRANCHES:
        member = (r >= past - window) & ((past - r) % d == 0)
        sg = s + jnp.where(member, 0.0, NEG_INF).astype(F32)
        m = jnp.maximum(jnp.max(sg, axis=1, keepdims=True), s_new)
        p = jnp.exp(sg - m)
        pn = jnp.exp(s_new - m)
        ms.append(m)
        ps.append(p)
        pns.append(pn)
        ls.append(jnp.sum(p, axis=1, keepdims=True) + pn)
    m_all = jnp.maximum(jnp.maximum(ms[0], ms[1]), ms[2])
    cs = [jnp.exp(m - m_all) for m in ms]
    tot = cs[0] * ls[0] + cs[1] * ls[1] + cs[2] * ls[2]
    w = (cs[0] * ps[0] + cs[1] * ps[1] + cs[2] * ps[2]) / tot
    w_new = (cs[0] * pns[0] + cs[1] * pns[1] + cs[2] * pns[2]) / tot
    outs = []
    for h in range(hb):
        lanes = slice(h * dim, (h + 1) * dim)
        from_cache = jnp.sum(vt_ref[h] * w[h:h + 1, :], axis=1, keepdims=True)
        outs.append(to_row(from_cache) + vn_row[:, lanes] * w_new[h:h + 1, :])
    o_ref[pl.ds(b, 1), :] = jnp.concatenate(outs, axis=1)


def _attn_sample_call(q, k_new, v_new, kt, vt):
    bsz, heads, dim, past = kt.shape
    hb = heads
    rows = pl.BlockSpec((bsz, hb * dim), lambda j, b: (0, j))
    cache = pl.BlockSpec((None, hb, dim, past), lambda j, b: (b, j, 0, 0))
    return pl.pallas_call(
        functools.partial(_attn_sample_body, hb, past),
        grid=(heads // hb, bsz),
        in_specs=[rows, rows, rows, cache, cache],
        out_specs=rows,
        out_shape=jax.ShapeDtypeStruct((bsz, heads * dim), F32),
        compiler_params=_cparams("arbitrary", "arbitrary"),
        name="attn_sample",
    )(q, k_new, v_new, kt, vt)


def _odd_post_value(in_refs, scratch):
    x_ref, att_ref, mod_ref, wo_ref = in_refs
    mix = jnp.dot(att_ref[...].astype(BF16), wo_ref[...], preferred_element_type=F32)
    return x_ref[...] + mod_ref[:, 2 * D_MODEL:3 * D_MODEL] * mix


def _odd_post_stage(x, att, mod, wo):
    rows = x.shape[0]
    tm = min(rows, ROW_TILE)
    row_spec = pl.BlockSpec((tm, D_MODEL), lambda i: (i, 0))
    return (_odd_post_value, [x, att, mod, wo],
            [row_spec, row_spec, _const_spec(mod.shape), _weight_spec(wo.shape)], [])


def _odd_post_merge_value(tm, in_refs, scratch):
    x_ref, o1_ref, s1_ref, o4_ref, s4_ref, o16_ref, s16_ref, mod_ref, wo_ref, spread_ref = in_refs
    ob4, ob16, sb4, sb16 = scratch
    per = tm // PLANES
    nblk = D_MODEL // STAT_LANES
    for r in range(PLANES):
        rows_r = pl.ds(r, per, stride=PLANES)
        for j in range(nblk):
            cols = slice(j * STAT_LANES, (j + 1) * STAT_LANES)
            ob4[j, rows_r, :] = o4_ref[r, :, cols].astype(F32)
            ob16[j, rows_r, :] = o16_ref[r, :, cols].astype(F32)
        sb4[rows_r, :] = s4_ref[r]
        sb16[rows_r, :] = s16_ref[r]
    stats = [s1_ref[...], sb4[...], sb16[...]]
    dens = [pltpu.roll(s, STAT_LANES - N_HEADS, 1) for s in stats]
    m_all = jnp.maximum(jnp.maximum(stats[0], stats[1]), stats[2])
    ws = [d * jnp.exp2(s - m_all) for s, d in zip(stats, dens)]
    tot = ws[0] + ws[1] + ws[2]
    lane = lax.broadcasted_iota(jnp.int32, (tm, STAT_LANES), 1)
    spreads = []
    for w in ws[:2]:
        coef = jnp.where(lane < N_HEADS, w / tot, 0.0)
        hi = coef.astype(BF16)
        lo = (coef - hi.astype(F32)).astype(BF16)
        spreads.append(jnp.dot(jnp.concatenate([hi, lo], axis=1), spread_ref[...], preferred_element_type=F32))
    pieces = []
    for j in range(nblk):
        cols = slice(j * STAT_LANES, (j + 1) * STAT_LANES)
        last = ob16[j]
        pieces.append((last + spreads[0][:, cols] * (o1_ref[:, cols].astype(F32) - last)
                       + spreads[1][:, cols] * (ob4[j] - last)).astype(BF16))
    att = jnp.concatenate(pieces, axis=1)
    mix = jnp.dot(att, wo_ref[...], preferred_element_type=F32)
    return x_ref[...] + mod_ref[:, 2 * D_MODEL:3 * D_MODEL] * mix


def _odd_post_merge_stage(x, branches, mod, wo):
    tm = ROW_TILE
    (o1, s1), (o4, s4), (o16, s16) = branches
    row_spec = pl.BlockSpec((tm, D_MODEL), lambda i: (i, 0))
    nat = lambda w: pl.BlockSpec((None, tm, w), lambda i: (0, i, 0))
    pln = lambda w: pl.BlockSpec((PLANES, tm // PLANES, w), lambda i: (0, i, 0))
    head_of_lane = jnp.arange(D_MODEL) // HEAD_DIM
    spread = (jnp.arange(STAT_LANES)[:, None] == head_of_lane[None, :]).astype(BF16)
    spread = jnp.concatenate([spread, spread], axis=0)
    in_specs = [row_spec, nat(D_MODEL), nat(STAT_LANES), pln(D_MODEL), pln(STAT_LANES),
                pln(D_MODEL), pln(STAT_LANES), _const_spec(mod.shape), _weight_spec(wo.shape),
                _const_spec(spread.shape)]
    scratch = [pltpu.VMEM((D_MODEL // STAT_LANES, tm, STAT_LANES), F32),
               pltpu.VMEM((D_MODEL // STAT_LANES, tm, STAT_LANES), F32),
               pltpu.VMEM((tm, STAT_LANES), F32), pltpu.VMEM((tm, STAT_LANES), F32)]
    return (functools.partial(_odd_post_merge_value, tm), [x, o1, s1, o4, s4, o16, s16, mod, wo, spread],
            in_specs, scratch)


def _s5_tables(lam_re, lam_im, log_dt, b_re, b_im, c_re, c_im):
    dt = jnp.exp(log_dt)[:, None]
    lr, li = lam_re, lam_im
    ks = jnp.arange(S5_T + 1, dtype=F32)[:, None, None]
    mag = jnp.exp(ks * (lr * dt))
    pw_r = mag * jnp.cos(ks * (li * dt))
    pw_i = mag * jnp.sin(ks * (li * dt))
    ar, ai = pw_r[1], pw_i[1]
    den = lr * lr + li * li
    fr = ((ar - 1.0) * lr + ai * li) / den
    fi = (ai * lr - (ar - 1.0) * li) / den
    bbr = fr[..., None] * b_re - fi[..., None] * b_im
    bbi = fr[..., None] * b_im + fi[..., None] * b_re
    ca_r = c_re[None] * pw_r[:, :, None, :] - c_im[None] * pw_i[:, :, None, :]
    ca_i = c_re[None] * pw_i[:, :, None, :] + c_im[None] * pw_r[:, :, None, :]
    oc, og = S5_OCTETS, S5_OGROUPS
    split = lambda a, axis: a.reshape(a.shape[:axis] + (oc, og) + a.shape[axis + 1:])
    kr = (S5_T - 1) - jnp.arange(S5_T, dtype=F32)[:, None, None]
    rev_mag = jnp.exp(kr * (lr * dt))
    rev_r = rev_mag * jnp.cos(kr * (li * dt))
    rev_i = rev_mag * jnp.sin(kr * (li * dt))
    pb_r = rev_r[..., None] * bbr[None] - rev_i[..., None] * bbi[None]
    pb_i = rev_r[..., None] * bbi[None] + rev_i[..., None] * bbr[None]
    twice = lambda a: jnp.concatenate([a, a], axis=-1)
    fold_b = lambda a: twice(split(a, 1).transpose(1, 0, 2, 4, 3).reshape(oc, -1, S5_OLANES, S5_STATE))
    fold_c = lambda a: twice(split(a, 1).transpose(1, 0, 2, 3, 4).reshape(oc, -1, S5_OLANES, S5_STATE))
    pb_r, pb_i = fold_b(pb_r), fold_b(pb_i)
    pc_r, pc_i = fold_c(ca_r), fold_c(-ca_i)
    per_octet = lambda a: a.reshape(oc, S5_OSTATE)
    a16 = jnp.stack([per_octet(pw_r[S5_T]), per_octet(pw_i[S5_T])], axis=1)
    n_all = S5_GROUPS * S5_STATE
    return dict(pb_r=pb_r.astype(BF16), pb_i=pb_i.astype(BF16),
                pc_r=pc_r.astype(BF16), pc_i=pc_i.astype(BF16), a16=a16,
                a_r=ar.reshape(1, n_all), a_i=ai.reshape(1, n_all))


def _rope_tables(tile_pos, row_pos):
    half = ROT_DIM // 2
    inv = jnp.power(ROPE_THETA, -jnp.arange(half, dtype=F32) * 2.0 / ROT_DIM)
    e = jnp.arange(ROPE_LANES) % HEAD_DIM
    inv_e = inv[e % half]
    freqs = jnp.stack([jnp.where(e < ROT_DIM, inv_e, 0.0),
                       jnp.where((e >= half) & (e < ROT_DIM), inv_e, 0.0),
                       jnp.where(e < half, inv_e, 0.0)])

    def trig(pos):
        ang = pos.astype(F32)[None, :, None] * freqs[:, None, :]
        return jnp.stack([jnp.cos(ang), jnp.sin(ang)], axis=1).reshape(6, pos.shape[0], ROPE_LANES)

    tile_trig = jnp.concatenate([trig(tile_pos), jnp.zeros((2, tile_pos.shape[0], ROPE_LANES), F32)], axis=0)
    return tile_trig.transpose(1, 0, 2), trig(row_pos)


def _trunk(sample, x, mods, state, w):
    outs = {}
    mod = mods[0]
    if sample:
        xa, bout, vn = _even_pre_call(True, x, mod, w['ng'][0][0], w['ev_w_in'], w['sg_ln_g'], w['sg_ln_b'],
                                      w['sg_wt'], w['sg_row0'])
        outs['vn'] = vn
        t = w['s5']
        yc, hr, hi = _s5_sample_call(xa, state['s5_re'], state['s5_im'], t['a_r'], t['a_i'],
                                     t['pb_r'], t['pb_i'], t['pc_r'], t['pc_i'])
        outs['s5_re'], outs['s5_im'] = hr, hi
    else:
        xa, bout, xf = _even_pre_call(False, x, mod, w['ng'][0][0], w['ev_w_in'], w['sg_ln_g'], w['sg_ln_b'],
                                      w['sg_wt'], w['sg_bias'])
        t = w['s5']
        yc, hfin = _s5_prompt_call(xf, t['pb_r'], t['pb_i'], t['pc_r'], t['pc_i'], t['a16'])
        outs['s5_re'], outs['s5_im'] = hfin[:, 0], hfin[:, 1]
    mix0 = _even_post_stage(x, yc, xa, bout, mod, w['s5_d'], w['s5_w_glu'], w['s5_b_glu'], w['ev_w_out'])
    prev = (state['conv'][0][:, 0], state['conv'][0][:, 1]) if sample else None
    x, conv0 = _ffn_call(sample, 0, None, mod, w['ng'][0][1], w['ffn_w_up'], w['ffn_conv_w'],
                         w['ffn_conv_b'], w['ffn_w_down'], prev=prev, pre=mix0)
    mod = mods[1]
    rows = x.shape[0]
    keep = rows if sample else min(WIN_MAX, rows)
    pre = _odd_pre_call(x, mod, w['ng'][1][0], w['od_w_qkv'], *w['rope'], keep, planar=not sample)
    q, k, v, k32, v32 = pre[:5]
    outs['k'], outs['v'] = k32, v32
    if sample:
        att = _attn_sample_call(q.astype(F32), k32, v32, state['ck'], state['cv'])
        mix1 = _odd_post_stage(x, att, mod, w['od_w_o'])
    else:
        q_pl, k_pl, v_pl = pre[5:]
        branches = (_attn_prompt_call(1, q[None], k[None], v[None]),
                    _attn_prompt_call(4, q_pl, k_pl, v_pl),
                    _attn_prompt_call(16, q_pl, k_pl, v_pl))
        mix1 = _odd_post_merge_stage(x, branches, mod, w['od_w_o'])
    prev = (state['conv'][1][:, 0], state['conv'][1][:, 1]) if sample else None
    y, conv1 = _ffn_call(sample, 1, None, mod, w['ng'][1][1], w['ffn_w_up'], w['ffn_conv_w'],
                         w['ffn_conv_b'], w['ffn_w_down'], prev=prev, final_g=w['final_g'], pre=mix1)
    outs['y'] = y
    outs['conv'] = (conv0, conv1)
    return outs


def kernel(x_prompt, x_sample, c_prompt, c_sample, state_s5_re, state_s5_im, cache_c_k, cache_c_v,
           state_ffn_conv, ada_w, ada_b, norm_g, final_g, ev_w_in, ev_w_out, s5_lam_re, s5_lam_im,
           s5_log_dt, s5_b_re, s5_b_im, s5_c_re, s5_c_im, s5_d, s5_w_glu, s5_b_glu, sg_ln_g, sg_ln_b,
           sg_w, sg_b, od_w_qkv, od_w_o, ffn_w_up, ffn_conv_w, ffn_conv_b, ffn_w_down):
    bp, seq, _ = x_prompt.shape
    bs = x_sample.shape[0]
    assert bp == 1 and seq == SEQ and bs == DEC_BATCH and x_sample.shape[1] == 1

    c_all = jnp.concatenate([c_sample, c_prompt, jnp.zeros((MOD_ROWS - bs - bp, D_MODEL), F32)], axis=0)
    mod_all = _ada_call(c_all, ada_w, ada_b)
    mods_s = [mod_all[l, :bs] for l in range(2)]
    mods_p = [mod_all[l, bs:bs + 1] for l in range(2)]

    hd = SGU_WIDTH // SGU_HEADS
    causal = jnp.tril(jnp.ones((CHUNK, CHUNK), F32))
    w = dict(
        ng=[[norm_g[l, j].reshape(1, D_MODEL) for j in range(2)] for l in range(2)],
        final_g=final_g.reshape(1, D_MODEL),
        ev_w_in=ev_w_in[0].astype(BF16), ev_w_out=ev_w_out[0].astype(BF16),
        sg_ln_g=sg_ln_g[0].reshape(1, SGU_WIDTH), sg_ln_b=sg_ln_b[0].reshape(1, SGU_WIDTH),
        sg_wt=(sg_w[0] * causal[None]).astype(BF16),
        sg_bias=jnp.repeat(sg_b[0].T, hd, axis=1),
        sg_row0=jnp.stack([jnp.repeat(sg_w[0, :, 0, 0], hd), jnp.repeat(sg_b[0, :, 0], hd)], axis=0),
        s5=_s5_tables(s5_lam_re[0], s5_lam_im[0], s5_log_dt[0], s5_b_re[0], s5_b_im[0], s5_c_re[0], s5_c_im[0]),
        s5_d=s5_d[0].reshape(1, S5_WIDTH), s5_w_glu=s5_w_glu[0].astype(BF16),
        s5_b_glu=s5_b_glu[0].reshape(1, S5_WIDTH),
        od_w_qkv=od_w_qkv[0].astype(BF16), od_w_o=od_w_o[0].astype(BF16),
        ffn_w_up=ffn_w_up.astype(BF16), ffn_w_down=ffn_w_down.astype(BF16),
        ffn_conv_w=ffn_conv_w, ffn_conv_b=ffn_conv_b.reshape(2, 1, D_FF),
    )

    wp = dict(w, rope=_rope_tables(jnp.arange(0, seq, ROW_TILE, dtype=jnp.int32),
                                   jnp.arange(ROW_TILE, dtype=jnp.int32)))
    ws = dict(w, rope=_rope_tables(jnp.full((1,), PAST_LEN, jnp.int32), jnp.zeros((bs,), jnp.int32)))

    p = _trunk(False, x_prompt[0], mods_p, None, wp)
    n_all = S5_GROUPS * S5_STATE
    state = dict(s5_re=state_s5_re[0].reshape(bs, n_all), s5_im=state_s5_im[0].reshape(bs, n_all),
                 ck=jnp.transpose(cache_c_k[0], (0, 2, 3, 1)), cv=jnp.transpose(cache_c_v[0], (0, 2, 3, 1)),
                 conv=state_ffn_conv)
    s = _trunk(True, x_sample[:, 0], mods_s, state, ws)

    keep = min(WIN_MAX, seq)
    kv_p = lambda a: a.reshape(1, 1, keep, N_HEADS, HEAD_DIM)
    kv_s = lambda a: a.reshape(1, bs, 1, N_HEADS, HEAD_DIM)
    s5_p = lambda a: a.reshape(1, 1, S5_GROUPS, S5_STATE)
    s5_s = lambda a: a.reshape(1, bs, S5_GROUPS, S5_STATE)
    conv_p = jnp.stack([c.reshape(1, 2, D_FF) for c in p['conv']])
    conv_s = jnp.stack([jnp.stack([state_ffn_conv[l][:, 1], s['conv'][l]], axis=1) for l in range(2)])
    return (p['y'][None], s['y'][:, None], s5_p(p['s5_re']), s5_p(p['s5_im']),
            s5_s(s['s5_re']), s5_s(s['s5_im']), s['vn'].reshape(1, bs, 1, SGU_WIDTH),
            kv_p(p['k']), kv_p(p['v']), kv_s(s['k']), kv_s(s['v']), conv_p, conv_s)
```

```python
import functools

import jax
import jax.numpy as jnp
from jax import lax
from jax.experimental import pallas as pl
from jax.experimental.pallas import tpu as pltpu

F32 = jnp.float32
BF16 = jnp.bfloat16

D_MODEL = 1024
SEQ = 16384
DEC_BATCH = 32
PAST_LEN = 16384
S5_WIDTH = 512
S5_GROUP = 16
S5_GROUPS = 32
S5_STATE = 64
SGU_WIDTH = 512
SGU_HEADS = 4
CHUNK = 128
HEAD_DIM = 64
N_HEADS = 16
ROT_DIM = 16
ROPE_THETA = 500000.0
DIL_BRANCHES = ((128, 1), (512, 4), (2048, 16))
BAND = 128
WIN_MAX = 2048
D_FF = 2816
EPS = 1e-6
NEG_INF = -1e30

ROW_TILE = 512
EVEN_PRE_TILE = 1024
MOD_ROWS = 40
S5_T = 16
S5_SUB = 8
S5_OLANES = 128
S5_OCTETS = S5_WIDTH // S5_OLANES
S5_OGROUPS = S5_OLANES // S5_GROUP
S5_OSTATE = S5_OGROUPS * S5_STATE
FF_CHUNKS = ((0, D_FF),)
ROPE_LANES = 128
PLANES = 16
LOG2_E = 1.4426950408889634
ATT_BLOCKS = 8
STAT_LANES = 128
VMEM_LIMIT = 56 * 1024 * 1024


def _cparams(*sem):
    return pltpu.CompilerParams(dimension_semantics=sem, vmem_limit_bytes=VMEM_LIMIT)


def _const_spec(shape):
    nd = len(shape)
    return pl.BlockSpec(shape, lambda *_: (0,) * nd)


def _weight_spec(shape):
    nd = len(shape)
    return pl.BlockSpec(shape, lambda *_: (0,) * nd, pipeline_mode=pl.Buffered(1))


def _gelu(x):
    return jax.nn.gelu(x)


def _mod_norm(x, ng, shift, scale):
    ms = jnp.mean(x * x, axis=-1, keepdims=True)
    return (x * lax.rsqrt(ms + EPS) * ng) * (1.0 + scale) + shift


def _ada_body(c_ref, w_ref, b_ref, o_ref):
    c = c_ref[...]
    s = c * jax.nn.sigmoid(c)
    o_ref[...] = jnp.dot(s.astype(BF16), w_ref[...].astype(BF16),
                         preferred_element_type=F32) + b_ref[...]


def _ada_call(c_all, ada_w, ada_b):
    depth = ada_w.shape[0]
    nt = 1536
    return pl.pallas_call(
        _ada_body,
        grid=(depth, 6 * D_MODEL // nt),
        in_specs=[
            pl.BlockSpec((MOD_ROWS, D_MODEL), lambda l, j: (0, 0)),
            pl.BlockSpec((None, D_MODEL, nt), lambda l, j: (l, 0, j)),
            pl.BlockSpec((None, 1, nt), lambda l, j: (l, 0, j)),
        ],
        out_specs=pl.BlockSpec((None, MOD_ROWS, nt), lambda l, j: (l, 0, j)),
        out_shape=jax.ShapeDtypeStruct((depth, MOD_ROWS, 6 * D_MODEL), F32),
        compiler_params=_cparams("arbitrary", "arbitrary"),
        name="ada_mod",
    )(c_all, ada_w, ada_b.reshape(depth, 1, 6 * D_MODEL))


def _even_pre_body(sample, tm, x_ref, mod_ref, ng_ref, win_ref, lng_ref, lnb_ref, wt_ref, bs_ref,
                   xa_ref, bout_ref, *vn_out):
    h = _mod_norm(x_ref[...], ng_ref[...], mod_ref[:, 0:D_MODEL], mod_ref[:, D_MODEL:2 * D_MODEL])
    proj = jnp.dot(h.astype(BF16), win_ref[...], preferred_element_type=F32)
    xa_ref[...] = proj[:, :S5_WIDTH]
    u = _gelu(proj[:, S5_WIDTH:S5_WIDTH + SGU_WIDTH])
    v = _gelu(proj[:, S5_WIDTH + SGU_WIDTH:])
    mu = jnp.mean(v, axis=-1, keepdims=True)
    var = jnp.mean(jnp.square(v - mu), axis=-1, keepdims=True)
    vn = (v - mu) * lax.rsqrt(var + EPS) * lng_ref[...] + lnb_ref[...]
    if sample:
        vn_out[0][...] = vn
        bout_ref[...] = (u * (vn * bs_ref[0:1, :] + bs_ref[1:2, :])).astype(BF16)
    else:
        vnb = vn.astype(BF16)
        hd = SGU_WIDTH // SGU_HEADS
        for ci in range(tm // CHUNK):
            rows = slice(ci * CHUNK, (ci + 1) * CHUNK)
            for hh in range(SGU_HEADS):
                cols = slice(hh * hd, (hh + 1) * hd)
                s = jnp.dot(wt_ref[hh], vnb[rows, cols], preferred_element_type=F32) + bs_ref[:, cols]
                bout_ref[rows, cols] = (u[rows, cols] * s).astype(BF16)
        xf_ref, stage = vn_out
        per = tm // S5_T
        for ov in range(S5_OCTETS):
            stage[ov] = proj[:, ov * S5_OLANES:(ov + 1) * S5_OLANES]
            for t in range(S5_T):
                c0 = (ov * S5_T + t) * S5_OLANES
                xf_ref[:, c0:c0 + S5_OLANES] = stage[ov, pl.ds(t, per, stride=S5_T), :].astype(BF16)


def _even_pre_call(sample, x, mod, ng, win, lng, lnb, wt, bs):
    rows = x.shape[0]
    tm = rows if sample else EVEN_PRE_TILE
    row_spec = lambda w: pl.BlockSpec((tm, w), lambda i: (i, 0))
    out_shape = [jax.ShapeDtypeStruct((rows, S5_WIDTH), F32),
                 jax.ShapeDtypeStruct((rows, SGU_WIDTH), BF16)]
    out_specs = [row_spec(S5_WIDTH), row_spec(SGU_WIDTH)]
    scratch = []
    if sample:
        out_shape.append(jax.ShapeDtypeStruct((rows, SGU_WIDTH), F32))
        out_specs.append(row_spec(SGU_WIDTH))
    else:
        out_shape.append(jax.ShapeDtypeStruct((rows // S5_T, S5_T * S5_WIDTH), BF16))
        out_specs.append(pl.BlockSpec((tm // S5_T, S5_T * S5_WIDTH), lambda i: (i, 0)))
        scratch = [pltpu.VMEM((S5_OCTETS, tm, S5_OLANES), F32)]
    return pl.pallas_call(
        functools.partial(_even_pre_body, sample, tm),
        grid=(rows // tm,),
        in_specs=[row_spec(D_MODEL), _const_spec(mod.shape), _const_spec(ng.shape), _weight_spec(win.shape),
                  _const_spec(lng.shape), _const_spec(lnb.shape), _const_spec(wt.shape), _const_spec(bs.shape)],
        out_specs=out_specs,
        out_shape=out_shape,
        scratch_shapes=scratch,
        compiler_params=_cparams("arbitrary"),
        name="even_pre_sample" if sample else "even_pre_prompt",
    )(x, mod, ng, win, lng, lnb, wt, bs)


def _s5_prompt_body(x_ref, pbr_ref, pbi_ref, pcr_ref, pci_ref, a16_ref,
                    y_ref, hfin_ref, mtt, pb_re, pb_im, pc_re, pc_im, s_re, s_im):
    ol, ns = S5_OLANES, S5_OSTATE
    nlb = ns // ol
    nt_dims = (((1,), (1,)), ((), ()))

    r_grp = lax.broadcasted_iota(jnp.int32, (ol, ns), 0) // S5_GROUP
    c_grp = lax.broadcasted_iota(jnp.int32, (ol, ns), 1) // S5_STATE
    same_group = jnp.where(r_grp == c_grp, 1.0, 0.0).astype(BF16)
    for src, dst in ((pbr_ref, pb_re), (pbi_ref, pb_im), (pcr_ref, pc_re), (pci_ref, pc_im)):
        for t in range(src.shape[0]):
            dst[t * ol:(t + 1) * ol, :] = jnp.concatenate([src[t]] * nlb, axis=1) * same_group

    last = slice((S5_T - 1) * ol, S5_T * ol)
    lagk = (lax.dot_general(pc_re[0:S5_T * ol, :], pb_re[last, :], nt_dims, preferred_element_type=F32)
            + lax.dot_general(pc_im[0:S5_T * ol, :], pb_im[last, :], nt_dims, preferred_element_type=F32)
            ).astype(BF16)

    @pl.when(pl.program_id(0) == 0)
    def _():
        mtt[...] = jnp.zeros_like(mtt)

    for to in range(S5_T):
        for ti in range(to + 1):
            mtt[to * ol:(to + 1) * ol, ti * ol:(ti + 1) * ol] = lagk[(to - ti) * ol:(to - ti + 1) * ol, :]

    x = x_ref[...]
    nrows = x.shape[0]
    sr = jnp.dot(x, pb_re[...], preferred_element_type=F32)
    si = jnp.dot(x, pb_im[...], preferred_element_type=F32)
    row_id = lax.broadcasted_iota(jnp.int32, (nrows, 1), 0)

    def shifted(t, k):
        return jnp.where(row_id >= k, pltpu.roll(t, k, 0), 0.0)

    pr, pi = a16_ref[0:1, :], a16_ref[1:2, :]
    k = 1
    while k < S5_SUB:
        tr, ti = shifted(sr, k), shifted(si, k)
        sr, si = sr + pr * tr - pi * ti, si + pr * ti + pi * tr
        pr, pi = pr * pr - pi * pi, 2.0 * pr * pi
        k *= 2
    s_re[...] = sr
    s_im[...] = si
    hr = jnp.zeros((S5_SUB, ns), F32)
    hi = jnp.zeros((S5_SUB, ns), F32)
    for tile in range(nrows // S5_SUB):
        rows_t = slice(tile * S5_SUB, (tile + 1) * S5_SUB)
        hr, hi = pr * hr - pi * hi + s_re[rows_t, :], pr * hi + pi * hr + s_im[rows_t, :]
        s_re[rows_t, :] = hr
        s_im[rows_t, :] = hi
    hfin_ref[0:1, :] = hr[S5_SUB - 1:S5_SUB, :]
    hfin_ref[1:2, :] = hi[S5_SUB - 1:S5_SUB, :]
    hb_re = shifted(s_re[...], 1).astype(BF16)
    hb_im = shifted(s_im[...], 1).astype(BF16)

    nq = 4
    qw = x.shape[1] // nq
    for j in range(nq):
        cols = slice(j * qw, (j + 1) * qw)
        carry_rows = slice(j * qw + ol, (j + 1) * qw + ol)
        y_ref[:, cols] = (lax.dot_general(hb_re, pc_re[carry_rows, :], nt_dims, preferred_element_type=F32)
                          + lax.dot_general(hb_im, pc_im[carry_rows, :], nt_dims, preferred_element_type=F32)
                          + lax.dot_general(x_ref[:, :(j + 1) * qw], mtt[cols, :(j + 1) * qw], nt_dims,
                                            preferred_element_type=F32)).astype(y_ref.dtype)


def _s5_prompt_call(xf, pbr, pbi, pcr, pci, a16):
    rows = xf.shape[0]
    ow = S5_T * S5_OLANES
    grp = lambda a: pl.BlockSpec((None,) + a.shape[1:], lambda i: (i,) + (0,) * (a.ndim - 1))
    cols = pl.BlockSpec((rows, ow), lambda i: (0, i))
    return pl.pallas_call(
        _s5_prompt_body,
        grid=(S5_OCTETS,),
        in_specs=[cols, grp(pbr), grp(pbi), grp(pcr), grp(pci), grp(a16)],
        out_specs=[cols, pl.BlockSpec((None, 2, S5_OSTATE), lambda i: (i, 0, 0))],
        out_shape=[jax.ShapeDtypeStruct((rows, S5_OCTETS * ow), BF16),
                   jax.ShapeDtypeStruct((S5_OCTETS, 2, S5_OSTATE), F32)],
        scratch_shapes=[pltpu.VMEM((ow, ow), BF16)]
        + [pltpu.VMEM((ow, S5_OSTATE), BF16)] * 2
        + [pltpu.VMEM((ow + S5_OLANES, S5_OSTATE), BF16)] * 2
        + [pltpu.VMEM((rows, S5_OSTATE), F32)] * 2,
        compiler_params=_cparams("arbitrary"),
        name="s5_prompt",
    )(xf, pbr, pbi, pcr, pci, a16)


def _s5_sample_body(xa_ref, h0r_ref, h0i_ref, ar_ref, ai_ref, b_r_ref, b_i_ref, c_r_ref, c_i_ref,
                    yc_ref, hr_ref, hi_ref):
    ol, ns = S5_OLANES, S5_OSTATE
    r_grp = lax.broadcasted_iota(jnp.int32, (ol, ns), 0) // S5_GROUP
    c_grp = lax.broadcasted_iota(jnp.int32, (ol, ns), 1) // S5_STATE
    same_group = jnp.where(r_grp == c_grp, 1.0, 0.0).astype(BF16)
    expand = lambda ref: jnp.concatenate([ref[...]] * (ns // ol), axis=1) * same_group
    nt_dims = (((1,), (1,)), ((), ()))
    u = xa_ref[...].astype(BF16)
    ar = ar_ref[...]
    ai = ai_ref[...]
    h0r = h0r_ref[...]
    h0i = h0i_ref[...]
    hr = ar * h0r - ai * h0i + jnp.dot(u, expand(b_r_ref), preferred_element_type=F32)
    hi = ar * h0i + ai * h0r + jnp.dot(u, expand(b_i_ref), preferred_element_type=F32)
    hr_ref[...] = hr
    hi_ref[...] = hi
    yc_ref[...] = (lax.dot_general(hr.astype(BF16), expand(c_r_ref), nt_dims, preferred_element_type=F32)
                   + lax.dot_general(hi.astype(BF16), expand(c_i_ref), nt_dims, preferred_element_type=F32))


def _s5_sample_call(xa, h0r, h0i, ar, ai, pbr, pbi, pcr, pci):
    rows = xa.shape[0]
    n = S5_GROUPS * S5_STATE
    octet_cols = lambda w: pl.BlockSpec((rows, w), lambda v: (0, v))
    table = lambda a, entry: pl.BlockSpec((None, None) + a.shape[2:], lambda v: (v, entry, 0, 0))
    return pl.pallas_call(
        _s5_sample_body,
        grid=(S5_OCTETS,),
        in_specs=[octet_cols(S5_OLANES), octet_cols(S5_OSTATE), octet_cols(S5_OSTATE),
                  pl.BlockSpec((1, S5_OSTATE), lambda v: (0, v)), pl.BlockSpec((1, S5_OSTATE), lambda v: (0, v)),
                  table(pbr, S5_T - 1), table(pbi, S5_T - 1), table(pcr, 0), table(pci, 0)],
        out_specs=[octet_cols(S5_OLANES), octet_cols(S5_OSTATE), octet_cols(S5_OSTATE)],
        out_shape=[jax.ShapeDtypeStruct((rows, S5_WIDTH), F32),
                   jax.ShapeDtypeStruct((rows, n), F32),
                   jax.ShapeDtypeStruct((rows, n), F32)],
        compiler_params=_cparams("arbitrary"),
        name="s5_sample",
    )(xa, h0r, h0i, ar, ai, pbr, pbi, pcr, pci)


def _even_post_value(folded, tm, in_refs, scratch):
    x_ref, yc_ref, xa_ref, bout_ref, mod_ref, d_ref, wglu_ref, bglu_ref, wout_ref = in_refs
    if folded:
        stage = scratch[0]
        per = tm // S5_T
        for ov in range(S5_OCTETS):
            for t in range(S5_T):
                c0 = (ov * S5_T + t) * S5_OLANES
                stage[ov, pl.ds(t, per, stride=S5_T), :] = yc_ref[:, c0:c0 + S5_OLANES].astype(F32)
        yc = jnp.concatenate([stage[ov] for ov in range(S5_OCTETS)], axis=1)
    else:
        yc = yc_ref[...]
    y = _gelu(yc + d_ref[...] * xa_ref[...])
    gate = jax.nn.sigmoid(jnp.dot(y.astype(BF16), wglu_ref[...], preferred_element_type=F32) + bglu_ref[...])
    a_out = (y * gate).astype(BF16)
    mix = (jnp.dot(a_out, wout_ref[0:S5_WIDTH, :], preferred_element_type=F32)
           + jnp.dot(bout_ref[...], wout_ref[S5_WIDTH:, :], preferred_element_type=F32))
    return x_ref[...] + mod_ref[:, 2 * D_MODEL:3 * D_MODEL] * mix


def _even_post_stage(x, yc, xa, bout, mod, d, wglu, bglu, wout):
    rows = x.shape[0]
    tm = min(rows, ROW_TILE)
    row_spec = lambda w: pl.BlockSpec((tm, w), lambda i: (i, 0))
    folded = yc.shape[0] != rows
    yc_spec = pl.BlockSpec((tm // S5_T, S5_T * S5_WIDTH), lambda i: (i, 0)) if folded else row_spec(S5_WIDTH)
    scratch = [pltpu.VMEM((S5_OCTETS, tm, S5_OLANES), F32)] if folded else []
    in_specs = [row_spec(D_MODEL), yc_spec, row_spec(S5_WIDTH), row_spec(SGU_WIDTH),
                _const_spec(mod.shape), _const_spec(d.shape), _weight_spec(wglu.shape),
                _const_spec(bglu.shape), _weight_spec(wout.shape)]
    return (functools.partial(_even_post_value, folded, tm), [x, yc, xa, bout, mod, d, wglu, bglu, wout],
            in_specs, scratch)


def _ffn_body(sample, final, tm, pre, *refs):
    refs = list(refs)
    if pre is None:
        x_in = refs[0]
        refs = refs[1:]
    else:
        pre_fn, n_pre_in, n_pre_scratch = pre
        pre_in = refs[:n_pre_in]
        pre_scratch = refs[len(refs) - n_pre_scratch:]
        refs = refs[n_pre_in:len(refs) - n_pre_scratch]
    mod_ref, ng_ref, wup_ref, cw_ref, cb_ref, wdn_ref = refs[:6]
    pos = 6
    if sample:
        p2_ref, p1_ref = refs[pos:pos + 2]
        pos += 2
    if final:
        fg_ref = refs[pos]
        pos += 1
    o_ref, conv_ref = refs[pos:pos + 2]
    pos += 2
    if not sample:
        carry_ref = refs[pos]

        @pl.when(pl.program_id(0) == 0)
        def _():
            carry_ref[...] = jnp.zeros_like(carry_ref)

    x = x_in[...] if pre is None else pre_fn(pre_in, pre_scratch)
    h = _mod_norm(x, ng_ref[...], mod_ref[:, 3 * D_MODEL:4 * D_MODEL], mod_ref[:, 4 * D_MODEL:5 * D_MODEL])
    hb = h.astype(BF16)
    acc = jnp.zeros((tm, D_MODEL), F32)
    if not sample:
        row = lax.broadcasted_iota(jnp.int32, (tm, 1), 0)
    for c0, cw in FF_CHUNKS:
        cols = slice(c0, c0 + cw)
        a = jnp.dot(hb, wup_ref[:, cols], preferred_element_type=F32)
        g = jnp.dot(hb, wup_ref[:, D_FF + c0:D_FF + c0 + cw], preferred_element_type=F32)
        if sample:
            am2 = p2_ref[:, cols]
            am1 = p1_ref[:, cols]
            conv_ref[:, cols] = a
        else:
            prev2 = carry_ref[0:1, cols]
            prev1 = carry_ref[1:2, cols]
            am1 = jnp.where(row == 0, prev1, pltpu.roll(a, 1, 0))
            am2 = jnp.where(row == 0, prev2, jnp.where(row == 1, prev1, pltpu.roll(a, 2, 0)))
            carry_ref[0:2, cols] = a[tm - 2:tm, :]
        y = cb_ref[:, cols] + cw_ref[0:1, cols] * am2 + cw_ref[1:2, cols] * am1 + cw_ref[2:3, cols] * a
        act = (_gelu(y) * g).astype(BF16)
        acc = acc + jnp.dot(act, wdn_ref[cols, :], preferred_element_type=F32)
    out = x + mod_ref[:, 5 * D_MODEL:6 * D_MODEL] * acc
    if final:
        ms = jnp.mean(out * out, axis=-1, keepdims=True)
        out = out * lax.rsqrt(ms + EPS) * fg_ref[...]
    o_ref[...] = out
    if not sample:
        conv_ref[...] = carry_ref[0:2, :]


def _ffn_call(sample, layer, x, mod, ng, wup, cw, cb, wdn, prev=None, final_g=None, pre=None):
    rows = (x if pre is None else pre[1][0]).shape[0]
    tm = rows if sample else ROW_TILE
    final = final_g is not None
    row_spec = lambda w: pl.BlockSpec((tm, w), lambda i: (i, 0))
    of_layer = lambda a, **kw: pl.BlockSpec((None,) + a.shape[1:], lambda i: (layer,) + (0,) * (a.ndim - 1), **kw)
    once = dict(pipeline_mode=pl.Buffered(1))
    args = [mod, ng, wup, cw, cb, wdn]
    in_specs = [_const_spec(mod.shape), _const_spec(ng.shape), of_layer(wup, **once),
                of_layer(cw), of_layer(cb), of_layer(wdn, **once)]
    pre_static, pre_scratch = None, []
    if pre is None:
        args.insert(0, x)
        in_specs.insert(0, row_spec(D_MODEL))
    else:
        pre_fn, pre_args, pre_specs, pre_scratch = pre
        args = list(pre_args) + args
        in_specs = list(pre_specs) + in_specs
        pre_static = (pre_fn, len(pre_args), len(pre_scratch))
    if sample:
        args += [prev[0], prev[1]]
        in_specs += [_const_spec(prev[0].shape), _const_spec(prev[1].shape)]
    if final:
        args.append(final_g)
        in_specs.append(_const_spec(final_g.shape))
    conv_rows = rows if sample else 2
    return pl.pallas_call(
        functools.partial(_ffn_body, sample, final, tm, pre_static),
        grid=(rows // tm,),
        in_specs=in_specs,
        out_specs=[row_spec(D_MODEL), _const_spec((conv_rows, D_FF))],
        out_shape=[jax.ShapeDtypeStruct((rows, D_MODEL), F32),
                   jax.ShapeDtypeStruct((conv_rows, D_FF), F32)],
        scratch_shapes=([] if sample else [pltpu.VMEM((8, D_FF), F32)]) + list(pre_scratch),
        compiler_params=_cparams("arbitrary"),
        name="ffn_sample" if sample else "ffn_prompt",
    )(*args)


def _odd_pre_body(planar, tm, first_kept, x_ref, mod_ref, ng_ref, wqkv_ref, tile_trig_ref, row_trig_ref,
                  q_ref, k_ref, v_ref, k32_ref, v32_ref, *rest):
    h = _mod_norm(x_ref[...], ng_ref[...], mod_ref[:, 0:D_MODEL], mod_ref[:, D_MODEL:2 * D_MODEL])
    qkv = jnp.dot(h.astype(BF16), wqkv_ref[...], preferred_element_type=F32)
    tt = tile_trig_ref[...]
    cos_sum = lambda f: tt[2 * f:2 * f + 1] * row_trig_ref[2 * f] - tt[2 * f + 1:2 * f + 2] * row_trig_ref[2 * f + 1]
    sin_sum = lambda f: tt[2 * f + 1:2 * f + 2] * row_trig_ref[2 * f] + tt[2 * f:2 * f + 1] * row_trig_ref[2 * f + 1]
    rc = cos_sum(0)
    ra = sin_sum(1)
    rb = -sin_sum(2)
    lanes = ROPE_LANES
    half = ROT_DIM // 2

    def rope(t):
        return t * rc + pltpu.roll(t, half, 1) * ra + pltpu.roll(t, lanes - half, 1) * rb

    nblk = D_MODEL // lanes
    if planar:
        qpl_ref, kpl_ref, vpl_ref, stage = rest
        per = tm // PLANES

        def to_planes(dst_ref, slot, cols, val):
            stage[slot] = val
            for r in range(PLANES):
                dst_ref[r, :, cols] = stage[slot, pl.ds(r, per, stride=PLANES), :].astype(BF16)

    for j in range(nblk):
        cols = slice(j * lanes, (j + 1) * lanes)
        q = rope(qkv[:, j * lanes:(j + 1) * lanes]) * (HEAD_DIM ** -0.5 * (LOG2_E if planar else 1.0))
        k = rope(qkv[:, D_MODEL + j * lanes:D_MODEL + (j + 1) * lanes])
        v = qkv[:, 2 * D_MODEL + j * lanes:2 * D_MODEL + (j + 1) * lanes]
        q_ref[:, cols] = q.astype(BF16)
        k_ref[:, cols] = k.astype(BF16)
        v_ref[:, cols] = v.astype(BF16)
        if planar:
            to_planes(qpl_ref, j, cols, q)
            to_planes(kpl_ref, nblk + j, cols, k)
            to_planes(vpl_ref, 2 * nblk + j, cols, v)
        else:
            k32_ref[:, cols] = k
            v32_ref[:, cols] = v
    if planar:
        @pl.when(pl.program_id(0) >= first_kept)
        def _():
            for j in range(nblk):
                cols = slice(j * lanes, (j + 1) * lanes)
                k32_ref[:, cols] = stage[nblk + j]
                v32_ref[:, cols] = stage[2 * nblk + j]


def _odd_pre_call(x, mod, ng, wqkv, tile_trig, row_trig, keep, planar):
    rows = x.shape[0]
    tm = min(rows, ROW_TILE)
    nt = rows // tm
    first_kept = (rows - keep) // tm
    row_spec = lambda w: pl.BlockSpec((tm, w), lambda i: (i, 0))
    keep_spec = pl.BlockSpec((tm, D_MODEL), lambda i: (jnp.maximum(i - first_kept, 0), 0))
    out_specs = [row_spec(D_MODEL), row_spec(D_MODEL), row_spec(D_MODEL), keep_spec, keep_spec]
    out_shape = ([jax.ShapeDtypeStruct((rows, D_MODEL), BF16)] * 3
                 + [jax.ShapeDtypeStruct((keep, D_MODEL), F32)] * 2)
    scratch = []
    if planar:
        plane_spec = pl.BlockSpec((PLANES, tm // PLANES, D_MODEL), lambda i: (0, i, 0))
        out_specs += [plane_spec] * 3
        out_shape += [jax.ShapeDtypeStruct((PLANES, rows // PLANES, D_MODEL), BF16)] * 3
        scratch = [pltpu.VMEM((3 * D_MODEL // ROPE_LANES, tm, ROPE_LANES), F32)]
    return pl.pallas_call(
        functools.partial(_odd_pre_body, planar, tm, first_kept),
        grid=(nt,),
        in_specs=[row_spec(D_MODEL), _const_spec(mod.shape), _const_spec(ng.shape), _weight_spec(wqkv.shape),
                  pl.BlockSpec((None,) + tile_trig.shape[1:], lambda i: (i, 0, 0)), _const_spec(row_trig.shape)],
        out_specs=out_specs,
        out_shape=out_shape,
        scratch_shapes=scratch,
        compiler_params=_cparams("arbitrary"),
        name="odd_pre",
    )(x, mod, ng, wqkv, tile_trig, row_trig)


def _attn_prompt_body(n_planes, n_blocks, q_ref, kp_ref, kc_ref, vp_ref, vc_ref, o_ref, st_ref):
    per = BAND // n_planes
    qi = lax.broadcasted_iota(jnp.int32, (BAND, 2 * BAND), 0)
    kj = lax.broadcasted_iota(jnp.int32, (BAND, 2 * BAND), 1)
    q_pos = n_planes * (qi % per) + qi // per
    k_half = kj // BAND
    k_pos = n_planes * (per * k_half + kj % per) + (kj % BAND) // per - BAND
    dist = q_pos - k_pos
    in_band = (dist >= 0) & (dist <= BAND)
    no_prev = jnp.where(pl.program_id(1) == 0, 1, 0)
    bias_any = jnp.where(in_band, 0.0, NEG_INF).astype(F32)
    bias_first = jnp.where(in_band & (k_half >= no_prev), 0.0, NEG_INF).astype(F32)
    lane = lax.broadcasted_iota(jnp.int32, (BAND, 2 * HEAD_DIM), 1)
    low_half = lane < HEAD_DIM
    lane_row = lax.broadcasted_iota(jnp.int32, (1, 2 * HEAD_DIM), 1)
    head_keep = [jnp.where(lane_row < HEAD_DIM, 1.0, 0.0).astype(BF16),
                 jnp.where(lane_row < HEAD_DIM, 0.0, 1.0).astype(BF16)]
    st_lane = lax.broadcasted_iota(jnp.int32, (BAND, STAT_LANES), 1)
    st_out = [jnp.zeros((BAND, STAT_LANES), F32)] * n_blocks

    def rows_of(ref, u, cols):
        t = ref[u * BAND:(u + 1) * BAND, cols] if n_planes == 1 else ref[:, u * per:(u + 1) * per, cols]
        return t.reshape(BAND, t.shape[-1])

    for u in range(n_blocks):
        for pair in range(N_HEADS // 2):
            cols = slice(pair * 2 * HEAD_DIM, (pair + 1) * 2 * HEAD_DIM)
            qp = rows_of(q_ref, u, cols)
            k_before = rows_of(kp_ref, 0, cols) if u == 0 else rows_of(kc_ref, u - 1, cols)
            v_before = rows_of(vp_ref, 0, cols) if u == 0 else rows_of(vc_ref, u - 1, cols)
            kk = jnp.concatenate([k_before, rows_of(kc_ref, u, cols)], axis=0)
            vv = jnp.concatenate([v_before, rows_of(vc_ref, u, cols)], axis=0)
            bias = bias_first if u == 0 else bias_any
            q2 = jnp.concatenate([qp * head_keep[0], qp * head_keep[1]], axis=0)
            s = (lax.dot_general(q2, kk, (((1,), (1,)), ((), ())), preferred_element_type=F32)
                 + jnp.concatenate([bias, bias], axis=0))
            m = jnp.max(s, axis=-1, keepdims=True)
            p = jnp.exp2(s - m)
            l = jnp.sum(p, axis=-1, keepdims=True)
            pv = jnp.dot(p.astype(BF16), vv, preferred_element_type=F32) / l
            for sub in range(2):
                head = 2 * pair + sub
                st_out[u] = jnp.where(st_lane == head, m[sub * BAND:(sub + 1) * BAND], st_out[u])
                st_out[u] = jnp.where(st_lane == N_HEADS + head, l[sub * BAND:(sub + 1) * BAND], st_out[u])
            o_pair = jnp.where(low_half, pv[:BAND], pv[BAND:]).astype(o_ref.dtype)
            if n_planes == 1:
                o_ref[u * BAND:(u + 1) * BAND, cols] = o_pair
            else:
                o_ref[:, u * per:(u + 1) * per, cols] = o_pair.reshape(n_planes, per, 2 * HEAD_DIM)
    for u in range(n_blocks):
        if n_planes == 1:
            st_ref[u * BAND:(u + 1) * BAND, :] = st_out[u]
        else:
            st_ref[:, u * per:(u + 1) * per, :] = st_out[u].reshape(n_planes, per, STAT_LANES)


def _attn_prompt_call(d, q, k, v):
    planes, rpp, width = q.shape
    nu = ATT_BLOCKS
    before = lambda b: jnp.maximum(nu * b - 1, 0)
    if planes // d == 1 or d == 1:
        n_planes, outer = 1, planes
        view = lambda a: a
        cur = lambda w: pl.BlockSpec((None, nu * BAND, w), lambda r, b: (r, b, 0))
        prev = lambda w: pl.BlockSpec((None, BAND, w), lambda r, b: (r, before(b), 0))
        nb = rpp // (nu * BAND)
    else:
        n_planes, outer = planes // d, d
        per = BAND // n_planes
        view = lambda a: a.reshape(n_planes, outer, rpp, a.shape[-1])
        cur = lambda w: pl.BlockSpec((n_planes, None, nu * per, w), lambda r, b: (0, r, b, 0))
        prev = lambda w: pl.BlockSpec((n_planes, None, per, w), lambda r, b: (0, r, before(b), 0))
        nb = rpp // (nu * per)
    qv, kv, vv = view(q), view(k), view(v)
    o, st = pl.pallas_call(
        functools.partial(_attn_prompt_body, n_planes, nu),
        grid=(outer, nb),
        in_specs=[cur(width), prev(width), cur(width), prev(width), cur(width)],
        out_specs=[cur(width), cur(STAT_LANES)],
        out_shape=[jax.ShapeDtypeStruct(qv.shape, BF16),
                   jax.ShapeDtypeStruct(qv.shape[:-1] + (STAT_LANES,), F32)],
        compiler_params=_cparams("arbitrary", "arbitrary"),
        name="attn_prompt_d%d" % d,
    )(qv, kv, kv, vv, vv)
    return o.reshape(planes, rpp, width), st.reshape(planes, rpp, STAT_LANES)


def _attn_sample_body(hb, past, q_ref, kn_ref, vn_ref, kt_ref, vt_ref, o_ref):
    b = pl.program_id(1)
    dim = kt_ref.shape[1]
    q_row = q_ref[pl.ds(b, 1), :]
    kn_row = kn_ref[pl.ds(b, 1), :]
    vn_row = vn_ref[pl.ds(b, 1), :]
    on_diag = (lax.broadcasted_iota(jnp.int32, (dim, dim), 0)
               == lax.broadcasted_iota(jnp.int32, (dim, dim), 1))

    def to_col(row):
        return jnp.sum(jnp.where(on_diag, jnp.broadcast_to(row, (dim, dim)), 0.0), axis=1, keepdims=True)

    def to_row(col):
        return jnp.sum(jnp.where(on_diag, jnp.broadcast_to(col, (dim, dim)), 0.0), axis=0, keepdims=True)

    rows, news = [], []
    for h in range(hb):
        lanes = slice(h * dim, (h + 1) * dim)
        rows.append(jnp.sum(kt_ref[h] * to_col(q_row[:, lanes]), axis=0, keepdims=True))
        news.append(jnp.sum(kn_row[:, lanes] * q_row[:, lanes], axis=1, keepdims=True))
    s = jnp.concatenate(rows, axis=0)
    s_new = jnp.concatenate(news, axis=0)
    r = lax.broadcasted_iota(jnp.int32, (1, past), 1)
    ms, ls, ps, pns = [], [], [], []
    for window, d in DIL_BRANCHES:
        member = (r >= past - window) & ((past - r) % d == 0)
        sg = s + jnp.where(member, 0.0, NEG_INF).astype(F32)
        m = jnp.maximum(jnp.max(sg, axis=1, keepdims=True), s_new)
        p = jnp.exp(sg - m)
        pn = jnp.exp(s_new - m)
        ms.append(m)
        ps.append(p)
        pns.append(pn)
        ls.append(jnp.sum(p, axis=1, keepdims=True) + pn)
    m_all = jnp.maximum(jnp.maximum(ms[0], ms[1]), ms[2])
    cs = [jnp.exp(m - m_all) for m in ms]
    tot = cs[0] * ls[0] + cs[1] * ls[1] + cs[2] * ls[2]
    w = (cs[0] * ps[0] + cs[1] * ps[1] + cs[2] * ps[2]) / tot
    w_new = (cs[0] * pns[0] + cs[1] * pns[1] + cs[2] * pns[2]) / tot
    outs = []
    for h in range(hb):
        lanes = slice(h * dim, (h + 1) * dim)
        from_cache = jnp.sum(vt_ref[h] * w[h:h + 1, :], axis=1, keepdims=True)
        outs.append(to_row(from_cache) + vn_row[:, lanes] * w_new[h:h + 1, :])
    o_ref[pl.ds(b, 1), :] = jnp.concatenate(outs, axis=1)


def _attn_sample_call(q, k_new, v_new, kt, vt):
    bsz, heads, dim, past = kt.shape
    hb = heads
    rows = pl.BlockSpec((bsz, hb * dim), lambda j, b: (0, j))
    cache = pl.BlockSpec((None, hb, dim, past), lambda j, b: (b, j, 0, 0))
    return pl.pallas_call(
        functools.partial(_attn_sample_body, hb, past),
        grid=(heads // hb, bsz),
        in_specs=[rows, rows, rows, cache, cache],
        out_specs=rows,
        out_shape=jax.ShapeDtypeStruct((bsz, heads * dim), F32),
        compiler_params=_cparams("arbitrary", "arbitrary"),
        name="attn_sample",
    )(q, k_new, v_new, kt, vt)


def _odd_post_value(in_refs, scratch):
    x_ref, att_ref, mod_ref, wo_ref = in_refs
    mix = jnp.dot(att_ref[...].astype(BF16), wo_ref[...], preferred_element_type=F32)
    return x_ref[...] + mod_ref[:, 2 * D_MODEL:3 * D_MODEL] * mix


def _odd_post_stage(x, att, mod, wo):
    rows = x.shape[0]
    tm = min(rows, ROW_TILE)
    row_spec = pl.BlockSpec((tm, D_MODEL), lambda i: (i, 0))
    return (_odd_post_value, [x, att, mod, wo],
            [row_spec, row_spec, _const_spec(mod.shape), _weight_spec(wo.shape)], [])


def _odd_post_merge_value(tm, in_refs, scratch):
    x_ref, o1_ref, s1_ref, o4_ref, s4_ref, o16_ref, s16_ref, mod_ref, wo_ref, spread_ref = in_refs
    ob4, ob16, sb4, sb16 = scratch
    per = tm // PLANES
    nblk = D_MODEL // STAT_LANES
    for r in range(PLANES):
        rows_r = pl.ds(r, per, stride=PLANES)
        for j in range(nblk):
            cols = slice(j * STAT_LANES, (j + 1) * STAT_LANES)
            ob4[j, rows_r, :] = o4_ref[r, :, cols].astype(F32)
            ob16[j, rows_r, :] = o16_ref[r, :, cols].astype(F32)
        sb4[rows_r, :] = s4_ref[r]
        sb16[rows_r, :] = s16_ref[r]
    stats = [s1_ref[...], sb4[...], sb16[...]]
    dens = [pltpu.roll(s, STAT_LANES - N_HEADS, 1) for s in stats]
    m_all = jnp.maximum(jnp.maximum(stats[0], stats[1]), stats[2])
    ws = [d * jnp.exp2(s - m_all) for s, d in zip(stats, dens)]
    tot = ws[0] + ws[1] + ws[2]
    lane = lax.broadcasted_iota(jnp.int32, (tm, STAT_LANES), 1)
    spreads = []
    for w in ws[:2]:
        coef = jnp.where(lane < N_HEADS, w / tot, 0.0)
        hi = coef.astype(BF16)
        lo = (coef - hi.astype(F32)).astype(BF16)
        spreads.append(jnp.dot(jnp.concatenate([hi, lo], axis=1), spread_ref[...], preferred_element_type=F32))
    pieces = []
    for j in range(nblk):
        cols = slice(j * STAT_LANES, (j + 1) * STAT_LANES)
        last = ob16[j]
        pieces.append((last + spreads[0][:, cols] * (o1_ref[:, cols].astype(F32) - last)
                       + spreads[1][:, cols] * (ob4[j] - last)).astype(BF16))
    att = jnp.concatenate(pieces, axis=1)
    mix = jnp.dot(att, wo_ref[...], preferred_element_type=F32)
    return x_ref[...] + mod_ref[:, 2 * D_MODEL:3 * D_MODEL] * mix


def _odd_post_merge_stage(x, branches, mod, wo):
    tm = ROW_TILE
    (o1, s1), (o4, s4), (o16, s16) = branches
    row_spec = pl.BlockSpec((tm, D_MODEL), lambda i: (i, 0))
    nat = lambda w: pl.BlockSpec((None, tm, w), lambda i: (0, i, 0))
    pln = lambda w: pl.BlockSpec((PLANES, tm // PLANES, w), lambda i: (0, i, 0))
    head_of_lane = jnp.arange(D_MODEL) // HEAD_DIM
    spread = (jnp.arange(STAT_LANES)[:, None] == head_of_lane[None, :]).astype(BF16)
    spread = jnp.concatenate([spread, spread], axis=0)
    in_specs = [row_spec, nat(D_MODEL), nat(STAT_LANES), pln(D_MODEL), pln(STAT_LANES),
                pln(D_MODEL), pln(STAT_LANES), _const_spec(mod.shape), _weight_spec(wo.shape),
                _const_spec(spread.shape)]
    scratch = [pltpu.VMEM((D_MODEL // STAT_LANES, tm, STAT_LANES), F32),
               pltpu.VMEM((D_MODEL // STAT_LANES, tm, STAT_LANES), F32),
               pltpu.VMEM((tm, STAT_LANES), F32), pltpu.VMEM((tm, STAT_LANES), F32)]
    return (functools.partial(_odd_post_merge_value, tm), [x, o1, s1, o4, s4, o16, s16, mod, wo, spread],
            in_specs, scratch)


def _s5_tables(lam_re, lam_im, log_dt, b_re, b_im, c_re, c_im):
    dt = jnp.exp(log_dt)[:, None]
    lr, li = lam_re, lam_im
    ks = jnp.arange(S5_T + 1, dtype=F32)[:, None, None]
    mag = jnp.exp(ks * (lr * dt))
    pw_r = mag * jnp.cos(ks * (li * dt))
    pw_i = mag * jnp.sin(ks * (li * dt))
    ar, ai = pw_r[1], pw_i[1]
    den = lr * lr + li * li
    fr = ((ar - 1.0) * lr + ai * li) / den
    fi = (ai * lr - (ar - 1.0) * li) / den
    bbr = fr[..., None] * b_re - fi[..., None] * b_im
    bbi = fr[..., None] * b_im + fi[..., None] * b_re
    ca_r = c_re[None] * pw_r[:, :, None, :] - c_im[None] * pw_i[:, :, None, :]
    ca_i = c_re[None] * pw_i[:, :, None, :] + c_im[None] * pw_r[:, :, None, :]
    oc, og = S5_OCTETS, S5_OGROUPS
    split = lambda a, axis: a.reshape(a.shape[:axis] + (oc, og) + a.shape[axis + 1:])
    kr = (S5_T - 1) - jnp.arange(S5_T, dtype=F32)[:, None, None]
    rev_mag = jnp.exp(kr * (lr * dt))
    rev_r = rev_mag * jnp.cos(kr * (li * dt))
    rev_i = rev_mag * jnp.sin(kr * (li * dt))
    pb_r = rev_r[..., None] * bbr[None] - rev_i[..., None] * bbi[None]
    pb_i = rev_r[..., None] * bbi[None] + rev_i[..., None] * bbr[None]
    twice = lambda a: jnp.concatenate([a, a], axis=-1)
    fold_b = lambda a: twice(split(a, 1).transpose(1, 0, 2, 4, 3).reshape(oc, -1, S5_OLANES, S5_STATE))
    fold_c = lambda a: twice(split(a, 1).transpose(1, 0, 2, 3, 4).reshape(oc, -1, S5_OLANES, S5_STATE))
    pb_r, pb_i = fold_b(pb_r), fold_b(pb_i)
    pc_r, pc_i = fold_c(ca_r), fold_c(-ca_i)
    per_octet = lambda a: a.reshape(oc, S5_OSTATE)
    a16 = jnp.stack([per_octet(pw_r[S5_T]), per_octet(pw_i[S5_T])], axis=1)
    n_all = S5_GROUPS * S5_STATE
    return dict(pb_r=pb_r.astype(BF16), pb_i=pb_i.astype(BF16),
                pc_r=pc_r.astype(BF16), pc_i=pc_i.astype(BF16), a16=a16,
                a_r=ar.reshape(1, n_all), a_i=ai.reshape(1, n_all))


def _rope_tables(tile_pos, row_pos):
    half = ROT_DIM // 2
    inv = jnp.power(ROPE_THETA, -jnp.arange(half, dtype=F32) * 2.0 / ROT_DIM)
    e = jnp.arange(ROPE_LANES) % HEAD_DIM
    inv_e = inv[e % half]
    freqs = jnp.stack([jnp.where(e < ROT_DIM, inv_e, 0.0),
                       jnp.where((e >= half) & (e < ROT_DIM), inv_e, 0.0),
                       jnp.where(e < half, inv_e, 0.0)])

    def trig(pos):
        ang = pos.astype(F32)[None, :, None] * freqs[:, None, :]
        return jnp.stack([jnp.cos(ang), jnp.sin(ang)], axis=1).reshape(6, pos.shape[0], ROPE_LANES)

    tile_trig = jnp.concatenate([trig(tile_pos), jnp.zeros((2, tile_pos.shape[0], ROPE_LANES), F32)], axis=0)
    return tile_trig.transpose(1, 0, 2), trig(row_pos)


def _trunk(sample, x, mods, state, w):
    outs = {}
    mod = mods[0]
    if sample:
        xa, bout, vn = _even_pre_call(True, x, mod, w['ng'][0][0], w['ev_w_in'], w['sg_ln_g'], w['sg_ln_b'],
                                      w['sg_wt'], w['sg_row0'])
        outs['vn'] = vn
        t = w['s5']
        yc, hr, hi = _s5_sample_call(xa, state['s5_re'], state['s5_im'], t['a_r'], t['a_i'],
                                     t['pb_r'], t['pb_i'], t['pc_r'], t['pc_i'])
        outs['s5_re'], outs['s5_im'] = hr, hi
    else:
        xa, bout, xf = _even_pre_call(False, x, mod, w['ng'][0][0], w['ev_w_in'], w['sg_ln_g'], w['sg_ln_b'],
                                      w['sg_wt'], w['sg_bias'])
        t = w['s5']
        yc, hfin = _s5_prompt_call(xf, t['pb_r'], t['pb_i'], t['pc_r'], t['pc_i'], t['a16'])
        outs['s5_re'], outs['s5_im'] = hfin[:, 0], hfin[:, 1]
    mix0 = _even_post_stage(x, yc, xa, bout, mod, w['s5_d'], w['s5_w_glu'], w['s5_b_glu'], w['ev_w_out'])
    prev = (state['conv'][0][:, 0], state['conv'][0][:, 1]) if sample else None
    x, conv0 = _ffn_call(sample, 0, None, mod, w['ng'][0][1], w['ffn_w_up'], w['ffn_conv_w'],
                         w['ffn_conv_b'], w['ffn_w_down'], prev=prev, pre=mix0)
    mod = mods[1]
    rows = x.shape[0]
    keep = rows if sample else min(WIN_MAX, rows)
    pre = _odd_pre_call(x, mod, w['ng'][1][0], w['od_w_qkv'], *w['rope'], keep, planar=not sample)
    q, k, v, k32, v32 = pre[:5]
    outs['k'], outs['v'] = k32, v32
    if sample:
        att = _attn_sample_call(q.astype(F32), k32, v32, state['ck'], state['cv'])
        mix1 = _odd_post_stage(x, att, mod, w['od_w_o'])
    else:
        q_pl, k_pl, v_pl = pre[5:]
        branches = (_attn_prompt_call(1, q[None], k[None], v[None]),
                    _attn_prompt_call(4, q_pl, k_pl, v_pl),
                    _attn_prompt_call(16, q_pl, k_pl, v_pl))
        mix1 = _odd_post_merge_stage(x, branches, mod, w['od_w_o'])
    prev = (state['conv'][1][:, 0], state['conv'][1][:, 1]) if sample else None
    y, conv1 = _ffn_call(sample, 1, None, mod, w['ng'][1][1], w['ffn_w_up'], w['ffn_conv_w'],
                         w['ffn_conv_b'], w['ffn_w_down'], prev=prev, final_g=w['final_g'], pre=mix1)
    outs['y'] = y
    outs['conv'] = (conv0, conv1)
    return outs


def kernel(x_prompt, x_sample, c_prompt, c_sample, state_s5_re, state_s5_im, cache_c_k, cache_c_v,
           state_ffn_conv, ada_w, ada_b, norm_g, final_g, ev_w_in, ev_w_out, s5_lam_re, s5_lam_im,
           s5_log_dt, s5_b_re, s5_b_im, s5_c_re, s5_c_im, s5_d, s5_w_glu, s5_b_glu, sg_ln_g, sg_ln_b,
           sg_w, sg_b, od_w_qkv, od_w_o, ffn_w_up, ffn_conv_w, ffn_conv_b, ffn_w_down):
    bp, seq, _ = x_prompt.shape
    bs = x_sample.shape[0]
    assert bp == 1 and seq == SEQ and bs == DEC_BATCH and x_sample.shape[1] == 1

    c_all = jnp.concatenate([c_sample, c_prompt, jnp.zeros((MOD_ROWS - bs - bp, D_MODEL), F32)], axis=0)
    mod_all = _ada_call(c_all, ada_w, ada_b)
    mods_s = [mod_all[l, :bs] for l in range(2)]
    mods_p = [mod_all[l, bs:bs + 1] for l in range(2)]

    hd = SGU_WIDTH // SGU_HEADS
    causal = jnp.tril(jnp.ones((CHUNK, CHUNK), F32))
    w = dict(
        ng=[[norm_g[l, j].reshape(1, D_MODEL) for j in range(2)] for l in range(2)],
        final_g=final_g.reshape(1, D_MODEL),
        ev_w_in=ev_w_in[0].astype(BF16), ev_w_out=ev_w_out[0].astype(BF16),
        sg_ln_g=sg_ln_g[0].reshape(1, SGU_WIDTH), sg_ln_b=sg_ln_b[0].reshape(1, SGU_WIDTH),
        sg_wt=(sg_w[0] * causal[None]).astype(BF16),
        sg_bias=jnp.repeat(sg_b[0].T, hd, axis=1),
        sg_row0=jnp.stack([jnp.repeat(sg_w[0, :, 0, 0], hd), jnp.repeat(sg_b[0, :, 0], hd)], axis=0),
        s5=_s5_tables(s5_lam_re[0], s5_lam_im[0], s5_log_dt[0], s5_b_re[0], s5_b_im[0], s5_c_re[0], s5_c_im[0]),
        s5_d=s5_d[0].reshape(1, S5_WIDTH), s5_w_glu=s5_w_glu[0].astype(BF16),
        s5_b_glu=s5_b_glu[0].reshape(1, S5_WIDTH),
        od_w_qkv=od_w_qkv[0].astype(BF16), od_w_o=od_w_o[0].astype(BF16),
        ffn_w_up=ffn_w_up.astype(BF16), ffn_w_down=ffn_w_down.astype(BF16),
        ffn_conv_w=ffn_conv_w, ffn_conv_b=ffn_conv_b.reshape(2, 1, D_FF),
    )

    wp = dict(w, rope=_rope_tables(jnp.arange(0, seq, ROW_TILE, dtype=jnp.int32),
                                   jnp.arange(ROW_TILE, dtype=jnp.int32)))
    ws = dict(w, rope=_rope_tables(jnp.full((1,), PAST_LEN, jnp.int32), jnp.zeros((bs,), jnp.int32)))

    p = _trunk(False, x_prompt[0], mods_p, None, wp)
    n_all = S5_GROUPS * S5_STATE
    state = dict(s5_re=state_s5_re[0].reshape(bs, n_all), s5_im=state_s5_im[0].reshape(bs, n_all),
                 ck=jnp.transpose(cache_c_k[0], (0, 2, 3, 1)), cv=jnp.transpose(cache_c_v[0], (0, 2, 3, 1)),
                 conv=state_ffn_conv)
    s = _trunk(True, x_sample[:, 0], mods_s, state, ws)

    keep = min(WIN_MAX, seq)
    kv_p = lambda a: a.reshape(1, 1, keep, N_HEADS, HEAD_DIM)
    kv_s = lambda a: a.reshape(1, bs, 1, N_HEADS, HEAD_DIM)
    s5_p = lambda a: a.reshape(1, 1, S5_GROUPS, S5_STATE)
    s5_s = lambda a: a.reshape(1, bs, S5_GROUPS, S5_STATE)
    conv_p = jnp.stack([c.reshape(1, 2, D_FF) for c in p['conv']])
    conv_s = jnp.stack([jnp.stack([state_ffn_conv[l][:, 1], s['conv'][l]], axis=1) for l in range(2)])
    return (p['y'][None], s['y'][:, None], s5_p(p['s5_re']), s5_p(p['s5_im']),
            s5_s(s['s5_re']), s5_s(s['s5_im']), s['vn'].reshape(1, bs, 1, SGU_WIDTH),
            kv_p(p['k']), kv_p(p['v']), kv_s(s['k']), kv_s(s['v']), conv_p, conv_s)
```

```python
import functools

import jax
import jax.numpy as jnp
from jax import lax
from jax.experimental import pallas as pl
from jax.experimental.pallas import tpu as pltpu

F32 = jnp.float32
BF16 = jnp.bfloat16

D_MODEL = 1024
SEQ = 16384
DEC_BATCH = 32
PAST_LEN = 16384
S5_WIDTH = 512
S5_GROUP = 16
S5_GROUPS = 32
S5_STATE = 64
SGU_WIDTH = 512
SGU_HEADS = 4
CHUNK = 128
HEAD_DIM = 64
N_HEADS = 16
ROT_DIM = 16
ROPE_THETA = 500000.0
DIL_BRANCHES = ((128, 1), (512, 4), (2048, 16))
BAND = 128
WIN_MAX = 2048
D_FF = 2816
EPS = 1e-6
NEG_INF = -1e30

ROW_TILE = 512
EVEN_PRE_TILE = 1024
MOD_ROWS = 40
S5_T = 16
S5_SUB = 8
S5_OLANES = 128
S5_OCTETS = S5_WIDTH // S5_OLANES
S5_OGROUPS = S5_OLANES // S5_GROUP
S5_OSTATE = S5_OGROUPS * S5_STATE
FF_CHUNKS = ((0, D_FF),)
ROPE_LANES = 128
PLANES = 16
LOG2_E = 1.4426950408889634
ATT_BLOCKS = 8
STAT_LANES = 128
VMEM_LIMIT = 56 * 1024 * 1024


def _cparams(*sem):
    return pltpu.CompilerParams(dimension_semantics=sem, vmem_limit_bytes=VMEM_LIMIT)


def _const_spec(shape):
    nd = len(shape)
    return pl.BlockSpec(shape, lambda *_: (0,) * nd)


def _weight_spec(shape):
    nd = len(shape)
    return pl.BlockSpec(shape, lambda *_: (0,) * nd, pipeline_mode=pl.Buffered(1))


def _gelu(x):
    return jax.nn.gelu(x)


def _mod_norm(x, ng, shift, scale):
    ms = jnp.mean(x * x, axis=-1, keepdims=True)
    return (x * lax.rsqrt(ms + EPS) * ng) * (1.0 + scale) + shift


def _ada_body(c_ref, w_ref, b_ref, o_ref):
    c = c_ref[...]
    s = c * jax.nn.sigmoid(c)
    o_ref[...] = jnp.dot(s.astype(BF16), w_ref[...].astype(BF16),
                         preferred_element_type=F32) + b_ref[...]


def _ada_call(c_all, ada_w, ada_b):
    depth = ada_w.shape[0]
    nt = 1536
    return pl.pallas_call(
        _ada_body,
        grid=(depth, 6 * D_MODEL // nt),
        in_specs=[
            pl.BlockSpec((MOD_ROWS, D_MODEL), lambda l, j: (0, 0)),
            pl.BlockSpec((None, D_MODEL, nt), lambda l, j: (l, 0, j)),
            pl.BlockSpec((None, 1, nt), lambda l, j: (l, 0, j)),
        ],
        out_specs=pl.BlockSpec((None, MOD_ROWS, nt), lambda l, j: (l, 0, j)),
        out_shape=jax.ShapeDtypeStruct((depth, MOD_ROWS, 6 * D_MODEL), F32),
        compiler_params=_cparams("arbitrary", "arbitrary"),
        name="ada_mod",
    )(c_all, ada_w, ada_b.reshape(depth, 1, 6 * D_MODEL))


def _even_pre_body(sample, tm, x_ref, mod_ref, ng_ref, win_ref, lng_ref, lnb_ref, wt_ref, bs_ref,
                   xa_ref, bout_ref, *vn_out):
    h = _mod_norm(x_ref[...], ng_ref[...], mod_ref[:, 0:D_MODEL], mod_ref[:, D_MODEL:2 * D_MODEL])
    proj = jnp.dot(h.astype(BF16), win_ref[...], preferred_element_type=F32)
    xa_ref[...] = proj[:, :S5_WIDTH]
    u = _gelu(proj[:, S5_WIDTH:S5_WIDTH + SGU_WIDTH])
    v = _gelu(proj[:, S5_WIDTH + SGU_WIDTH:])
    mu = jnp.mean(v, axis=-1, keepdims=True)
    var = jnp.mean(jnp.square(v - mu), axis=-1, keepdims=True)
    vn = (v - mu) * lax.rsqrt(var + EPS) * lng_ref[...] + lnb_ref[...]
    if sample:
        vn_out[0][...] = vn
        bout_ref[...] = (u * (vn * bs_ref[0:1, :] + bs_ref[1:2, :])).astype(BF16)
    else:
        vnb = vn.astype(BF16)
        hd = SGU_WIDTH // SGU_HEADS
        for ci in range(tm // CHUNK):
            rows = slice(ci * CHUNK, (ci + 1) * CHUNK)
            for hh in range(SGU_HEADS):
                cols = slice(hh * hd, (hh + 1) * hd)
                s = jnp.dot(wt_ref[hh], vnb[rows, cols], preferred_element_type=F32) + bs_ref[:, cols]
                bout_ref[rows, cols] = (u[rows, cols] * s).astype(BF16)
        xf_ref, stage = vn_out
        per = tm // S5_T
        for ov in range(S5_OCTETS):
            stage[ov] = proj[:, ov * S5_OLANES:(ov + 1) * S5_OLANES]
            for t in range(S5_T):
                c0 = (ov * S5_T + t) * S5_OLANES
                xf_ref[:, c0:c0 + S5_OLANES] = stage[ov, pl.ds(t, per, stride=S5_T), :].astype(BF16)


def _even_pre_call(sample, x, mod, ng, win, lng, lnb, wt, bs):
    rows = x.shape[0]
    tm = rows if sample else EVEN_PRE_TILE
    row_spec = lambda w: pl.BlockSpec((tm, w), lambda i: (i, 0))
    out_shape = [jax.ShapeDtypeStruct((rows, S5_WIDTH), F32),
                 jax.ShapeDtypeStruct((rows, SGU_WIDTH), BF16)]
    out_specs = [row_spec(S5_WIDTH), row_spec(SGU_WIDTH)]
    scratch = []
    if sample:
        out_shape.append(jax.ShapeDtypeStruct((rows, SGU_WIDTH), F32))
        out_specs.append(row_spec(SGU_WIDTH))
    else:
        out_shape.append(jax.ShapeDtypeStruct((rows // S5_T, S5_T * S5_WIDTH), BF16))
        out_specs.append(pl.BlockSpec((tm // S5_T, S5_T * S5_WIDTH), lambda i: (i, 0)))
        scratch = [pltpu.VMEM((S5_OCTETS, tm, S5_OLANES), F32)]
    return pl.pallas_call(
        functools.partial(_even_pre_body, sample, tm),
        grid=(rows // tm,),
        in_specs=[row_spec(D_MODEL), _const_spec(mod.shape), _const_spec(ng.shape), _weight_spec(win.shape),
                  _const_spec(lng.shape), _const_spec(lnb.shape), _const_spec(wt.shape), _const_spec(bs.shape)],
        out_specs=out_specs,
        out_shape=out_shape,
        scratch_shapes=scratch,
        compiler_params=_cparams("arbitrary"),
        name="even_pre_sample" if sample else "even_pre_prompt",
    )(x, mod, ng, win, lng, lnb, wt, bs)


def _s5_prompt_body(x_ref, pbr_ref, pbi_ref, pcr_ref, pci_ref, a16_ref,
                    y_ref, hfin_ref, mtt, pb_re, pb_im, pc_re, pc_im, s_re, s_im):
    ol, ns = S5_OLANES, S5_OSTATE
    nlb = ns // ol
    nt_dims = (((1,), (1,)), ((), ()))

    r_grp = lax.broadcasted_iota(jnp.int32, (ol, ns), 0) // S5_GROUP
    c_grp = lax.broadcasted_iota(jnp.int32, (ol, ns), 1) // S5_STATE
    same_group = jnp.where(r_grp == c_grp, 1.0, 0.0).astype(BF16)
    for src, dst in ((pbr_ref, pb_re), (pbi_ref, pb_im), (pcr_ref, pc_re), (pci_ref, pc_im)):
        for t in range(src.shape[0]):
            dst[t * ol:(t + 1) * ol, :] = jnp.concatenate([src[t]] * nlb, axis=1) * same_group

    last = slice((S5_T - 1) * ol, S5_T * ol)
    lagk = (lax.dot_general(pc_re[0:S5_T * ol, :], pb_re[last, :], nt_dims, preferred_element_type=F32)
            + lax.dot_general(pc_im[0:S5_T * ol, :], pb_im[last, :], nt_dims, preferred_element_type=F32)
            ).astype(BF16)

    @pl.when(pl.program_id(0) == 0)
    def _():
        mtt[...] = jnp.zeros_like(mtt)

    for to in range(S5_T):
        for ti in range(to + 1):
            mtt[to * ol:(to + 1) * ol, ti * ol:(ti + 1) * ol] = lagk[(to - ti) * ol:(to - ti + 1) * ol, :]

    x = x_ref[...]
    nrows = x.shape[0]
    sr = jnp.dot(x, pb_re[...], preferred_element_type=F32)
    si = jnp.dot(x, pb_im[...], preferred_element_type=F32)
    row_id = lax.broadcasted_iota(jnp.int32, (nrows, 1), 0)

    def shifted(t, k):
        return jnp.where(row_id >= k, pltpu.roll(t, k, 0), 0.0)

    pr, pi = a16_ref[0:1, :], a16_ref[1:2, :]
    k = 1
    while k < S5_SUB:
        tr, ti = shifted(sr, k), shifted(si, k)
        sr, si = sr + pr * tr - pi * ti, si + pr * ti + pi * tr
        pr, pi = pr * pr - pi * pi, 2.0 * pr * pi
        k *= 2
    s_re[...] = sr
    s_im[...] = si
    hr = jnp.zeros((S5_SUB, ns), F32)
    hi = jnp.zeros((S5_SUB, ns), F32)
    for tile in range(nrows // S5_SUB):
        rows_t = slice(tile * S5_SUB, (tile + 1) * S5_SUB)
        hr, hi = pr * hr - pi * hi + s_re[rows_t, :], pr * hi + pi * hr + s_im[rows_t, :]
        s_re[rows_t, :] = hr
        s_im[rows_t, :] = hi
    hfin_ref[0:1, :] = hr[S5_SUB - 1:S5_SUB, :]
    hfin_ref[1:2, :] = hi[S5_SUB - 1:S5_SUB, :]
    hb_re = shifted(s_re[...], 1).astype(BF16)
    hb_im = shifted(s_im[...], 1).astype(BF16)

    nq = 4
    qw = x.shape[1] // nq
    for j in range(nq):
        cols = slice(j * qw, (j + 1) * qw)
        carry_rows = slice(j * qw + ol, (j + 1) * qw + ol)
        y_ref[:, cols] = (lax.dot_general(hb_re, pc_re[carry_rows, :], nt_dims, preferred_element_type=F32)
                          + lax.dot_general(hb_im, pc_im[carry_rows, :], nt_dims, preferred_element_type=F32)
                          + lax.dot_general(x_ref[:, :(j + 1) * qw], mtt[cols, :(j + 1) * qw], nt_dims,
                                            preferred_element_type=F32)).astype(y_ref.dtype)


def _s5_prompt_call(xf, pbr, pbi, pcr, pci, a16):
    rows = xf.shape[0]
    ow = S5_T * S5_OLANES
    grp = lambda a: pl.BlockSpec((None,) + a.shape[1:], lambda i: (i,) + (0,) * (a.ndim - 1))
    cols = pl.BlockSpec((rows, ow), lambda i: (0, i))
    return pl.pallas_call(
        _s5_prompt_body,
        grid=(S5_OCTETS,),
        in_specs=[cols, grp(pbr), grp(pbi), grp(pcr), grp(pci), grp(a16)],
        out_specs=[cols, pl.BlockSpec((None, 2, S5_OSTATE), lambda i: (i, 0, 0))],
        out_shape=[jax.ShapeDtypeStruct((rows, S5_OCTETS * ow), BF16),
                   jax.ShapeDtypeStruct((S5_OCTETS, 2, S5_OSTATE), F32)],
        scratch_shapes=[pltpu.VMEM((ow, ow), BF16)]
        + [pltpu.VMEM((ow, S5_OSTATE), BF16)] * 2
        + [pltpu.VMEM((ow + S5_OLANES, S5_OSTATE), BF16)] * 2
        + [pltpu.VMEM((rows, S5_OSTATE), F32)] * 2,
        compiler_params=_cparams("arbitrary"),
        name="s5_prompt",
    )(xf, pbr, pbi, pcr, pci, a16)


def _s5_sample_body(xa_ref, h0r_ref, h0i_ref, ar_ref, ai_ref, b_r_ref, b_i_ref, c_r_ref, c_i_ref,
                    yc_ref, hr_ref, hi_ref):
    ol, ns = S5_OLANES, S5_OSTATE
    r_grp = lax.broadcasted_iota(jnp.int32, (ol, ns), 0) // S5_GROUP
    c_grp = lax.broadcasted_iota(jnp.int32, (ol, ns), 1) // S5_STATE
    same_group = jnp.where(r_grp == c_grp, 1.0, 0.0).astype(BF16)
    expand = lambda ref: jnp.concatenate([ref[...]] * (ns // ol), axis=1) * same_group
    nt_dims = (((1,), (1,)), ((), ()))
    u = xa_ref[...].astype(BF16)
    ar = ar_ref[...]
    ai = ai_ref[...]
    h0r = h0r_ref[...]
    h0i = h0i_ref[...]
    hr = ar * h0r - ai * h0i + jnp.dot(u, expand(b_r_ref), preferred_element_type=F32)
    hi = ar * h0i + ai * h0r + jnp.dot(u, expand(b_i_ref), preferred_element_type=F32)
    hr_ref[...] = hr
    hi_ref[...] = hi
    yc_ref[...] = (lax.dot_general(hr.astype(BF16), expand(c_r_ref), nt_dims, preferred_element_type=F32)
                   + lax.dot_general(hi.astype(BF16), expand(c_i_ref), nt_dims, preferred_element_type=F32))


def _s5_sample_call(xa, h0r, h0i, ar, ai, pbr, pbi, pcr, pci):
    rows = xa.shape[0]
    n = S5_GROUPS * S5_STATE
    octet_cols = lambda w: pl.BlockSpec((rows, w), lambda v: (0, v))
    table = lambda a, entry: pl.BlockSpec((None, None) + a.shape[2:], lambda v: (v, entry, 0, 0))
    return pl.pallas_call(
        _s5_sample_body,
        grid=(S5_OCTETS,),
        in_specs=[octet_cols(S5_OLANES), octet_cols(S5_OSTATE), octet_cols(S5_OSTATE),
                  pl.BlockSpec((1, S5_OSTATE), lambda v: (0, v)), pl.BlockSpec((1, S5_OSTATE), lambda v: (0, v)),
                  table(pbr, S5_T - 1), table(pbi, S5_T - 1), table(pcr, 0), table(pci, 0)],
        out_specs=[octet_cols(S5_OLANES), octet_cols(S5_OSTATE), octet_cols(S5_OSTATE)],
        out_shape=[jax.ShapeDtypeStruct((rows, S5_WIDTH), F32),
                   jax.ShapeDtypeStruct((rows, n), F32),
                   jax.ShapeDtypeStruct((rows, n), F32)],
        compiler_params=_cparams("arbitrary"),
        name="s5_sample",
    )(xa, h0r, h0i, ar, ai, pbr, pbi, pcr, pci)


def _even_post_value(folded, tm, in_refs, scratch):
    x_ref, yc_ref, xa_ref, bout_ref, mod_ref, d_ref, wglu_ref, bglu_ref, wout_ref = in_refs
    if folded:
        stage = scratch[0]
        per = tm // S5_T
        for ov in range(S5_OCTETS):
            for t in range(S5_T):
                c0 = (ov * S5_T + t) * S5_OLANES
                stage[ov, pl.ds(t, per, stride=S5_T), :] = yc_ref[:, c0:c0 + S5_OLANES].astype(F32)
        yc = jnp.concatenate([stage[ov] for ov in range(S5_OCTETS)], axis=1)
    else:
        yc = yc_ref[...]
    y = _gelu(yc + d_ref[...] * xa_ref[...])
    gate = jax.nn.sigmoid(jnp.dot(y.astype(BF16), wglu_ref[...], preferred_element_type=F32) + bglu_ref[...])
    a_out = (y * gate).astype(BF16)
    mix = (jnp.dot(a_out, wout_ref[0:S5_WIDTH, :], preferred_element_type=F32)
           + jnp.dot(bout_ref[...], wout_ref[S5_WIDTH:, :], preferred_element_type=F32))
    return x_ref[...] + mod_ref[:, 2 * D_MODEL:3 * D_MODEL] * mix


def _even_post_stage(x, yc, xa, bout, mod, d, wglu, bglu, wout):
    rows = x.shape[0]
    tm = min(rows, ROW_TILE)
    row_spec = lambda w: pl.BlockSpec((tm, w), lambda i: (i, 0))
    folded = yc.shape[0] != rows
    yc_spec = pl.BlockSpec((tm // S5_T, S5_T * S5_WIDTH), lambda i: (i, 0)) if folded else row_spec(S5_WIDTH)
    scratch = [pltpu.VMEM((S5_OCTETS, tm, S5_OLANES), F32)] if folded else []
    in_specs = [row_spec(D_MODEL), yc_spec, row_spec(S5_WIDTH), row_spec(SGU_WIDTH),
                _const_spec(mod.shape), _const_spec(d.shape), _weight_spec(wglu.shape),
                _const_spec(bglu.shape), _weight_spec(wout.shape)]
    return (functools.partial(_even_post_value, folded, tm), [x, yc, xa, bout, mod, d, wglu, bglu, wout],
            in_specs, scratch)


def _ffn_body(sample, final, tm, pre, *refs):
    refs = list(refs)
    if pre is None:
        x_in = refs[0]
        refs = refs[1:]
    else:
        pre_fn, n_pre_in, n_pre_scratch = pre
        pre_in = refs[:n_pre_in]
        pre_scratch = refs[len(refs) - n_pre_scratch:]
        refs = refs[n_pre_in:len(refs) - n_pre_scratch]
    mod_ref, ng_ref, wup_ref, cw_ref, cb_ref, wdn_ref = refs[:6]
    pos = 6
    if sample:
        p2_ref, p1_ref = refs[pos:pos + 2]
        pos += 2
    if final:
        fg_ref = refs[pos]
        pos += 1
    o_ref, conv_ref = refs[pos:pos + 2]
    pos += 2
    if not sample:
        carry_ref = refs[pos]

        @pl.when(pl.program_id(0) == 0)
        def _():
            carry_ref[...] = jnp.zeros_like(carry_ref)

    x = x_in[...] if pre is None else pre_fn(pre_in, pre_scratch)
    h = _mod_norm(x, ng_ref[...], mod_ref[:, 3 * D_MODEL:4 * D_MODEL], mod_ref[:, 4 * D_MODEL:5 * D_MODEL])
    hb = h.astype(BF16)
    acc = jnp.zeros((tm, D_MODEL), F32)
    if not sample:
        row = lax.broadcasted_iota(jnp.int32, (tm, 1), 0)
    for c0, cw in FF_CHUNKS:
        cols = slice(c0, c0 + cw)
        a = jnp.dot(hb, wup_ref[:, cols], preferred_element_type=F32)
        g = jnp.dot(hb, wup_ref[:, D_FF + c0:D_FF + c0 + cw], preferred_element_type=F32)
        if sample:
            am2 = p2_ref[:, cols]
            am1 = p1_ref[:, cols]
            conv_ref[:, cols] = a
        else:
            prev2 = carry_ref[0:1, cols]
            prev1 = carry_ref[1:2, cols]
            am1 = jnp.where(row == 0, prev1, pltpu.roll(a, 1, 0))
            am2 = jnp.where(row == 0, prev2, jnp.where(row == 1, prev1, pltpu.roll(a, 2, 0)))
            carry_ref[0:2, cols] = a[tm - 2:tm, :]
        y = cb_ref[:, cols] + cw_ref[0:1, cols] * am2 + cw_ref[1:2, cols] * am1 + cw_ref[2:3, cols] * a
        act = (_gelu(y) * g).astype(BF16)
        acc = acc + jnp.dot(act, wdn_ref[cols, :], preferred_element_type=F32)
    out = x + mod_ref[:, 5 * D_MODEL:6 * D_MODEL] * acc
    if final:
        ms = jnp.mean(out * out, axis=-1, keepdims=True)
        out = out * lax.rsqrt(ms + EPS) * fg_ref[...]
    o_ref[...] = out
    if not sample:
        conv_ref[...] = carry_ref[0:2, :]


def _ffn_call(sample, layer, x, mod, ng, wup, cw, cb, wdn, prev=None, final_g=None, pre=None):
    rows = (x if pre is None else pre[1][0]).shape[0]
    tm = rows if sample else ROW_TILE
    final = final_g is not None
    row_spec = lambda w: pl.BlockSpec((tm, w), lambda i: (i, 0))
    of_layer = lambda a, **kw: pl.BlockSpec((None,) + a.shape[1:], lambda i: (layer,) + (0,) * (a.ndim - 1), **kw)
    once = dict(pipeline_mode=pl.Buffered(1))
    args = [mod, ng, wup, cw, cb, wdn]
    in_specs = [_const_spec(mod.shape), _const_spec(ng.shape), of_layer(wup, **once),
                of_layer(cw), of_layer(cb), of_layer(wdn, **once)]
    pre_static, pre_scratch = None, []
    if pre is None:
        args.insert(0, x)
        in_specs.insert(0, row_spec(D_MODEL))
    else:
        pre_fn, pre_args, pre_specs, pre_scratch = pre
        args = list(pre_args) + args
        in_specs = list(pre_specs) + in_specs
        pre_static = (pre_fn, len(pre_args), len(pre_scratch))
    if sample:
        args += [prev[0], prev[1]]
        in_specs += [_const_spec(prev[0].shape), _const_spec(prev[1].shape)]
    if final:
        args.append(final_g)
        in_specs.append(_const_spec(final_g.shape))
    conv_rows = rows if sample else 2
    return pl.pallas_call(
        functools.partial(_ffn_body, sample, final, tm, pre_static),
        grid=(rows // tm,),
        in_specs=in_specs,
        out_specs=[row_spec(D_MODEL), _const_spec((conv_rows, D_FF))],
        out_shape=[jax.ShapeDtypeStruct((rows, D_MODEL), F32),
                   jax.ShapeDtypeStruct((conv_rows, D_FF), F32)],
        scratch_shapes=([] if sample else [pltpu.VMEM((8, D_FF), F32)]) + list(pre_scratch),
        compiler_params=_cparams("arbitrary"),
        name="ffn_sample" if sample else "ffn_prompt",
    )(*args)


def _odd_pre_body(planar, tm, first_kept, x_ref, mod_ref, ng_ref, wqkv_ref, tile_trig_ref, row_trig_ref,
                  q_ref, k_ref, v_ref, k32_ref, v32_ref, *rest):
    h = _mod_norm(x_ref[...], ng_ref[...], mod_ref[:, 0:D_MODEL], mod_ref[:, D_MODEL:2 * D_MODEL])
    qkv = jnp.dot(h.astype(BF16), wqkv_ref[...], preferred_element_type=F32)
    tt = tile_trig_ref[...]
    cos_sum = lambda f: tt[2 * f:2 * f + 1] * row_trig_ref[2 * f] - tt[2 * f + 1:2 * f + 2] * row_trig_ref[2 * f + 1]
    sin_sum = lambda f: tt[2 * f + 1:2 * f + 2] * row_trig_ref[2 * f] + tt[2 * f:2 * f + 1] * row_trig_ref[2 * f + 1]
    rc = cos_sum(0)
    ra = sin_sum(1)
    rb = -sin_sum(2)
    lanes = ROPE_LANES
    half = ROT_DIM // 2

    def rope(t):
        return t * rc + pltpu.roll(t, half, 1) * ra + pltpu.roll(t, lanes - half, 1) * rb

    nblk = D_MODEL // lanes
    if planar:
        qpl_ref, kpl_ref, vpl_ref, stage = rest
        per = tm // PLANES

        def to_planes(dst_ref, slot, cols, val):
            stage[slot] = val
            for r in range(PLANES):
                dst_ref[r, :, cols] = stage[slot, pl.ds(r, per, stride=PLANES), :].astype(BF16)

    for j in range(nblk):
        cols = slice(j * lanes, (j + 1) * lanes)
        q = rope(qkv[:, j * lanes:(j + 1) * lanes]) * (HEAD_DIM ** -0.5 * (LOG2_E if planar else 1.0))
        k = rope(qkv[:, D_MODEL + j * lanes:D_MODEL + (j + 1) * lanes])
        v = qkv[:, 2 * D_MODEL + j * lanes:2 * D_MODEL + (j + 1) * lanes]
        q_ref[:, cols] = q.astype(BF16)
        k_ref[:, cols] = k.astype(BF16)
        v_ref[:, cols] = v.astype(BF16)
        if planar:
            to_planes(qpl_ref, j, cols, q)
            to_planes(kpl_ref, nblk + j, cols, k)
            to_planes(vpl_ref, 2 * nblk + j, cols, v)
        else:
            k32_ref[:, cols] = k
            v32_ref[:, cols] = v
    if planar:
        @pl.when(pl.program_id(0) >= first_kept)
        def _():
            for j in range(nblk):
                cols = slice(j * lanes, (j + 1) * lanes)
                k32_ref[:, cols] = stage[nblk + j]
                v32_ref[:, cols] = stage[2 * nblk + j]


def _odd_pre_call(x, mod, ng, wqkv, tile_trig, row_trig, keep, planar):
    rows = x.shape[0]
    tm = min(rows, ROW_TILE)
    nt = rows // tm
    first_kept = (rows - keep) // tm
    row_spec = lambda w: pl.BlockSpec((tm, w), lambda i: (i, 0))
    keep_spec = pl.BlockSpec((tm, D_MODEL), lambda i: (jnp.maximum(i - first_kept, 0), 0))
    out_specs = [row_spec(D_MODEL), row_spec(D_MODEL), row_spec(D_MODEL), keep_spec, keep_spec]
    out_shape = ([jax.ShapeDtypeStruct((rows, D_MODEL), BF16)] * 3
                 + [jax.ShapeDtypeStruct((keep, D_MODEL), F32)] * 2)
    scratch = []
    if planar:
        plane_spec = pl.BlockSpec((PLANES, tm // PLANES, D_MODEL), lambda i: (0, i, 0))
        out_specs += [plane_spec] * 3
        out_shape += [jax.ShapeDtypeStruct((PLANES, rows // PLANES, D_MODEL), BF16)] * 3
        scratch = [pltpu.VMEM((3 * D_MODEL // ROPE_LANES, tm, ROPE_LANES), F32)]
    return pl.pallas_call(
        functools.partial(_odd_pre_body, planar, tm, first_kept),
        grid=(nt,),
        in_specs=[row_spec(D_MODEL), _const_spec(mod.shape), _const_spec(ng.shape), _weight_spec(wqkv.shape),
                  pl.BlockSpec((None,) + tile_trig.shape[1:], lambda i: (i, 0, 0)), _const_spec(row_trig.shape)],
        out_specs=out_specs,
        out_shape=out_shape,
        scratch_shapes=scratch,
        compiler_params=_cparams("arbitrary"),
        name="odd_pre",
    )(x, mod, ng, wqkv, tile_trig, row_trig)


def _attn_prompt_body(n_planes, n_blocks, q_ref, kp_ref, kc_ref, vp_ref, vc_ref, o_ref, st_ref):
    per = BAND // n_planes
    qi = lax.broadcasted_iota(jnp.int32, (BAND, 2 * BAND), 0)
    kj = lax.broadcasted_iota(jnp.int32, (BAND, 2 * BAND), 1)
    q_pos = n_planes * (qi % per) + qi // per
    k_half = kj // BAND
    k_pos = n_planes * (per * k_half + kj % per) + (kj % BAND) // per - BAND
    dist = q_pos - k_pos
    in_band = (dist >= 0) & (dist <= BAND)
    no_prev = jnp.where(pl.program_id(1) == 0, 1, 0)
    bias_any = jnp.where(in_band, 0.0, NEG_INF).astype(F32)
    bias_first = jnp.where(in_band & (k_half >= no_prev), 0.0, NEG_INF).astype(F32)
    lane = lax.broadcasted_iota(jnp.int32, (BAND, 2 * HEAD_DIM), 1)
    low_half = lane < HEAD_DIM
    lane_row = lax.broadcasted_iota(jnp.int32, (1, 2 * HEAD_DIM), 1)
    head_keep = [jnp.where(lane_row < HEAD_DIM, 1.0, 0.0).astype(BF16),
                 jnp.where(lane_row < HEAD_DIM, 0.0, 1.0).astype(BF16)]
    st_lane = lax.broadcasted_iota(jnp.int32, (BAND, STAT_LANES), 1)
    st_out = [jnp.zeros((BAND, STAT_LANES), F32)] * n_blocks

    def rows_of(ref, u, cols):
        t = ref[u * BAND:(u + 1) * BAND, cols] if n_planes == 1 else ref[:, u * per:(u + 1) * per, cols]
        return t.reshape(BAND, t.shape[-1])

    for u in range(n_blocks):
        for pair in range(N_HEADS // 2):
            cols = slice(pair * 2 * HEAD_DIM, (pair + 1) * 2 * HEAD_DIM)
            qp = rows_of(q_ref, u, cols)
            k_before = rows_of(kp_ref, 0, cols) if u == 0 else rows_of(kc_ref, u - 1, cols)
            v_before = rows_of(vp_ref, 0, cols) if u == 0 else rows_of(vc_ref, u - 1, cols)
            kk = jnp.concatenate([k_before, rows_of(kc_ref, u, cols)], axis=0)
            vv = jnp.concatenate([v_before, rows_of(vc_ref, u, cols)], axis=0)
            bias = bias_first if u == 0 else bias_any
            q2 = jnp.concatenate([qp * head_keep[0], qp * head_keep[1]], axis=0)
            s = (lax.dot_general(q2, kk, (((1,), (1,)), ((), ())), preferred_element_type=F32)
                 + jnp.concatenate([bias, bias], axis=0))
            m = jnp.max(s, axis=-1, keepdims=True)
            p = jnp.exp2(s - m)
            l = jnp.sum(p, axis=-1, keepdims=True)
            pv = jnp.dot(p.astype(BF16), vv, preferred_element_type=F32) / l
            for sub in range(2):
                head = 2 * pair + sub
                st_out[u] = jnp.where(st_lane == head, m[sub * BAND:(sub + 1) * BAND], st_out[u])
                st_out[u] = jnp.where(st_lane == N_HEADS + head, l[sub * BAND:(sub + 1) * BAND], st_out[u])
            o_pair = jnp.where(low_half, pv[:BAND], pv[BAND:]).astype(o_ref.dtype)
            if n_planes == 1:
                o_ref[u * BAND:(u + 1) * BAND, cols] = o_pair
            else:
                o_ref[:, u * per:(u + 1) * per, cols] = o_pair.reshape(n_planes, per, 2 * HEAD_DIM)
    for u in range(n_blocks):
        if n_planes == 1:
            st_ref[u * BAND:(u + 1) * BAND, :] = st_out[u]
        else:
            st_ref[:, u * per:(u + 1) * per, :] = st_out[u].reshape(n_planes, per, STAT_LANES)


def _attn_prompt_call(d, q, k, v):
    planes, rpp, width = q.shape
    nu = min(2 * ATT_BLOCKS, rpp * planes // d // BAND)
    before = lambda b: jnp.maximum(nu * b - 1, 0)
    if planes // d == 1 or d == 1:
        n_planes, outer = 1, planes
        view = lambda a: a
        cur = lambda w: pl.BlockSpec((None, nu * BAND, w), lambda r, b: (r, b, 0))
        prev = lambda w: pl.BlockSpec((None, BAND, w), lambda r, b: (r, before(b), 0))
        nb = rpp // (nu * BAND)
    else:
        n_planes, outer = planes // d, d
        per = BAND // n_planes
        view = lambda a: a.reshape(n_planes, outer, rpp, a.shape[-1])
        cur = lambda w: pl.BlockSpec((n_planes, None, nu * per, w), lambda r, b: (0, r, b, 0))
        prev = lambda w: pl.BlockSpec((n_planes, None, per, w), lambda r, b: (0, r, before(b), 0))
        nb = rpp // (nu * per)
    qv, kv, vv = view(q), view(k), view(v)
    o, st = pl.pallas_call(
        functools.partial(_attn_prompt_body, n_planes, nu),
        grid=(outer, nb),
        in_specs=[cur(width), prev(width), cur(width), prev(width), cur(width)],
        out_specs=[cur(width), cur(STAT_LANES)],
        out_shape=[jax.ShapeDtypeStruct(qv.shape, BF16),
                   jax.ShapeDtypeStruct(qv.shape[:-1] + (STAT_LANES,), F32)],
        compiler_params=_cparams("arbitrary", "arbitrary"),
        name="attn_prompt_d%d" % d,
    )(qv, kv, kv, vv, vv)
    return o.reshape(planes, rpp, width), st.reshape(planes, rpp, STAT_LANES)


def _attn_sample_body(hb, past, q_ref, kn_ref, vn_ref, kt_ref, vt_ref, o_ref):
    b = pl.program_id(1)
    dim = kt_ref.shape[1]
    q_row = q_ref[pl.ds(b, 1), :]
    kn_row = kn_ref[pl.ds(b, 1), :]
    vn_row = vn_ref[pl.ds(b, 1), :]
    on_diag = (lax.broadcasted_iota(jnp.int32, (dim, dim), 0)
               == lax.broadcasted_iota(jnp.int32, (dim, dim), 1))

    def to_col(row):
        return jnp.sum(jnp.where(on_diag, jnp.broadcast_to(row, (dim, dim)), 0.0), axis=1, keepdims=True)

    def to_row(col):
        return jnp.sum(jnp.where(on_diag, jnp.broadcast_to(col, (dim, dim)), 0.0), axis=0, keepdims=True)

    rows, news = [], []
    for h in range(hb):
        lanes = slice(h * dim, (h + 1) * dim)
        rows.append(jnp.sum(kt_ref[h] * to_col(q_row[:, lanes]), axis=0, keepdims=True))
        news.append(jnp.sum(kn_row[:, lanes] * q_row[:, lanes], axis=1, keepdims=True))
    s = jnp.concatenate(rows, axis=0)
    s_new = jnp.concatenate(news, axis=0)
    r = lax.broadcasted_iota(jnp.int32, (1, past), 1)
    ms, ls, ps, pns = [], [], [], []
    for window, d in DIL_BRANCHES:
        member = (r >= past - window) & ((past - r) % d == 0)
        sg = s + jnp.where(member, 0.0, NEG_INF).astype(F32)
        m = jnp.maximum(jnp.max(sg, axis=1, keepdims=True), s_new)
        p = jnp.exp(sg - m)
        pn = jnp.exp(s_new - m)
        ms.append(m)
        ps.append(p)
        pns.append(pn)
        ls.append(jnp.sum(p, axis=1, keepdims=True) + pn)
    m_all = jnp.maximum(jnp.maximum(ms[0], ms[1]), ms[2])
    cs = [jnp.exp(m - m_all) for m in ms]
    tot = cs[0] * ls[0] + cs[1] * ls[1] + cs[2] * ls[2]
    w = (cs[0] * ps[0] + cs[1] * ps[1] + cs[2] * ps[2]) / tot
    w_new = (cs[0] * pns[0] + cs[1] * pns[1] + cs[2] * pns[2]) / tot
    outs = []
    for h in range(hb):
        lanes = slice(h * dim, (h + 1) * dim)
        from_cache = jnp.sum(vt_ref[h] * w[h:h + 1, :], axis=1, keepdims=True)
        outs.append(to_row(from_cache) + vn_row[:, lanes] * w_new[h:h + 1, :])
    o_ref[pl.ds(b, 1), :] = jnp.concatenate(outs, axis=1)


def _attn_sample_call(q, k_new, v_new, kt, vt):
    bsz, heads, dim, past = kt.shape
    hb = heads
    rows = pl.BlockSpec((bsz, hb * dim), lambda j, b: (0, j))
    cache = pl.BlockSpec((None, hb, dim, past), lambda j, b: (b, j, 0, 0))
    return pl.pallas_call(
        functools.partial(_attn_sample_body, hb, past),
        grid=(heads // hb, bsz),
        in_specs=[rows, rows, rows, cache, cache],
        out_specs=rows,
        out_shape=jax.ShapeDtypeStruct((bsz, heads * dim), F32),
        compiler_params=_cparams("arbitrary", "arbitrary"),
        name="attn_sample",
    )(q, k_new, v_new, kt, vt)


def _odd_post_value(in_refs, scratch):
    x_ref, att_ref, mod_ref, wo_ref = in_refs
    mix = jnp.dot(att_ref[...].astype(BF16), wo_ref[...], preferred_element_type=F32)
    return x_ref[...] + mod_ref[:, 2 * D_MODEL:3 * D_MODEL] * mix


def _odd_post_stage(x, att, mod, wo):
    rows = x.shape[0]
    tm = min(rows, ROW_TILE)
    row_spec = pl.BlockSpec((tm, D_MODEL), lambda i: (i, 0))
    return (_odd_post_value, [x, att, mod, wo],
            [row_spec, row_spec, _const_spec(mod.shape), _weight_spec(wo.shape)], [])


def _odd_post_merge_value(tm, in_refs, scratch):
    x_ref, o1_ref, s1_ref, o4_ref, s4_ref, o16_ref, s16_ref, mod_ref, wo_ref, spread_ref = in_refs
    ob4, ob16, sb4, sb16 = scratch
    per = tm // PLANES
    nblk = D_MODEL // STAT_LANES
    for r in range(PLANES):
        rows_r = pl.ds(r, per, stride=PLANES)
        for j in range(nblk):
            cols = slice(j * STAT_LANES, (j + 1) * STAT_LANES)
            ob4[j, rows_r, :] = o4_ref[r, :, cols].astype(F32)
            ob16[j, rows_r, :] = o16_ref[r, :, cols].astype(F32)
        sb4[rows_r, :] = s4_ref[r]
        sb16[rows_r, :] = s16_ref[r]
    stats = [s1_ref[...], sb4[...], sb16[...]]
    dens = [pltpu.roll(s, STAT_LANES - N_HEADS, 1) for s in stats]
    m_all = jnp.maximum(jnp.maximum(stats[0], stats[1]), stats[2])
    ws = [d * jnp.exp2(s - m_all) for s, d in zip(stats, dens)]
    tot = ws[0] + ws[1] + ws[2]
    lane = lax.broadcasted_iota(jnp.int32, (tm, STAT_LANES), 1)
    spreads = []
    for w in ws[:2]:
        coef = jnp.where(lane < N_HEADS, w / tot, 0.0)
        hi = coef.astype(BF16)
        lo = (coef - hi.astype(F32)).astype(BF16)
        spreads.append(jnp.dot(jnp.concatenate([hi, lo], axis=1), spread_ref[...], preferred_element_type=F32))
    pieces = []
    for j in range(nblk):
        cols = slice(j * STAT_LANES, (j + 1) * STAT_LANES)
        last = ob16[j]
        pieces.append((last + spreads[0][:, cols] * (o1_ref[:, cols].astype(F32) - last)
                       + spreads[1][:, cols] * (ob4[j] - last)).astype(BF16))
    att = jnp.concatenate(pieces, axis=1)
    mix = jnp.dot(att, wo_ref[...], preferred_element_type=F32)
    return x_ref[...] + mod_ref[:, 2 * D_MODEL:3 * D_MODEL] * mix


def _odd_post_merge_stage(x, branches, mod, wo):
    tm = ROW_TILE
    (o1, s1), (o4, s4), (o16, s16) = branches
    row_spec = pl.BlockSpec((tm, D_MODEL), lambda i: (i, 0))
    nat = lambda w: pl.BlockSpec((None, tm, w), lambda i: (0, i, 0))
    pln = lambda w: pl.BlockSpec((PLANES, tm // PLANES, w), lambda i: (0, i, 0))
    head_of_lane = jnp.arange(D_MODEL) // HEAD_DIM
    spread = (jnp.arange(STAT_LANES)[:, None] == head_of_lane[None, :]).astype(BF16)
    spread = jnp.concatenate([spread, spread], axis=0)
    in_specs = [row_spec, nat(D_MODEL), nat(STAT_LANES), pln(D_MODEL), pln(STAT_LANES),
                pln(D_MODEL), pln(STAT_LANES), _const_spec(mod.shape), _weight_spec(wo.shape),
                _const_spec(spread.shape)]
    scratch = [pltpu.VMEM((D_MODEL // STAT_LANES, tm, STAT_LANES), F32),
               pltpu.VMEM((D_MODEL // STAT_LANES, tm, STAT_LANES), F32),
               pltpu.VMEM((tm, STAT_LANES), F32), pltpu.VMEM((tm, STAT_LANES), F32)]
    return (functools.partial(_odd_post_merge_value, tm), [x, o1, s1, o4, s4, o16, s16, mod, wo, spread],
            in_specs, scratch)


def _s5_tables(lam_re, lam_im, log_dt, b_re, b_im, c_re, c_im):
    dt = jnp.exp(log_dt)[:, None]
    lr, li = lam_re, lam_im
    ks = jnp.arange(S5_T + 1, dtype=F32)[:, None, None]
    mag = jnp.exp(ks * (lr * dt))
    pw_r = mag * jnp.cos(ks * (li * dt))
    pw_i = mag * jnp.sin(ks * (li * dt))
    ar, ai = pw_r[1], pw_i[1]
    den = lr * lr + li * li
    fr = ((ar - 1.0) * lr + ai * li) / den
    fi = (ai * lr - (ar - 1.0) * li) / den
    bbr = fr[..., None] * b_re - fi[..., None] * b_im
    bbi = fr[..., None] * b_im + fi[..., None] * b_re
    ca_r = c_re[None] * pw_r[:, :, None, :] - c_im[None] * pw_i[:, :, None, :]
    ca_i = c_re[None] * pw_i[:, :, None, :] + c_im[None] * pw_r[:, :, None, :]
    oc, og = S5_OCTETS, S5_OGROUPS
    split = lambda a, axis: a.reshape(a.shape[:axis] + (oc, og) + a.shape[axis + 1:])
    kr = (S5_T - 1) - jnp.arange(S5_T, dtype=F32)[:, None, None]
    rev_mag = jnp.exp(kr * (lr * dt))
    rev_r = rev_mag * jnp.cos(kr * (li * dt))
    rev_i = rev_mag * jnp.sin(kr * (li * dt))
    pb_r = rev_r[..., None] * bbr[None] - rev_i[..., None] * bbi[None]
    pb_i = rev_r[..., None] * bbi[None] + rev_i[..., None] * bbr[None]
    twice = lambda a: jnp.concatenate([a, a], axis=-1)
    fold_b = lambda a: twice(split(a, 1).transpose(1, 0, 2, 4, 3).reshape(oc, -1, S5_OLANES, S5_STATE))
    fold_c = lambda a: twice(split(a, 1).transpose(1, 0, 2, 3, 4).reshape(oc, -1, S5_OLANES, S5_STATE))
    pb_r, pb_i = fold_b(pb_r), fold_b(pb_i)
    pc_r, pc_i = fold_c(ca_r), fold_c(-ca_i)
    per_octet = lambda a: a.reshape(oc, S5_OSTATE)
    a16 = jnp.stack([per_octet(pw_r[S5_T]), per_octet(pw_i[S5_T])], axis=1)
    n_all = S5_GROUPS * S5_STATE
    return dict(pb_r=pb_r.astype(BF16), pb_i=pb_i.astype(BF16),
                pc_r=pc_r.astype(BF16), pc_i=pc_i.astype(BF16), a16=a16,
                a_r=ar.reshape(1, n_all), a_i=ai.reshape(1, n_all))


def _rope_tables(tile_pos, row_pos):
    half = ROT_DIM // 2
    inv = jnp.power(ROPE_THETA, -jnp.arange(half, dtype=F32) * 2.0 / ROT_DIM)
    e = jnp.arange(ROPE_LANES) % HEAD_DIM
    inv_e = inv[e % half]
    freqs = jnp.stack([jnp.where(e < ROT_DIM, inv_e, 0.0),
                       jnp.where((e >= half) & (e < ROT_DIM), inv_e, 0.0),
                       jnp.where(e < half, inv_e, 0.0)])

    def trig(pos):
        ang = pos.astype(F32)[None, :, None] * freqs[:, None, :]
        return jnp.stack([jnp.cos(ang), jnp.sin(ang)], axis=1).reshape(6, pos.shape[0], ROPE_LANES)

    tile_trig = jnp.concatenate([trig(tile_pos), jnp.zeros((2, tile_pos.shape[0], ROPE_LANES), F32)], axis=0)
    return tile_trig.transpose(1, 0, 2), trig(row_pos)


def _trunk(sample, x, mods, state, w):
    outs = {}
    mod = mods[0]
    if sample:
        xa, bout, vn = _even_pre_call(True, x, mod, w['ng'][0][0], w['ev_w_in'], w['sg_ln_g'], w['sg_ln_b'],
                                      w['sg_wt'], w['sg_row0'])
        outs['vn'] = vn
        t = w['s5']
        yc, hr, hi = _s5_sample_call(xa, state['s5_re'], state['s5_im'], t['a_r'], t['a_i'],
                                     t['pb_r'], t['pb_i'], t['pc_r'], t['pc_i'])
        outs['s5_re'], outs['s5_im'] = hr, hi
    else:
        xa, bout, xf = _even_pre_call(False, x, mod, w['ng'][0][0], w['ev_w_in'], w['sg_ln_g'], w['sg_ln_b'],
                                      w['sg_wt'], w['sg_bias'])
        t = w['s5']
        yc, hfin = _s5_prompt_call(xf, t['pb_r'], t['pb_i'], t['pc_r'], t['pc_i'], t['a16'])
        outs['s5_re'], outs['s5_im'] = hfin[:, 0], hfin[:, 1]
    mix0 = _even_post_stage(x, yc, xa, bout, mod, w['s5_d'], w['s5_w_glu'], w['s5_b_glu'], w['ev_w_out'])
    prev = (state['conv'][0][:, 0], state['conv'][0][:, 1]) if sample else None
    x, conv0 = _ffn_call(sample, 0, None, mod, w['ng'][0][1], w['ffn_w_up'], w['ffn_conv_w'],
                         w['ffn_conv_b'], w['ffn_w_down'], prev=prev, pre=mix0)
    mod = mods[1]
    rows = x.shape[0]
    keep = rows if sample else min(WIN_MAX, rows)
    pre = _odd_pre_call(x, mod, w['ng'][1][0], w['od_w_qkv'], *w['rope'], keep, planar=not sample)
    q, k, v, k32, v32 = pre[:5]
    outs['k'], outs['v'] = k32, v32
    if sample:
        att = _attn_sample_call(q.astype(F32), k32, v32, state['ck'], state['cv'])
        mix1 = _odd_post_stage(x, att, mod, w['od_w_o'])
    else:
        q_pl, k_pl, v_pl = pre[5:]
        branches = (_attn_prompt_call(1, q[None], k[None], v[None]),
                    _attn_prompt_call(4, q_pl, k_pl, v_pl),
                    _attn_prompt_call(16, q_pl, k_pl, v_pl))
        mix1 = _odd_post_merge_stage(x, branches, mod, w['od_w_o'])
    prev = (state['conv'][1][:, 0], state['conv'][1][:, 1]) if sample else None
    y, conv1 = _ffn_call(sample, 1, None, mod, w['ng'][1][1], w['ffn_w_up'], w['ffn_conv_w'],
                         w['ffn_conv_b'], w['ffn_w_down'], prev=prev, final_g=w['final_g'], pre=mix1)
    outs['y'] = y
    outs['conv'] = (conv0, conv1)
    return outs


def kernel(x_prompt, x_sample, c_prompt, c_sample, state_s5_re, state_s5_im, cache_c_k, cache_c_v,
           state_ffn_conv, ada_w, ada_b, norm_g, final_g, ev_w_in, ev_w_out, s5_lam_re, s5_lam_im,
           s5_log_dt, s5_b_re, s5_b_im, s5_c_re, s5_c_im, s5_d, s5_w_glu, s5_b_glu, sg_ln_g, sg_ln_b,
           sg_w, sg_b, od_w_qkv, od_w_o, ffn_w_up, ffn_conv_w, ffn_conv_b, ffn_w_down):
    bp, seq, _ = x_prompt.shape
    bs = x_sample.shape[0]
    assert bp == 1 and seq == SEQ and bs == DEC_BATCH and x_sample.shape[1] == 1

    c_all = jnp.concatenate([c_sample, c_prompt, jnp.zeros((MOD_ROWS - bs - bp, D_MODEL), F32)], axis=0)
    mod_all = _ada_call(c_all, ada_w, ada_b)
    mods_s = [mod_all[l, :bs] for l in range(2)]
    mods_p = [mod_all[l, bs:bs + 1] for l in range(2)]

    hd = SGU_WIDTH // SGU_HEADS
    causal = jnp.tril(jnp.ones((CHUNK, CHUNK), F32))
    w = dict(
        ng=[[norm_g[l, j].reshape(1, D_MODEL) for j in range(2)] for l in range(2)],
        final_g=final_g.reshape(1, D_MODEL),
        ev_w_in=ev_w_in[0].astype(BF16), ev_w_out=ev_w_out[0].astype(BF16),
        sg_ln_g=sg_ln_g[0].reshape(1, SGU_WIDTH), sg_ln_b=sg_ln_b[0].reshape(1, SGU_WIDTH),
        sg_wt=(sg_w[0] * causal[None]).astype(BF16),
        sg_bias=jnp.repeat(sg_b[0].T, hd, axis=1),
        sg_row0=jnp.stack([jnp.repeat(sg_w[0, :, 0, 0], hd), jnp.repeat(sg_b[0, :, 0], hd)], axis=0),
        s5=_s5_tables(s5_lam_re[0], s5_lam_im[0], s5_log_dt[0], s5_b_re[0], s5_b_im[0], s5_c_re[0], s5_c_im[0]),
        s5_d=s5_d[0].reshape(1, S5_WIDTH), s5_w_glu=s5_w_glu[0].astype(BF16),
        s5_b_glu=s5_b_glu[0].reshape(1, S5_WIDTH),
        od_w_qkv=od_w_qkv[0].astype(BF16), od_w_o=od_w_o[0].astype(BF16),
        ffn_w_up=ffn_w_up.astype(BF16), ffn_w_down=ffn_w_down.astype(BF16),
        ffn_conv_w=ffn_conv_w, ffn_conv_b=ffn_conv_b.reshape(2, 1, D_FF),
    )

    wp = dict(w, rope=_rope_tables(jnp.arange(0, seq, ROW_TILE, dtype=jnp.int32),
                                   jnp.arange(ROW_TILE, dtype=jnp.int32)))
    ws = dict(w, rope=_rope_tables(jnp.full((1,), PAST_LEN, jnp.int32), jnp.zeros((bs,), jnp.int32)))

    p = _trunk(False, x_prompt[0], mods_p, None, wp)
    n_all = S5_GROUPS * S5_STATE
    state = dict(s5_re=state_s5_re[0].reshape(bs, n_all), s5_im=state_s5_im[0].reshape(bs, n_all),
                 ck=jnp.transpose(cache_c_k[0], (0, 2, 3, 1)), cv=jnp.transpose(cache_c_v[0], (0, 2, 3, 1)),
                 conv=state_ffn_conv)
    s = _trunk(True, x_sample[:, 0], mods_s, state, ws)

    keep = min(WIN_MAX, seq)
    kv_p = lambda a: a.reshape(1, 1, keep, N_HEADS, HEAD_DIM)
    kv_s = lambda a: a.reshape(1, bs, 1, N_HEADS, HEAD_DIM)
    s5_p = lambda a: a.reshape(1, 1, S5_GROUPS, S5_STATE)
    s5_s = lambda a: a.reshape(1, bs, S5_GROUPS, S5_STATE)
    conv_p = jnp.stack([c.reshape(1, 2, D_FF) for c in p['conv']])
    conv_s = jnp.stack([jnp.stack([state_ffn_conv[l][:, 1], s['conv'][l]], axis=1) for l in range(2)])
    return (p['y'][None], s['y'][:, None], s5_p(p['s5_re']), s5_p(p['s5_im']),
            s5_s(s['s5_re']), s5_s(s['s5_im']), s['vn'].reshape(1, bs, 1, SGU_WIDTH),
            kv_p(p['k']), kv_p(p['v']), kv_s(s['k']), kv_s(s['v']), conv_p, conv_s)
```

```python
import functools

import jax
import jax.numpy as jnp
from jax import lax
from jax.experimental import pallas as pl
from jax.experimental.pallas import tpu as pltpu

F32 = jnp.float32
BF16 = jnp.bfloat16

D_MODEL = 1024
SEQ = 16384
DEC_BATCH = 32
PAST_LEN = 16384
S5_WIDTH = 512
S5_GROUP = 16
S5_GROUPS = 32
S5_STATE = 64
SGU_WIDTH = 512
SGU_HEADS = 4
CHUNK = 128
HEAD_DIM = 64
N_HEADS = 16
ROT_DIM = 16
ROPE_THETA = 500000.0
DIL_BRANCHES = ((128, 1), (512, 4), (2048, 16))
BAND = 128
WIN_MAX = 2048
D_FF = 2816
EPS = 1e-6
NEG_INF = -1e30

ROW_TILE = 512
EVEN_PRE_TILE = 1024
MOD_ROWS = 40
S5_T = 16
S5_SUB = 8
S5_OLANES = 128
S5_OCTETS = S5_WIDTH // S5_OLANES
S5_OGROUPS = S5_OLANES // S5_GROUP
S5_OSTATE = S5_OGROUPS * S5_STATE
FF_CHUNKS = ((0, D_FF),)
ROPE_LANES = 128
PLANES = 16
LOG2_E = 1.4426950408889634
ATT_BLOCKS = 8
STAT_LANES = 128
VMEM_LIMIT = 56 * 1024 * 1024


def _cparams(*sem):
    return pltpu.CompilerParams(dimension_semantics=sem, vmem_limit_bytes=VMEM_LIMIT)


def _const_spec(shape):
    nd = len(shape)
    return pl.BlockSpec(shape, lambda *_: (0,) * nd)


def _weight_spec(shape):
    nd = len(shape)
    return pl.BlockSpec(shape, lambda *_: (0,) * nd, pipeline_mode=pl.Buffered(1))


def _gelu(x):
    return jax.nn.gelu(x)


def _mod_norm(x, ng, shift, scale):
    ms = jnp.mean(x * x, axis=-1, keepdims=True)
    return (x * lax.rsqrt(ms + EPS) * ng) * (1.0 + scale) + shift


def _ada_body(c_ref, w_ref, b_ref, o_ref):
    c = c_ref[...]
    s = c * jax.nn.sigmoid(c)
    o_ref[...] = jnp.dot(s.astype(BF16), w_ref[...].astype(BF16),
                         preferred_element_type=F32) + b_ref[...]


def _ada_call(c_all, ada_w, ada_b):
    depth = ada_w.shape[0]
    nt = 1536
    return pl.pallas_call(
        _ada_body,
        grid=(depth, 6 * D_MODEL // nt),
        in_specs=[
            pl.BlockSpec((MOD_ROWS, D_MODEL), lambda l, j: (0, 0)),
            pl.BlockSpec((None, D_MODEL, nt), lambda l, j: (l, 0, j)),
            pl.BlockSpec((None, 1, nt), lambda l, j: (l, 0, j)),
        ],
        out_specs=pl.BlockSpec((None, MOD_ROWS, nt), lambda l, j: (l, 0, j)),
        out_shape=jax.ShapeDtypeStruct((depth, MOD_ROWS, 6 * D_MODEL), F32),
        compiler_params=_cparams("arbitrary", "arbitrary"),
        name="ada_mod",
    )(c_all, ada_w, ada_b.reshape(depth, 1, 6 * D_MODEL))


def _even_pre_body(sample, tm, x_ref, mod_ref, ng_ref, win_ref, lng_ref, lnb_ref, wt_ref, bs_ref,
                   xa_ref, bout_ref, *vn_out):
    h = _mod_norm(x_ref[...], ng_ref[...], mod_ref[:, 0:D_MODEL], mod_ref[:, D_MODEL:2 * D_MODEL])
    proj = jnp.dot(h.astype(BF16), win_ref[...], preferred_element_type=F32)
    xa_ref[...] = proj[:, :S5_WIDTH]
    u = _gelu(proj[:, S5_WIDTH:S5_WIDTH + SGU_WIDTH])
    v = _gelu(proj[:, S5_WIDTH + SGU_WIDTH:])
    mu = jnp.mean(v, axis=-1, keepdims=True)
    var = jnp.mean(jnp.square(v - mu), axis=-1, keepdims=True)
    vn = (v - mu) * lax.rsqrt(var + EPS) * lng_ref[...] + lnb_ref[...]
    if sample:
        vn_out[0][...] = vn
        bout_ref[...] = (u * (vn * bs_ref[0:1, :] + bs_ref[1:2, :])).astype(BF16)
    else:
        vnb = vn.astype(BF16)
        hd = SGU_WIDTH // SGU_HEADS
        for ci in range(tm // CHUNK):
            rows = slice(ci * CHUNK, (ci + 1) * CHUNK)
            for hh in range(SGU_HEADS):
                cols = slice(hh * hd, (hh + 1) * hd)
                s = jnp.dot(wt_ref[hh], vnb[rows, cols], preferred_element_type=F32) + bs_ref[:, cols]
                bout_ref[rows, cols] = (u[rows, cols] * s).astype(BF16)
        xf_ref, stage = vn_out
        per = tm // S5_T
        for ov in range(S5_OCTETS):
            stage[ov] = proj[:, ov * S5_OLANES:(ov + 1) * S5_OLANES]
            for t in range(S5_T):
                c0 = (ov * S5_T + t) * S5_OLANES
                xf_ref[:, c0:c0 + S5_OLANES] = stage[ov, pl.ds(t, per, stride=S5_T), :].astype(BF16)


def _even_pre_call(sample, x, mod, ng, win, lng, lnb, wt, bs):
    rows = x.shape[0]
    tm = rows if sample else EVEN_PRE_TILE
    row_spec = lambda w: pl.BlockSpec((tm, w), lambda i: (i, 0))
    out_shape = [jax.ShapeDtypeStruct((rows, S5_WIDTH), F32),
                 jax.ShapeDtypeStruct((rows, SGU_WIDTH), BF16)]
    out_specs = [row_spec(S5_WIDTH), row_spec(SGU_WIDTH)]
    scratch = []
    if sample:
        out_shape.append(jax.ShapeDtypeStruct((rows, SGU_WIDTH), F32))
        out_specs.append(row_spec(SGU_WIDTH))
    else:
        out_shape.append(jax.ShapeDtypeStruct((rows // S5_T, S5_T * S5_WIDTH), BF16))
        out_specs.append(pl.BlockSpec((tm // S5_T, S5_T * S5_WIDTH), lambda i: (i, 0)))
        scratch = [pltpu.VMEM((S5_OCTETS, tm, S5_OLANES), F32)]
    return pl.pallas_call(
        functools.partial(_even_pre_body, sample, tm),
        grid=(rows // tm,),
        in_specs=[row_spec(D_MODEL), _const_spec(mod.shape), _const_spec(ng.shape), _weight_spec(win.shape),
                  _const_spec(lng.shape), _const_spec(lnb.shape), _const_spec(wt.shape), _const_spec(bs.shape)],
        out_specs=out_specs,
        out_shape=out_shape,
        scratch_shapes=scratch,
        compiler_params=_cparams("arbitrary"),
        name="even_pre_sample" if sample else "even_pre_prompt",
    )(x, mod, ng, win, lng, lnb, wt, bs)


def _s5_prompt_body(x_ref, pbr_ref, pbi_ref, pcr_ref, pci_ref, a16_ref,
                    y_ref, hfin_ref, mtt, pb_re, pb_im, pc_re, pc_im, s_re, s_im):
    ol, ns = S5_OLANES, S5_OSTATE
    nlb = ns // ol
    nt_dims = (((1,), (1,)), ((), ()))

    r_grp = lax.broadcasted_iota(jnp.int32, (ol, ns), 0) // S5_GROUP
    c_grp = lax.broadcasted_iota(jnp.int32, (ol, ns), 1) // S5_STATE
    same_group = jnp.where(r_grp == c_grp, 1.0, 0.0).astype(BF16)
    for src, dst in ((pbr_ref, pb_re), (pbi_ref, pb_im), (pcr_ref, pc_re), (pci_ref, pc_im)):
        for t in range(src.shape[0]):
            dst[t * ol:(t + 1) * ol, :] = jnp.concatenate([src[t]] * nlb, axis=1) * same_group

    last = slice((S5_T - 1) * ol, S5_T * ol)
    lagk = (lax.dot_general(pc_re[0:S5_T * ol, :], pb_re[last, :], nt_dims, preferred_element_type=F32)
            + lax.dot_general(pc_im[0:S5_T * ol, :], pb_im[last, :], nt_dims, preferred_element_type=F32)
            ).astype(BF16)

    @pl.when(pl.program_id(0) == 0)
    def _():
        mtt[...] = jnp.zeros_like(mtt)

    for to in range(S5_T):
        for ti in range(to + 1):
            mtt[to * ol:(to + 1) * ol, ti * ol:(ti + 1) * ol] = lagk[(to - ti) * ol:(to - ti + 1) * ol, :]

    x = x_ref[...]
    nrows = x.shape[0]
    sr = jnp.dot(x, pb_re[...], preferred_element_type=F32)
    si = jnp.dot(x, pb_im[...], preferred_element_type=F32)
    row_id = lax.broadcasted_iota(jnp.int32, (nrows, 1), 0)

    def shifted(t, k):
        return jnp.where(row_id >= k, pltpu.roll(t, k, 0), 0.0)

    pr, pi = a16_ref[0:1, :], a16_ref[1:2, :]
    k = 1
    while k < S5_SUB:
        tr, ti = shifted(sr, k), shifted(si, k)
        sr, si = sr + pr * tr - pi * ti, si + pr * ti + pi * tr
        pr, pi = pr * pr - pi * pi, 2.0 * pr * pi
        k *= 2
    s_re[...] = sr
    s_im[...] = si
    hr = jnp.zeros((S5_SUB, ns), F32)
    hi = jnp.zeros((S5_SUB, ns), F32)
    for tile in range(nrows // S5_SUB):
        rows_t = slice(tile * S5_SUB, (tile + 1) * S5_SUB)
        hr, hi = pr * hr - pi * hi + s_re[rows_t, :], pr * hi + pi * hr + s_im[rows_t, :]
        s_re[rows_t, :] = hr
        s_im[rows_t, :] = hi
    hfin_ref[0:1, :] = hr[S5_SUB - 1:S5_SUB, :]
    hfin_ref[1:2, :] = hi[S5_SUB - 1:S5_SUB, :]
    hb_re = shifted(s_re[...], 1).astype(BF16)
    hb_im = shifted(s_im[...], 1).astype(BF16)

    nq = 4
    qw = x.shape[1] // nq
    for j in range(nq):
        cols = slice(j * qw, (j + 1) * qw)
        carry_rows = slice(j * qw + ol, (j + 1) * qw + ol)
        y_ref[:, cols] = (lax.dot_general(hb_re, pc_re[carry_rows, :], nt_dims, preferred_element_type=F32)
                          + lax.dot_general(hb_im, pc_im[carry_rows, :], nt_dims, preferred_element_type=F32)
                          + lax.dot_general(x_ref[:, :(j + 1) * qw], mtt[cols, :(j + 1) * qw], nt_dims,
                                            preferred_element_type=F32)).astype(y_ref.dtype)


def _s5_prompt_call(xf, pbr, pbi, pcr, pci, a16):
    rows = xf.shape[0]
    ow = S5_T * S5_OLANES
    grp = lambda a: pl.BlockSpec((None,) + a.shape[1:], lambda i: (i,) + (0,) * (a.ndim - 1))
    cols = pl.BlockSpec((rows, ow), lambda i: (0, i))
    return pl.pallas_call(
        _s5_prompt_body,
        grid=(S5_OCTETS,),
        in_specs=[cols, grp(pbr), grp(pbi), grp(pcr), grp(pci), grp(a16)],
        out_specs=[cols, pl.BlockSpec((None, 2, S5_OSTATE), lambda i: (i, 0, 0))],
        out_shape=[jax.ShapeDtypeStruct((rows, S5_OCTETS * ow), BF16),
                   jax.ShapeDtypeStruct((S5_OCTETS, 2, S5_OSTATE), F32)],
        scratch_shapes=[pltpu.VMEM((ow, ow), BF16)]
        + [pltpu.VMEM((ow, S5_OSTATE), BF16)] * 2
        + [pltpu.VMEM((ow + S5_OLANES, S5_OSTATE), BF16)] * 2
        + [pltpu.VMEM((rows, S5_OSTATE), F32)] * 2,
        compiler_params=_cparams("arbitrary"),
        name="s5_prompt",
    )(xf, pbr, pbi, pcr, pci, a16)


def _s5_sample_body(xa_ref, h0r_ref, h0i_ref, ar_ref, ai_ref, b_r_ref, b_i_ref, c_r_ref, c_i_ref,
                    yc_ref, hr_ref, hi_ref):
    ol, ns = S5_OLANES, S5_OSTATE
    r_grp = lax.broadcasted_iota(jnp.int32, (ol, ns), 0) // S5_GROUP
    c_grp = lax.broadcasted_iota(jnp.int32, (ol, ns), 1) // S5_STATE
    same_group = jnp.where(r_grp == c_grp, 1.0, 0.0).astype(BF16)
    expand = lambda ref: jnp.concatenate([ref[...]] * (ns // ol), axis=1) * same_group
    nt_dims = (((1,), (1,)), ((), ()))
    u = xa_ref[...].astype(BF16)
    ar = ar_ref[...]
    ai = ai_ref[...]
    h0r = h0r_ref[...]
    h0i = h0i_ref[...]
    hr = ar * h0r - ai * h0i + jnp.dot(u, expand(b_r_ref), preferred_element_type=F32)
    hi = ar * h0i + ai * h0r + jnp.dot(u, expand(b_i_ref), preferred_element_type=F32)
    hr_ref[...] = hr
    hi_ref[...] = hi
    yc_ref[...] = (lax.dot_general(hr.astype(BF16), expand(c_r_ref), nt_dims, preferred_element_type=F32)
                   + lax.dot_general(hi.astype(BF16), expand(c_i_ref), nt_dims, preferred_element_type=F32))


def _s5_sample_call(xa, h0r, h0i, ar, ai, pbr, pbi, pcr, pci):
    rows = xa.shape[0]
    n = S5_GROUPS * S5_STATE
    octet_cols = lambda w: pl.BlockSpec((rows, w), lambda v: (0, v))
    table = lambda a, entry: pl.BlockSpec((None, None) + a.shape[2:], lambda v: (v, entry, 0, 0))
    return pl.pallas_call(
        _s5_sample_body,
        grid=(S5_OCTETS,),
        in_specs=[octet_cols(S5_OLANES), octet_cols(S5_OSTATE), octet_cols(S5_OSTATE),
                  pl.BlockSpec((1, S5_OSTATE), lambda v: (0, v)), pl.BlockSpec((1, S5_OSTATE), lambda v: (0, v)),
                  table(pbr, S5_T - 1), table(pbi, S5_T - 1), table(pcr, 0), table(pci, 0)],
        out_specs=[octet_cols(S5_OLANES), octet_cols(S5_OSTATE), octet_cols(S5_OSTATE)],
        out_shape=[jax.ShapeDtypeStruct((rows, S5_WIDTH), F32),
                   jax.ShapeDtypeStruct((rows, n), F32),
                   jax.ShapeDtypeStruct((rows, n), F32)],
        compiler_params=_cparams("arbitrary"),
        name="s5_sample",
    )(xa, h0r, h0i, ar, ai, pbr, pbi, pcr, pci)


def _even_post_value(folded, tm, in_refs, scratch):
    x_ref, yc_ref, xa_ref, bout_ref, mod_ref, d_ref, wglu_ref, bglu_ref, wout_ref = in_refs
    if folded:
        stage = scratch[0]
        per = tm // S5_T
        for ov in range(S5_OCTETS):
            for t in range(S5_T):
                c0 = (ov * S5_T + t) * S5_OLANES
                stage[ov, pl.ds(t, per, stride=S5_T), :] = yc_ref[:, c0:c0 + S5_OLANES].astype(F32)
        yc = jnp.concatenate([stage[ov] for ov in range(S5_OCTETS)], axis=1)
    else:
        yc = yc_ref[...]
    y = _gelu(yc + d_ref[...] * xa_ref[...])
    gate = jax.nn.sigmoid(jnp.dot(y.astype(BF16), wglu_ref[...], preferred_element_type=F32) + bglu_ref[...])
    a_out = (y * gate).astype(BF16)
    mix = (jnp.dot(a_out, wout_ref[0:S5_WIDTH, :], preferred_element_type=F32)
           + jnp.dot(bout_ref[...], wout_ref[S5_WIDTH:, :], preferred_element_type=F32))
    return x_ref[...] + mod_ref[:, 2 * D_MODEL:3 * D_MODEL] * mix


def _even_post_stage(x, yc, xa, bout, mod, d, wglu, bglu, wout):
    rows = x.shape[0]
    tm = min(rows, ROW_TILE)
    row_spec = lambda w: pl.BlockSpec((tm, w), lambda i: (i, 0))
    folded = yc.shape[0] != rows
    yc_spec = pl.BlockSpec((tm // S5_T, S5_T * S5_WIDTH), lambda i: (i, 0)) if folded else row_spec(S5_WIDTH)
    scratch = [pltpu.VMEM((S5_OCTETS, tm, S5_OLANES), F32)] if folded else []
    in_specs = [row_spec(D_MODEL), yc_spec, row_spec(S5_WIDTH), row_spec(SGU_WIDTH),
                _const_spec(mod.shape), _const_spec(d.shape), _weight_spec(wglu.shape),
                _const_spec(bglu.shape), _weight_spec(wout.shape)]
    return (functools.partial(_even_post_value, folded, tm), [x, yc, xa, bout, mod, d, wglu, bglu, wout],
            in_specs, scratch)


def _ffn_body(sample, final, tm, pre, *refs):
    refs = list(refs)
    if pre is None:
        x_in = refs[0]
        refs = refs[1:]
    else:
        pre_fn, n_pre_in, n_pre_scratch = pre
        pre_in = refs[:n_pre_in]
        pre_scratch = refs[len(refs) - n_pre_scratch:]
        refs = refs[n_pre_in:len(refs) - n_pre_scratch]
    mod_ref, ng_ref, wup_ref, cw_ref, cb_ref, wdn_ref = refs[:6]
    pos = 6
    if sample:
        p2_ref, p1_ref = refs[pos:pos + 2]
        pos += 2
    if final:
        fg_ref = refs[pos]
        pos += 1
    o_ref, conv_ref = refs[pos:pos + 2]
    pos += 2
    if not sample:
        carry_ref = refs[pos]

        @pl.when(pl.program_id(0) == 0)
        def _():
            carry_ref[...] = jnp.zeros_like(carry_ref)

    x = x_in[...] if pre is None else pre_fn(pre_in, pre_scratch)
    h = _mod_norm(x, ng_ref[...], mod_ref[:, 3 * D_MODEL:4 * D_MODEL], mod_ref[:, 4 * D_MODEL:5 * D_MODEL])
    hb = h.astype(BF16)
    acc = jnp.zeros((tm, D_MODEL), F32)
    if not sample:
        row = lax.broadcasted_iota(jnp.int32, (tm, 1), 0)
    for c0, cw in FF_CHUNKS:
        cols = slice(c0, c0 + cw)
        a = jnp.dot(hb, wup_ref[:, cols], preferred_element_type=F32)
        g = jnp.dot(hb, wup_ref[:, D_FF + c0:D_FF + c0 + cw], preferred_element_type=F32)
        if sample:
            am2 = p2_ref[:, cols]
            am1 = p1_ref[:, cols]
            conv_ref[:, cols] = a
        else:
            prev2 = carry_ref[0:1, cols]
            prev1 = carry_ref[1:2, cols]
            am1 = jnp.where(row == 0, prev1, pltpu.roll(a, 1, 0))
            am2 = jnp.where(row == 0, prev2, jnp.where(row == 1, prev1, pltpu.roll(a, 2, 0)))
            carry_ref[0:2, cols] = a[tm - 2:tm, :]
        y = cb_ref[:, cols] + cw_ref[0:1, cols] * am2 + cw_ref[1:2, cols] * am1 + cw_ref[2:3, cols] * a
        act = (_gelu(y) * g).astype(BF16)
        acc = acc + jnp.dot(act, wdn_ref[cols, :], preferred_element_type=F32)
    out = x + mod_ref[:, 5 * D_MODEL:6 * D_MODEL] * acc
    if final:
        ms = jnp.mean(out * out, axis=-1, keepdims=True)
        out = out * lax.rsqrt(ms + EPS) * fg_ref[...]
    o_ref[...] = out
    if not sample:
        conv_ref[...] = carry_ref[0:2, :]


def _ffn_call(sample, layer, x, mod, ng, wup, cw, cb, wdn, prev=None, final_g=None, pre=None):
    rows = (x if pre is None else pre[1][0]).shape[0]
    tm = rows if sample else ROW_TILE
    final = final_g is not None
    row_spec = lambda w: pl.BlockSpec((tm, w), lambda i: (i, 0))
    of_layer = lambda a, **kw: pl.BlockSpec((None,) + a.shape[1:], lambda i: (layer,) + (0,) * (a.ndim - 1), **kw)
    once = dict(pipeline_mode=pl.Buffered(1))
    args = [mod, ng, wup, cw, cb, wdn]
    in_specs = [_const_spec(mod.shape), _const_spec(ng.shape), of_layer(wup, **once),
                of_layer(cw), of_layer(cb), of_layer(wdn, **once)]
    pre_static, pre_scratch = None, []
    if pre is None:
        args.insert(0, x)
        in_specs.insert(0, row_spec(D_MODEL))
    else:
        pre_fn, pre_args, pre_specs, pre_scratch = pre
        args = list(pre_args) + args
        in_specs = list(pre_specs) + in_specs
        pre_static = (pre_fn, len(pre_args), len(pre_scratch))
    if sample:
        args += [prev[0], prev[1]]
        in_specs += [_const_spec(prev[0].shape), _const_spec(prev[1].shape)]
    if final:
        args.append(final_g)
        in_specs.append(_const_spec(final_g.shape))
    conv_rows = rows if sample else 2
    return pl.pallas_call(
        functools.partial(_ffn_body, sample, final, tm, pre_static),
        grid=(rows // tm,),
        in_specs=in_specs,
        out_specs=[row_spec(D_MODEL), _const_spec((conv_rows, D_FF))],
        out_shape=[jax.ShapeDtypeStruct((rows, D_MODEL), F32),
                   jax.ShapeDtypeStruct((conv_rows, D_FF), F32)],
        scratch_shapes=([] if sample else [pltpu.VMEM((8, D_FF), F32)]) + list(pre_scratch),
        compiler_params=_cparams("arbitrary"),
        name="ffn_sample" if sample else "ffn_prompt",
    )(*args)


def _odd_pre_body(planar, tm, first_kept, x_ref, mod_ref, ng_ref, wqkv_ref, tile_trig_ref, row_trig_ref,
                  q_ref, k_ref, v_ref, k32_ref, v32_ref, *rest):
    h = _mod_norm(x_ref[...], ng_ref[...], mod_ref[:, 0:D_MODEL], mod_ref[:, D_MODEL:2 * D_MODEL])
    qkv = jnp.dot(h.astype(BF16), wqkv_ref[...], preferred_element_type=F32)
    tt = tile_trig_ref[...]
    cos_sum = lambda f: tt[2 * f:2 * f + 1] * row_trig_ref[2 * f] - tt[2 * f + 1:2 * f + 2] * row_trig_ref[2 * f + 1]
    sin_sum = lambda f: tt[2 * f + 1:2 * f + 2] * row_trig_ref[2 * f] + tt[2 * f:2 * f + 1] * row_trig_ref[2 * f + 1]
    rc = cos_sum(0)
    ra = sin_sum(1)
    rb = -sin_sum(2)
    lanes = ROPE_LANES
    half = ROT_DIM // 2

    def rope(t):
        return t * rc + pltpu.roll(t, half, 1) * ra + pltpu.roll(t, lanes - half, 1) * rb

    nblk = D_MODEL // lanes
    if planar:
        qpl_ref, kpl_ref, vpl_ref, stage = rest
        per = tm // PLANES

        def to_planes(dst_ref, slot, cols, val):
            stage[slot] = val
            for r in range(PLANES):
                dst_ref[r, :, cols] = stage[slot, pl.ds(r, per, stride=PLANES), :].astype(BF16)

    for j in range(nblk):
        cols = slice(j * lanes, (j + 1) * lanes)
        q = rope(qkv[:, j * lanes:(j + 1) * lanes]) * (HEAD_DIM ** -0.5 * (LOG2_E if planar else 1.0))
        k = rope(qkv[:, D_MODEL + j * lanes:D_MODEL + (j + 1) * lanes])
        v = qkv[:, 2 * D_MODEL + j * lanes:2 * D_MODEL + (j + 1) * lanes]
        q_ref[:, cols] = q.astype(BF16)
        k_ref[:, cols] = k.astype(BF16)
        v_ref[:, cols] = v.astype(BF16)
        if planar:
            to_planes(qpl_ref, j, cols, q)
            to_planes(kpl_ref, nblk + j, cols, k)
            to_planes(vpl_ref, 2 * nblk + j, cols, v)
        else:
            k32_ref[:, cols] = k
            v32_ref[:, cols] = v
    if planar:
        @pl.when(pl.program_id(0) >= first_kept)
        def _():
            for j in range(nblk):
                cols = slice(j * lanes, (j + 1) * lanes)
                k32_ref[:, cols] = stage[nblk + j]
                v32_ref[:, cols] = stage[2 * nblk + j]


def _odd_pre_call(x, mod, ng, wqkv, tile_trig, row_trig, keep, planar):
    rows = x.shape[0]
    tm = min(rows, ROW_TILE)
    nt = rows // tm
    first_kept = (rows - keep) // tm
    row_spec = lambda w: pl.BlockSpec((tm, w), lambda i: (i, 0))
    keep_spec = pl.BlockSpec((tm, D_MODEL), lambda i: (jnp.maximum(i - first_kept, 0), 0))
    out_specs = [row_spec(D_MODEL), row_spec(D_MODEL), row_spec(D_MODEL), keep_spec, keep_spec]
    out_shape = ([jax.ShapeDtypeStruct((rows, D_MODEL), BF16)] * 3
                 + [jax.ShapeDtypeStruct((keep, D_MODEL), F32)] * 2)
    scratch = []
    if planar:
        plane_spec = pl.BlockSpec((PLANES, tm // PLANES, D_MODEL), lambda i: (0, i, 0))
        out_specs += [plane_spec] * 3
        out_shape += [jax.ShapeDtypeStruct((PLANES, rows // PLANES, D_MODEL), BF16)] * 3
        scratch = [pltpu.VMEM((3 * D_MODEL // ROPE_LANES, tm, ROPE_LANES), F32)]
    return pl.pallas_call(
        functools.partial(_odd_pre_body, planar, tm, first_kept),
        grid=(nt,),
        in_specs=[row_spec(D_MODEL), _const_spec(mod.shape), _const_spec(ng.shape), _weight_spec(wqkv.shape),
                  pl.BlockSpec((None,) + tile_trig.shape[1:], lambda i: (i, 0, 0)), _const_spec(row_trig.shape)],
        out_specs=out_specs,
        out_shape=out_shape,
        scratch_shapes=scratch,
        compiler_params=_cparams("arbitrary"),
        name="odd_pre",
    )(x, mod, ng, wqkv, tile_trig, row_trig)


def _attn_prompt_body(n_planes, n_blocks, q_ref, kp_ref, kc_ref, vp_ref, vc_ref, o_ref, st_ref):
    per = BAND // n_planes
    qi = lax.broadcasted_iota(jnp.int32, (BAND, 2 * BAND), 0)
    kj = lax.broadcasted_iota(jnp.int32, (BAND, 2 * BAND), 1)
    q_pos = n_planes * (qi % per) + qi // per
    k_half = kj // BAND
    k_pos = n_planes * (per * k_half + kj % per) + (kj % BAND) // per - BAND
    dist = q_pos - k_pos
    in_band = (dist >= 0) & (dist <= BAND)
    no_prev = jnp.where(pl.program_id(1) == 0, 1, 0)
    bias_any = jnp.where(in_band, 0.0, NEG_INF).astype(F32)
    bias_first = jnp.where(in_band & (k_half >= no_prev), 0.0, NEG_INF).astype(F32)
    lane = lax.broadcasted_iota(jnp.int32, (BAND, 2 * HEAD_DIM), 1)
    low_half = lane < HEAD_DIM
    lane_row = lax.broadcasted_iota(jnp.int32, (1, 2 * HEAD_DIM), 1)
    head_keep = [jnp.where(lane_row < HEAD_DIM, 1.0, 0.0).astype(BF16),
                 jnp.where(lane_row < HEAD_DIM, 0.0, 1.0).astype(BF16)]
    st_ref[...] = jnp.zeros_like(st_ref)

    def rows_of(ref, u, cols):
        t = ref[u * BAND:(u + 1) * BAND, cols] if n_planes == 1 else ref[:, u * per:(u + 1) * per, cols]
        return t.reshape(BAND, t.shape[-1])

    for u in range(n_blocks):
        for pair in range(N_HEADS // 2):
            cols = slice(pair * 2 * HEAD_DIM, (pair + 1) * 2 * HEAD_DIM)
            qp = rows_of(q_ref, u, cols)
            k_before = rows_of(kp_ref, 0, cols) if u == 0 else rows_of(kc_ref, u - 1, cols)
            v_before = rows_of(vp_ref, 0, cols) if u == 0 else rows_of(vc_ref, u - 1, cols)
            kk = jnp.concatenate([k_before, rows_of(kc_ref, u, cols)], axis=0)
            vv = jnp.concatenate([v_before, rows_of(vc_ref, u, cols)], axis=0)
            bias = bias_first if u == 0 else bias_any
            q2 = jnp.concatenate([qp * head_keep[0], qp * head_keep[1]], axis=0)
            s = (lax.dot_general(q2, kk, (((1,), (1,)), ((), ())), preferred_element_type=F32)
                 + jnp.concatenate([bias, bias], axis=0))
            m = jnp.max(s, axis=-1, keepdims=True)
            p = jnp.exp2(s - m)
            l = jnp.sum(p, axis=-1, keepdims=True)
            pv = jnp.dot(p.astype(BF16), vv, preferred_element_type=F32) / l
            for sub in range(2):
                head = 2 * pair + sub
                mm = m[sub * BAND:(sub + 1) * BAND]
                ll = l[sub * BAND:(sub + 1) * BAND]
                if n_planes == 1:
                    st_ref[u * BAND:(u + 1) * BAND, head:head + 1] = mm
                    st_ref[u * BAND:(u + 1) * BAND, N_HEADS + head:N_HEADS + head + 1] = ll
                else:
                    st_ref[:, u * per:(u + 1) * per, head:head + 1] = mm.reshape(n_planes, per, 1)
                    st_ref[:, u * per:(u + 1) * per, N_HEADS + head:N_HEADS + head + 1] = ll.reshape(n_planes, per, 1)
            o_pair = jnp.where(low_half, pv[:BAND], pv[BAND:]).astype(o_ref.dtype)
            if n_planes == 1:
                o_ref[u * BAND:(u + 1) * BAND, cols] = o_pair
            else:
                o_ref[:, u * per:(u + 1) * per, cols] = o_pair.reshape(n_planes, per, 2 * HEAD_DIM)


def _attn_prompt_call(d, q, k, v):
    planes, rpp, width = q.shape
    nu = ATT_BLOCKS
    before = lambda b: jnp.maximum(nu * b - 1, 0)
    if planes // d == 1 or d == 1:
        n_planes, outer = 1, planes
        view = lambda a: a
        cur = lambda w: pl.BlockSpec((None, nu * BAND, w), lambda r, b: (r, b, 0))
        prev = lambda w: pl.BlockSpec((None, BAND, w), lambda r, b: (r, before(b), 0))
        nb = rpp // (nu * BAND)
    else:
        n_planes, outer = planes // d, d
        per = BAND // n_planes
        view = lambda a: a.reshape(n_planes, outer, rpp, a.shape[-1])
        cur = lambda w: pl.BlockSpec((n_planes, None, nu * per, w), lambda r, b: (0, r, b, 0))
        prev = lambda w: pl.BlockSpec((n_planes, None, per, w), lambda r, b: (0, r, before(b), 0))
        nb = rpp // (nu * per)
    qv, kv, vv = view(q), view(k), view(v)
    o, st = pl.pallas_call(
        functools.partial(_attn_prompt_body, n_planes, nu),
        grid=(outer, nb),
        in_specs=[cur(width), prev(width), cur(width), prev(width), cur(width)],
        out_specs=[cur(width), cur(STAT_LANES)],
        out_shape=[jax.ShapeDtypeStruct(qv.shape, BF16),
                   jax.ShapeDtypeStruct(qv.shape[:-1] + (STAT_LANES,), F32)],
        compiler_params=_cparams("arbitrary", "arbitrary"),
        name="attn_prompt_d%d" % d,
    )(qv, kv, kv, vv, vv)
    return o.reshape(planes, rpp, width), st.reshape(planes, rpp, STAT_LANES)


def _attn_sample_body(hb, past, q_ref, kn_ref, vn_ref, kt_ref, vt_ref, o_ref):
    b = pl.program_id(1)
    dim = kt_ref.shape[1]
    q_row = q_ref[pl.ds(b, 1), :]
    kn_row = kn_ref[pl.ds(b, 1), :]
    vn_row = vn_ref[pl.ds(b, 1), :]
    on_diag = (lax.broadcasted_iota(jnp.int32, (dim, dim), 0)
               == lax.broadcasted_iota(jnp.int32, (dim, dim), 1))

    def to_col(row):
        return jnp.sum(jnp.where(on_diag, jnp.broadcast_to(row, (dim, dim)), 0.0), axis=1, keepdims=True)

    def to_row(col):
        return jnp.sum(jnp.where(on_diag, jnp.broadcast_to(col, (dim, dim)), 0.0), axis=0, keepdims=True)

    rows, news = [], []
    for h in range(hb):
        lanes = slice(h * dim, (h + 1) * dim)
        rows.append(jnp.sum(kt_ref[h] * to_col(q_row[:, lanes]), axis=0, keepdims=True))
        news.append(jnp.sum(kn_row[:, lanes] * q_row[:, lanes], axis=1, keepdims=True))
    s = jnp.concatenate(rows, axis=0)
    s_new = jnp.concatenate(news, axis=0)
    r = lax.broadcasted_iota(jnp.int32, (1, past), 1)
    ms, ls, ps, pns = [], [], [], []
    for window, d in DIL_BRANCHES:
        member = (r >= past - window) & ((past - r) % d == 0)
        sg = s + jnp.where(member, 0.0, NEG_INF).astype(F32)
        m = jnp.maximum(jnp.max(sg, axis=1, keepdims=True), s_new)
        p = jnp.exp(sg - m)
        pn = jnp.exp(s_new - m)
        ms.append(m)
        ps.append(p)
        pns.append(pn)
        ls.append(jnp.sum(p, axis=1, keepdims=True) + pn)
    m_all = jnp.maximum(jnp.maximum(ms[0], ms[1]), ms[2])
    cs = [jnp.exp(m - m_all) for m in ms]
    tot = cs[0] * ls[0] + cs[1] * ls[1] + cs[2] * ls[2]
    w = (cs[0] * ps[0] + cs[1] * ps[1] + cs[2] * ps[2]) / tot
    w_new = (cs[0] * pns[0] + cs[1] * pns[1] + cs[2] * pns[2]) / tot
    outs = []
    for h in range(hb):
        lanes = slice(h * dim, (h + 1) * dim)
        from_cache = jnp.sum(vt_ref[h] * w[h:h + 1, :], axis=1, keepdims=True)
        outs.append(to_row(from_cache) + vn_row[:, lanes] * w_new[h:h + 1, :])
    o_ref[pl.ds(b, 1), :] = jnp.concatenate(outs, axis=1)


def _attn_sample_call(q, k_new, v_new, kt, vt):
    bsz, heads, dim, past = kt.shape
    hb = heads
    rows = pl.BlockSpec((bsz, hb * dim), lambda j, b: (0, j))
    cache = pl.BlockSpec((None, hb, dim, past), lambda j, b: (b, j, 0, 0))
    return pl.pallas_call(
        functools.partial(_attn_sample_body, hb, past),
        grid=(heads // hb, bsz),
        in_specs=[rows, rows, rows, cache, cache],
        out_specs=rows,
        out_shape=jax.ShapeDtypeStruct((bsz, heads * dim), F32),
        compiler_params=_cparams("arbitrary", "arbitrary"),
        name="attn_sample",
    )(q, k_new, v_new, kt, vt)


def _odd_post_value(in_refs, scratch):
    x_ref, att_ref, mod_ref, wo_ref = in_refs
    mix = jnp.dot(att_ref[...].astype(BF16), wo_ref[...], preferred_element_type=F32)
    return x_ref[...] + mod_ref[:, 2 * D_MODEL:3 * D_MODEL] * mix


def _odd_post_stage(x, att, mod, wo):
    rows = x.shape[0]
    tm = min(rows, ROW_TILE)
    row_spec = pl.BlockSpec((tm, D_MODEL), lambda i: (i, 0))
    return (_odd_post_value, [x, att, mod, wo],
            [row_spec, row_spec, _const_spec(mod.shape), _weight_spec(wo.shape)], [])


def _odd_post_merge_value(tm, in_refs, scratch):
    x_ref, o1_ref, s1_ref, o4_ref, s4_ref, o16_ref, s16_ref, mod_ref, wo_ref, spread_ref = in_refs
    ob4, ob16, sb4, sb16 = scratch
    per = tm // PLANES
    nblk = D_MODEL // STAT_LANES
    for r in range(PLANES):
        rows_r = pl.ds(r, per, stride=PLANES)
        for j in range(nblk):
            cols = slice(j * STAT_LANES, (j + 1) * STAT_LANES)
            ob4[j, rows_r, :] = o4_ref[r, :, cols].astype(F32)
            ob16[j, rows_r, :] = o16_ref[r, :, cols].astype(F32)
        sb4[rows_r, :] = s4_ref[r]
        sb16[rows_r, :] = s16_ref[r]
    stats = [s1_ref[...], sb4[...], sb16[...]]
    dens = [pltpu.roll(s, STAT_LANES - N_HEADS, 1) for s in stats]
    m_all = jnp.maximum(jnp.maximum(stats[0], stats[1]), stats[2])
    ws = [d * jnp.exp2(s - m_all) for s, d in zip(stats, dens)]
    tot = ws[0] + ws[1] + ws[2]
    lane = lax.broadcasted_iota(jnp.int32, (tm, STAT_LANES), 1)
    spreads = []
    for w in ws[:2]:
        coef = jnp.where(lane < N_HEADS, w / tot, 0.0)
        hi = coef.astype(BF16)
        lo = (coef - hi.astype(F32)).astype(BF16)
        spreads.append(jnp.dot(jnp.concatenate([hi, lo], axis=1), spread_ref[...], preferred_element_type=F32))
    pieces = []
    for j in range(nblk):
        cols = slice(j * STAT_LANES, (j + 1) * STAT_LANES)
        last = ob16[j]
        pieces.append((last + spreads[0][:, cols] * (o1_ref[:, cols].astype(F32) - last)
                       + spreads[1][:, cols] * (ob4[j] - last)).astype(BF16))
    att = jnp.concatenate(pieces, axis=1)
    mix = jnp.dot(att, wo_ref[...], preferred_element_type=F32)
    return x_ref[...] + mod_ref[:, 2 * D_MODEL:3 * D_MODEL] * mix


def _odd_post_merge_stage(x, branches, mod, wo):
    tm = ROW_TILE
    (o1, s1), (o4, s4), (o16, s16) = branches
    row_spec = pl.BlockSpec((tm, D_MODEL), lambda i: (i, 0))
    nat = lambda w: pl.BlockSpec((None, tm, w), lambda i: (0, i, 0))
    pln = lambda w: pl.BlockSpec((PLANES, tm // PLANES, w), lambda i: (0, i, 0))
    head_of_lane = jnp.arange(D_MODEL) // HEAD_DIM
    spread = (jnp.arange(STAT_LANES)[:, None] == head_of_lane[None, :]).astype(BF16)
    spread = jnp.concatenate([spread, spread], axis=0)
    in_specs = [row_spec, nat(D_MODEL), nat(STAT_LANES), pln(D_MODEL), pln(STAT_LANES),
                pln(D_MODEL), pln(STAT_LANES), _const_spec(mod.shape), _weight_spec(wo.shape),
                _const_spec(spread.shape)]
    scratch = [pltpu.VMEM((D_MODEL // STAT_LANES, tm, STAT_LANES), F32),
               pltpu.VMEM((D_MODEL // STAT_LANES, tm, STAT_LANES), F32),
               pltpu.VMEM((tm, STAT_LANES), F32), pltpu.VMEM((tm, STAT_LANES), F32)]
    return (functools.partial(_odd_post_merge_value, tm), [x, o1, s1, o4, s4, o16, s16, mod, wo, spread],
            in_specs, scratch)


def _s5_tables(lam_re, lam_im, log_dt, b_re, b_im, c_re, c_im):
    dt = jnp.exp(log_dt)[:, None]
    lr, li = lam_re, lam_im
    ks = jnp.arange(S5_T + 1, dtype=F32)[:, None, None]
    mag = jnp.exp(ks * (lr * dt))
    pw_r = mag * jnp.cos(ks * (li * dt))
    pw_i = mag * jnp.sin(ks * (li * dt))
    ar, ai = pw_r[1], pw_i[1]
    den = lr * lr + li * li
    fr = ((ar - 1.0) * lr + ai * li) / den
    fi = (ai * lr - (ar - 1.0) * li) / den
    bbr = fr[..., None] * b_re - fi[..., None] * b_im
    bbi = fr[..., None] * b_im + fi[..., None] * b_re
    ca_r = c_re[None] * pw_r[:, :, None, :] - c_im[None] * pw_i[:, :, None, :]
    ca_i = c_re[None] * pw_i[:, :, None, :] + c_im[None] * pw_r[:, :, None, :]
    oc, og = S5_OCTETS, S5_OGROUPS
    split = lambda a, axis: a.reshape(a.shape[:axis] + (oc, og) + a.shape[axis + 1:])
    kr = (S5_T - 1) - jnp.arange(S5_T, dtype=F32)[:, None, None]
    rev_mag = jnp.exp(kr * (lr * dt))
    rev_r = rev_mag * jnp.cos(kr * (li * dt))
    rev_i = rev_mag * jnp.sin(kr * (li * dt))
    pb_r = rev_r[..., None] * bbr[None] - rev_i[..., None] * bbi[None]
    pb_i = rev_r[..., None] * bbi[None] + rev_i[..., None] * bbr[None]
    twice = lambda a: jnp.concatenate([a, a], axis=-1)
    fold_b = lambda a: twice(split(a, 1).transpose(1, 0, 2, 4, 3).reshape(oc, -1, S5_OLANES, S5_STATE))
    fold_c = lambda a: twice(split(a, 1).transpose(1, 0, 2, 3, 4).reshape(oc, -1, S5_OLANES, S5_STATE))
    pb_r, pb_i = fold_b(pb_r), fold_b(pb_i)
    pc_r, pc_i = fold_c(ca_r), fold_c(-ca_i)
    per_octet = lambda a: a.reshape(oc, S5_OSTATE)
    a16 = jnp.stack([per_octet(pw_r[S5_T]), per_octet(pw_i[S5_T])], axis=1)
    n_all = S5_GROUPS * S5_STATE
    return dict(pb_r=pb_r.astype(BF16), pb_i=pb_i.astype(BF16),
                pc_r=pc_r.astype(BF16), pc_i=pc_i.astype(BF16), a16=a16,
                a_r=ar.reshape(1, n_all), a_i=ai.reshape(1, n_all))


def _rope_tables(tile_pos, row_pos):
    half = ROT_DIM // 2
    inv = jnp.power(ROPE_THETA, -jnp.arange(half, dtype=F32) * 2.0 / ROT_DIM)
    e = jnp.arange(ROPE_LANES) % HEAD_DIM
    inv_e = inv[e % half]
    freqs = jnp.stack([jnp.where(e < ROT_DIM, inv_e, 0.0),
                       jnp.where((e >= half) & (e < ROT_DIM), inv_e, 0.0),
                       jnp.where(e < half, inv_e, 0.0)])

    def trig(pos):
        ang = pos.astype(F32)[None, :, None] * freqs[:, None, :]
        return jnp.stack([jnp.cos(ang), jnp.sin(ang)], axis=1).reshape(6, pos.shape[0], ROPE_LANES)

    tile_trig = jnp.concatenate([trig(tile_pos), jnp.zeros((2, tile_pos.shape[0], ROPE_LANES), F32)], axis=0)
    return tile_trig.transpose(1, 0, 2), trig(row_pos)


def _trunk(sample, x, mods, state, w):
    outs = {}
    mod = mods[0]
    if sample:
        xa, bout, vn = _even_pre_call(True, x, mod, w['ng'][0][0], w['ev_w_in'], w['sg_ln_g'], w['sg_ln_b'],
                                      w['sg_wt'], w['sg_row0'])
        outs['vn'] = vn
        t = w['s5']
        yc, hr, hi = _s5_sample_call(xa, state['s5_re'], state['s5_im'], t['a_r'], t['a_i'],
                                     t['pb_r'], t['pb_i'], t['pc_r'], t['pc_i'])
        outs['s5_re'], outs['s5_im'] = hr, hi
    else:
        xa, bout, xf = _even_pre_call(False, x, mod, w['ng'][0][0], w['ev_w_in'], w['sg_ln_g'], w['sg_ln_b'],
                                      w['sg_wt'], w['sg_bias'])
        t = w['s5']
        yc, hfin = _s5_prompt_call(xf, t['pb_r'], t['pb_i'], t['pc_r'], t['pc_i'], t['a16'])
        outs['s5_re'], outs['s5_im'] = hfin[:, 0], hfin[:, 1]
    mix0 = _even_post_stage(x, yc, xa, bout, mod, w['s5_d'], w['s5_w_glu'], w['s5_b_glu'], w['ev_w_out'])
    prev = (state['conv'][0][:, 0], state['conv'][0][:, 1]) if sample else None
    x, conv0 = _ffn_call(sample, 0, None, mod, w['ng'][0][1], w['ffn_w_up'], w['ffn_conv_w'],
                         w['ffn_conv_b'], w['ffn_w_down'], prev=prev, pre=mix0)
    mod = mods[1]
    rows = x.shape[0]
    keep = rows if sample else min(WIN_MAX, rows)
    pre = _odd_pre_call(x, mod, w['ng'][1][0], w['od_w_qkv'], *w['rope'], keep, planar=not sample)
    q, k, v, k32, v32 = pre[:5]
    outs['k'], outs['v'] = k32, v32
    if sample:
        att = _attn_sample_call(q.astype(F32), k32, v32, state['ck'], state['cv'])
        mix1 = _odd_post_stage(x, att, mod, w['od_w_o'])
    else:
        q_pl, k_pl, v_pl = pre[5:]
        branches = (_attn_prompt_call(1, q[None], k[None], v[None]),
                    _attn_prompt_call(4, q_pl, k_pl, v_pl),
                    _attn_prompt_call(16, q_pl, k_pl, v_pl))
        mix1 = _odd_post_merge_stage(x, branches, mod, w['od_w_o'])
    prev = (state['conv'][1][:, 0], state['conv'][1][:, 1]) if sample else None
    y, conv1 = _ffn_call(sample, 1, None, mod, w['ng'][1][1], w['ffn_w_up'], w['ffn_conv_w'],
                         w['ffn_conv_b'], w['ffn_w_down'], prev=prev, final_g=w['final_g'], pre=mix1)
    outs['y'] = y
    outs['conv'] = (conv0, conv1)
    return outs


def kernel(x_prompt, x_sample, c_prompt, c_sample, state_s5_re, state_s5_im, cache_c_k, cache_c_v,
           state_ffn_conv, ada_w, ada_b, norm_g, final_g, ev_w_in, ev_w_out, s5_lam_re, s5_lam_im,
           s5_log_dt, s5_b_re, s5_b_im, s5_c_re, s5_c_im, s5_d, s5_w_glu, s5_b_glu, sg_ln_g, sg_ln_b,
           sg_w, sg_b, od_w_qkv, od_w_o, ffn_w_up, ffn_conv_w, ffn_conv_b, ffn_w_down):
    bp, seq, _ = x_prompt.shape
    bs = x_sample.shape[0]
    assert bp == 1 and seq == SEQ and bs == DEC_BATCH and x_sample.shape[1] == 1

    c_all = jnp.concatenate([c_sample, c_prompt, jnp.zeros((MOD_ROWS - bs - bp, D_MODEL), F32)], axis=0)
    mod_all = _ada_call(c_all, ada_w, ada_b)
    mods_s = [mod_all[l, :bs] for l in range(2)]
    mods_p = [mod_all[l, bs:bs + 1] for l in range(2)]

    hd = SGU_WIDTH // SGU_HEADS
    causal = jnp.tril(jnp.ones((CHUNK, CHUNK), F32))
    w = dict(
        ng=[[norm_g[l, j].reshape(1, D_MODEL) for j in range(2)] for l in range(2)],
        final_g=final_g.reshape(1, D_MODEL),
        ev_w_in=ev_w_in[0].astype(BF16), ev_w_out=ev_w_out[0].astype(BF16),
        sg_ln_g=sg_ln_g[0].reshape(1, SGU_WIDTH), sg_ln_b=sg_ln_b[0].reshape(1, SGU_WIDTH),
        sg_wt=(sg_w[0] * causal[None]).astype(BF16),
        sg_bias=jnp.repeat(sg_b[0].T, hd, axis=1),
        sg_row0=jnp.stack([jnp.repeat(sg_w[0, :, 0, 0], hd), jnp.repeat(sg_b[0, :, 0], hd)], axis=0),
        s5=_s5_tables(s5_lam_re[0], s5_lam_im[0], s5_log_dt[0], s5_b_re[0], s5_b_im[0], s5_c_re[0], s5_c_im[0]),
        s5_d=s5_d[0].reshape(1, S5_WIDTH), s5_w_glu=s5_w_glu[0].astype(BF16),
        s5_b_glu=s5_b_glu[0].reshape(1, S5_WIDTH),
        od_w_qkv=od_w_qkv[0].astype(BF16), od_w_o=od_w_o[0].astype(BF16),
        ffn_w_up=ffn_w_up.astype(BF16), ffn_w_down=ffn_w_down.astype(BF16),
        ffn_conv_w=ffn_conv_w, ffn_conv_b=ffn_conv_b.reshape(2, 1, D_FF),
    )

    wp = dict(w, rope=_rope_tables(jnp.arange(0, seq, ROW_TILE, dtype=jnp.int32),
                                   jnp.arange(ROW_TILE, dtype=jnp.int32)))
    ws = dict(w, rope=_rope_tables(jnp.full((1,), PAST_LEN, jnp.int32), jnp.zeros((bs,), jnp.int32)))

    p = _trunk(False, x_prompt[0], mods_p, None, wp)
    n_all = S5_GROUPS * S5_STATE
    state = dict(s5_re=state_s5_re[0].reshape(bs, n_all), s5_im=state_s5_im[0].reshape(bs, n_all),
                 ck=jnp.transpose(cache_c_k[0], (0, 2, 3, 1)), cv=jnp.transpose(cache_c_v[0], (0, 2, 3, 1)),
                 conv=state_ffn_conv)
    s = _trunk(True, x_sample[:, 0], mods_s, state, ws)

    keep = min(WIN_MAX, seq)
    kv_p = lambda a: a.reshape(1, 1, keep, N_HEADS, HEAD_DIM)
    kv_s = lambda a: a.reshape(1, bs, 1, N_HEADS, HEAD_DIM)
    s5_p = lambda a: a.reshape(1, 1, S5_GROUPS, S5_STATE)
    s5_s = lambda a: a.reshape(1, bs, S5_GROUPS, S5_STATE)
    conv_p = jnp.stack([c.reshape(1, 2, D_FF) for c in p['conv']])
    conv_s = jnp.stack([jnp.stack([state_ffn_conv[l][:, 1], s['conv'][l]], axis=1) for l in range(2)])
    return (p['y'][None], s['y'][:, None], s5_p(p['s5_re']), s5_p(p['s5_im']),
            s5_s(s['s5_re']), s5_s(s['s5_im']), s['vn'].reshape(1, bs, 1, SGU_WIDTH),
            kv_p(p['k']), kv_p(p['v']), kv_s(s['k']), kv_s(s['v']), conv_p, conv_s)
```
